```python
import math
import jax, jax.numpy as jnp
from jax import lax
import numpy as np

D_MODEL = 1024
BATCH = 2
SEQ = 16384
DEPTH = 4
DEC_BATCH = 16
DEC_SEQ = 16
PAST_LEN = 2048

CHUNK = 64
Q_BLOCK = 128
N_AB = (DEPTH + 1) // 2
N_CD = DEPTH // 2
HALF = D_MODEL // 2
HG_HEADS = 4
HG_DK = HALF // HG_HEADS
HG_DV = HALF // HG_HEADS
LRU_WIDTH = HALF
LRU_BLOCKS = 8
LRU_BLOCK = LRU_WIDTH // LRU_BLOCKS
LRU_CONV = 4
LRU_C = 8.0
GD_HEADS = 4
GD_DK = HALF // GD_HEADS
GD_DV = HALF // GD_HEADS
GD_CONV = 4
MLA_HEADS = 4
MLA_NOPE = 128
MLA_ROPE = 64
MLA_V = HALF // MLA_HEADS
MLA_Q_RANK = 384
MLA_KV_RANK = 256
MLA_SCALE = (MLA_NOPE + MLA_ROPE) ** -0.5
ROPE_THETA = 10000.0
D_FF = 2816
FFN_CONV = 3
EPS = 1e-6
NEG_BIG = -1e30
SQRT_FLOOR = 1e-12
AB_IN = 6 * HALF
CD_SPLITS = (3 * HALF, 4 * HALF, 4 * HALF + GD_HEADS, 4 * HALF + 2 * GD_HEADS,
             4 * HALF + 2 * GD_HEADS + MLA_Q_RANK, 4 * HALF + 2 * GD_HEADS + MLA_Q_RANK + MLA_KV_RANK)
CD_IN = CD_SPLITS[-1] + MLA_ROPE
F32 = jnp.float32

kernel_name = 'hybrid_streaming_encoder_step'


def rmsnorm(x, g):
    xf = x.astype(F32)
    y = xf * lax.rsqrt(jnp.mean(xf * xf, axis=-1, keepdims=True) + EPS)
    return (y * g.astype(F32)).astype(x.dtype)


def l2norm(x):
    return x * lax.rsqrt(jnp.sum(x * x, axis=-1, keepdims=True) + EPS)


def rope(x, pos):
    half = x.shape[-1] // 2
    freqs = jnp.exp(-math.log(ROPE_THETA) * jnp.arange(half, dtype=F32) / half)
    ang = pos.astype(F32)[:, None] * freqs
    shape = (ang.shape[0],) + (1,) * (x.ndim - 3) + (half,)
    cos = jnp.cos(ang).reshape(shape)
    sin = jnp.sin(ang).reshape(shape)
    xf = x.astype(F32)
    x1, x2 = xf[..., :half], xf[..., half:]
    return jnp.concatenate([x1 * cos - x2 * sin, x2 * cos + x1 * sin], axis=-1).astype(x.dtype)


def causal_dwconv(x, buf, w, b=None):
    width = w.shape[0]
    T = x.shape[1]
    xp = jnp.concatenate([buf.astype(x.dtype), x], axis=1)
    y = xp[:, 0:T] * w[0]
    for tap in range(1, width):
        y = y + xp[:, tap:tap + T] * w[tap]
    if b is not None:
        y = y + b
    return y, xp[:, T:]


def ada_mod(c, w, b):
    m = (jax.nn.silu(c) @ w + b)[:, None, :]
    return jnp.split(m, 6, axis=-1)


def hgrn2_chunked(q, k, v, log_f, s0):
    B, T, H, DK = q.shape
    DV = v.shape[-1]
    L = CHUNK if T % CHUNK == 0 else T
    n = T // L
    qc, kc, vc, gc = (a.reshape(B, n, L, H, a.shape[-1]).transpose(1, 0, 3, 2, 4) for a in (q, k, v, log_f))
    incl = jnp.tril(jnp.ones((L, L), bool))[:, :, None]

    def step(S, inp):
        q_, k_, v_, g_ = inp
        b = jnp.cumsum(g_, axis=2)
        diff = b[:, :, :, None, :] - b[:, :, None, :, :]
        decay = jnp.exp(jnp.where(incl, diff, NEG_BIG))
        att = jnp.einsum('bhtsd,bhtd,bhsd->bhts', decay, q_, k_)
        o = jnp.einsum('bhts,bhsv->bhtv', att, v_) + jnp.einsum('bhtd,bhdv->bhtv', q_ * jnp.exp(b), S)
        b_last = b[:, :, -1:, :]
        S = jnp.exp(b_last[:, :, 0, :, None]) * S + jnp.einsum('bhsd,bhsv->bhdv', k_ * jnp.exp(b_last - b), v_)
        return S, o

    S, oc = lax.scan(step, s0, (qc, kc, vc, gc))
    return oc.transpose(1, 0, 3, 2, 4).reshape(B, T, H, DV), S


def gated_delta_chunked(q, k, v, log_alpha, beta, s0):
    B, T, H, DK = q.shape
    DV = v.shape[-1]
    L = CHUNK if T % CHUNK == 0 else T
    n = T // L
    qc, kc, vc = (a.reshape(B, n, L, H, a.shape[-1]).transpose(1, 0, 3, 2, 4) for a in (q, k, v))
    gc, bc = (a.reshape(B, n, L, H).transpose(1, 0, 3, 2) for a in (log_alpha, beta))
    g = jnp.cumsum(gc, axis=-1)
    diff = g[..., :, None] - g[..., None, :]
    incl = jnp.tril(jnp.ones((L, L), bool))
    strict = jnp.tril(jnp.ones((L, L), bool), -1)
    dec_incl = jnp.exp(jnp.where(incl, diff, NEG_BIG))
    dec_strict = jnp.where(strict, dec_incl, 0.0)
    m = bc[..., :, None] * jnp.einsum('nbhtd,nbhsd->nbhts', kc, kc) * dec_strict
    eye = jnp.eye(L, dtype=F32)
    rhs = jnp.concatenate([bc[..., None] * vc, (bc * jnp.exp(g))[..., None] * kc], axis=-1)
    sol = lax.linalg.triangular_solve(eye + m, rhs, left_side=True, lower=True, unit_diagonal=True)
    u_v, w_k = sol[..., :DV], sol[..., DV:]
    qk = jnp.einsum('nbhtd,nbhsd->nbhts', qc, kc) * dec_incl

    def step(S, inp):
        u_v_, w_k_, qk_, q_, k_, g_ = inp
        u = u_v_ - jnp.einsum('bhtd,bhdv->bhtv', w_k_, S)
        o = jnp.einsum('bhts,bhsv->bhtv', qk_, u) + jnp.exp(g_)[..., None] * jnp.einsum('bhtd,bhdv->bhtv', q_, S)
        g_last = g_[..., -1:]
        S = jnp.exp(g_last)[..., None] * S + jnp.einsum('bhsd,bhsv->bhdv', k_ * jnp.exp(g_last - g_)[..., None], u)
        return S, o

    S, oc = lax.scan(step, s0, (u_v, w_k, qk, qc, kc, g))
    return oc.transpose(1, 0, 3, 2, 4).reshape(B, T, H, DV), S


def rglru(xc, h0, w_a, b_a, w_x, b_x, lam):
    B, T, W = xc.shape
    xf = xc.astype(F32)
    xb = xf.reshape(B, T, LRU_BLOCKS, LRU_BLOCK)
    r = jax.nn.sigmoid(jnp.einsum('btki,kij->btkj', xb, w_a.astype(F32)).reshape(B, T, W) + b_a.astype(F32))
    i = jax.nn.sigmoid(jnp.einsum('btki,kij->btkj', xb, w_x.astype(F32)).reshape(B, T, W) + b_x.astype(F32))
    log_a = -LRU_C * r * jax.nn.softplus(-lam.astype(F32))
    a = jnp.exp(log_a)
    u = jnp.sqrt(jnp.maximum(-jnp.expm1(2.0 * log_a), SQRT_FLOOR)) * i * xf

    def combine(left, right):
        a_l, u_l = left
        a_r, u_r = right
        return a_l * a_r, a_r * u_l + u_r

    a_cum, h = lax.associative_scan(combine, (a, u), axis=1)
    h = h + a_cum * h0.astype(F32)[:, None, :]
    return h, h[:, -1]


def mla_attend(q_nope, q_rope, k_nope, k_rope, v, q_pos, k_pos):
    B, T, H, _ = q_nope.shape
    blk = Q_BLOCK if T % Q_BLOCK == 0 else T
    k_chunk = k_pos // CHUNK

    def one_block(i):
        start = i * blk
        qn = lax.dynamic_slice_in_dim(q_nope, start, blk, axis=1)
        qr = lax.dynamic_slice_in_dim(q_rope, start, blk, axis=1)
        q_chunk = lax.dynamic_slice_in_dim(q_pos, start, blk, axis=0) // CHUNK
        s = (jnp.einsum('bqhd,bkhd->bhqk', qn, k_nope) + jnp.einsum('bqhd,bkd->bhqk', qr, k_rope)).astype(F32) * MLA_SCALE
        s = jnp.where(k_chunk[None, :] <= q_chunk[:, None], s, NEG_BIG)
        p = jax.nn.softmax(s, axis=-1).astype(v.dtype)
        return jnp.einsum('bhqk,bkhv->bqhv', p, v)

    out = lax.map(one_block, jnp.arange(T // blk))
    return out.transpose(1, 0, 2, 3, 4).reshape(B, T, H * v.shape[-1])


def ab_mixer(h, hg_s0, lru_h0, lru_buf0, w_in, w_out, lower_bound, hg_norm_g,
             conv_w, conv_b, w_a, b_a, w_x, b_x, lam):
    B, T, _ = h.shape
    hq, hf, hi, hz, lx, ly = jnp.split(h @ w_in, 6, axis=-1)
    lb = lower_bound.astype(F32).reshape(HG_HEADS, HG_DK)
    z = hf.astype(F32).reshape(B, T, HG_HEADS, HG_DK)
    log_f = jnp.log(lb + (1.0 - lb) * jax.nn.sigmoid(z))
    k = (1.0 - lb) * jax.nn.sigmoid(-z)
    q = jax.nn.silu(hq.astype(F32)).reshape(B, T, HG_HEADS, HG_DK)
    v = hi.astype(F32).reshape(B, T, HG_HEADS, HG_DV)
    o, hg_s = hgrn2_chunked(q, k, v, log_f, hg_s0.astype(F32))
    o = rmsnorm(o, hg_norm_g) * jax.nn.silu(hz.astype(F32).reshape(B, T, HG_HEADS, HG_DV))
    o_a = o.reshape(B, T, HALF).astype(h.dtype)
    xc, lru_buf = causal_dwconv(lx, lru_buf0, conv_w, conv_b)
    hr, lru_h = rglru(xc, lru_h0, w_a, b_a, w_x, b_x, lam)
    o_b = hr.astype(h.dtype) * jax.nn.gelu(ly)
    y = jnp.concatenate([o_a, o_b], axis=-1) @ w_out
    return y, hg_s, lru_h, lru_buf


def cd_mixer(h, gd_s0, gd_buf0, lat_past, kr_past, w_in, w_out, gd_conv_w, a_log, dt_bias, gd_norm_g,
             q_norm_g, w_qb, kv_norm_g, w_kvb):
    B, T, _ = h.shape
    past_len = lat_past.shape[1]
    qkv, gz, gb, ga, qa, kva, kr = jnp.split(h @ w_in, CD_SPLITS, axis=-1)
    qkv, gd_buf = causal_dwconv(qkv, gd_buf0, gd_conv_w)
    qkv = jax.nn.silu(qkv.astype(F32))
    q, k, v = jnp.split(qkv, 3, axis=-1)
    q = l2norm(q.reshape(B, T, GD_HEADS, GD_DK)) * GD_DK ** -0.5
    k = l2norm(k.reshape(B, T, GD_HEADS, GD_DK))
    v = v.reshape(B, T, GD_HEADS, GD_DV)
    beta = jax.nn.sigmoid(gb.astype(F32))
    log_alpha = -jnp.exp(a_log.astype(F32)) * jax.nn.softplus(ga.astype(F32) + dt_bias.astype(F32))
    o, gd_s = gated_delta_chunked(q, k, v, log_alpha, beta, gd_s0.astype(F32))
    o = rmsnorm(o, gd_norm_g) * jax.nn.silu(gz.astype(F32).reshape(B, T, GD_HEADS, GD_DV))
    o_c = o.reshape(B, T, HALF).astype(h.dtype)
    q_pos = past_len + jnp.arange(T, dtype=jnp.int32)
    qh = (rmsnorm(qa, q_norm_g) @ w_qb).reshape(B, T, MLA_HEADS, MLA_NOPE + MLA_ROPE)
    q_nope = qh[..., :MLA_NOPE]
    q_rope = rope(qh[..., MLA_NOPE:], q_pos)
    c_kv = rmsnorm(kva, kv_norm_g)
    k_r = rope(kr, q_pos)
    lat_all = jnp.concatenate([lat_past.astype(h.dtype), c_kv], axis=1)
    kr_all = jnp.concatenate([kr_past.astype(h.dtype), k_r], axis=1)
    kv = (lat_all @ w_kvb).reshape(B, lat_all.shape[1], MLA_HEADS, MLA_NOPE + MLA_V)
    k_pos = jnp.arange(lat_all.shape[1], dtype=jnp.int32)
    o_d = mla_attend(q_nope, q_rope, kv[..., :MLA_NOPE], kr_all, kv[..., MLA_NOPE:], q_pos, k_pos)
    y = jnp.concatenate([o_c, o_d], axis=-1) @ w_out
    return y, gd_s, gd_buf, c_kv, k_r


def conv_ffn(h, buf, w_up, conv_w, w_down):
    u, buf = causal_dwconv(h @ w_up, buf, conv_w)
    gate, val = jnp.split(u, 2, axis=-1)
    return (jax.nn.silu(gate) * val) @ w_down, buf


def run_group(x, c, hg_s, lru_h, lru_buf, gd_s, gd_buf, lat_past, kr_past, ffn_buf, wts):
    lb_p = jax.nn.softmax(wts['hgrn_lb_logits'].astype(F32), axis=0)
    lower_bounds = jnp.cumsum(lb_p, axis=0) - lb_p[0:1]
    n_hg, n_lru, n_lrub, n_gd, n_gdb, n_lat, n_kr, n_ffn = ([] for _ in range(8))
    for l in range(DEPTH):
        j = l // 2
        shift1, scale1, gate1, shift2, scale2, gate2 = ada_mod(c, wts['ada_w'][l], wts['ada_b'][l])
        g = wts['norm_g'][l]
        h = rmsnorm(x, g[0]) * (1 + scale1) + shift1
        if l % 2 == 0:
            y, s_hg, s_lru, s_lrub = ab_mixer(
                h, hg_s[j], lru_h[j], lru_buf[j], wts['ab_w_in'][j], wts['ab_w_out'][j], lower_bounds[j],
                wts['hgrn_norm_g'][j], wts['lru_conv_w'][j], wts['lru_conv_b'][j], wts['lru_w_a'][j],
                wts['lru_b_a'][j], wts['lru_w_x'][j], wts['lru_b_x'][j], wts['lru_lambda'][j])
            n_hg.append(s_hg)
            n_lru.append(s_lru)
            n_lrub.append(s_lrub)
        else:
            y, s_gd, s_gdb, s_lat, s_kr = cd_mixer(
                h, gd_s[j], gd_buf[j], lat_past[j], kr_past[j], wts['cd_w_in'][j], wts['cd_w_out'][j],
                wts['gdn_conv_w'][j], wts['gdn_a_log'][j], wts['gdn_dt_bias'][j], wts['gdn_norm_g'][j],
                wts['mla_q_norm_g'][j], wts['mla_w_qb'][j], wts['mla_kv_norm_g'][j], wts['mla_w_kvb'][j])
            n_gd.append(s_gd)
            n_gdb.append(s_gdb)
            n_lat.append(s_lat)
            n_kr.append(s_kr)
        x = x + gate1 * rmsnorm(y, g[1])
        h = rmsnorm(x, g[2]) * (1 + scale2) + shift2
        y, s_ffn = conv_ffn(h, ffn_buf[l], wts['ffn_w_up'][l], wts['ffn_conv_w'][l], wts['ffn_w_down'][l])
        n_ffn.append(s_ffn)
        x = x + gate2 * rmsnorm(y, g[3])
    return x, (jnp.stack(n_hg), jnp.stack(n_lru), jnp.stack(n_lrub), jnp.stack(n_gd), jnp.stack(n_gdb),
               jnp.stack(n_lat), jnp.stack(n_kr), jnp.stack(n_ffn))


def setup_inputs(seed: int = 0) -> dict:
    key = jax.random.key(seed)
    it = iter(jax.random.split(key, 48))

    def nrm(shape, scale):
        return jax.random.normal(next(it), shape, F32) * scale

    u = jax.random.uniform(next(it), (N_AB, LRU_WIDTH), F32, 0.9, 0.999)
    a0 = u ** (1.0 / LRU_C)
    dt = jnp.exp(jax.random.uniform(next(it), (N_CD, GD_HEADS), F32, math.log(1e-3), math.log(1e-1)))
    return {
        'x_prompt': nrm((BATCH, SEQ, D_MODEL), 1.0),
        'x_sample': nrm((DEC_BATCH, DEC_SEQ, D_MODEL), 1.0),
        'c_prompt': nrm((BATCH, D_MODEL), 1.0),
        'c_sample': nrm((DEC_BATCH, D_MODEL), 1.0),
        'state_hgrn': nrm((N_AB, DEC_BATCH, HG_HEADS, HG_DK, HG_DV), 0.1),
        'state_rglru': nrm((N_AB, DEC_BATCH, LRU_WIDTH), 0.5),
        'state_rglru_conv': nrm((N_AB, DEC_BATCH, LRU_CONV - 1, LRU_WIDTH), 1.0),
        'state_gdn': nrm((N_CD, DEC_BATCH, GD_HEADS, GD_DK, GD_DV), 0.1),
        'state_gdn_conv': nrm((N_CD, DEC_BATCH, GD_CONV - 1, 3 * HALF), 1.0),
        'cache_mla_latent': nrm((N_CD, DEC_BATCH, PAST_LEN, MLA_KV_RANK), 1.0),
        'cache_mla_krope': nrm((N_CD, DEC_BATCH, PAST_LEN, MLA_ROPE), 1.0),
        'state_ffn_conv': nrm((DEPTH, DEC_BATCH, FFN_CONV - 1, 2 * D_FF), 1.0),
        'ada_w': nrm((DEPTH, D_MODEL, 6 * D_MODEL), 0.5 * D_MODEL ** -0.5),
        'ada_b': nrm((DEPTH, 6 * D_MODEL), 0.02),
        'norm_g': 1.0 + nrm((DEPTH, 4, D_MODEL), 0.05),
        'ab_w_in': nrm((N_AB, D_MODEL, AB_IN), D_MODEL ** -0.5),
        'ab_w_out': nrm((N_AB, 2 * HALF, D_MODEL), (2 * HALF) ** -0.5),
        'hgrn_lb_logits': nrm((N_AB, HALF), 0.5),
        'hgrn_norm_g': 1.0 + nrm((N_AB, HG_DV), 0.05),
        'lru_conv_w': nrm((N_AB, LRU_CONV, LRU_WIDTH), LRU_CONV ** -0.5),
        'lru_conv_b': nrm((N_AB, LRU_WIDTH), 0.02),
        'lru_w_a': nrm((N_AB, LRU_BLOCKS, LRU_BLOCK, LRU_BLOCK), LRU_BLOCK ** -0.5),
        'lru_b_a': nrm((N_AB, LRU_WIDTH), 0.02),
        'lru_w_x': nrm((N_AB, LRU_BLOCKS, LRU_BLOCK, LRU_BLOCK), LRU_BLOCK ** -0.5),
        'lru_b_x': nrm((N_AB, LRU_WIDTH), 0.02),
        'lru_lambda': jnp.log(a0) - jnp.log1p(-a0),
        'cd_w_in': nrm((N_CD, D_MODEL, CD_IN), D_MODEL ** -0.5),
        'cd_w_out': nrm((N_CD, 2 * HALF, D_MODEL), (2 * HALF) ** -0.5),
        'gdn_conv_w': nrm((N_CD, GD_CONV, 3 * HALF), GD_CONV ** -0.5),
        'gdn_a_log': jnp.log(jax.random.uniform(next(it), (N_CD, GD_HEADS), F32, 1.0, 16.0)),
        'gdn_dt_bias': dt + jnp.log(-jnp.expm1(-dt)),
        'gdn_norm_g': 1.0 + nrm((N_CD, GD_DV), 0.05),
        'mla_q_norm_g': 1.0 + nrm((N_CD, MLA_Q_RANK), 0.05),
        'mla_w_qb': nrm((N_CD, MLA_Q_RANK, MLA_HEADS * (MLA_NOPE + MLA_ROPE)), MLA_Q_RANK ** -0.5),
        'mla_kv_norm_g': 1.0 + nrm((N_CD, MLA_KV_RANK), 0.05),
        'mla_w_kvb': nrm((N_CD, MLA_KV_RANK, MLA_HEADS * (MLA_NOPE + MLA_V)), MLA_KV_RANK ** -0.5),
        'ffn_w_up': nrm((DEPTH, D_MODEL, 2 * D_FF), D_MODEL ** -0.5),
        'ffn_conv_w': nrm((DEPTH, FFN_CONV, 2 * D_FF), FFN_CONV ** -0.5),
        'ffn_w_down': nrm((DEPTH, D_FF, D_MODEL), D_FF ** -0.5),
    }


def reference(x_prompt, x_sample, c_prompt, c_sample, state_hgrn, state_rglru, state_rglru_conv, state_gdn,
              state_gdn_conv, cache_mla_latent, cache_mla_krope, state_ffn_conv, ada_w, ada_b, norm_g,
              ab_w_in, ab_w_out, hgrn_lb_logits, hgrn_norm_g, lru_conv_w, lru_conv_b, lru_w_a, lru_b_a,
              lru_w_x, lru_b_x, lru_lambda, cd_w_in, cd_w_out, gdn_conv_w, gdn_a_log, gdn_dt_bias, gdn_norm_g,
              mla_q_norm_g, mla_w_qb, mla_kv_norm_g, mla_w_kvb, ffn_w_up, ffn_conv_w, ffn_w_down):
    wts = dict(ada_w=ada_w, ada_b=ada_b, norm_g=norm_g, ab_w_in=ab_w_in, ab_w_out=ab_w_out,
               hgrn_lb_logits=hgrn_lb_logits, hgrn_norm_g=hgrn_norm_g, lru_conv_w=lru_conv_w,
               lru_conv_b=lru_conv_b, lru_w_a=lru_w_a, lru_b_a=lru_b_a, lru_w_x=lru_w_x, lru_b_x=lru_b_x,
               lru_lambda=lru_lambda, cd_w_in=cd_w_in, cd_w_out=cd_w_out, gdn_conv_w=gdn_conv_w,
               gdn_a_log=gdn_a_log, gdn_dt_bias=gdn_dt_bias, gdn_norm_g=gdn_norm_g, mla_q_norm_g=mla_q_norm_g,
               mla_w_qb=mla_w_qb, mla_kv_norm_g=mla_kv_norm_g, mla_w_kvb=mla_w_kvb, ffn_w_up=ffn_w_up,
               ffn_conv_w=ffn_conv_w, ffn_w_down=ffn_w_down)
    bp = x_prompt.shape[0]
    dt_ = x_prompt.dtype
    y_prompt, p_states = run_group(
        x_prompt, c_prompt,
        jnp.zeros((N_AB, bp, HG_HEADS, HG_DK, HG_DV), F32),
        jnp.zeros((N_AB, bp, LRU_WIDTH), F32),
        jnp.zeros((N_AB, bp, LRU_CONV - 1, LRU_WIDTH), dt_),
        jnp.zeros((N_CD, bp, GD_HEADS, GD_DK, GD_DV), F32),
        jnp.zeros((N_CD, bp, GD_CONV - 1, 3 * HALF), dt_),
        jnp.zeros((N_CD, bp, 0, MLA_KV_RANK), dt_),
        jnp.zeros((N_CD, bp, 0, MLA_ROPE), dt_),
        jnp.zeros((DEPTH, bp, FFN_CONV - 1, 2 * D_FF), dt_),
        wts)
    y_sample, s_states = run_group(
        x_sample, c_sample, state_hgrn, state_rglru, state_rglru_conv, state_gdn, state_gdn_conv,
        cache_mla_latent, cache_mla_krope, state_ffn_conv, wts)
    p_hgrn, p_rglru, p_rglru_conv, p_gdn, p_gdn_conv, p_mla_latent, p_mla_krope, p_ffn_conv = p_states
    s_hgrn, s_rglru, s_rglru_conv, s_gdn, s_gdn_conv, s_mla_latent, s_mla_krope, s_ffn_conv = s_states
    return (y_prompt, y_sample,
            p_hgrn, p_rglru, p_rglru_conv, p_gdn, p_gdn_conv, p_mla_latent, p_mla_krope, p_ffn_conv,
            s_hgrn, s_rglru, s_rglru_conv, s_gdn, s_gdn_conv, s_mla_latent, s_mla_krope, s_ffn_conv)
```

```python
import functools
import math

import jax
import jax.numpy as jnp
from jax import lax
from jax.experimental import pallas as pl
from jax.experimental.pallas import tpu as pltpu

F32 = jnp.float32
BF16 = jnp.bfloat16

D_MODEL = 1024
DEPTH = 4
CHUNK = 64
HALF = D_MODEL // 2
N_AB = (DEPTH + 1) // 2
N_CD = DEPTH // 2
HEADS = 4
HEAD_DIM = HALF // HEADS
LRU_BLOCKS = 8
LRU_BLOCK = HALF // LRU_BLOCKS
LRU_CONV = 4
LRU_C = 8.0
GD_CONV = 4
MLA_NOPE = 128
MLA_ROPE = 64
MLA_QK = MLA_NOPE + MLA_ROPE
MLA_Q_RANK = 384
MLA_KV_RANK = 256
MLA_SCALE = (MLA_NOPE + MLA_ROPE) ** -0.5
ROPE_THETA = 10000.0
D_FF = 2816
FFN_CONV = 3
FF_TILE = 256
N_FF_TILES = D_FF // FF_TILE
EPS = 1e-6
NEG_BIG = -1e30
SQRT_FLOOR = 1e-12
CD_COLS = 2816
SMALL_KR = 0
SMALL_GB = 64
SMALL_GA = 68

VMEM_LIMIT_BYTES = 56 * 1024 * 1024


def _cparams(*sem):
    return pltpu.CompilerParams(dimension_semantics=sem, vmem_limit_bytes=VMEM_LIMIT_BYTES)


def _dot(a, b):
    return jnp.dot(a, b, preferred_element_type=F32)


def _dot_nt(a, b):
    return lax.dot_general(a, b, (((1,), (1,)), ((), ())), preferred_element_type=F32)


def _dot_tn(a, b):
    return lax.dot_general(a, b, (((0,), (0,)), ((), ())), preferred_element_type=F32)


def _rms(x, g):
    return x * lax.rsqrt(jnp.mean(x * x, axis=-1, keepdims=True) + EPS) * g


def _silu(x):
    return x * jax.nn.sigmoid(x)


def _softplus(x):
    return jnp.maximum(x, 0.0) + jnp.log1p(jnp.exp(-jnp.abs(x)))


def _gelu_tanh(x):
    return 0.5 * x * (1.0 + jnp.tanh(math.sqrt(2.0 / math.pi) * (x + 0.044715 * (x * x * x))))


def _split3(x):
    x1 = x.astype(BF16)
    r1 = x - x1.astype(F32)
    x2 = r1.astype(BF16)
    x3 = (r1 - x2.astype(F32)).astype(BF16)
    return x1, x2, x3


def _mm3(a, b):
    a1 = a.astype(BF16)
    a2 = (a - a1.astype(F32)).astype(BF16)
    b1 = b.astype(BF16)
    b2 = (b - b1.astype(F32)).astype(BF16)
    return _dot(a1, b1) + (_dot(a1, b2) + _dot(a2, b1))


def _log2(n):
    assert n & (n - 1) == 0
    return n.bit_length() - 1


def _cumsum_rows(x, row):
    s = 1
    while s < x.shape[0]:
        x = x + jnp.where(row >= s, pltpu.roll(x, s, 0), 0.0)
        s *= 2
    return x


def _block_row_bcast(b, row, h):
    L, n = b.shape
    blk = 2 * h
    if blk >= 8:
        b3 = b.reshape(L // blk, blk, n)
        return jnp.broadcast_to(b3[:, h - 1:h, :], (L // blk, blk, n)).reshape(L, n)
    pos = row & (blk - 1)
    x0 = jnp.where(pos == h - 1, b, 0.0)
    out = x0
    for j in range(1, h + 1):
        out = out + pltpu.roll(x0, j, 0)
    for j in range(1, h):
        out = out + pltpu.roll(x0, L - j, 0)
    return out


def _ada_kernel(c_ref, w_ref, b_ref, o_ref):
    c = _silu(c_ref[...]).astype(BF16)
    o_ref[0] = _dot(c, w_ref[0].astype(BF16)) + b_ref[0]


def _ada_call(c_all, ada_w, ada_b):
    rows = c_all.shape[0]
    tn = 2048
    return pl.pallas_call(
        _ada_kernel,
        out_shape=jax.ShapeDtypeStruct((DEPTH, rows, 6 * D_MODEL), F32),
        grid=(DEPTH, 6 * D_MODEL // tn),
        in_specs=[
            pl.BlockSpec((rows, D_MODEL), lambda l, j: (0, 0)),
            pl.BlockSpec((1, D_MODEL, tn), lambda l, j: (l, 0, j)),
            pl.BlockSpec((1, 1, tn), lambda l, j: (l, 0, j)),
        ],
        out_specs=pl.BlockSpec((1, rows, tn), lambda l, j: (l, 0, j)),
        compiler_params=_cparams("arbitrary", "arbitrary"),
        name="ada_mod",
    )(c_all, ada_w, ada_b.reshape(DEPTH, 1, 6 * D_MODEL))


def _nmm_kernel(x_ref, g_ref, sc_ref, sh_ref, w_ref, o_ref):
    nb, tt, d = x_ref.shape
    h = _rms(x_ref[...], g_ref[...]) * (1.0 + sc_ref[...]) + sh_ref[...]
    y = _dot(h.reshape(nb * tt, d).astype(BF16), w_ref[...])
    o_ref[...] = y.reshape(nb, tt, y.shape[-1])


def _tiles(B, T):
    if T >= 512:
        return 1, 512
    assert B * T <= 512
    return B, T


def _nmm_call(x, g, scale, shift, w):
    B, T, D = x.shape
    N = w.shape[1]
    nb, tt = _tiles(B, T)
    return pl.pallas_call(
        _nmm_kernel,
        out_shape=jax.ShapeDtypeStruct((B, T, N), F32),
        grid=(B // nb, T // tt),
        in_specs=[
            pl.BlockSpec((nb, tt, D), lambda b, i: (b, i, 0)),
            pl.BlockSpec((1, D), lambda b, i: (0, 0)),
            pl.BlockSpec((nb, 1, D), lambda b, i: (b, 0, 0)),
            pl.BlockSpec((nb, 1, D), lambda b, i: (b, 0, 0)),
            pl.BlockSpec((D, N), lambda b, i: (0, 0)),
        ],
        out_specs=pl.BlockSpec((nb, tt, N), lambda b, i: (b, i, 0)),
        compiler_params=_cparams("arbitrary", "arbitrary"),
        name="norm_mod_proj",
    )(x, g.reshape(1, D), scale[:, None, :], shift[:, None, :], w)


def _out_kernel(x_ref, oa_ref, ob_ref, w_ref, g_ref, gate_ref, o_ref):
    nb, tt, d = x_ref.shape
    oa = oa_ref[...].reshape(nb * tt, HALF)
    ob = ob_ref[...].reshape(nb * tt, HALF)
    y = _dot(oa, w_ref[0:HALF, :]) + _dot(ob, w_ref[HALF:2 * HALF, :])
    y = _rms(y, g_ref[...]).reshape(nb, tt, d)
    o_ref[...] = x_ref[...] + gate_ref[...] * y


def _out_call(x, oa, ob, w, g, gate):
    B, T, D = x.shape
    nb, tt = _tiles(B, T)
    return pl.pallas_call(
        _out_kernel,
        out_shape=jax.ShapeDtypeStruct((B, T, D), F32),
        grid=(B // nb, T // tt),
        in_specs=[
            pl.BlockSpec((nb, tt, D), lambda b, i: (b, i, 0)),
            pl.BlockSpec((nb, tt, HALF), lambda b, i: (b, i, 0)),
            pl.BlockSpec((nb, tt, HALF), lambda b, i: (b, i, 0)),
            pl.BlockSpec((D, D), lambda b, i: (0, 0)),
            pl.BlockSpec((1, D), lambda b, i: (0, 0)),
            pl.BlockSpec((nb, 1, D), lambda b, i: (b, 0, 0)),
        ],
        out_specs=pl.BlockSpec((nb, tt, D), lambda b, i: (b, i, 0)),
        compiler_params=_cparams("arbitrary", "arbitrary"),
        name="out_proj_residual",
    )(x, oa, ob, w, g.reshape(1, D), gate[:, None, :])


def _ffn_kernel(x_ref, g1_ref, sc_ref, sh_ref, gate_ref, g2_ref, buf0_ref, wg_ref, wv_ref, wd_ref,
                cw_ref, o_ref, st_ref, carry_ref, ug_ref, uv_ref):
    nb, tt, d = x_ref.shape
    i = pl.program_id(1)
    x = x_ref[...]
    h = (_rms(x, g1_ref[...]) * (1.0 + sc_ref[...]) + sh_ref[...]).reshape(nb * tt, d).astype(BF16)

    @pl.when(i == 0)
    def _():
        for c in range(2 * N_FF_TILES):
            carry_ref[c] = buf0_ref[:, :, c * FF_TILE:(c + 1) * FF_TILE]

    def conv(u, ubuf_ref, c):
        ubuf_ref[:, 8:8 + tt, :] = u
        ubuf_ref[:, 6:8, :] = carry_ref[c]
        cw = cw_ref[c]
        y = (cw[0:1, :] * ubuf_ref[:, 6:6 + tt, :] + cw[1:2, :] * ubuf_ref[:, 7:7 + tt, :]
             + cw[2:3, :] * u)
        tail = ubuf_ref[:, 6 + tt:8 + tt, :]
        carry_ref[c] = tail
        st_ref[:, :, c * FF_TILE:(c + 1) * FF_TILE] = tail
        return y

    acc = jnp.zeros((nb * tt, d), F32)
    for c in range(N_FF_TILES):
        ug = _dot(h, wg_ref[c]).reshape(nb, tt, FF_TILE)
        uv = _dot(h, wv_ref[c]).reshape(nb, tt, FF_TILE)
        yg = conv(ug, ug_ref, c)
        yv = conv(uv, uv_ref, N_FF_TILES + c)
        a = (_silu(yg) * yv).reshape(nb * tt, FF_TILE).astype(BF16)
        acc = acc + _dot(a, wd_ref[c])
    y = _rms(acc, g2_ref[...]).reshape(nb, tt, d)
    o_ref[...] = x + gate_ref[...] * y


def _ffn_call(x, g1, scale, shift, gate, g2, buf0, wg, wv, wd, cw):
    B, T, D = x.shape
    nb, tt = _tiles(B, T)
    assert T >= FFN_CONV - 1
    const3 = lambda b, i: (0, 0, 0)
    return pl.pallas_call(
        _ffn_kernel,
        out_shape=(jax.ShapeDtypeStruct((B, T, D), F32),
                   jax.ShapeDtypeStruct((B, FFN_CONV - 1, 2 * D_FF), F32)),
        grid=(B // nb, T // tt),
        in_specs=[
            pl.BlockSpec((nb, tt, D), lambda b, i: (b, i, 0)),
            pl.BlockSpec((1, D), lambda b, i: (0, 0)),
            pl.BlockSpec((nb, 1, D), lambda b, i: (b, 0, 0)),
            pl.BlockSpec((nb, 1, D), lambda b, i: (b, 0, 0)),
            pl.BlockSpec((nb, 1, D), lambda b, i: (b, 0, 0)),
            pl.BlockSpec((1, D), lambda b, i: (0, 0)),
            pl.BlockSpec((nb, FFN_CONV - 1, 2 * D_FF), lambda b, i: (b, 0, 0)),
            pl.BlockSpec((N_FF_TILES, D, FF_TILE), const3, pipeline_mode=pl.Buffered(1)),
            pl.BlockSpec((N_FF_TILES, D, FF_TILE), const3, pipeline_mode=pl.Buffered(1)),
            pl.BlockSpec((N_FF_TILES, FF_TILE, D), const3, pipeline_mode=pl.Buffered(1)),
            pl.BlockSpec((2 * N_FF_TILES, FFN_CONV, FF_TILE), const3),
        ],
        out_specs=(pl.BlockSpec((nb, tt, D), lambda b, i: (b, i, 0)),
                   pl.BlockSpec((nb, FFN_CONV - 1, 2 * D_FF), lambda b, i: (b, 0, 0))),
        scratch_shapes=[
            pltpu.VMEM((2 * N_FF_TILES, nb, FFN_CONV - 1, FF_TILE), F32),
            pltpu.VMEM((nb, 8 + tt, FF_TILE), F32),
            pltpu.VMEM((nb, 8 + tt, FF_TILE), F32),
        ],
        compiler_params=_cparams("arbitrary", "arbitrary"),
        name="conv_ffn",
    )(x, g1.reshape(1, D), scale[:, None, :], shift[:, None, :], gate[:, None, :], g2.reshape(1, D),
      buf0, wg, wv, wd, cw)


def _hgrn_kernel(hq_ref, hf_ref, hi_ref, hz_ref, lb_ref, ng_ref, s0_ref, o_ref, st_ref):
    L = hq_ref.shape[1]
    i = pl.program_id(1)

    @pl.when(i == 0)
    def _():
        st_ref[...] = s0_ref[...]

    row = lax.broadcasted_iota(jnp.int32, (L, HEAD_DIM), 0)
    r2 = lax.broadcasted_iota(jnp.int32, (L, L), 0)
    c2 = lax.broadcasted_iota(jnp.int32, (L, L), 1)
    for hd in range(HEADS):
        sl = slice(hd * HEAD_DIM, (hd + 1) * HEAD_DIM)
        z = hf_ref[0, :, sl]
        lb = lb_ref[:, sl]
        g = jnp.log(lb + (1.0 - lb) * jax.nn.sigmoid(z))
        k = (1.0 - lb) * jax.nn.sigmoid(-z)
        q = _silu(hq_ref[0, :, sl])
        v = hi_ref[0, :, sl].astype(BF16)
        b = _cumsum_rows(g, row)
        att = jnp.where(r2 == c2, _dot_nt(q.astype(BF16), k.astype(BF16)), 0.0)
        h = L // 2
        while h >= 1:
            r = _block_row_bcast(b, row, h)
            upper = (row & (2 * h - 1)) >= h
            qt = jnp.where(upper, q * jnp.exp(b - r), 0.0).astype(BF16)
            kt = jnp.where(upper, 0.0, k * jnp.exp(r - b)).astype(BF16)
            a = _dot_nt(qt, kt)
            sh = _log2(2 * h)
            att = att + jnp.where((r2 >> sh) == (c2 >> sh), a, 0.0)
            h //= 2
        st = st_ref[0, hd]
        o = _dot(att.astype(BF16), v) + _dot_nt((q * jnp.exp(b)).astype(BF16), st.astype(BF16))
        b_last = b[L - 1:L, :]
        kd = (k * jnp.exp(b_last - b)).astype(BF16)
        st_ref[0, hd] = jnp.exp(b_last) * st + _dot_tn(v, kd)
        o = _rms(o, ng_ref[...]) * _silu(hz_ref[0, :, sl])
        o_ref[0, :, sl] = o.astype(BF16)


def _hgrn_call(proj, lb, ng, s0t):
    B, T, _ = proj.shape
    L = 256 if T % 256 == 0 else T
    assert T % L == 0 and L & (L - 1) == 0 and L >= 8
    col = lambda c: pl.BlockSpec((1, L, HALF), lambda b, i, c=c: (b, i, c))
    return pl.pallas_call(
        _hgrn_kernel,
        out_shape=(jax.ShapeDtypeStruct((B, T, HALF), BF16),
                   jax.ShapeDtypeStruct((B, HEADS, HEAD_DIM, HEAD_DIM), F32)),
        grid=(B, T // L),
        in_specs=[col(0), col(1), col(2), col(3),
                  pl.BlockSpec((1, HALF), lambda b, i: (0, 0)),
                  pl.BlockSpec((1, HEAD_DIM), lambda b, i: (0, 0)),
                  pl.BlockSpec((1, HEADS, HEAD_DIM, HEAD_DIM), lambda b, i: (b, 0, 0, 0))],
        out_specs=(pl.BlockSpec((1, L, HALF), lambda b, i: (b, i, 0)),
                   pl.BlockSpec((1, HEADS, HEAD_DIM, HEAD_DIM), lambda b, i: (b, 0, 0, 0))),
        compiler_params=_cparams("arbitrary", "arbitrary"),
        name="hgrn2",
    )(proj, proj, proj, proj, lb.reshape(1, HALF), ng.reshape(1, HEAD_DIM), s0t)


def _lru_kernel(lx_ref, ly_ref, buf0_ref, h0_ref, cw_ref, cb_ref, wa_ref, wx_ref, ba_ref, bx_ref,
                lam_ref, o_ref, hl_ref, bufo_ref, xp_ref):
    tt = lx_ref.shape[1]
    i = pl.program_id(1)
    npad = LRU_CONV - 1

    @pl.when(i == 0)
    def _():
        xp_ref[8 - npad:8, :] = buf0_ref[0]
        hl_ref[0] = h0_ref[0]

    x = lx_ref[0]
    xp_ref[8:8 + tt, :] = x
    xc = cb_ref[...] + cw_ref[npad:npad + 1, :] * x
    for tap in range(npad):
        xc = xc + cw_ref[tap:tap + 1, :] * xp_ref[8 - npad + tap:8 - npad + tap + tt, :]
    tail = xp_ref[8 + tt - npad:8 + tt, :]
    xp_ref[8 - npad:8, :] = tail
    bufo_ref[0] = tail

    xb = xc.astype(BF16)
    half = HALF // 2
    rpre = jnp.concatenate([_dot(xb[:, 0:half], wa_ref[0]), _dot(xb[:, half:HALF], wa_ref[1])], axis=-1)
    ipre = jnp.concatenate([_dot(xb[:, 0:half], wx_ref[0]), _dot(xb[:, half:HALF], wx_ref[1])], axis=-1)
    r = jax.nn.sigmoid(rpre + ba_ref[...])
    ig = jax.nn.sigmoid(ipre + bx_ref[...])
    log_a = -LRU_C * r * _softplus(-lam_ref[...])
    a = jnp.exp(log_a)
    u = jnp.sqrt(jnp.maximum(-jnp.tanh(log_a) * (1.0 + a * a), SQRT_FLOOR)) * ig * xc

    row = lax.broadcasted_iota(jnp.int32, (tt, HALF), 0)
    s = 1
    while s < tt:
        keep = row >= s
        a_sh = jnp.where(keep, pltpu.roll(a, s, 0), 1.0)
        u_sh = jnp.where(keep, pltpu.roll(u, s, 0), 0.0)
        u = a * u_sh + u
        a = a * a_sh
        s *= 2
    hseq = u + a * hl_ref[0]
    hl_ref[0] = hseq[tt - 1:tt, :]
    o_ref[0] = (hseq * _gelu_tanh(ly_ref[0])).astype(BF16)


def _lru_call(proj, buf0, h0, cw, cb, wa_bd, wx_bd, ba, bx, lam):
    B, T, _ = proj.shape
    tt = 256 if T % 256 == 0 else T
    assert T % tt == 0 and T >= LRU_CONV - 1 and tt % 8 == 0
    vec = pl.BlockSpec((1, HALF), lambda b, i: (0, 0))
    wspec = pl.BlockSpec((2, HALF // 2, HALF // 2), lambda b, i: (0, 0, 0))
    return pl.pallas_call(
        _lru_kernel,
        out_shape=(jax.ShapeDtypeStruct((B, T, HALF), BF16),
                   jax.ShapeDtypeStruct((B, 1, HALF), F32),
                   jax.ShapeDtypeStruct((B, LRU_CONV - 1, HALF), F32)),
        grid=(B, T // tt),
        in_specs=[pl.BlockSpec((1, tt, HALF), lambda b, i: (b, i, 4)),
                  pl.BlockSpec((1, tt, HALF), lambda b, i: (b, i, 5)),
                  pl.BlockSpec((1, LRU_CONV - 1, HALF), lambda b, i: (b, 0, 0)),
                  pl.BlockSpec((1, 1, HALF), lambda b, i: (b, 0, 0)),
                  pl.BlockSpec((LRU_CONV, HALF), lambda b, i: (0, 0)),
                  vec, wspec, wspec, vec, vec, vec],
        out_specs=(pl.BlockSpec((1, tt, HALF), lambda b, i: (b, i, 0)),
                   pl.BlockSpec((1, 1, HALF), lambda b, i: (b, 0, 0)),
                   pl.BlockSpec((1, LRU_CONV - 1, HALF), lambda b, i: (b, 0, 0))),
        scratch_shapes=[pltpu.VMEM((8 + tt, HALF), F32)],
        compiler_params=_cparams("arbitrary", "arbitrary"),
        name="rglru",
    )(proj, proj, buf0, h0[:, None, :], cw, cb.reshape(1, HALF), wa_bd, wx_bd,
      ba.reshape(1, HALF), bx.reshape(1, HALF), lam.reshape(1, HALF))


def _gdn_kernel(qkv_ref, gz_ref, sm_ref, smt_ref, buf0_ref, cw_ref, pcol_ref, prow_ref, ng_ref, s0_ref,
                o_ref, st_ref, bufo_ref, xp_ref, *, L):
    tt = qkv_ref.shape[1]
    nc = tt // L
    sh = _log2(L)
    i = pl.program_id(1)
    npad = GD_CONV - 1

    @pl.when(i == 0)
    def _():
        xp_ref[8 - npad:8, :] = buf0_ref[0]
        st_ref[...] = s0_ref[...]

    x = qkv_ref[0]
    xp_ref[8:8 + tt, :] = x
    xc = cw_ref[npad:npad + 1, :] * x
    for tap in range(npad):
        xc = xc + cw_ref[tap:tap + 1, :] * xp_ref[8 - npad + tap:8 - npad + tap + tt, :]
    tail = xp_ref[8 + tt - npad:8 + tt, :]
    xp_ref[8 - npad:8, :] = tail
    bufo_ref[0] = tail
    xc = _silu(xc)

    r2 = lax.broadcasted_iota(jnp.int32, (tt, tt), 0)
    c2 = lax.broadcasted_iota(jnp.int32, (tt, tt), 1)
    same = (r2 >> sh) == (c2 >> sh)
    incl = jnp.logical_and(same, c2 <= r2)
    strict = jnp.logical_and(same, c2 < r2)
    tri_lo = jnp.where(incl, 1.0, 0.0).astype(BF16)
    tri_up = jnp.where(jnp.logical_and(same, r2 <= c2), 1.0, 0.0).astype(BF16)
    eye = jnp.where(r2 == c2, 1.0, 0.0)

    sm = sm_ref[0]
    beta_cols = jax.nn.sigmoid(sm)
    la_cols = -jnp.exp(pcol_ref[0:1, :]) * _softplus(sm + pcol_ref[1:2, :])
    c1, c2_, c3 = _split3(la_cols)
    g_cols = _dot(tri_lo, c1) + _dot(tri_lo, c2_) + _dot(tri_lo, c3)
    la_rows = -jnp.exp(prow_ref[:, 0:1]) * _softplus(smt_ref[0] + prow_ref[:, 1:2])
    w1, w2, w3 = _split3(la_rows)
    g_rows = _dot(w1, tri_up) + _dot(w2, tri_up) + _dot(w3, tri_up)

    for hd in range(HEADS):
        sl = slice(hd * HEAD_DIM, (hd + 1) * HEAD_DIM)
        q = xc[:, hd * HEAD_DIM:(hd + 1) * HEAD_DIM]
        k = xc[:, HALF + hd * HEAD_DIM:HALF + (hd + 1) * HEAD_DIM]
        v = xc[:, 2 * HALF + hd * HEAD_DIM:2 * HALF + (hd + 1) * HEAD_DIM]
        q = q * lax.rsqrt(jnp.sum(q * q, axis=-1, keepdims=True) + EPS) * (HEAD_DIM ** -0.5)
        k = k * lax.rsqrt(jnp.sum(k * k, axis=-1, keepdims=True) + EPS)
        beta = beta_cols[:, SMALL_GB + hd:SMALL_GB + hd + 1]
        gcol = g_cols[:, SMALL_GA + hd:SMALL_GA + hd + 1]
        grow = g_rows[HEADS + hd:HEADS + hd + 1, :]
        kb = k.astype(BF16)
        dec = jnp.exp(jnp.where(incl, gcol - grow, NEG_BIG))
        m = beta * _dot_nt(kb, kb) * jnp.where(strict, dec, 0.0)
        base = min(L, 8)
        bsh = _log2(base)
        mb = jnp.where((r2 >> bsh) == (c2 >> bsh), m, 0.0)
        tinv = eye - mb
        p = mb
        for _ in range(bsh - 1):
            p = _mm3(p, p)
            tinv = tinv + _mm3(tinv, p)
        s = base
        while s < L:
            ssh = _log2(s)
            off = jnp.where(jnp.logical_and((r2 >> (ssh + 1)) == (c2 >> (ssh + 1)),
                                            (r2 >> ssh) != (c2 >> ssh)), m, 0.0)
            tinv = tinv - _mm3(_mm3(tinv, off), tinv)
            s *= 2
        eg = jnp.exp(gcol)
        sol = _mm3(tinv, jnp.concatenate([beta * v, (beta * eg) * k], axis=-1))
        u_v = sol[:, 0:HEAD_DIM]
        w_k = sol[:, HEAD_DIM:2 * HEAD_DIM].astype(BF16)
        qb = q.astype(BF16)
        qk = (_dot_nt(qb, kb) * dec).astype(BF16)

        S = st_ref[0, hd]
        us, inters = [], []
        for c in range(nc):
            rs = slice(c * L, (c + 1) * L)
            Sb = S.astype(BF16)
            u = u_v[rs] - _dot(w_k[rs], Sb)
            inters.append(eg[rs] * _dot(qb[rs], Sb))
            g_last = gcol[(c + 1) * L - 1:(c + 1) * L, :]
            kd = (k[rs] * jnp.exp(g_last - gcol[rs])).astype(BF16)
            S = jnp.exp(g_last) * S + _dot_tn(kd, u.astype(BF16))
            us.append(u)
        st_ref[0, hd] = S
        u_all = jnp.concatenate(us, axis=0) if nc > 1 else us[0]
        inter = jnp.concatenate(inters, axis=0) if nc > 1 else inters[0]
        o = _dot(qk, u_all.astype(BF16)) + inter
        o = _rms(o, ng_ref[...]) * _silu(gz_ref[0, :, sl])
        o_ref[0, :, sl] = o.astype(BF16)


def _gdn_call(proj, small_t, buf0, cw, pcol, prow, ng, s0):
    B, T, _ = proj.shape
    L = CHUNK if T % CHUNK == 0 else T
    tt = 256 if T % 256 == 0 else T
    assert T % tt == 0 and tt % L == 0 and T >= GD_CONV - 1 and L >= 2
    W = 3 * HALF
    return pl.pallas_call(
        functools.partial(_gdn_kernel, L=L),
        out_shape=(jax.ShapeDtypeStruct((B, T, HALF), BF16),
                   jax.ShapeDtypeStruct((B, HEADS, HEAD_DIM, HEAD_DIM), F32),
                   jax.ShapeDtypeStruct((B, GD_CONV - 1, W), F32)),
        grid=(B, T // tt),
        in_specs=[pl.BlockSpec((1, tt, W), lambda b, i: (b, i, 0)),
                  pl.BlockSpec((1, tt, HALF), lambda b, i: (b, i, 3)),
                  pl.BlockSpec((1, tt, 128), lambda b, i: (b, i, 21)),
                  pl.BlockSpec((1, 2 * HEADS, tt), lambda b, i: (b, 0, i)),
                  pl.BlockSpec((1, GD_CONV - 1, W), lambda b, i: (b, 0, 0)),
                  pl.BlockSpec((GD_CONV, W), lambda b, i: (0, 0)),
                  pl.BlockSpec((2, 128), lambda b, i: (0, 0)),
                  pl.BlockSpec((2 * HEADS, 2), lambda b, i: (0, 0)),
                  pl.BlockSpec((1, HEAD_DIM), lambda b, i: (0, 0)),
                  pl.BlockSpec((1, HEADS, HEAD_DIM, HEAD_DIM), lambda b, i: (b, 0, 0, 0))],
        out_specs=(pl.BlockSpec((1, tt, HALF), lambda b, i: (b, i, 0)),
                   pl.BlockSpec((1, HEADS, HEAD_DIM, HEAD_DIM), lambda b, i: (b, 0, 0, 0)),
                   pl.BlockSpec((1, GD_CONV - 1, W), lambda b, i: (b, 0, 0))),
        scratch_shapes=[pltpu.VMEM((8 + tt, W), F32)],
        compiler_params=_cparams("arbitrary", "arbitrary"),
        name="gated_deltanet",
    )(proj, proj, proj, small_t, buf0, cw, pcol, prow, ng.reshape(1, HEAD_DIM), s0)


def _rope64(x, cs, sn):
    half = MLA_ROPE // 2
    swapped = jnp.concatenate([x[:, half:], x[:, :half]], axis=-1)
    return x * cs + swapped * sn


def _expand_kv(c_kv, k_r, wkvb_ref, kf_ref, v_ref):
    kv = _dot(c_kv.astype(BF16), wkvb_ref[...])
    for hd in range(HEADS):
        base = hd * 2 * HEAD_DIM
        kf_ref[0, hd] = jnp.concatenate([kv[:, base:base + MLA_NOPE], k_r], axis=-1).astype(BF16)
        v_ref[0, hd] = kv[:, base + MLA_NOPE:base + 2 * HEAD_DIM].astype(BF16)


def _mla_prep_kernel(qa_ref, kva_ref, sm_ref, cs_ref, sn_ref, qng_ref, wqb_ref, kvng_ref, wkvb_ref,
                     q_ref, ckv_ref, kr_ref, kf_ref, v_ref):
    cs = cs_ref[...]
    sn = sn_ref[...]
    qn = _rms(qa_ref[0], qng_ref[...]).astype(BF16)
    qh = _dot(qn, wqb_ref[...]) * MLA_SCALE
    for hd in range(HEADS):
        nope = qh[:, hd * MLA_NOPE:(hd + 1) * MLA_NOPE]
        off = HEADS * MLA_NOPE + hd * MLA_ROPE
        rot = _rope64(qh[:, off:off + MLA_ROPE], cs, sn)
        q_ref[0, hd] = jnp.concatenate([nope, rot], axis=-1).astype(BF16)
    c_kv = _rms(kva_ref[0], kvng_ref[...])
    ckv_ref[0] = c_kv
    k_r = _rope64(sm_ref[0, :, SMALL_KR:SMALL_KR + MLA_ROPE], cs, sn)
    kr_ref[0] = k_r
    _expand_kv(c_kv, k_r, wkvb_ref, kf_ref, v_ref)


def _mla_prep_call(proj, cs, sn, qng, wqb, kvng, wkvb):
    B, T, _ = proj.shape
    tt = 512 if T % 512 == 0 else T
    c2 = lambda b, i: (0, 0)
    return pl.pallas_call(
        _mla_prep_kernel,
        out_shape=(jax.ShapeDtypeStruct((B, HEADS, T, MLA_QK), BF16),
                   jax.ShapeDtypeStruct((B, T, MLA_KV_RANK), F32),
                   jax.ShapeDtypeStruct((B, T, MLA_ROPE), F32),
                   jax.ShapeDtypeStruct((B, HEADS, T, MLA_QK), BF16),
                   jax.ShapeDtypeStruct((B, HEADS, T, HEAD_DIM), BF16)),
        grid=(B, T // tt),
        in_specs=[pl.BlockSpec((1, tt, MLA_Q_RANK), lambda b, i: (b, i, 6)),
                  pl.BlockSpec((1, tt, MLA_KV_RANK), lambda b, i: (b, i, 8)),
                  pl.BlockSpec((1, tt, 128), lambda b, i: (b, i, 21)),
                  pl.BlockSpec((tt, MLA_ROPE), lambda b, i: (i, 0)),
                  pl.BlockSpec((tt, MLA_ROPE), lambda b, i: (i, 0)),
                  pl.BlockSpec((1, MLA_Q_RANK), c2),
                  pl.BlockSpec((MLA_Q_RANK, HEADS * MLA_QK), c2),
                  pl.BlockSpec((1, MLA_KV_RANK), c2),
                  pl.BlockSpec((MLA_KV_RANK, HEADS * 2 * HEAD_DIM), c2)],
        out_specs=(pl.BlockSpec((1, HEADS, tt, MLA_QK), lambda b, i: (b, 0, i, 0)),
                   pl.BlockSpec((1, tt, MLA_KV_RANK), lambda b, i: (b, i, 0)),
                   pl.BlockSpec((1, tt, MLA_ROPE), lambda b, i: (b, i, 0)),
                   pl.BlockSpec((1, HEADS, tt, MLA_QK), lambda b, i: (b, 0, i, 0)),
                   pl.BlockSpec((1, HEADS, tt, HEAD_DIM), lambda b, i: (b, 0, i, 0))),
        compiler_params=_cparams("arbitrary", "arbitrary"),
        name="mla_prep",
    )(proj, proj, proj, cs, sn, qng.reshape(1, MLA_Q_RANK), wqb, kvng.reshape(1, MLA_KV_RANK), wkvb)


def _mla_past_kernel(lat_ref, kr_ref, wkvb_ref, kf_ref, v_ref):
    _expand_kv(lat_ref[0], kr_ref[0], wkvb_ref, kf_ref, v_ref)


def _mla_past_call(lat, kr, wkvb):
    B, P, _ = lat.shape
    tt = 512 if P % 512 == 0 else P
    return pl.pallas_call(
        _mla_past_kernel,
        out_shape=(jax.ShapeDtypeStruct((B, HEADS, P, MLA_QK), BF16),
                   jax.ShapeDtypeStruct((B, HEADS, P, HEAD_DIM), BF16)),
        grid=(B, P // tt),
        in_specs=[pl.BlockSpec((1, tt, MLA_KV_RANK), lambda b, i: (b, i, 0)),
                  pl.BlockSpec((1, tt, MLA_ROPE), lambda b, i: (b, i, 0)),
                  pl.BlockSpec((MLA_KV_RANK, HEADS * 2 * HEAD_DIM), lambda b, i: (0, 0))],
        out_specs=(pl.BlockSpec((1, HEADS, tt, MLA_QK), lambda b, i: (b, 0, i, 0)),
                   pl.BlockSpec((1, HEADS, tt, HEAD_DIM), lambda b, i: (b, 0, i, 0))),
        compiler_params=_cparams("arbitrary", "arbitrary"),
        name="mla_past_kv",
    )(lat, kr, wkvb)


def _attn_kernel(q_ref, k_ref, v_ref, o_ref, m_ref, l_ref, acc_ref, *, past_len, tq, tk):
    i = pl.program_id(2)
    j = pl.program_id(3)
    nk = pl.num_programs(3)
    sh = _log2(CHUNK)

    @pl.when(j == 0)
    def _():
        m_ref[...] = jnp.full(m_ref.shape, NEG_BIG, F32)
        l_ref[...] = jnp.zeros(l_ref.shape, F32)
        acc_ref[...] = jnp.zeros(acc_ref.shape, F32)

    q_lo = past_len + i * tq
    k_lo = j * tk
    needed = (k_lo >> sh) <= ((q_lo + tq - 1) >> sh)
    full = ((k_lo + tk - 1) >> sh) <= (q_lo >> sh)

    def step(masked):
        s = _dot_nt(q_ref[0, 0], k_ref[0, 0])
        if masked:
            qc = (q_lo + lax.broadcasted_iota(jnp.int32, (tq, tk), 0)) >> sh
            kc = (k_lo + lax.broadcasted_iota(jnp.int32, (tq, tk), 1)) >> sh
            s = jnp.where(kc <= qc, s, NEG_BIG)
        m_prev = m_ref[...]
        m_new = jnp.maximum(m_prev, jnp.max(s, axis=-1, keepdims=True))
        alpha = jnp.exp(m_prev - m_new)
        p = jnp.exp(s - m_new)
        l_ref[...] = alpha * l_ref[...] + jnp.sum(p, axis=-1, keepdims=True)
        acc_ref[...] = alpha * acc_ref[...] + _dot(p.astype(BF16), v_ref[0, 0])
        m_ref[...] = m_new

    @pl.when(jnp.logical_and(needed, full))
    def _():
        step(False)

    @pl.when(jnp.logical_and(needed, jnp.logical_not(full)))
    def _():
        step(True)

    @pl.when(j == nk - 1)
    def _():
        o_ref[0] = (acc_ref[...] / l_ref[...]).astype(BF16)


def _attn_call(q, kf, v, past_len):
    B, H, Tq, _ = q.shape
    Tk = kf.shape[2]
    tq = 512 if Tq % 512 == 0 else Tq
    tk = 512 if Tk % 512 == 0 else Tk
    nk = Tk // tk

    def kv_map(b, h, i, j):
        last_chunk = (past_len + i * tq + tq - 1) // CHUNK
        j_max = jnp.minimum(nk - 1, (last_chunk * CHUNK + CHUNK - 1) // tk)
        return (b, h, jnp.minimum(j, j_max), 0)

    return pl.pallas_call(
        functools.partial(_attn_kernel, past_len=past_len, tq=tq, tk=tk),
        out_shape=jax.ShapeDtypeStruct((B, Tq, H * HEAD_DIM), BF16),
        grid=(B, H, Tq // tq, nk),
        in_specs=[pl.BlockSpec((1, 1, tq, MLA_QK), lambda b, h, i, j: (b, h, i, 0)),
                  pl.BlockSpec((1, 1, tk, MLA_QK), kv_map),
                  pl.BlockSpec((1, 1, tk, HEAD_DIM), kv_map)],
        out_specs=pl.BlockSpec((1, tq, HEAD_DIM), lambda b, h, i, j: (b, i, h)),
        scratch_shapes=[pltpu.VMEM((tq, 1), F32), pltpu.VMEM((tq, 1), F32),
                        pltpu.VMEM((tq, HEAD_DIM), F32)],
        compiler_params=_cparams("arbitrary", "arbitrary", "arbitrary", "arbitrary"),
        name="mla_attention",
    )(q, kf, v)


def _block_diag_pairs(w):
    per = (HALF // 2) // LRU_BLOCK
    out = jnp.zeros((2, HALF // 2, HALF // 2), w.dtype)
    for p in range(2):
        for q in range(per):
            blk = w[p * per + q]
            out = out.at[p, q * LRU_BLOCK:(q + 1) * LRU_BLOCK, q * LRU_BLOCK:(q + 1) * LRU_BLOCK].set(blk)
    return out.astype(BF16)


def _prep_cd_w_in(w):
    o = 3 * HALF
    qkv, gz = w[:, :o], w[:, o:o + HALF]
    o += HALF
    gb, ga = w[:, o:o + HEADS], w[:, o + HEADS:o + 2 * HEADS]
    o += 2 * HEADS
    qa, kva = w[:, o:o + MLA_Q_RANK], w[:, o + MLA_Q_RANK:o + MLA_Q_RANK + MLA_KV_RANK]
    o += MLA_Q_RANK + MLA_KV_RANK
    kr = w[:, o:o + MLA_ROPE]
    small = jnp.zeros((w.shape[0], 128), w.dtype)
    small = small.at[:, SMALL_KR:SMALL_KR + MLA_ROPE].set(kr)
    small = small.at[:, SMALL_GB:SMALL_GB + HEADS].set(gb)
    small = small.at[:, SMALL_GA:SMALL_GA + HEADS].set(ga)
    out = jnp.concatenate([qkv, gz, kva, qa, small], axis=-1)
    assert out.shape[1] == CD_COLS
    return out.astype(BF16)


def _prep_wqb(w):
    w4 = w.reshape(MLA_Q_RANK, HEADS, MLA_QK)
    nope = w4[:, :, :MLA_NOPE].reshape(MLA_Q_RANK, HEADS * MLA_NOPE)
    rope = w4[:, :, MLA_NOPE:].reshape(MLA_Q_RANK, HEADS * MLA_ROPE)
    return jnp.concatenate([nope, rope], axis=-1).astype(BF16)


def _rope_tables(T, past_len):
    half = MLA_ROPE // 2
    freqs = jnp.exp(-math.log(ROPE_THETA) * jnp.arange(half, dtype=F32) / half)
    pos = past_len + jnp.arange(T, dtype=jnp.int32)
    ang = pos.astype(F32)[:, None] * freqs
    cos, sin = jnp.cos(ang), jnp.sin(ang)
    return jnp.concatenate([cos, cos], axis=-1), jnp.concatenate([-sin, sin], axis=-1)


def _run_group(x, mods, hg_s, lru_h, lru_buf, gd_s, gd_buf, lat_past, kr_past, ffn_buf, W):
    B, T, _ = x.shape
    n_hg, n_lru, n_lrub, n_gd, n_gdb, n_lat, n_kr, n_ffn = ([] for _ in range(8))
    for l in range(DEPTH):
        j = l // 2
        shift1, scale1, gate1, shift2, scale2, gate2 = jnp.split(mods[l], 6, axis=-1)
        g = W['norm_g'][l]
        if l % 2 == 0:
            proj = _nmm_call(x, g[0], scale1, shift1, W['ab_w_in'][j])
            o_a, s_hg_t = _hgrn_call(proj, W['lower_bounds'][j], W['hgrn_norm_g'][j],
                                     jnp.swapaxes(hg_s[j], -1, -2))
            o_b, s_lru, s_lrub = _lru_call(proj, lru_buf[j], lru_h[j], W['lru_conv_w'][j], W['lru_conv_b'][j],
                                           W['lru_wa_bd'][j], W['lru_wx_bd'][j], W['lru_b_a'][j],
                                           W['lru_b_x'][j], W['lru_lambda'][j])
            n_hg.append(jnp.swapaxes(s_hg_t, -1, -2))
            n_lru.append(s_lru[:, 0, :])
            n_lrub.append(s_lrub)
            x = _out_call(x, o_a, o_b, W['ab_w_out'][j], g[1], gate1)
        else:
            past_len = lat_past.shape[2]
            proj = _nmm_call(x, g[0], scale1, shift1, W['cd_w_in'][j])
            small_t = jnp.swapaxes(proj[:, :, CD_COLS - 128 + SMALL_GB:CD_COLS - 128 + SMALL_GB + 2 * HEADS], 1, 2)
            o_c, s_gd, s_gdb = _gdn_call(proj, small_t, gd_buf[j], W['gdn_conv_w'][j], W['gdn_pcol'][j],
                                         W['gdn_prow'][j], W['gdn_norm_g'][j], gd_s[j])
            cs, sn = _rope_tables(T, past_len)
            q, c_kv, k_r, kf, v = _mla_prep_call(proj, cs, sn, W['mla_q_norm_g'][j], W['mla_w_qb'][j],
                                                 W['mla_kv_norm_g'][j], W['mla_w_kvb'][j])
            if past_len > 0:
                kf_p, v_p = _mla_past_call(lat_past[j], kr_past[j], W['mla_w_kvb'][j])
                kf = jnp.concatenate([kf_p, kf], axis=2)
                v = jnp.concatenate([v_p, v], axis=2)
            o_d = _attn_call(q, kf, v, past_len)
            n_gd.append(s_gd)
            n_gdb.append(s_gdb)
            n_lat.append(c_kv)
            n_kr.append(k_r)
            x = _out_call(x, o_c, o_d, W['cd_w_out'][j], g[1], gate1)
        x, s_ffn = _ffn_call(x, g[2], scale2, shift2, gate2, g[3], ffn_buf[l], W['ffn_wg'][l], W['ffn_wv'][l],
                             W['ffn_wd'][l], W['ffn_cw'][l])
        n_ffn.append(s_ffn)
    return x, (jnp.stack(n_hg), jnp.stack(n_lru), jnp.stack(n_lrub), jnp.stack(n_gd), jnp.stack(n_gdb),
               jnp.stack(n_lat), jnp.stack(n_kr), jnp.stack(n_ffn))


def _prep_weights(norm_g, ab_w_in, ab_w_out, hgrn_lb_logits, hgrn_norm_g, lru_conv_w, lru_conv_b, lru_w_a, lru_b_a,
                  lru_w_x, lru_b_x, lru_lambda, cd_w_in, cd_w_out, gdn_conv_w, gdn_a_log, gdn_dt_bias, gdn_norm_g,
                  mla_q_norm_g, mla_w_qb, mla_kv_norm_g, mla_w_kvb, ffn_w_up, ffn_conv_w, ffn_w_down):
    lb_p = jax.nn.softmax(hgrn_lb_logits.astype(F32), axis=0)
    pcol = jnp.zeros((N_CD, 2, 128), F32)
    pcol = pcol.at[:, 0, SMALL_GA:SMALL_GA + HEADS].set(gdn_a_log).at[:, 1, SMALL_GA:SMALL_GA + HEADS].set(gdn_dt_bias)
    prow = jnp.zeros((N_CD, 2 * HEADS, 2), F32)
    prow = prow.at[:, HEADS:, 0].set(gdn_a_log).at[:, HEADS:, 1].set(gdn_dt_bias)
    ffn_cw = jnp.swapaxes(ffn_conv_w.reshape(DEPTH, FFN_CONV, 2 * N_FF_TILES, FF_TILE), 1, 2)
    W = dict(
        norm_g=norm_g,
        ab_w_in=ab_w_in.astype(BF16), ab_w_out=ab_w_out.astype(BF16),
        lower_bounds=jnp.cumsum(lb_p, axis=0) - lb_p[0:1],
        hgrn_norm_g=hgrn_norm_g, lru_conv_w=lru_conv_w, lru_conv_b=lru_conv_b,
        lru_wa_bd=jax.vmap(_block_diag_pairs)(lru_w_a), lru_wx_bd=jax.vmap(_block_diag_pairs)(lru_w_x),
        lru_b_a=lru_b_a, lru_b_x=lru_b_x, lru_lambda=lru_lambda,
        cd_w_in=jax.vmap(_prep_cd_w_in)(cd_w_in), cd_w_out=cd_w_out.astype(BF16),
        gdn_conv_w=gdn_conv_w, gdn_pcol=pcol, gdn_prow=prow, gdn_norm_g=gdn_norm_g,
        mla_q_norm_g=mla_q_norm_g, mla_w_qb=jax.vmap(_prep_wqb)(mla_w_qb),
        mla_kv_norm_g=mla_kv_norm_g, mla_w_kvb=mla_w_kvb.astype(BF16),
        ffn_wg=jnp.swapaxes(ffn_w_up[:, :, :D_FF].reshape(DEPTH, D_MODEL, N_FF_TILES, FF_TILE), 1, 2).astype(BF16),
        ffn_wv=jnp.swapaxes(ffn_w_up[:, :, D_FF:].reshape(DEPTH, D_MODEL, N_FF_TILES, FF_TILE), 1, 2).astype(BF16),
        ffn_wd=ffn_w_down.reshape(DEPTH, N_FF_TILES, FF_TILE, D_MODEL).astype(BF16),
        ffn_cw=ffn_cw,
    )
    return W


def kernel(x_prompt, x_sample, c_prompt, c_sample, state_hgrn, state_rglru, state_rglru_conv, state_gdn, state_gdn_conv, cache_mla_latent, cache_mla_krope, state_ffn_conv, ada_w, ada_b, norm_g, ab_w_in, ab_w_out, hgrn_lb_logits, hgrn_norm_g, lru_conv_w, lru_conv_b, lru_w_a, lru_b_a, lru_w_x, lru_b_x, lru_lambda, cd_w_in, cd_w_out, gdn_conv_w, gdn_a_log, gdn_dt_bias, gdn_norm_g, mla_q_norm_g, mla_w_qb, mla_kv_norm_g, mla_w_kvb, ffn_w_up, ffn_conv_w, ffn_w_down):
    bp, bs = x_prompt.shape[0], x_sample.shape[0]
    W = _prep_weights(norm_g, ab_w_in, ab_w_out, hgrn_lb_logits, hgrn_norm_g, lru_conv_w, lru_conv_b, lru_w_a, lru_b_a,
                      lru_w_x, lru_b_x, lru_lambda, cd_w_in, cd_w_out, gdn_conv_w, gdn_a_log, gdn_dt_bias,
                      gdn_norm_g, mla_q_norm_g, mla_w_qb, mla_kv_norm_g, mla_w_kvb, ffn_w_up, ffn_conv_w,
                      ffn_w_down)
    rows = bp + bs
    rows_pad = -(-rows // 8) * 8
    c_all = jnp.concatenate([c_prompt, c_sample, jnp.zeros((rows_pad - rows, D_MODEL), F32)], axis=0)
    mods = _ada_call(c_all, ada_w, ada_b)
    dt_ = x_prompt.dtype
    y_prompt, p_states = _run_group(
        x_prompt, mods[:, :bp],
        jnp.zeros((N_AB, bp, HEADS, HEAD_DIM, HEAD_DIM), F32),
        jnp.zeros((N_AB, bp, HALF), F32),
        jnp.zeros((N_AB, bp, LRU_CONV - 1, HALF), dt_),
        jnp.zeros((N_CD, bp, HEADS, HEAD_DIM, HEAD_DIM), F32),
        jnp.zeros((N_CD, bp, GD_CONV - 1, 3 * HALF), dt_),
        jnp.zeros((N_CD, bp, 0, MLA_KV_RANK), dt_),
        jnp.zeros((N_CD, bp, 0, MLA_ROPE), dt_),
        jnp.zeros((DEPTH, bp, FFN_CONV - 1, 2 * D_FF), dt_),
        W)
    y_sample, s_states = _run_group(
        x_sample, mods[:, bp:rows], state_hgrn, state_rglru, state_rglru_conv, state_gdn, state_gdn_conv,
        cache_mla_latent, cache_mla_krope, state_ffn_conv, W)
    return (y_prompt, y_sample) + tuple(p_states) + tuple(s_states)
```

```python
import functools
import math

import jax
import jax.numpy as jnp
from jax import lax
from jax.experimental import pallas as pl
from jax.experimental.pallas import tpu as pltpu

F32 = jnp.float32
BF16 = jnp.bfloat16

D_MODEL = 1024
DEPTH = 4
CHUNK = 64
HALF = D_MODEL // 2
N_AB = (DEPTH + 1) // 2
N_CD = DEPTH // 2
HEADS = 4
HEAD_DIM = HALF // HEADS
LRU_BLOCKS = 8
LRU_BLOCK = HALF // LRU_BLOCKS
LRU_CONV = 4
LRU_C = 8.0
GD_CONV = 4
MLA_NOPE = 128
MLA_ROPE = 64
MLA_QK = MLA_NOPE + MLA_ROPE
MLA_Q_RANK = 384
MLA_KV_RANK = 256
MLA_SCALE = (MLA_NOPE + MLA_ROPE) ** -0.5
ROPE_THETA = 10000.0
D_FF = 2816
FFN_CONV = 3
FF_TILE = 256
N_FF_TILES = D_FF // FF_TILE
FF_GROUP = 4
FF_SLOTS = 4
EPS = 1e-6
NEG_BIG = -1e30
SQRT_FLOOR = 1e-12
CD_COLS = 2816
SMALL_KR = 0
SMALL_GB = 64
SMALL_GA = 68

ATTN_UNROLL = 4

VMEM_LIMIT_BYTES =56 * 1024 * 1024


def _cparams(*sem):
    return pltpu.CompilerParams(dimension_semantics=sem, vmem_limit_bytes=VMEM_LIMIT_BYTES)


def _dot(a, b):
    return jnp.dot(a, b, preferred_element_type=F32)


def _dot_nt(a, b):
    return lax.dot_general(a, b, (((1,), (1,)), ((), ())), preferred_element_type=F32)


def _dot_tn(a, b):
    return lax.dot_general(a, b, (((0,), (0,)), ((), ())), preferred_element_type=F32)


def _rms(x, g):
    return x * lax.rsqrt(jnp.mean(x * x, axis=-1, keepdims=True) + EPS) * g


def _silu(x):
    return x * jax.nn.sigmoid(x)


def _softplus(x):
    return jnp.maximum(x, 0.0) + jnp.log1p(jnp.exp(-jnp.abs(x)))


def _gelu_tanh(x):
    return 0.5 * x * (1.0 + jnp.tanh(math.sqrt(2.0 / math.pi) * (x + 0.044715 * (x * x * x))))


def _split3(x):
    x1 = x.astype(BF16)
    r1 = x - x1.astype(F32)
    x2 = r1.astype(BF16)
    x3 = (r1 - x2.astype(F32)).astype(BF16)
    return x1, x2, x3


def _log2(n):
    assert n & (n - 1) == 0
    return n.bit_length() - 1


def _cumsum_rows(x, row):
    s = 1
    while s < x.shape[0]:
        x = x + jnp.where(row >= s, pltpu.roll(x, s, 0), 0.0)
        s *= 2
    return x


def _block_row_bcast(b, row, h):
    L, n = b.shape
    blk = 2 * h
    if blk >= 8:
        b3 = b.reshape(L // blk, blk, n)
        return jnp.broadcast_to(b3[:, h - 1:h, :], (L // blk, blk, n)).reshape(L, n)
    pos = row & (blk - 1)
    x0 = jnp.where(pos == h - 1, b, 0.0)
    out = x0
    for j in range(1, h + 1):
        out = out + pltpu.roll(x0, j, 0)
    for j in range(1, h):
        out = out + pltpu.roll(x0, L - j, 0)
    return out


def _ada_kernel(c_ref, w_ref, b_ref, o_ref):
    c = _silu(c_ref[...]).astype(BF16)
    o_ref[0] = _dot(c, w_ref[0].astype(BF16)) + b_ref[0]


def _ada_call(c_all, ada_w, ada_b):
    rows = c_all.shape[0]
    tn = 2048
    return pl.pallas_call(
        _ada_kernel,
        out_shape=jax.ShapeDtypeStruct((DEPTH, rows, 6 * D_MODEL), F32),
        grid=(DEPTH, 6 * D_MODEL // tn),
        in_specs=[
            pl.BlockSpec((rows, D_MODEL), lambda l, j: (0, 0)),
            pl.BlockSpec((1, D_MODEL, tn), lambda l, j: (l, 0, j)),
            pl.BlockSpec((1, 1, tn), lambda l, j: (l, 0, j)),
        ],
        out_specs=pl.BlockSpec((1, rows, tn), lambda l, j: (l, 0, j)),
        compiler_params=_cparams("arbitrary", "arbitrary"),
        name="ada_mod",
    )(c_all, ada_w, ada_b.reshape(DEPTH, 1, 6 * D_MODEL))


def _nmm_kernel(x_ref, g_ref, sc_ref, sh_ref, w_ref, o_ref):
    nb, tt, d = x_ref.shape
    h = _rms(x_ref[...], g_ref[...]) * (1.0 + sc_ref[...]) + sh_ref[...]
    y = _dot(h.reshape(nb * tt, d).astype(BF16), w_ref[...])
    o_ref[...] = y.reshape(nb, tt, y.shape[-1])


def _tiles(B, T):
    if T >= 512:
        return 1, 512
    assert B * T <= 512
    return B, T


def _nmm_call(x, g, scale, shift, w):
    B, T, D = x.shape
    N = w.shape[1]
    nb, tt = _tiles(B, T)
    return pl.pallas_call(
        _nmm_kernel,
        out_shape=jax.ShapeDtypeStruct((B, T, N), F32),
        grid=(B // nb, T // tt),
        in_specs=[
            pl.BlockSpec((nb, tt, D), lambda b, i: (b, i, 0)),
            pl.BlockSpec((1, D), lambda b, i: (0, 0)),
            pl.BlockSpec((nb, 1, D), lambda b, i: (b, 0, 0)),
            pl.BlockSpec((nb, 1, D), lambda b, i: (b, 0, 0)),
            pl.BlockSpec((D, N), lambda b, i: (0, 0)),
        ],
        out_specs=pl.BlockSpec((nb, tt, N), lambda b, i: (b, i, 0)),
        compiler_params=_cparams("arbitrary", "arbitrary"),
        name="norm_mod_proj",
    )(x, g.reshape(1, D), scale[:, None, :], shift[:, None, :], w)


def _out_kernel(x_ref, oa_ref, ob_ref, w_ref, g_ref, gate_ref, o_ref):
    nb, tt, d = x_ref.shape
    oa = oa_ref[...].reshape(nb * tt, HALF)
    ob = ob_ref[...].reshape(nb * tt, HALF)
    y = _dot(oa, w_ref[0:HALF, :]) + _dot(ob, w_ref[HALF:2 * HALF, :])
    y = _rms(y, g_ref[...]).reshape(nb, tt, d)
    o_ref[...] = x_ref[...] + gate_ref[...] * y


def _out_call(x, oa, ob, w, g, gate):
    B, T, D = x.shape
    nb, tt = _tiles(B, T)
    return pl.pallas_call(
        _out_kernel,
        out_shape=jax.ShapeDtypeStruct((B, T, D), F32),
        grid=(B // nb, T // tt),
        in_specs=[
            pl.BlockSpec((nb, tt, D), lambda b, i: (b, i, 0)),
            pl.BlockSpec((nb, tt, HALF), lambda b, i: (b, i, 0)),
            pl.BlockSpec((nb, tt, HALF), lambda b, i: (b, i, 0)),
            pl.BlockSpec((D, D), lambda b, i: (0, 0)),
            pl.BlockSpec((1, D), lambda b, i: (0, 0)),
            pl.BlockSpec((nb, 1, D), lambda b, i: (b, 0, 0)),
        ],
        out_specs=pl.BlockSpec((nb, tt, D), lambda b, i: (b, i, 0)),
        compiler_params=_cparams("arbitrary", "arbitrary"),
        name="out_proj_residual",
    )(x, oa, ob, w, g.reshape(1, D), gate[:, None, :])


def _ffn_kernel(x_ref, g1_ref, sc_ref, sh_ref, gate_ref, g2_ref, buf0_ref, wg_ref, wv_ref, wd_ref,
                cw_ref, o_ref, st_ref, carry_ref, ubuf_ref):
    nb, tt, d = x_ref.shape
    i = pl.program_id(1)
    x = x_ref[...]
    h = (_rms(x, g1_ref[...]) * (1.0 + sc_ref[...]) + sh_ref[...]).reshape(nb * tt, d).astype(BF16)

    @pl.when(i == 0)
    def _():
        for c in range(2 * N_FF_TILES):
            carry_ref[c] = buf0_ref[:, :, c * FF_TILE:(c + 1) * FF_TILE]

    def conv(u, slot, c):
        ubuf_ref[slot, :, 8:8 + tt, :] = u
        ubuf_ref[slot, :, 6:8, :] = carry_ref[c]
        cw = cw_ref[c]
        y = (cw[0:1, :] * ubuf_ref[slot, :, 6:6 + tt, :] + cw[1:2, :] * ubuf_ref[slot, :, 7:7 + tt, :]
             + cw[2:3, :] * u)
        tail = ubuf_ref[slot, :, 6 + tt:8 + tt, :]
        carry_ref[c] = tail
        st_ref[:, :, c * FF_TILE:(c + 1) * FF_TILE] = tail
        return y

    acc = None
    for g0 in range(0, N_FF_TILES, FF_GROUP):
        acts = []
        for c in range(g0, min(g0 + FF_GROUP, N_FF_TILES)):
            ug = _dot(h, wg_ref[c]).reshape(nb, tt, FF_TILE)
            uv = _dot(h, wv_ref[c]).reshape(nb, tt, FF_TILE)
            slot = 2 * (c % FF_SLOTS)
            yg = conv(ug, slot, c)
            yv = conv(uv, slot + 1, N_FF_TILES + c)
            acts.append((_silu(yg) * yv).reshape(nb * tt, FF_TILE).astype(BF16))
        a = jnp.concatenate(acts, axis=-1) if len(acts) > 1 else acts[0]
        part = _dot(a, wd_ref[g0 * FF_TILE:g0 * FF_TILE + a.shape[-1], :])
        acc = part if acc is None else acc + part
    y = _rms(acc, g2_ref[...]).reshape(nb, tt, d)
    o_ref[...] = x + gate_ref[...] * y


def _ffn_call(x, g1, scale, shift, gate, g2, buf0, wg, wv, wd, cw):
    B, T, D = x.shape
    nb, tt = _tiles(B, T)
    assert T >= FFN_CONV - 1
    const3 = lambda b, i: (0, 0, 0)
    return pl.pallas_call(
        _ffn_kernel,
        out_shape=(jax.ShapeDtypeStruct((B, T, D), F32),
                   jax.ShapeDtypeStruct((B, FFN_CONV - 1, 2 * D_FF), F32)),
        grid=(B // nb, T // tt),
        in_specs=[
            pl.BlockSpec((nb, tt, D), lambda b, i: (b, i, 0)),
            pl.BlockSpec((1, D), lambda b, i: (0, 0)),
            pl.BlockSpec((nb, 1, D), lambda b, i: (b, 0, 0)),
            pl.BlockSpec((nb, 1, D), lambda b, i: (b, 0, 0)),
            pl.BlockSpec((nb, 1, D), lambda b, i: (b, 0, 0)),
            pl.BlockSpec((1, D), lambda b, i: (0, 0)),
            pl.BlockSpec((nb, FFN_CONV - 1, 2 * D_FF), lambda b, i: (b, 0, 0)),
            pl.BlockSpec((N_FF_TILES, D, FF_TILE), const3, pipeline_mode=pl.Buffered(1)),
            pl.BlockSpec((N_FF_TILES, D, FF_TILE), const3, pipeline_mode=pl.Buffered(1)),
            pl.BlockSpec((D_FF, D), lambda b, i: (0, 0), pipeline_mode=pl.Buffered(1)),
            pl.BlockSpec((2 * N_FF_TILES, FFN_CONV, FF_TILE), const3),
        ],
        out_specs=(pl.BlockSpec((nb, tt, D), lambda b, i: (b, i, 0)),
                   pl.BlockSpec((nb, FFN_CONV - 1, 2 * D_FF), lambda b, i: (b, 0, 0))),
        scratch_shapes=[
            pltpu.VMEM((2 * N_FF_TILES, nb, FFN_CONV - 1, FF_TILE), F32),
            pltpu.VMEM((2 * FF_SLOTS, nb, 8 + tt, FF_TILE), F32),
        ],
        compiler_params=_cparams("arbitrary", "arbitrary"),
        name="conv_ffn",
    )(x, g1.reshape(1, D), scale[:, None, :], shift[:, None, :], gate[:, None, :], g2.reshape(1, D),
      buf0, wg, wv, wd, cw)


def _hgrn_kernel(hq_ref, hf_ref, hi_ref, hz_ref, lb_ref, ng_ref, s0_ref, o_ref, st_ref):
    L = hq_ref.shape[1]
    i = pl.program_id(1)

    @pl.when(i == 0)
    def _():
        st_ref[...] = s0_ref[...]

    row = lax.broadcasted_iota(jnp.int32, (L, HEAD_DIM), 0)
    r2 = lax.broadcasted_iota(jnp.int32, (L, L), 0)
    c2 = lax.broadcasted_iota(jnp.int32, (L, L), 1)
    for hd in range(HEADS):
        sl = slice(hd * HEAD_DIM, (hd + 1) * HEAD_DIM)
        z = hf_ref[0, :, sl]
        lb = lb_ref[:, sl]
        g = jnp.log(lb + (1.0 - lb) * jax.nn.sigmoid(z))
        k = (1.0 - lb) * jax.nn.sigmoid(-z)
        q = _silu(hq_ref[0, :, sl])
        v = hi_ref[0, :, sl].astype(BF16)
        b = _cumsum_rows(g, row)
        att = jnp.where(r2 == c2, _dot_nt(q.astype(BF16), k.astype(BF16)), 0.0)
        h = L // 2
        while h >= 1:
            r = _block_row_bcast(b, row, h)
            upper = (row & (2 * h - 1)) >= h
            qt = jnp.where(upper, q * jnp.exp(b - r), 0.0).astype(BF16)
            kt = jnp.where(upper, 0.0, k * jnp.exp(r - b)).astype(BF16)
            a = _dot_nt(qt, kt)
            sh = _log2(2 * h)
            att = att + jnp.where((r2 >> sh) == (c2 >> sh), a, 0.0)
            h //= 2
        st = st_ref[0, hd]
        o = _dot(att.astype(BF16), v) + _dot_nt((q * jnp.exp(b)).astype(BF16), st.astype(BF16))
        b_last = b[L - 1:L, :]
        kd = (k * jnp.exp(b_last - b)).astype(BF16)
        st_ref[0, hd] = jnp.exp(b_last) * st + _dot_tn(v, kd)
        o = _rms(o, ng_ref[...]) * _silu(hz_ref[0, :, sl])
        o_ref[0, :, sl] = o.astype(BF16)


def _hgrn_call(proj, lb, ng, s0t):
    B, T, _ = proj.shape
    L = 256 if T % 256 == 0 else T
    assert T % L == 0 and L & (L - 1) == 0 and L >= 8
    col = lambda c: pl.BlockSpec((1, L, HALF), lambda b, i, c=c: (b, i, c))
    return pl.pallas_call(
        _hgrn_kernel,
        out_shape=(jax.ShapeDtypeStruct((B, T, HALF), BF16),
                   jax.ShapeDtypeStruct((B, HEADS, HEAD_DIM, HEAD_DIM), F32)),
        grid=(B, T // L),
        in_specs=[col(0), col(1), col(2), col(3),
                  pl.BlockSpec((1, HALF), lambda b, i: (0, 0)),
                  pl.BlockSpec((1, HEAD_DIM), lambda b, i: (0, 0)),
                  pl.BlockSpec((1, HEADS, HEAD_DIM, HEAD_DIM), lambda b, i: (b, 0, 0, 0))],
        out_specs=(pl.BlockSpec((1, L, HALF), lambda b, i: (b, i, 0)),
                   pl.BlockSpec((1, HEADS, HEAD_DIM, HEAD_DIM), lambda b, i: (b, 0, 0, 0))),
        compiler_params=_cparams("arbitrary", "arbitrary"),
        name="hgrn2",
    )(proj, proj, proj, proj, lb.reshape(1, HALF), ng.reshape(1, HEAD_DIM), s0t)


def _lru_kernel(lx_ref, ly_ref, buf0_ref, h0_ref, cw_ref, cb_ref, wa_ref, wx_ref, ba_ref, bx_ref,
                lam_ref, o_ref, hl_ref, bufo_ref, xp_ref):
    tt = lx_ref.shape[1]
    i = pl.program_id(1)
    npad = LRU_CONV - 1

    @pl.when(i == 0)
    def _():
        xp_ref[8 - npad:8, :] = buf0_ref[0]
        hl_ref[0] = h0_ref[0]

    x = lx_ref[0]
    xp_ref[8:8 + tt, :] = x
    xc = cb_ref[...] + cw_ref[npad:npad + 1, :] * x
    for tap in range(npad):
        xc = xc + cw_ref[tap:tap + 1, :] * xp_ref[8 - npad + tap:8 - npad + tap + tt, :]
    tail = xp_ref[8 + tt - npad:8 + tt, :]
    xp_ref[8 - npad:8, :] = tail
    bufo_ref[0] = tail

    xb = xc.astype(BF16)
    half = HALF // 2
    rpre = jnp.concatenate([_dot(xb[:, 0:half], wa_ref[0]), _dot(xb[:, half:HALF], wa_ref[1])], axis=-1)
    ipre = jnp.concatenate([_dot(xb[:, 0:half], wx_ref[0]), _dot(xb[:, half:HALF], wx_ref[1])], axis=-1)
    r = jax.nn.sigmoid(rpre + ba_ref[...])
    ig = jax.nn.sigmoid(ipre + bx_ref[...])
    log_a = -LRU_C * r * _softplus(-lam_ref[...])
    a = jnp.exp(log_a)
    u = jnp.sqrt(jnp.maximum(-jnp.tanh(log_a) * (1.0 + a * a), SQRT_FLOOR)) * ig * xc

    row = lax.broadcasted_iota(jnp.int32, (tt, HALF), 0)
    s = 1
    while s < tt:
        keep = row >= s
        a_sh = jnp.where(keep, pltpu.roll(a, s, 0), 1.0)
        u_sh = jnp.where(keep, pltpu.roll(u, s, 0), 0.0)
        u = a * u_sh + u
        a = a * a_sh
        s *= 2
    hseq = u + a * hl_ref[0]
    hl_ref[0] = hseq[tt - 1:tt, :]
    o_ref[0] = (hseq * _gelu_tanh(ly_ref[0])).astype(BF16)


def _lru_call(proj, buf0, h0, cw, cb, wa_bd, wx_bd, ba, bx, lam):
    B, T, _ = proj.shape
    tt = 256 if T % 256 == 0 else T
    assert T % tt == 0 and T >= LRU_CONV - 1 and tt % 8 == 0
    vec = pl.BlockSpec((1, HALF), lambda b, i: (0, 0))
    wspec = pl.BlockSpec((2, HALF // 2, HALF // 2), lambda b, i: (0, 0, 0))
    return pl.pallas_call(
        _lru_kernel,
        out_shape=(jax.ShapeDtypeStruct((B, T, HALF), BF16),
                   jax.ShapeDtypeStruct((B, 1, HALF), F32),
                   jax.ShapeDtypeStruct((B, LRU_CONV - 1, HALF), F32)),
        grid=(B, T // tt),
        in_specs=[pl.BlockSpec((1, tt, HALF), lambda b, i: (b, i, 4)),
                  pl.BlockSpec((1, tt, HALF), lambda b, i: (b, i, 5)),
                  pl.BlockSpec((1, LRU_CONV - 1, HALF), lambda b, i: (b, 0, 0)),
                  pl.BlockSpec((1, 1, HALF), lambda b, i: (b, 0, 0)),
                  pl.BlockSpec((LRU_CONV, HALF), lambda b, i: (0, 0)),
                  vec, wspec, wspec, vec, vec, vec],
        out_specs=(pl.BlockSpec((1, tt, HALF), lambda b, i: (b, i, 0)),
                   pl.BlockSpec((1, 1, HALF), lambda b, i: (b, 0, 0)),
                   pl.BlockSpec((1, LRU_CONV - 1, HALF), lambda b, i: (b, 0, 0))),
        scratch_shapes=[pltpu.VMEM((8 + tt, HALF), F32)],
        compiler_params=_cparams("arbitrary", "arbitrary"),
        name="rglru",
    )(proj, proj, buf0, h0[:, None, :], cw, cb.reshape(1, HALF), wa_bd, wx_bd,
      ba.reshape(1, HALF), bx.reshape(1, HALF), lam.reshape(1, HALF))


def _gdn_kernel(qkv_ref, gz_ref, sm_ref, smt_ref, buf0_ref, cw_ref, pcol_ref, prow_ref, ng_ref, s0_ref,
                o_ref, st_ref, bufo_ref, xp_ref, *, L):
    tt = qkv_ref.shape[1]
    nc = tt // L
    sh = _log2(L)
    i = pl.program_id(1)
    npad = GD_CONV - 1

    @pl.when(i == 0)
    def _():
        xp_ref[8 - npad:8, :] = buf0_ref[0]
        st_ref[...] = s0_ref[...]

    x = qkv_ref[0]
    xp_ref[8:8 + tt, :] = x
    xc = cw_ref[npad:npad + 1, :] * x
    for tap in range(npad):
        xc = xc + cw_ref[tap:tap + 1, :] * xp_ref[8 - npad + tap:8 - npad + tap + tt, :]
    tail = xp_ref[8 + tt - npad:8 + tt, :]
    xp_ref[8 - npad:8, :] = tail
    bufo_ref[0] = tail
    xc = _silu(xc)

    r2 = lax.broadcasted_iota(jnp.int32, (tt, tt), 0)
    c2 = lax.broadcasted_iota(jnp.int32, (tt, tt), 1)
    same = (r2 >> sh) == (c2 >> sh)
    incl = jnp.logical_and(same, c2 <= r2)
    strict = jnp.logical_and(same, c2 < r2)
    tri_lo = jnp.where(incl, 1.0, 0.0).astype(BF16)
    tri_up = jnp.where(jnp.logical_and(same, r2 <= c2), 1.0, 0.0).astype(BF16)
    eye = jnp.where(r2 == c2, 1.0, 0.0)

    sm = sm_ref[0]
    beta_cols = jax.nn.sigmoid(sm)
    la_cols = -jnp.exp(pcol_ref[0:1, :]) * _softplus(sm + pcol_ref[1:2, :])
    c1, c2_, c3 = _split3(la_cols)
    g_cols = _dot(tri_lo, c1) + _dot(tri_lo, c2_) + _dot(tri_lo, c3)
    la_rows = -jnp.exp(prow_ref[:, 0:1]) * _softplus(smt_ref[0] + prow_ref[:, 1:2])
    w1, w2, w3 = _split3(la_rows)
    g_rows = _dot(w1, tri_up) + _dot(w2, tri_up) + _dot(w3, tri_up)

    for hd in range(HEADS):
        sl = slice(hd * HEAD_DIM, (hd + 1) * HEAD_DIM)
        q = xc[:, hd * HEAD_DIM:(hd + 1) * HEAD_DIM]
        k = xc[:, HALF + hd * HEAD_DIM:HALF + (hd + 1) * HEAD_DIM]
        v = xc[:, 2 * HALF + hd * HEAD_DIM:2 * HALF + (hd + 1) * HEAD_DIM]
        q = q * lax.rsqrt(jnp.sum(q * q, axis=-1, keepdims=True) + EPS) * (HEAD_DIM ** -0.5)
        k = k * lax.rsqrt(jnp.sum(k * k, axis=-1, keepdims=True) + EPS)
        beta = beta_cols[:, SMALL_GB + hd:SMALL_GB + hd + 1]
        gcol = g_cols[:, SMALL_GA + hd:SMALL_GA + hd + 1]
        grow = g_rows[HEADS + hd:HEADS + hd + 1, :]
        kb = k.astype(BF16)
        dec = jnp.exp(jnp.where(incl, gcol - grow, NEG_BIG))
        m = beta * _dot_nt(kb, kb) * jnp.where(strict, dec, 0.0)
        tinv = eye - jnp.where((r2 >> 1) == (c2 >> 1), m, 0.0)
        s = 2
        while s < L:
            ssh = _log2(s)
            off = jnp.where(jnp.logical_and((r2 >> (ssh + 1)) == (c2 >> (ssh + 1)),
                                            (r2 >> ssh) != (c2 >> ssh)), m, 0.0).astype(BF16)
            tb = tinv.astype(BF16)
            tinv = tinv - _dot(_dot(tb, off).astype(BF16), tb)
            s *= 2
        eg = jnp.exp(gcol)
        rhs = jnp.concatenate([beta * v, (beta * eg) * k], axis=-1)
        sol = _dot(tinv.astype(BF16), rhs.astype(BF16))
        u_v = sol[:, 0:HEAD_DIM]
        w_k = sol[:, HEAD_DIM:2 * HEAD_DIM].astype(BF16)
        qb = q.astype(BF16)
        qk = (_dot_nt(qb, kb) * dec).astype(BF16)

        S = st_ref[0, hd]
        us, inters = [], []
        for c in range(nc):
            rs = slice(c * L, (c + 1) * L)
            Sb = S.astype(BF16)
            u = u_v[rs] - _dot(w_k[rs], Sb)
            inters.append(eg[rs] * _dot(qb[rs], Sb))
            g_last = gcol[(c + 1) * L - 1:(c + 1) * L, :]
            kd = (k[rs] * jnp.exp(g_last - gcol[rs])).astype(BF16)
            S = jnp.exp(g_last) * S + _dot_tn(kd, u.astype(BF16))
            us.append(u)
        st_ref[0, hd] = S
        u_all = jnp.concatenate(us, axis=0) if nc > 1 else us[0]
        inter = jnp.concatenate(inters, axis=0) if nc > 1 else inters[0]
        o = _dot(qk, u_all.astype(BF16)) + inter
        o = _rms(o, ng_ref[...]) * _silu(gz_ref[0, :, sl])
        o_ref[0, :, sl] = o.astype(BF16)


def _gdn_call(proj, small_t, buf0, cw, pcol, prow, ng, s0):
    B, T, _ = proj.shape
    L = CHUNK if T % CHUNK == 0 else T
    tt = 256 if T % 256 == 0 else T
    assert T % tt == 0 and tt % L == 0 and T >= GD_CONV - 1 and L >= 2
    W = 3 * HALF
    return pl.pallas_call(
        functools.partial(_gdn_kernel, L=L),
        out_shape=(jax.ShapeDtypeStruct((B, T, HALF), BF16),
                   jax.ShapeDtypeStruct((B, HEADS, HEAD_DIM, HEAD_DIM), F32),
                   jax.ShapeDtypeStruct((B, GD_CONV - 1, W), F32)),
        grid=(B, T // tt),
        in_specs=[pl.BlockSpec((1, tt, W), lambda b, i: (b, i, 0)),
                  pl.BlockSpec((1, tt, HALF), lambda b, i: (b, i, 3)),
                  pl.BlockSpec((1, tt, 128), lambda b, i: (b, i, 21)),
                  pl.BlockSpec((1, 2 * HEADS, tt), lambda b, i: (b, 0, i)),
                  pl.BlockSpec((1, GD_CONV - 1, W), lambda b, i: (b, 0, 0)),
                  pl.BlockSpec((GD_CONV, W), lambda b, i: (0, 0)),
                  pl.BlockSpec((2, 128), lambda b, i: (0, 0)),
                  pl.BlockSpec((2 * HEADS, 2), lambda b, i: (0, 0)),
                  pl.BlockSpec((1, HEAD_DIM), lambda b, i: (0, 0)),
                  pl.BlockSpec((1, HEADS, HEAD_DIM, HEAD_DIM), lambda b, i: (b, 0, 0, 0))],
        out_specs=(pl.BlockSpec((1, tt, HALF), lambda b, i: (b, i, 0)),
                   pl.BlockSpec((1, HEADS, HEAD_DIM, HEAD_DIM), lambda b, i: (b, 0, 0, 0)),
                   pl.BlockSpec((1, GD_CONV - 1, W), lambda b, i: (b, 0, 0))),
        scratch_shapes=[pltpu.VMEM((8 + tt, W), F32)],
        compiler_params=_cparams("arbitrary", "arbitrary"),
        name="gated_deltanet",
    )(proj, proj, proj, small_t, buf0, cw, pcol, prow, ng.reshape(1, HEAD_DIM), s0)


def _rope64(x, cs, sn):
    half = MLA_ROPE // 2
    swapped = jnp.concatenate([x[:, half:], x[:, :half]], axis=-1)
    return x * cs + swapped * sn


def _expand_kv(c_kv, k_r, wkvb_ref, kf_ref, v_ref):
    kv = _dot(c_kv.astype(BF16), wkvb_ref[...])
    for hd in range(HEADS):
        base = hd * 2 * HEAD_DIM
        kf_ref[0, hd] = jnp.concatenate([kv[:, base:base + MLA_NOPE], k_r], axis=-1).astype(BF16)
        v_ref[0, hd] = kv[:, base + MLA_NOPE:base + 2 * HEAD_DIM].astype(BF16)


def _mla_prep_kernel(qa_ref, kva_ref, sm_ref, cs_ref, sn_ref, qng_ref, wqb_ref, kvng_ref, wkvb_ref,
                     q_ref, ckv_ref, kr_ref, kf_ref, v_ref):
    cs = cs_ref[...]
    sn = sn_ref[...]
    qn = _rms(qa_ref[0], qng_ref[...]).astype(BF16)
    qh = _dot(qn, wqb_ref[...]) * (MLA_SCALE * math.log2(math.e))
    for hd in range(HEADS):
        nope = qh[:, hd * MLA_NOPE:(hd + 1) * MLA_NOPE]
        off = HEADS * MLA_NOPE + hd * MLA_ROPE
        rot = _rope64(qh[:, off:off + MLA_ROPE], cs, sn)
        q_ref[0, hd] = jnp.concatenate([nope, rot], axis=-1).astype(BF16)
    c_kv = _rms(kva_ref[0], kvng_ref[...])
    ckv_ref[0] = c_kv
    k_r = _rope64(sm_ref[0, :, SMALL_KR:SMALL_KR + MLA_ROPE], cs, sn)
    kr_ref[0] = k_r
    _expand_kv(c_kv, k_r, wkvb_ref, kf_ref, v_ref)


def _mla_prep_call(proj, cs, sn, qng, wqb, kvng, wkvb):
    B, T, _ = proj.shape
    tt = 512 if T % 512 == 0 else T
    c2 = lambda b, i: (0, 0)
    return pl.pallas_call(
        _mla_prep_kernel,
        out_shape=(jax.ShapeDtypeStruct((B, HEADS, T, MLA_QK), BF16),
                   jax.ShapeDtypeStruct((B, T, MLA_KV_RANK), F32),
                   jax.ShapeDtypeStruct((B, T, MLA_ROPE), F32),
                   jax.ShapeDtypeStruct((B, HEADS, T, MLA_QK), BF16),
                   jax.ShapeDtypeStruct((B, HEADS, T, HEAD_DIM), BF16)),
        grid=(B, T // tt),
        in_specs=[pl.BlockSpec((1, tt, MLA_Q_RANK), lambda b, i: (b, i, 6)),
                  pl.BlockSpec((1, tt, MLA_KV_RANK), lambda b, i: (b, i, 8)),
                  pl.BlockSpec((1, tt, 128), lambda b, i: (b, i, 21)),
                  pl.BlockSpec((tt, MLA_ROPE), lambda b, i: (i, 0)),
                  pl.BlockSpec((tt, MLA_ROPE), lambda b, i: (i, 0)),
                  pl.BlockSpec((1, MLA_Q_RANK), c2),
                  pl.BlockSpec((MLA_Q_RANK, HEADS * MLA_QK), c2),
                  pl.BlockSpec((1, MLA_KV_RANK), c2),
                  pl.BlockSpec((MLA_KV_RANK, HEADS * 2 * HEAD_DIM), c2)],
        out_specs=(pl.BlockSpec((1, HEADS, tt, MLA_QK), lambda b, i: (b, 0, i, 0)),
                   pl.BlockSpec((1, tt, MLA_KV_RANK), lambda b, i: (b, i, 0)),
                   pl.BlockSpec((1, tt, MLA_ROPE), lambda b, i: (b, i, 0)),
                   pl.BlockSpec((1, HEADS, tt, MLA_QK), lambda b, i: (b, 0, i, 0)),
                   pl.BlockSpec((1, HEADS, tt, HEAD_DIM), lambda b, i: (b, 0, i, 0))),
        compiler_params=_cparams("arbitrary", "arbitrary"),
        name="mla_prep",
    )(proj, proj, proj, cs, sn, qng.reshape(1, MLA_Q_RANK), wqb, kvng.reshape(1, MLA_KV_RANK), wkvb)


def _mla_past_kernel(lat_ref, kr_ref, wkvb_ref, kf_ref, v_ref):
    _expand_kv(lat_ref[0], kr_ref[0], wkvb_ref, kf_ref, v_ref)


def _mla_past_call(lat, kr, wkvb):
    B, P, _ = lat.shape
    tt = 512 if P % 512 == 0 else P
    return pl.pallas_call(
        _mla_past_kernel,
        out_shape=(jax.ShapeDtypeStruct((B, HEADS, P, MLA_QK), BF16),
                   jax.ShapeDtypeStruct((B, HEADS, P, HEAD_DIM), BF16)),
        grid=(B, P // tt),
        in_specs=[pl.BlockSpec((1, tt, MLA_KV_RANK), lambda b, i: (b, i, 0)),
                  pl.BlockSpec((1, tt, MLA_ROPE), lambda b, i: (b, i, 0)),
                  pl.BlockSpec((MLA_KV_RANK, HEADS * 2 * HEAD_DIM), lambda b, i: (0, 0))],
        out_specs=(pl.BlockSpec((1, HEADS, tt, MLA_QK), lambda b, i: (b, 0, i, 0)),
                   pl.BlockSpec((1, HEADS, tt, HEAD_DIM), lambda b, i: (b, 0, i, 0))),
        compiler_params=_cparams("arbitrary", "arbitrary"),
        name="mla_past_kv",
    )(lat, kr, wkvb)


def _attn_kernel(q_ref, k_ref, v_ref, o_ref, m_ref, l_ref, acc_ref, *, past_len, tq, tk, nk):
    i = pl.program_id(2)
    sh = _log2(CHUNK)
    lanes = HEAD_DIM
    m_ref[...] = jnp.full(m_ref.shape, NEG_BIG, F32)
    l_ref[...] = jnp.zeros(l_ref.shape, F32)
    acc_ref[...] = jnp.zeros(acc_ref.shape, F32)
    q = q_ref[0, 0]
    q_lo = past_len + i * tq
    n_full = jnp.minimum(nk, (((q_lo >> sh) + 1) * CHUNK) // tk)
    n_need = jnp.minimum(nk, ((((q_lo + tq - 1) >> sh) + 1) * CHUNK + tk - 1) // tk)

    def steps(j0, count, masked):
        k_los = [pl.multiple_of((j0 + u) * tk, tk) for u in range(count)]
        scores = [_dot_nt(q, k_ref[0, 0, pl.ds(k_lo, tk), :]) for k_lo in k_los]
        for k_lo, s in zip(k_los, scores):
            if masked:
                qc = (q_lo + lax.broadcasted_iota(jnp.int32, (tq, tk), 0)) >> sh
                kc = (k_lo + lax.broadcasted_iota(jnp.int32, (tq, tk), 1)) >> sh
                s = jnp.where(kc <= qc, s, NEG_BIG)
            m_prev = m_ref[...]
            m_new = jnp.maximum(m_prev, jnp.max(s, axis=-1, keepdims=True))
            alpha = jnp.exp2(m_prev - m_new)
            if tk % lanes == 0:
                p = jnp.exp2(s - jnp.tile(m_new, (1, tk // lanes)))
            else:
                p = jnp.exp2(s - m_new[:, 0:1])
            l_ref[...] = alpha * l_ref[...] + jnp.sum(p, axis=-1, keepdims=True)
            acc_ref[...] = alpha * acc_ref[...] + _dot(p.astype(BF16), v_ref[0, 0, pl.ds(k_lo, tk), :])
            m_ref[...] = m_new

    def body_group(g, carry):
        steps(g * ATTN_UNROLL, ATTN_UNROLL, False)
        return carry

    def body_full(j, carry):
        steps(j, 1, False)
        return carry

    def body_masked(j, carry):
        steps(j, 1, True)
        return carry

    n_groups = n_full // ATTN_UNROLL
    lax.fori_loop(0, n_groups, body_group, 0)
    lax.fori_loop(n_groups * ATTN_UNROLL, n_full, body_full, 0)
    lax.fori_loop(n_full, n_need, body_masked, 0)
    o_ref[0] = (acc_ref[...] / l_ref[...]).astype(BF16)


def _attn_call(q, kf, v, past_len):
    B, H, Tq, _ = q.shape
    Tk = kf.shape[2]
    tq = 512 if Tq % 512 == 0 else Tq
    tk = 512 if Tk % 512 == 0 else Tk
    nk = Tk // tk
    return pl.pallas_call(
        functools.partial(_attn_kernel, past_len=past_len, tq=tq, tk=tk, nk=nk),
        out_shape=jax.ShapeDtypeStruct((B, Tq, H * HEAD_DIM), BF16),
        grid=(B, H, Tq // tq),
        in_specs=[pl.BlockSpec((1, 1, tq, MLA_QK), lambda b, h, i: (b, h, i, 0)),
                  pl.BlockSpec((1, 1, Tk, MLA_QK), lambda b, h, i: (b, h, 0, 0)),
                  pl.BlockSpec((1, 1, Tk, HEAD_DIM), lambda b, h, i: (b, h, 0, 0))],
        out_specs=pl.BlockSpec((1, tq, HEAD_DIM), lambda b, h, i: (b, i, h)),
        scratch_shapes=[pltpu.VMEM((tq, HEAD_DIM), F32), pltpu.VMEM((tq, HEAD_DIM), F32),
                        pltpu.VMEM((tq, HEAD_DIM), F32)],
        compiler_params=_cparams("arbitrary", "arbitrary", "arbitrary"),
        name="mla_attention",
    )(q, kf, v)


def _block_diag_pairs(w):
    per = (HALF // 2) // LRU_BLOCK
    out = jnp.zeros((2, HALF // 2, HALF // 2), w.dtype)
    for p in range(2):
        for q in range(per):
            blk = w[p * per + q]
            out = out.at[p, q * LRU_BLOCK:(q + 1) * LRU_BLOCK, q * LRU_BLOCK:(q + 1) * LRU_BLOCK].set(blk)
    return out.astype(BF16)


def _prep_cd_w_in(w):
    o = 3 * HALF
    qkv, gz = w[:, :o], w[:, o:o + HALF]
    o += HALF
    gb, ga = w[:, o:o + HEADS], w[:, o + HEADS:o + 2 * HEADS]
    o += 2 * HEADS
    qa, kva = w[:, o:o + MLA_Q_RANK], w[:, o + MLA_Q_RANK:o + MLA_Q_RANK + MLA_KV_RANK]
    o += MLA_Q_RANK + MLA_KV_RANK
    kr = w[:, o:o + MLA_ROPE]
    small = jnp.zeros((w.shape[0], 128), w.dtype)
    small = small.at[:, SMALL_KR:SMALL_KR + MLA_ROPE].set(kr)
    small = small.at[:, SMALL_GB:SMALL_GB + HEADS].set(gb)
    small = small.at[:, SMALL_GA:SMALL_GA + HEADS].set(ga)
    out = jnp.concatenate([qkv, gz, kva, qa, small], axis=-1)
    assert out.shape[1] == CD_COLS
    return out.astype(BF16)


def _prep_wqb(w):
    w4 = w.reshape(MLA_Q_RANK, HEADS, MLA_QK)
    nope = w4[:, :, :MLA_NOPE].reshape(MLA_Q_RANK, HEADS * MLA_NOPE)
    rope = w4[:, :, MLA_NOPE:].reshape(MLA_Q_RANK, HEADS * MLA_ROPE)
    return jnp.concatenate([nope, rope], axis=-1).astype(BF16)


def _rope_tables(T, past_len):
    half = MLA_ROPE // 2
    freqs = jnp.exp(-math.log(ROPE_THETA) * jnp.arange(half, dtype=F32) / half)
    pos = past_len + jnp.arange(T, dtype=jnp.int32)
    ang = pos.astype(F32)[:, None] * freqs
    cos, sin = jnp.cos(ang), jnp.sin(ang)
    return jnp.concatenate([cos, cos], axis=-1), jnp.concatenate([-sin, sin], axis=-1)


def _run_group(x, mods, hg_s, lru_h, lru_buf, gd_s, gd_buf, lat_past, kr_past, ffn_buf, W):
    B, T, _ = x.shape
    n_hg, n_lru, n_lrub, n_gd, n_gdb, n_lat, n_kr, n_ffn = ([] for _ in range(8))
    for l in range(DEPTH):
        j = l // 2
        shift1, scale1, gate1, shift2, scale2, gate2 = jnp.split(mods[l], 6, axis=-1)
        g = W['norm_g'][l]
        if l % 2 == 0:
            proj = _nmm_call(x, g[0], scale1, shift1, W['ab_w_in'][j])
            o_a, s_hg_t = _hgrn_call(proj, W['lower_bounds'][j], W['hgrn_norm_g'][j],
                                     jnp.swapaxes(hg_s[j], -1, -2))
            o_b, s_lru, s_lrub = _lru_call(proj, lru_buf[j], lru_h[j], W['lru_conv_w'][j], W['lru_conv_b'][j],
                                           W['lru_wa_bd'][j], W['lru_wx_bd'][j], W['lru_b_a'][j],
                                           W['lru_b_x'][j], W['lru_lambda'][j])
            n_hg.append(jnp.swapaxes(s_hg_t, -1, -2))
            n_lru.append(s_lru[:, 0, :])
            n_lrub.append(s_lrub)
            x = _out_call(x, o_a, o_b, W['ab_w_out'][j], g[1], gate1)
        else:
            past_len = lat_past.shape[2]
            proj = _nmm_call(x, g[0], scale1, shift1, W['cd_w_in'][j])
            small_t = jnp.swapaxes(proj[:, :, CD_COLS - 128 + SMALL_GB:CD_COLS - 128 + SMALL_GB + 2 * HEADS], 1, 2)
            o_c, s_gd, s_gdb = _gdn_call(proj, small_t, gd_buf[j], W['gdn_conv_w'][j], W['gdn_pcol'][j],
                                         W['gdn_prow'][j], W['gdn_norm_g'][j], gd_s[j])
            cs, sn = _rope_tables(T, past_len)
            q, c_kv, k_r, kf, v = _mla_prep_call(proj, cs, sn, W['mla_q_norm_g'][j], W['mla_w_qb'][j],
                                                 W['mla_kv_norm_g'][j], W['mla_w_kvb'][j])
            if past_len > 0:
                kf_p, v_p = _mla_past_call(lat_past[j], kr_past[j], W['mla_w_kvb'][j])
                kf = jnp.concatenate([kf_p, kf], axis=2)
                v = jnp.concatenate([v_p, v], axis=2)
            o_d = _attn_call(q, kf, v, past_len)
            n_gd.append(s_gd)
            n_gdb.append(s_gdb)
            n_lat.append(c_kv)
            n_kr.append(k_r)
            x = _out_call(x, o_c, o_d, W['cd_w_out'][j], g[1], gate1)
        x, s_ffn = _ffn_call(x, g[2], scale2, shift2, gate2, g[3], ffn_buf[l], W['ffn_wg'][l], W['ffn_wv'][l],
                             W['ffn_wd'][l], W['ffn_cw'][l])
        n_ffn.append(s_ffn)
    return x, (jnp.stack(n_hg), jnp.stack(n_lru), jnp.stack(n_lrub), jnp.stack(n_gd), jnp.stack(n_gdb),
               jnp.stack(n_lat), jnp.stack(n_kr), jnp.stack(n_ffn))


def _prep_weights(norm_g, ab_w_in, ab_w_out, hgrn_lb_logits, hgrn_norm_g, lru_conv_w, lru_conv_b, lru_w_a, lru_b_a,
                  lru_w_x, lru_b_x, lru_lambda, cd_w_in, cd_w_out, gdn_conv_w, gdn_a_log, gdn_dt_bias, gdn_norm_g,
                  mla_q_norm_g, mla_w_qb, mla_kv_norm_g, mla_w_kvb, ffn_w_up, ffn_conv_w, ffn_w_down):
    lb_p = jax.nn.softmax(hgrn_lb_logits.astype(F32), axis=0)
    pcol = jnp.zeros((N_CD, 2, 128), F32)
    pcol = pcol.at[:, 0, SMALL_GA:SMALL_GA + HEADS].set(gdn_a_log).at[:, 1, SMALL_GA:SMALL_GA + HEADS].set(gdn_dt_bias)
    prow = jnp.zeros((N_CD, 2 * HEADS, 2), F32)
    prow = prow.at[:, HEADS:, 0].set(gdn_a_log).at[:, HEADS:, 1].set(gdn_dt_bias)
    ffn_cw = jnp.swapaxes(ffn_conv_w.reshape(DEPTH, FFN_CONV, 2 * N_FF_TILES, FF_TILE), 1, 2)
    W = dict(
        norm_g=norm_g,
        ab_w_in=ab_w_in.astype(BF16), ab_w_out=ab_w_out.astype(BF16),
        lower_bounds=jnp.cumsum(lb_p, axis=0) - lb_p[0:1],
        hgrn_norm_g=hgrn_norm_g, lru_conv_w=lru_conv_w, lru_conv_b=lru_conv_b,
        lru_wa_bd=jax.vmap(_block_diag_pairs)(lru_w_a), lru_wx_bd=jax.vmap(_block_diag_pairs)(lru_w_x),
        lru_b_a=lru_b_a, lru_b_x=lru_b_x, lru_lambda=lru_lambda,
        cd_w_in=jax.vmap(_prep_cd_w_in)(cd_w_in), cd_w_out=cd_w_out.astype(BF16),
        gdn_conv_w=gdn_conv_w, gdn_pcol=pcol, gdn_prow=prow, gdn_norm_g=gdn_norm_g,
        mla_q_norm_g=mla_q_norm_g, mla_w_qb=jax.vmap(_prep_wqb)(mla_w_qb),
        mla_kv_norm_g=mla_kv_norm_g, mla_w_kvb=mla_w_kvb.astype(BF16),
        ffn_wg=jnp.swapaxes(ffn_w_up[:, :, :D_FF].reshape(DEPTH, D_MODEL, N_FF_TILES, FF_TILE), 1, 2).astype(BF16),
        ffn_wv=jnp.swapaxes(ffn_w_up[:, :, D_FF:].reshape(DEPTH, D_MODEL, N_FF_TILES, FF_TILE), 1, 2).astype(BF16),
        ffn_wd=ffn_w_down.astype(BF16),
        ffn_cw=ffn_cw,
    )
    return W


def kernel(x_prompt, x_sample, c_prompt, c_sample, state_hgrn, state_rglru, state_rglru_conv, state_gdn, state_gdn_conv, cache_mla_latent, cache_mla_krope, state_ffn_conv, ada_w, ada_b, norm_g, ab_w_in, ab_w_out, hgrn_lb_logits, hgrn_norm_g, lru_conv_w, lru_conv_b, lru_w_a, lru_b_a, lru_w_x, lru_b_x, lru_lambda, cd_w_in, cd_w_out, gdn_conv_w, gdn_a_log, gdn_dt_bias, gdn_norm_g, mla_q_norm_g, mla_w_qb, mla_kv_norm_g, mla_w_kvb, ffn_w_up, ffn_conv_w, ffn_w_down):
    bp, bs = x_prompt.shape[0], x_sample.shape[0]
    W = _prep_weights(norm_g, ab_w_in, ab_w_out, hgrn_lb_logits, hgrn_norm_g, lru_conv_w, lru_conv_b, lru_w_a, lru_b_a,
                      lru_w_x, lru_b_x, lru_lambda, cd_w_in, cd_w_out, gdn_conv_w, gdn_a_log, gdn_dt_bias,
                      gdn_norm_g, mla_q_norm_g, mla_w_qb, mla_kv_norm_g, mla_w_kvb, ffn_w_up, ffn_conv_w,
                      ffn_w_down)
    rows = bp + bs
    rows_pad = -(-rows // 8) * 8
    c_all = jnp.concatenate([c_prompt, c_sample, jnp.zeros((rows_pad - rows, D_MODEL), F32)], axis=0)
    mods = _ada_call(c_all, ada_w, ada_b)
    dt_ = x_prompt.dtype
    y_prompt, p_states = _run_group(
        x_prompt, mods[:, :bp],
        jnp.zeros((N_AB, bp, HEADS, HEAD_DIM, HEAD_DIM), F32),
        jnp.zeros((N_AB, bp, HALF), F32),
        jnp.zeros((N_AB, bp, LRU_CONV - 1, HALF), dt_),
        jnp.zeros((N_CD, bp, HEADS, HEAD_DIM, HEAD_DIM), F32),
        jnp.zeros((N_CD, bp, GD_CONV - 1, 3 * HALF), dt_),
        jnp.zeros((N_CD, bp, 0, MLA_KV_RANK), dt_),
        jnp.zeros((N_CD, bp, 0, MLA_ROPE), dt_),
        jnp.zeros((DEPTH, bp, FFN_CONV - 1, 2 * D_FF), dt_),
        W)
    y_sample, s_states = _run_group(
        x_sample, mods[:, bp:rows], state_hgrn, state_rglru, state_rglru_conv, state_gdn, state_gdn_conv,
        cache_mla_latent, cache_mla_krope, state_ffn_conv, W)
    return (y_prompt, y_sample) + tuple(p_states) + tuple(s_states)
```

```python
import functools
import math

import jax
import jax.numpy as jnp
from jax import lax
from jax.experimental import pallas as pl
from jax.experimental.pallas import tpu as pltpu

F32 = jnp.float32
BF16 = jnp.bfloat16

D_MODEL = 1024
DEPTH = 4
CHUNK = 64
HALF = D_MODEL // 2
N_AB = (DEPTH + 1) // 2
N_CD = DEPTH // 2
HEADS = 4
HEAD_DIM = HALF // HEADS
LRU_BLOCKS = 8
LRU_BLOCK = HALF // LRU_BLOCKS
LRU_CONV = 4
LRU_C = 8.0
GD_CONV = 4
MLA_NOPE = 128
MLA_ROPE = 64
MLA_QK = MLA_NOPE + MLA_ROPE
MLA_Q_RANK = 384
MLA_KV_RANK = 256
MLA_SCALE = (MLA_NOPE + MLA_ROPE) ** -0.5
ROPE_THETA = 10000.0
D_FF = 2816
FFN_CONV = 3
FF_TILE = 256
N_FF_TILES = D_FF // FF_TILE
FF_GROUP = 4
FF_SLOTS = 4
EPS = 1e-6
NEG_BIG = -1e30
SQRT_FLOOR = 1e-12
CD_COLS = 2816
SMALL_KR = 0
SMALL_GB = 64
SMALL_GA = 68

ATTN_UNROLL = 2

VMEM_LIMIT_BYTES =56 * 1024 * 1024


def _cparams(*sem):
    return pltpu.CompilerParams(dimension_semantics=sem, vmem_limit_bytes=VMEM_LIMIT_BYTES)


def _dot(a, b):
    return jnp.dot(a, b, preferred_element_type=F32)


def _dot_nt(a, b):
    return lax.dot_general(a, b, (((1,), (1,)), ((), ())), preferred_element_type=F32)


def _dot_tn(a, b):
    return lax.dot_general(a, b, (((0,), (0,)), ((), ())), preferred_element_type=F32)


def _rms(x, g):
    return x * lax.rsqrt(jnp.mean(x * x, axis=-1, keepdims=True) + EPS) * g


def _silu(x):
    return x * jax.nn.sigmoid(x)


def _softplus(x):
    return jnp.maximum(x, 0.0) + jnp.log1p(jnp.exp(-jnp.abs(x)))


def _gelu_tanh(x):
    return 0.5 * x * (1.0 + jnp.tanh(math.sqrt(2.0 / math.pi) * (x + 0.044715 * (x * x * x))))


def _split3(x):
    x1 = x.astype(BF16)
    r1 = x - x1.astype(F32)
    x2 = r1.astype(BF16)
    x3 = (r1 - x2.astype(F32)).astype(BF16)
    return x1, x2, x3


def _log2(n):
    assert n & (n - 1) == 0
    return n.bit_length() - 1


def _cumsum_rows(x, row):
    s = 1
    while s < x.shape[0]:
        x = x + jnp.where(row >= s, pltpu.roll(x, s, 0), 0.0)
        s *= 2
    return x


def _block_row_bcast(b, row, h):
    L, n = b.shape
    blk = 2 * h
    if blk >= 8:
        b3 = b.reshape(L // blk, blk, n)
        return jnp.broadcast_to(b3[:, h - 1:h, :], (L // blk, blk, n)).reshape(L, n)
    pos = row & (blk - 1)
    x0 = jnp.where(pos == h - 1, b, 0.0)
    out = x0
    for j in range(1, h + 1):
        out = out + pltpu.roll(x0, j, 0)
    for j in range(1, h):
        out = out + pltpu.roll(x0, L - j, 0)
    return out


def _ada_kernel(c_ref, w_ref, b_ref, o_ref):
    c = _silu(c_ref[...]).astype(BF16)
    o_ref[0] = _dot(c, w_ref[0].astype(BF16)) + b_ref[0]


def _ada_call(c_all, ada_w, ada_b):
    rows = c_all.shape[0]
    tn = 2048
    return pl.pallas_call(
        _ada_kernel,
        out_shape=jax.ShapeDtypeStruct((DEPTH, rows, 6 * D_MODEL), F32),
        grid=(DEPTH, 6 * D_MODEL // tn),
        in_specs=[
            pl.BlockSpec((rows, D_MODEL), lambda l, j: (0, 0)),
            pl.BlockSpec((1, D_MODEL, tn), lambda l, j: (l, 0, j)),
            pl.BlockSpec((1, 1, tn), lambda l, j: (l, 0, j)),
        ],
        out_specs=pl.BlockSpec((1, rows, tn), lambda l, j: (l, 0, j)),
        compiler_params=_cparams("arbitrary", "arbitrary"),
        name="ada_mod",
    )(c_all, ada_w, ada_b.reshape(DEPTH, 1, 6 * D_MODEL))


def _nmm_kernel(x_ref, g_ref, sc_ref, sh_ref, w_ref, o_ref):
    nb, tt, d = x_ref.shape
    h = _rms(x_ref[...], g_ref[...]) * (1.0 + sc_ref[...]) + sh_ref[...]
    y = _dot(h.reshape(nb * tt, d).astype(BF16), w_ref[...])
    o_ref[...] = y.reshape(nb, tt, y.shape[-1])


def _tiles(B, T):
    if T >= 512:
        return 1, 512
    assert B * T <= 512
    return B, T


def _nmm_call(x, g, scale, shift, w):
    B, T, D = x.shape
    N = w.shape[1]
    nb, tt = _tiles(B, T)
    return pl.pallas_call(
        _nmm_kernel,
        out_shape=jax.ShapeDtypeStruct((B, T, N), F32),
        grid=(B // nb, T // tt),
        in_specs=[
            pl.BlockSpec((nb, tt, D), lambda b, i: (b, i, 0)),
            pl.BlockSpec((1, D), lambda b, i: (0, 0)),
            pl.BlockSpec((nb, 1, D), lambda b, i: (b, 0, 0)),
            pl.BlockSpec((nb, 1, D), lambda b, i: (b, 0, 0)),
            pl.BlockSpec((D, N), lambda b, i: (0, 0)),
        ],
        out_specs=pl.BlockSpec((nb, tt, N), lambda b, i: (b, i, 0)),
        compiler_params=_cparams("arbitrary", "arbitrary"),
        name="norm_mod_proj",
    )(x, g.reshape(1, D), scale[:, None, :], shift[:, None, :], w)


def _out_kernel(x_ref, oa_ref, ob_ref, w_ref, g_ref, gate_ref, o_ref):
    nb, tt, d = x_ref.shape
    oa = oa_ref[...].reshape(nb * tt, HALF)
    ob = ob_ref[...].reshape(nb * tt, HALF)
    y = _dot(oa, w_ref[0:HALF, :]) + _dot(ob, w_ref[HALF:2 * HALF, :])
    y = _rms(y, g_ref[...]).reshape(nb, tt, d)
    o_ref[...] = x_ref[...] + gate_ref[...] * y


def _out_call(x, oa, ob, w, g, gate):
    B, T, D = x.shape
    nb, tt = _tiles(B, T)
    return pl.pallas_call(
        _out_kernel,
        out_shape=jax.ShapeDtypeStruct((B, T, D), F32),
        grid=(B // nb, T // tt),
        in_specs=[
            pl.BlockSpec((nb, tt, D), lambda b, i: (b, i, 0)),
            pl.BlockSpec((nb, tt, HALF), lambda b, i: (b, i, 0)),
            pl.BlockSpec((nb, tt, HALF), lambda b, i: (b, i, 0)),
            pl.BlockSpec((D, D), lambda b, i: (0, 0)),
            pl.BlockSpec((1, D), lambda b, i: (0, 0)),
            pl.BlockSpec((nb, 1, D), lambda b, i: (b, 0, 0)),
        ],
        out_specs=pl.BlockSpec((nb, tt, D), lambda b, i: (b, i, 0)),
        compiler_params=_cparams("arbitrary", "arbitrary"),
        name="out_proj_residual",
    )(x, oa, ob, w, g.reshape(1, D), gate[:, None, :])


def _ffn_kernel(x_ref, g1_ref, sc_ref, sh_ref, gate_ref, g2_ref, buf0_ref, wg_ref, wv_ref, wd_ref,
                cw_ref, o_ref, st_ref, carry_ref, ubuf_ref):
    nb, tt, d = x_ref.shape
    i = pl.program_id(1)
    x = x_ref[...]
    h = (_rms(x, g1_ref[...]) * (1.0 + sc_ref[...]) + sh_ref[...]).reshape(nb * tt, d).astype(BF16)

    @pl.when(i == 0)
    def _():
        for c in range(2 * N_FF_TILES):
            carry_ref[c] = buf0_ref[:, :, c * FF_TILE:(c + 1) * FF_TILE]

    def conv(u, slot, c):
        ubuf_ref[slot, :, 8:8 + tt, :] = u
        ubuf_ref[slot, :, 6:8, :] = carry_ref[c]
        cw = cw_ref[c]
        y = (cw[0:1, :] * ubuf_ref[slot, :, 6:6 + tt, :] + cw[1:2, :] * ubuf_ref[slot, :, 7:7 + tt, :]
             + cw[2:3, :] * u)
        tail = ubuf_ref[slot, :, 6 + tt:8 + tt, :]
        carry_ref[c] = tail
        st_ref[:, :, c * FF_TILE:(c + 1) * FF_TILE] = tail
        return y

    def up_proj(c):
        return (_dot(h, wg_ref[c]).reshape(nb, tt, FF_TILE), _dot(h, wv_ref[c]).reshape(nb, tt, FF_TILE))

    acc = None
    nxt = up_proj(0)
    for g0 in range(0, N_FF_TILES, FF_GROUP):
        acts = []
        for c in range(g0, min(g0 + FF_GROUP, N_FF_TILES)):
            ug, uv = nxt
            if c + 1 < N_FF_TILES:
                nxt = up_proj(c + 1)
            slot = 2 * (c % FF_SLOTS)
            yg = conv(ug, slot, c)
            yv = conv(uv, slot + 1, N_FF_TILES + c)
            acts.append((_silu(yg) * yv).reshape(nb * tt, FF_TILE).astype(BF16))
        a = jnp.concatenate(acts, axis=-1) if len(acts) > 1 else acts[0]
        part = _dot(a, wd_ref[g0 * FF_TILE:g0 * FF_TILE + a.shape[-1], :])
        acc = part if acc is None else acc + part
    y = _rms(acc, g2_ref[...]).reshape(nb, tt, d)
    o_ref[...] = x + gate_ref[...] * y


def _ffn_call(x, g1, scale, shift, gate, g2, buf0, wg, wv, wd, cw):
    B, T, D = x.shape
    nb, tt = _tiles(B, T)
    assert T >= FFN_CONV - 1
    const3 = lambda b, i: (0, 0, 0)
    return pl.pallas_call(
        _ffn_kernel,
        out_shape=(jax.ShapeDtypeStruct((B, T, D), F32),
                   jax.ShapeDtypeStruct((B, FFN_CONV - 1, 2 * D_FF), F32)),
        grid=(B // nb, T // tt),
        in_specs=[
            pl.BlockSpec((nb, tt, D), lambda b, i: (b, i, 0)),
            pl.BlockSpec((1, D), lambda b, i: (0, 0)),
            pl.BlockSpec((nb, 1, D), lambda b, i: (b, 0, 0)),
            pl.BlockSpec((nb, 1, D), lambda b, i: (b, 0, 0)),
            pl.BlockSpec((nb, 1, D), lambda b, i: (b, 0, 0)),
            pl.BlockSpec((1, D), lambda b, i: (0, 0)),
            pl.BlockSpec((nb, FFN_CONV - 1, 2 * D_FF), lambda b, i: (b, 0, 0)),
            pl.BlockSpec((N_FF_TILES, D, FF_TILE), const3, pipeline_mode=pl.Buffered(1)),
            pl.BlockSpec((N_FF_TILES, D, FF_TILE), const3, pipeline_mode=pl.Buffered(1)),
            pl.BlockSpec((D_FF, D), lambda b, i: (0, 0), pipeline_mode=pl.Buffered(1)),
            pl.BlockSpec((2 * N_FF_TILES, FFN_CONV, FF_TILE), const3),
        ],
        out_specs=(pl.BlockSpec((nb, tt, D), lambda b, i: (b, i, 0)),
                   pl.BlockSpec((nb, FFN_CONV - 1, 2 * D_FF), lambda b, i: (b, 0, 0))),
        scratch_shapes=[
            pltpu.VMEM((2 * N_FF_TILES, nb, FFN_CONV - 1, FF_TILE), F32),
            pltpu.VMEM((2 * FF_SLOTS, nb, 8 + tt, FF_TILE), F32),
        ],
        compiler_params=_cparams("arbitrary", "arbitrary"),
        name="conv_ffn",
    )(x, g1.reshape(1, D), scale[:, None, :], shift[:, None, :], gate[:, None, :], g2.reshape(1, D),
      buf0, wg, wv, wd, cw)


def _hgrn_kernel(hq_ref, hf_ref, hi_ref, hz_ref, lb_ref, ng_ref, s0_ref, o_ref, st_ref):
    L = hq_ref.shape[1]
    i = pl.program_id(1)

    @pl.when(i == 0)
    def _():
        st_ref[...] = s0_ref[...]

    row = lax.broadcasted_iota(jnp.int32, (L, HEAD_DIM), 0)
    r2 = lax.broadcasted_iota(jnp.int32, (L, L), 0)
    c2 = lax.broadcasted_iota(jnp.int32, (L, L), 1)
    heads = range(HEADS)
    sls = [slice(hd * HEAD_DIM, (hd + 1) * HEAD_DIM) for hd in heads]
    q, k, v, b = ([None] * HEADS for _ in range(4))
    for hd in heads:
        z = hf_ref[0, :, sls[hd]]
        lb = lb_ref[:, sls[hd]]
        g = jnp.log(lb + (1.0 - lb) * jax.nn.sigmoid(z))
        k[hd] = (1.0 - lb) * jax.nn.sigmoid(-z)
        q[hd] = _silu(hq_ref[0, :, sls[hd]])
        v[hd] = hi_ref[0, :, sls[hd]].astype(BF16)
        b[hd] = _cumsum_rows(g, row)
    att = [jnp.where(r2 == c2, _dot_nt(q[hd].astype(BF16), k[hd].astype(BF16)), 0.0) for hd in heads]
    h = L // 2
    while h >= 1:
        upper = (row & (2 * h - 1)) >= h
        sh = _log2(2 * h)
        same_block = (r2 >> sh) == (c2 >> sh)
        for hd in heads:
            r = _block_row_bcast(b[hd], row, h)
            e = jnp.exp(jnp.where(upper, b[hd] - r, r - b[hd]))
            qt = jnp.where(upper, q[hd] * e, 0.0).astype(BF16)
            kt = jnp.where(upper, 0.0, k[hd] * e).astype(BF16)
            att[hd] = att[hd] + jnp.where(same_block, _dot_nt(qt, kt), 0.0)
        h //= 2
    for hd in heads:
        st = st_ref[0, hd]
        o = _dot(att[hd].astype(BF16), v[hd]) + _dot_nt((q[hd] * jnp.exp(b[hd])).astype(BF16), st.astype(BF16))
        b_last = b[hd][L - 1:L, :]
        kd = (k[hd] * jnp.exp(b_last - b[hd])).astype(BF16)
        st_ref[0, hd] = jnp.exp(b_last) * st + _dot_tn(v[hd], kd)
        o = _rms(o, ng_ref[...]) * _silu(hz_ref[0, :, sls[hd]])
        o_ref[0, :, sls[hd]] = o.astype(BF16)


def _hgrn_call(proj, lb, ng, s0t):
    B, T, _ = proj.shape
    L = 256 if T % 256 == 0 else T
    assert T % L == 0 and L & (L - 1) == 0 and L >= 8
    col = lambda c: pl.BlockSpec((1, L, HALF), lambda b, i, c=c: (b, i, c))
    return pl.pallas_call(
        _hgrn_kernel,
        out_shape=(jax.ShapeDtypeStruct((B, T, HALF), BF16),
                   jax.ShapeDtypeStruct((B, HEADS, HEAD_DIM, HEAD_DIM), F32)),
        grid=(B, T // L),
        in_specs=[col(0), col(1), col(2), col(3),
                  pl.BlockSpec((1, HALF), lambda b, i: (0, 0)),
                  pl.BlockSpec((1, HEAD_DIM), lambda b, i: (0, 0)),
                  pl.BlockSpec((1, HEADS, HEAD_DIM, HEAD_DIM), lambda b, i: (b, 0, 0, 0))],
        out_specs=(pl.BlockSpec((1, L, HALF), lambda b, i: (b, i, 0)),
                   pl.BlockSpec((1, HEADS, HEAD_DIM, HEAD_DIM), lambda b, i: (b, 0, 0, 0))),
        compiler_params=_cparams("arbitrary", "arbitrary"),
        name="hgrn2",
    )(proj, proj, proj, proj, lb.reshape(1, HALF), ng.reshape(1, HEAD_DIM), s0t)


def _lru_kernel(lx_ref, ly_ref, buf0_ref, h0_ref, cw_ref, cb_ref, wa_ref, wx_ref, ba_ref, bx_ref,
                lam_ref, o_ref, hl_ref, bufo_ref, xp_ref):
    tt = lx_ref.shape[1]
    i = pl.program_id(1)
    npad = LRU_CONV - 1

    @pl.when(i == 0)
    def _():
        xp_ref[8 - npad:8, :] = buf0_ref[0]
        hl_ref[0] = h0_ref[0]

    x = lx_ref[0]
    xp_ref[8:8 + tt, :] = x
    xc = cb_ref[...] + cw_ref[npad:npad + 1, :] * x
    for tap in range(npad):
        xc = xc + cw_ref[tap:tap + 1, :] * xp_ref[8 - npad + tap:8 - npad + tap + tt, :]
    tail = xp_ref[8 + tt - npad:8 + tt, :]
    xp_ref[8 - npad:8, :] = tail
    bufo_ref[0] = tail

    xb = xc.astype(BF16)
    half = HALF // 2
    rpre = jnp.concatenate([_dot(xb[:, 0:half], wa_ref[0]), _dot(xb[:, half:HALF], wa_ref[1])], axis=-1)
    ipre = jnp.concatenate([_dot(xb[:, 0:half], wx_ref[0]), _dot(xb[:, half:HALF], wx_ref[1])], axis=-1)
    r = jax.nn.sigmoid(rpre + ba_ref[...])
    ig = jax.nn.sigmoid(ipre + bx_ref[...])
    log_a = -LRU_C * r * _softplus(-lam_ref[...])
    a = jnp.exp(log_a)
    u = jnp.sqrt(jnp.maximum(-jnp.tanh(log_a) * (1.0 + a * a), SQRT_FLOOR)) * ig * xc

    row = lax.broadcasted_iota(jnp.int32, (tt, HALF), 0)
    s = 1
    while s < tt:
        keep = row >= s
        a_sh = jnp.where(keep, pltpu.roll(a, s, 0), 1.0)
        u_sh = jnp.where(keep, pltpu.roll(u, s, 0), 0.0)
        u = a * u_sh + u
        a = a * a_sh
        s *= 2
    hseq = u + a * hl_ref[0]
    hl_ref[0] = hseq[tt - 1:tt, :]
    o_ref[0] = (hseq * _gelu_tanh(ly_ref[0])).astype(BF16)


def _lru_call(proj, buf0, h0, cw, cb, wa_bd, wx_bd, ba, bx, lam):
    B, T, _ = proj.shape
    tt = 256 if T % 256 == 0 else T
    assert T % tt == 0 and T >= LRU_CONV - 1 and tt % 8 == 0
    vec = pl.BlockSpec((1, HALF), lambda b, i: (0, 0))
    wspec = pl.BlockSpec((2, HALF // 2, HALF // 2), lambda b, i: (0, 0, 0))
    return pl.pallas_call(
        _lru_kernel,
        out_shape=(jax.ShapeDtypeStruct((B, T, HALF), BF16),
                   jax.ShapeDtypeStruct((B, 1, HALF), F32),
                   jax.ShapeDtypeStruct((B, LRU_CONV - 1, HALF), F32)),
        grid=(B, T // tt),
        in_specs=[pl.BlockSpec((1, tt, HALF), lambda b, i: (b, i, 4)),
                  pl.BlockSpec((1, tt, HALF), lambda b, i: (b, i, 5)),
                  pl.BlockSpec((1, LRU_CONV - 1, HALF), lambda b, i: (b, 0, 0)),
                  pl.BlockSpec((1, 1, HALF), lambda b, i: (b, 0, 0)),
                  pl.BlockSpec((LRU_CONV, HALF), lambda b, i: (0, 0)),
                  vec, wspec, wspec, vec, vec, vec],
        out_specs=(pl.BlockSpec((1, tt, HALF), lambda b, i: (b, i, 0)),
                   pl.BlockSpec((1, 1, HALF), lambda b, i: (b, 0, 0)),
                   pl.BlockSpec((1, LRU_CONV - 1, HALF), lambda b, i: (b, 0, 0))),
        scratch_shapes=[pltpu.VMEM((8 + tt, HALF), F32)],
        compiler_params=_cparams("arbitrary", "arbitrary"),
        name="rglru",
    )(proj, proj, buf0, h0[:, None, :], cw, cb.reshape(1, HALF), wa_bd, wx_bd,
      ba.reshape(1, HALF), bx.reshape(1, HALF), lam.reshape(1, HALF))


def _gdn_kernel(qkv_ref, gz_ref, sm_ref, smt_ref, buf0_ref, cw_ref, pcol_ref, prow_ref, ng_ref, s0_ref,
                o_ref, st_ref, bufo_ref, xp_ref, *, L):
    tt = qkv_ref.shape[1]
    nc = tt // L
    sh = _log2(L)
    i = pl.program_id(1)
    npad = GD_CONV - 1

    @pl.when(i == 0)
    def _():
        xp_ref[8 - npad:8, :] = buf0_ref[0]
        st_ref[...] = s0_ref[...]

    x = qkv_ref[0]
    xp_ref[8:8 + tt, :] = x
    xc = cw_ref[npad:npad + 1, :] * x
    for tap in range(npad):
        xc = xc + cw_ref[tap:tap + 1, :] * xp_ref[8 - npad + tap:8 - npad + tap + tt, :]
    tail = xp_ref[8 + tt - npad:8 + tt, :]
    xp_ref[8 - npad:8, :] = tail
    bufo_ref[0] = tail
    xc = _silu(xc)

    r2 = lax.broadcasted_iota(jnp.int32, (tt, tt), 0)
    c2 = lax.broadcasted_iota(jnp.int32, (tt, tt), 1)
    same = (r2 >> sh) == (c2 >> sh)
    incl = jnp.logical_and(same, c2 <= r2)
    strict = jnp.logical_and(same, c2 < r2)
    tri_lo = jnp.where(incl, 1.0, 0.0).astype(BF16)
    tri_up = jnp.where(jnp.logical_and(same, r2 <= c2), 1.0, 0.0).astype(BF16)
    eye = jnp.where(r2 == c2, 1.0, 0.0)

    sm = sm_ref[0]
    beta_cols = jax.nn.sigmoid(sm)
    la_cols = -jnp.exp(pcol_ref[0:1, :]) * _softplus(sm + pcol_ref[1:2, :])
    c1, c2_, c3 = _split3(la_cols)
    g_cols = _dot(tri_lo, c1) + _dot(tri_lo, c2_) + _dot(tri_lo, c3)
    la_rows = -jnp.exp(prow_ref[:, 0:1]) * _softplus(smt_ref[0] + prow_ref[:, 1:2])
    w1, w2, w3 = _split3(la_rows)
    g_rows = _dot(w1, tri_up) + _dot(w2, tri_up) + _dot(w3, tri_up)

    heads = range(HEADS)
    q, k, v, beta, gcol, kb, dec, m = ([None] * HEADS for _ in range(8))
    for hd in heads:
        qh = xc[:, hd * HEAD_DIM:(hd + 1) * HEAD_DIM]
        kh = xc[:, HALF + hd * HEAD_DIM:HALF + (hd + 1) * HEAD_DIM]
        v[hd] = xc[:, 2 * HALF + hd * HEAD_DIM:2 * HALF + (hd + 1) * HEAD_DIM]
        q[hd] = qh * lax.rsqrt(jnp.sum(qh * qh, axis=-1, keepdims=True) + EPS) * (HEAD_DIM ** -0.5)
        k[hd] = kh * lax.rsqrt(jnp.sum(kh * kh, axis=-1, keepdims=True) + EPS)
        beta[hd] = beta_cols[:, SMALL_GB + hd:SMALL_GB + hd + 1]
        gcol[hd] = g_cols[:, SMALL_GA + hd:SMALL_GA + hd + 1]
        grow = g_rows[HEADS + hd:HEADS + hd + 1, :]
        kb[hd] = k[hd].astype(BF16)
        dec[hd] = jnp.exp(jnp.where(incl, gcol[hd] - grow, NEG_BIG))
        m[hd] = beta[hd] * _dot_nt(kb[hd], kb[hd]) * jnp.where(strict, dec[hd], 0.0)

    pair = (r2 >> 1) == (c2 >> 1)
    tinv = [eye - jnp.where(pair, m[hd], 0.0) for hd in heads]
    s = 2
    while s < L:
        ssh = _log2(s)
        lower_left = jnp.logical_and((r2 >> (ssh + 1)) == (c2 >> (ssh + 1)), (r2 >> ssh) != (c2 >> ssh))
        tb = [tinv[hd].astype(BF16) for hd in heads]
        tc = [_dot(tb[hd], jnp.where(lower_left, m[hd], 0.0).astype(BF16)).astype(BF16) for hd in heads]
        tinv = [tinv[hd] - _dot(tc[hd], tb[hd]) for hd in heads]
        s *= 2

    eg = [jnp.exp(gcol[hd]) for hd in heads]
    sol = [_dot(tinv[hd].astype(BF16),
                jnp.concatenate([beta[hd] * v[hd], (beta[hd] * eg[hd]) * k[hd]], axis=-1).astype(BF16))
           for hd in heads]
    u_v = [sol[hd][:, 0:HEAD_DIM] for hd in heads]
    w_k = [sol[hd][:, HEAD_DIM:2 * HEAD_DIM].astype(BF16) for hd in heads]
    qb = [q[hd].astype(BF16) for hd in heads]
    qk = [(_dot_nt(qb[hd], kb[hd]) * dec[hd]).astype(BF16) for hd in heads]

    S = [st_ref[0, hd] for hd in heads]
    us = [[] for _ in heads]
    inters = [[] for _ in heads]
    for c in range(nc):
        rs = slice(c * L, (c + 1) * L)
        for hd in heads:
            Sb = S[hd].astype(BF16)
            u = u_v[hd][rs] - _dot(w_k[hd][rs], Sb)
            inters[hd].append(eg[hd][rs] * _dot(qb[hd][rs], Sb))
            g_last = gcol[hd][(c + 1) * L - 1:(c + 1) * L, :]
            kd = (k[hd][rs] * jnp.exp(g_last - gcol[hd][rs])).astype(BF16)
            S[hd] = jnp.exp(g_last) * S[hd] + _dot_tn(kd, u.astype(BF16))
            us[hd].append(u)
    for hd in heads:
        sl = slice(hd * HEAD_DIM, (hd + 1) * HEAD_DIM)
        st_ref[0, hd] = S[hd]
        u_all = jnp.concatenate(us[hd], axis=0) if nc > 1 else us[hd][0]
        inter = jnp.concatenate(inters[hd], axis=0) if nc > 1 else inters[hd][0]
        o = _dot(qk[hd], u_all.astype(BF16)) + inter
        o = _rms(o, ng_ref[...]) * _silu(gz_ref[0, :, sl])
        o_ref[0, :, sl] = o.astype(BF16)


def _gdn_call(proj, small_t, buf0, cw, pcol, prow, ng, s0):
    B, T, _ = proj.shape
    L = CHUNK if T % CHUNK == 0 else T
    tt = 256 if T % 256 == 0 else T
    assert T % tt == 0 and tt % L == 0 and T >= GD_CONV - 1 and L >= 2
    W = 3 * HALF
    return pl.pallas_call(
        functools.partial(_gdn_kernel, L=L),
        out_shape=(jax.ShapeDtypeStruct((B, T, HALF), BF16),
                   jax.ShapeDtypeStruct((B, HEADS, HEAD_DIM, HEAD_DIM), F32),
                   jax.ShapeDtypeStruct((B, GD_CONV - 1, W), F32)),
        grid=(B, T // tt),
        in_specs=[pl.BlockSpec((1, tt, W), lambda b, i: (b, i, 0)),
                  pl.BlockSpec((1, tt, HALF), lambda b, i: (b, i, 3)),
                  pl.BlockSpec((1, tt, 128), lambda b, i: (b, i, 21)),
                  pl.BlockSpec((1, 2 * HEADS, tt), lambda b, i: (b, 0, i)),
                  pl.BlockSpec((1, GD_CONV - 1, W), lambda b, i: (b, 0, 0)),
                  pl.BlockSpec((GD_CONV, W), lambda b, i: (0, 0)),
                  pl.BlockSpec((2, 128), lambda b, i: (0, 0)),
                  pl.BlockSpec((2 * HEADS, 2), lambda b, i: (0, 0)),
                  pl.BlockSpec((1, HEAD_DIM), lambda b, i: (0, 0)),
                  pl.BlockSpec((1, HEADS, HEAD_DIM, HEAD_DIM), lambda b, i: (b, 0, 0, 0))],
        out_specs=(pl.BlockSpec((1, tt, HALF), lambda b, i: (b, i, 0)),
                   pl.BlockSpec((1, HEADS, HEAD_DIM, HEAD_DIM), lambda b, i: (b, 0, 0, 0)),
                   pl.BlockSpec((1, GD_CONV - 1, W), lambda b, i: (b, 0, 0))),
        scratch_shapes=[pltpu.VMEM((8 + tt, W), F32)],
        compiler_params=_cparams("arbitrary", "arbitrary"),
        name="gated_deltanet",
    )(proj, proj, proj, small_t, buf0, cw, pcol, prow, ng.reshape(1, HEAD_DIM), s0)


def _rope64(x, cs, sn):
    half = MLA_ROPE // 2
    swapped = jnp.concatenate([x[:, half:], x[:, :half]], axis=-1)
    return x * cs + swapped * sn


def _expand_kv(c_kv, k_r, wkvb_ref, kf_ref, v_ref):
    kv = _dot(c_kv.astype(BF16), wkvb_ref[...])
    for hd in range(HEADS):
        base = hd * 2 * HEAD_DIM
        kf_ref[0, hd] = jnp.concatenate([kv[:, base:base + MLA_NOPE], k_r], axis=-1).astype(BF16)
        v_ref[0, hd] = kv[:, base + MLA_NOPE:base + 2 * HEAD_DIM].astype(BF16)


def _mla_prep_kernel(qa_ref, kva_ref, sm_ref, cs_ref, sn_ref, qng_ref, wqb_ref, kvng_ref, wkvb_ref,
                     q_ref, ckv_ref, kr_ref, kf_ref, v_ref):
    cs = cs_ref[...]
    sn = sn_ref[...]
    qn = _rms(qa_ref[0], qng_ref[...]).astype(BF16)
    qh = _dot(qn, wqb_ref[...]) * (MLA_SCALE * math.log2(math.e))
    for hd in range(HEADS):
        nope = qh[:, hd * MLA_NOPE:(hd + 1) * MLA_NOPE]
        off = HEADS * MLA_NOPE + hd * MLA_ROPE
        rot = _rope64(qh[:, off:off + MLA_ROPE], cs, sn)
        q_ref[0, hd] = jnp.concatenate([nope, rot], axis=-1).astype(BF16)
    c_kv = _rms(kva_ref[0], kvng_ref[...])
    ckv_ref[0] = c_kv
    k_r = _rope64(sm_ref[0, :, SMALL_KR:SMALL_KR + MLA_ROPE], cs, sn)
    kr_ref[0] = k_r
    _expand_kv(c_kv, k_r, wkvb_ref, kf_ref, v_ref)


def _mla_prep_call(proj, cs, sn, qng, wqb, kvng, wkvb):
    B, T, _ = proj.shape
    tt = 512 if T % 512 == 0 else T
    c2 = lambda b, i: (0, 0)
    return pl.pallas_call(
        _mla_prep_kernel,
        out_shape=(jax.ShapeDtypeStruct((B, HEADS, T, MLA_QK), BF16),
                   jax.ShapeDtypeStruct((B, T, MLA_KV_RANK), F32),
                   jax.ShapeDtypeStruct((B, T, MLA_ROPE), F32),
                   jax.ShapeDtypeStruct((B, HEADS, T, MLA_QK), BF16),
                   jax.ShapeDtypeStruct((B, HEADS, T, HEAD_DIM), BF16)),
        grid=(B, T // tt),
        in_specs=[pl.BlockSpec((1, tt, MLA_Q_RANK), lambda b, i: (b, i, 6)),
                  pl.BlockSpec((1, tt, MLA_KV_RANK), lambda b, i: (b, i, 8)),
                  pl.BlockSpec((1, tt, 128), lambda b, i: (b, i, 21)),
                  pl.BlockSpec((tt, MLA_ROPE), lambda b, i: (i, 0)),
                  pl.BlockSpec((tt, MLA_ROPE), lambda b, i: (i, 0)),
                  pl.BlockSpec((1, MLA_Q_RANK), c2),
                  pl.BlockSpec((MLA_Q_RANK, HEADS * MLA_QK), c2),
                  pl.BlockSpec((1, MLA_KV_RANK), c2),
                  pl.BlockSpec((MLA_KV_RANK, HEADS * 2 * HEAD_DIM), c2)],
        out_specs=(pl.BlockSpec((1, HEADS, tt, MLA_QK), lambda b, i: (b, 0, i, 0)),
                   pl.BlockSpec((1, tt, MLA_KV_RANK), lambda b, i: (b, i, 0)),
                   pl.BlockSpec((1, tt, MLA_ROPE), lambda b, i: (b, i, 0)),
                   pl.BlockSpec((1, HEADS, tt, MLA_QK), lambda b, i: (b, 0, i, 0)),
                   pl.BlockSpec((1, HEADS, tt, HEAD_DIM), lambda b, i: (b, 0, i, 0))),
        compiler_params=_cparams("arbitrary", "arbitrary"),
        name="mla_prep",
    )(proj, proj, proj, cs, sn, qng.reshape(1, MLA_Q_RANK), wqb, kvng.reshape(1, MLA_KV_RANK), wkvb)


def _mla_past_kernel(lat_ref, kr_ref, wkvb_ref, kf_ref, v_ref):
    _expand_kv(lat_ref[0], kr_ref[0], wkvb_ref, kf_ref, v_ref)


def _mla_past_call(lat, kr, wkvb):
    B, P, _ = lat.shape
    tt = 512 if P % 512 == 0 else P
    return pl.pallas_call(
        _mla_past_kernel,
        out_shape=(jax.ShapeDtypeStruct((B, HEADS, P, MLA_QK), BF16),
                   jax.ShapeDtypeStruct((B, HEADS, P, HEAD_DIM), BF16)),
        grid=(B, P // tt),
        in_specs=[pl.BlockSpec((1, tt, MLA_KV_RANK), lambda b, i: (b, i, 0)),
                  pl.BlockSpec((1, tt, MLA_ROPE), lambda b, i: (b, i, 0)),
                  pl.BlockSpec((MLA_KV_RANK, HEADS * 2 * HEAD_DIM), lambda b, i: (0, 0))],
        out_specs=(pl.BlockSpec((1, HEADS, tt, MLA_QK), lambda b, i: (b, 0, i, 0)),
                   pl.BlockSpec((1, HEADS, tt, HEAD_DIM), lambda b, i: (b, 0, i, 0))),
        compiler_params=_cparams("arbitrary", "arbitrary"),
        name="mla_past_kv",
    )(lat, kr, wkvb)


def _attn_kernel(q_ref, k_ref, v_ref, o_ref, m_ref, l_ref, acc_ref, *, past_len, tq, tk, nk):
    i = pl.program_id(2)
    sh = _log2(CHUNK)
    lanes = HEAD_DIM
    m_ref[...] = jnp.full(m_ref.shape, NEG_BIG, F32)
    l_ref[...] = jnp.zeros(l_ref.shape, F32)
    acc_ref[...] = jnp.zeros(acc_ref.shape, F32)
    q = q_ref[0, 0]
    q_lo = past_len + i * tq
    n_full = jnp.minimum(nk, (((q_lo >> sh) + 1) * CHUNK) // tk)
    n_need = jnp.minimum(nk, ((((q_lo + tq - 1) >> sh) + 1) * CHUNK + tk - 1) // tk)

    def steps(j0, count, masked):
        k_los = [pl.multiple_of((j0 + u) * tk, tk) for u in range(count)]
        scores = [_dot_nt(q, k_ref[0, 0, pl.ds(k_lo, tk), :]) for k_lo in k_los]
        for k_lo, s in zip(k_los, scores):
            if masked:
                qc = (q_lo + lax.broadcasted_iota(jnp.int32, (tq, tk), 0)) >> sh
                kc = (k_lo + lax.broadcasted_iota(jnp.int32, (tq, tk), 1)) >> sh
                s = jnp.where(kc <= qc, s, NEG_BIG)
            m_prev = m_ref[...]
            m_new = jnp.maximum(m_prev, jnp.max(s, axis=-1, keepdims=True))
            alpha = jnp.exp2(m_prev - m_new)
            if tk % lanes == 0:
                p = jnp.exp2(s - jnp.tile(m_new, (1, tk // lanes)))
            else:
                p = jnp.exp2(s - m_new[:, 0:1])
            l_ref[...] = alpha * l_ref[...] + jnp.sum(p, axis=-1, keepdims=True)
            acc_ref[...] = alpha * acc_ref[...] + _dot(p.astype(BF16), v_ref[0, 0, pl.ds(k_lo, tk), :])
            m_ref[...] = m_new

    def body_group(g, carry):
        steps(g * ATTN_UNROLL, ATTN_UNROLL, False)
        return carry

    def body_full(j, carry):
        steps(j, 1, False)
        return carry

    def body_masked(j, carry):
        steps(j, 1, True)
        return carry

    n_groups = n_full // ATTN_UNROLL
    lax.fori_loop(0, n_groups, body_group, 0)
    lax.fori_loop(n_groups * ATTN_UNROLL, n_full, body_full, 0)
    lax.fori_loop(n_full, n_need, body_masked, 0)
    o_ref[0] = (acc_ref[...] / l_ref[...]).astype(BF16)


def _attn_call(q, kf, v, past_len):
    B, H, Tq, _ = q.shape
    Tk = kf.shape[2]
    tq = 1024 if Tq % 1024 == 0 else Tq
    tk = 512 if Tk % 512 == 0 else Tk
    nk = Tk // tk
    return pl.pallas_call(
        functools.partial(_attn_kernel, past_len=past_len, tq=tq, tk=tk, nk=nk),
        out_shape=jax.ShapeDtypeStruct((B, Tq, H * HEAD_DIM), BF16),
        grid=(B, H, Tq // tq),
        in_specs=[pl.BlockSpec((1, 1, tq, MLA_QK), lambda b, h, i: (b, h, i, 0)),
                  pl.BlockSpec((1, 1, Tk, MLA_QK), lambda b, h, i: (b, h, 0, 0)),
                  pl.BlockSpec((1, 1, Tk, HEAD_DIM), lambda b, h, i: (b, h, 0, 0))],
        out_specs=pl.BlockSpec((1, tq, HEAD_DIM), lambda b, h, i: (b, i, h)),
        scratch_shapes=[pltpu.VMEM((tq, HEAD_DIM), F32), pltpu.VMEM((tq, HEAD_DIM), F32),
                        pltpu.VMEM((tq, HEAD_DIM), F32)],
        compiler_params=_cparams("arbitrary", "arbitrary", "arbitrary"),
        name="mla_attention",
    )(q, kf, v)


def _block_diag_pairs(w):
    per = (HALF // 2) // LRU_BLOCK
    out = jnp.zeros((2, HALF // 2, HALF // 2), w.dtype)
    for p in range(2):
        for q in range(per):
            blk = w[p * per + q]
            out = out.at[p, q * LRU_BLOCK:(q + 1) * LRU_BLOCK, q * LRU_BLOCK:(q + 1) * LRU_BLOCK].set(blk)
    return out.astype(BF16)


def _prep_cd_w_in(w):
    o = 3 * HALF
    qkv, gz = w[:, :o], w[:, o:o + HALF]
    o += HALF
    gb, ga = w[:, o:o + HEADS], w[:, o + HEADS:o + 2 * HEADS]
    o += 2 * HEADS
    qa, kva = w[:, o:o + MLA_Q_RANK], w[:, o + MLA_Q_RANK:o + MLA_Q_RANK + MLA_KV_RANK]
    o += MLA_Q_RANK + MLA_KV_RANK
    kr = w[:, o:o + MLA_ROPE]
    small = jnp.zeros((w.shape[0], 128), w.dtype)
    small = small.at[:, SMALL_KR:SMALL_KR + MLA_ROPE].set(kr)
    small = small.at[:, SMALL_GB:SMALL_GB + HEADS].set(gb)
    small = small.at[:, SMALL_GA:SMALL_GA + HEADS].set(ga)
    out = jnp.concatenate([qkv, gz, kva, qa, small], axis=-1)
    assert out.shape[1] == CD_COLS
    return out.astype(BF16)


def _prep_wqb(w):
    w4 = w.reshape(MLA_Q_RANK, HEADS, MLA_QK)
    nope = w4[:, :, :MLA_NOPE].reshape(MLA_Q_RANK, HEADS * MLA_NOPE)
    rope = w4[:, :, MLA_NOPE:].reshape(MLA_Q_RANK, HEADS * MLA_ROPE)
    return jnp.concatenate([nope, rope], axis=-1).astype(BF16)


def _rope_tables(T, past_len):
    half = MLA_ROPE // 2
    freqs = jnp.exp(-math.log(ROPE_THETA) * jnp.arange(half, dtype=F32) / half)
    pos = past_len + jnp.arange(T, dtype=jnp.int32)
    ang = pos.astype(F32)[:, None] * freqs
    cos, sin = jnp.cos(ang), jnp.sin(ang)
    return jnp.concatenate([cos, cos], axis=-1), jnp.concatenate([-sin, sin], axis=-1)


def _run_group(x, mods, hg_s, lru_h, lru_buf, gd_s, gd_buf, lat_past, kr_past, ffn_buf, W):
    B, T, _ = x.shape
    n_hg, n_lru, n_lrub, n_gd, n_gdb, n_lat, n_kr, n_ffn = ([] for _ in range(8))
    for l in range(DEPTH):
        j = l // 2
        shift1, scale1, gate1, shift2, scale2, gate2 = jnp.split(mods[l], 6, axis=-1)
        g = W['norm_g'][l]
        if l % 2 == 0:
            proj = _nmm_call(x, g[0], scale1, shift1, W['ab_w_in'][j])
            o_a, s_hg_t = _hgrn_call(proj, W['lower_bounds'][j], W['hgrn_norm_g'][j],
                                     jnp.swapaxes(hg_s[j], -1, -2))
            o_b, s_lru, s_lrub = _lru_call(proj, lru_buf[j], lru_h[j], W['lru_conv_w'][j], W['lru_conv_b'][j],
                                           W['lru_wa_bd'][j], W['lru_wx_bd'][j], W['lru_b_a'][j],
                                           W['lru_b_x'][j], W['lru_lambda'][j])
            n_hg.append(jnp.swapaxes(s_hg_t, -1, -2))
            n_lru.append(s_lru[:, 0, :])
            n_lrub.append(s_lrub)
            x = _out_call(x, o_a, o_b, W['ab_w_out'][j], g[1], gate1)
        else:
            past_len = lat_past.shape[2]
            proj = _nmm_call(x, g[0], scale1, shift1, W['cd_w_in'][j])
            small_t = jnp.swapaxes(proj[:, :, CD_COLS - 128 + SMALL_GB:CD_COLS - 128 + SMALL_GB + 2 * HEADS], 1, 2)
            o_c, s_gd, s_gdb = _gdn_call(proj, small_t, gd_buf[j], W['gdn_conv_w'][j], W['gdn_pcol'][j],
                                         W['gdn_prow'][j], W['gdn_norm_g'][j], gd_s[j])
            cs, sn = _rope_tables(T, past_len)
            q, c_kv, k_r, kf, v = _mla_prep_call(proj, cs, sn, W['mla_q_norm_g'][j], W['mla_w_qb'][j],
                                                 W['mla_kv_norm_g'][j], W['mla_w_kvb'][j])
            if past_len > 0:
                kf_p, v_p = _mla_past_call(lat_past[j], kr_past[j], W['mla_w_kvb'][j])
                kf = jnp.concatenate([kf_p, kf], axis=2)
                v = jnp.concatenate([v_p, v], axis=2)
            o_d = _attn_call(q, kf, v, past_len)
            n_gd.append(s_gd)
            n_gdb.append(s_gdb)
            n_lat.append(c_kv)
            n_kr.append(k_r)
            x = _out_call(x, o_c, o_d, W['cd_w_out'][j], g[1], gate1)
        x, s_ffn = _ffn_call(x, g[2], scale2, shift2, gate2, g[3], ffn_buf[l], W['ffn_wg'][l], W['ffn_wv'][l],
                             W['ffn_wd'][l], W['ffn_cw'][l])
        n_ffn.append(s_ffn)
    return x, (jnp.stack(n_hg), jnp.stack(n_lru), jnp.stack(n_lrub), jnp.stack(n_gd), jnp.stack(n_gdb),
               jnp.stack(n_lat), jnp.stack(n_kr), jnp.stack(n_ffn))


def _prep_weights(norm_g, ab_w_in, ab_w_out, hgrn_lb_logits, hgrn_norm_g, lru_conv_w, lru_conv_b, lru_w_a, lru_b_a,
                  lru_w_x, lru_b_x, lru_lambda, cd_w_in, cd_w_out, gdn_conv_w, gdn_a_log, gdn_dt_bias, gdn_norm_g,
                  mla_q_norm_g, mla_w_qb, mla_kv_norm_g, mla_w_kvb, ffn_w_up, ffn_conv_w, ffn_w_down):
    lb_p = jax.nn.softmax(hgrn_lb_logits.astype(F32), axis=0)
    pcol = jnp.zeros((N_CD, 2, 128), F32)
    pcol = pcol.at[:, 0, SMALL_GA:SMALL_GA + HEADS].set(gdn_a_log).at[:, 1, SMALL_GA:SMALL_GA + HEADS].set(gdn_dt_bias)
    prow = jnp.zeros((N_CD, 2 * HEADS, 2), F32)
    prow = prow.at[:, HEADS:, 0].set(gdn_a_log).at[:, HEADS:, 1].set(gdn_dt_bias)
    ffn_cw = jnp.swapaxes(ffn_conv_w.reshape(DEPTH, FFN_CONV, 2 * N_FF_TILES, FF_TILE), 1, 2)
    W = dict(
        norm_g=norm_g,
        ab_w_in=ab_w_in.astype(BF16), ab_w_out=ab_w_out.astype(BF16),
        lower_bounds=jnp.cumsum(lb_p, axis=0) - lb_p[0:1],
        hgrn_norm_g=hgrn_norm_g, lru_conv_w=lru_conv_w, lru_conv_b=lru_conv_b,
        lru_wa_bd=jax.vmap(_block_diag_pairs)(lru_w_a), lru_wx_bd=jax.vmap(_block_diag_pairs)(lru_w_x),
        lru_b_a=lru_b_a, lru_b_x=lru_b_x, lru_lambda=lru_lambda,
        cd_w_in=jax.vmap(_prep_cd_w_in)(cd_w_in), cd_w_out=cd_w_out.astype(BF16),
        gdn_conv_w=gdn_conv_w, gdn_pcol=pcol, gdn_prow=prow, gdn_norm_g=gdn_norm_g,
        mla_q_norm_g=mla_q_norm_g, mla_w_qb=jax.vmap(_prep_wqb)(mla_w_qb),
        mla_kv_norm_g=mla_kv_norm_g, mla_w_kvb=mla_w_kvb.astype(BF16),
        ffn_wg=jnp.swapaxes(ffn_w_up[:, :, :D_FF].reshape(DEPTH, D_MODEL, N_FF_TILES, FF_TILE), 1, 2).astype(BF16),
        ffn_wv=jnp.swapaxes(ffn_w_up[:, :, D_FF:].reshape(DEPTH, D_MODEL, N_FF_TILES, FF_TILE), 1, 2).astype(BF16),
        ffn_wd=ffn_w_down.astype(BF16),
        ffn_cw=ffn_cw,
    )
    return W


def kernel(x_prompt, x_sample, c_prompt, c_sample, state_hgrn, state_rglru, state_rglru_conv, state_gdn, state_gdn_conv, cache_mla_latent, cache_mla_krope, state_ffn_conv, ada_w, ada_b, norm_g, ab_w_in, ab_w_out, hgrn_lb_logits, hgrn_norm_g, lru_conv_w, lru_conv_b, lru_w_a, lru_b_a, lru_w_x, lru_b_x, lru_lambda, cd_w_in, cd_w_out, gdn_conv_w, gdn_a_log, gdn_dt_bias, gdn_norm_g, mla_q_norm_g, mla_w_qb, mla_kv_norm_g, mla_w_kvb, ffn_w_up, ffn_conv_w, ffn_w_down):
    bp, bs = x_prompt.shape[0], x_sample.shape[0]
    W = _prep_weights(norm_g, ab_w_in, ab_w_out, hgrn_lb_logits, hgrn_norm_g, lru_conv_w, lru_conv_b, lru_w_a, lru_b_a,
                      lru_w_x, lru_b_x, lru_lambda, cd_w_in, cd_w_out, gdn_conv_w, gdn_a_log, gdn_dt_bias,
                      gdn_norm_g, mla_q_norm_g, mla_w_qb, mla_kv_norm_g, mla_w_kvb, ffn_w_up, ffn_conv_w,
                      ffn_w_down)
    rows = bp + bs
    rows_pad = -(-rows // 8) * 8
    c_all = jnp.concatenate([c_prompt, c_sample, jnp.zeros((rows_pad - rows, D_MODEL), F32)], axis=0)
    mods = _ada_call(c_all, ada_w, ada_b)
    dt_ = x_prompt.dtype
    y_prompt, p_states = _run_group(
        x_prompt, mods[:, :bp],
        jnp.zeros((N_AB, bp, HEADS, HEAD_DIM, HEAD_DIM), F32),
        jnp.zeros((N_AB, bp, HALF), F32),
        jnp.zeros((N_AB, bp, LRU_CONV - 1, HALF), dt_),
        jnp.zeros((N_CD, bp, HEADS, HEAD_DIM, HEAD_DIM), F32),
        jnp.zeros((N_CD, bp, GD_CONV - 1, 3 * HALF), dt_),
        jnp.zeros((N_CD, bp, 0, MLA_KV_RANK), dt_),
        jnp.zeros((N_CD, bp, 0, MLA_ROPE), dt_),
        jnp.zeros((DEPTH, bp, FFN_CONV - 1, 2 * D_FF), dt_),
        W)
    y_sample, s_states = _run_group(
        x_sample, mods[:, bp:rows], state_hgrn, state_rglru, state_rglru_conv, state_gdn, state_gdn_conv,
        cache_mla_latent, cache_mla_krope, state_ffn_conv, W)
    return (y_prompt, y_sample) + tuple(p_states) + tuple(s_states)
```

```python
import functools
import math

import jax
import jax.numpy as jnp
from jax import lax
from jax.experimental import pallas as pl
from jax.experimental.pallas import tpu as pltpu

F32 = jnp.float32
BF16 = jnp.bfloat16

D_MODEL = 1024
DEPTH = 4
CHUNK = 64
HALF = D_MODEL // 2
N_AB = (DEPTH + 1) // 2
N_CD = DEPTH // 2
HEADS = 4
HEAD_DIM = HALF // HEADS
LRU_BLOCKS = 8
LRU_BLOCK = HALF // LRU_BLOCKS
LRU_CONV = 4
LRU_C = 8.0
GD_CONV = 4
MLA_NOPE = 128
MLA_ROPE = 64
MLA_QK = MLA_NOPE + MLA_ROPE
MLA_Q_RANK = 384
MLA_KV_RANK = 256
MLA_SCALE = (MLA_NOPE + MLA_ROPE) ** -0.5
ROPE_THETA = 10000.0
D_FF = 2816
FFN_CONV = 3
FF_TILE = 256
N_FF_TILES = D_FF // FF_TILE
FF_GROUP = 4
FF_SLOTS = 4
EPS = 1e-6
NEG_BIG = -1e30
SQRT_FLOOR = 1e-12
CD_COLS = 2816
SMALL_KR = 0
SMALL_GB = 64
SMALL_GA = 68


VMEM_LIMIT_BYTES =56 * 1024 * 1024


def _cparams(*sem):
    return pltpu.CompilerParams(dimension_semantics=sem, vmem_limit_bytes=VMEM_LIMIT_BYTES)


def _dot(a, b):
    return jnp.dot(a, b, preferred_element_type=F32)


def _dot_nt(a, b):
    return lax.dot_general(a, b, (((1,), (1,)), ((), ())), preferred_element_type=F32)


def _dot_tn(a, b):
    return lax.dot_general(a, b, (((0,), (0,)), ((), ())), preferred_element_type=F32)


def _rms(x, g):
    return x * lax.rsqrt(jnp.mean(x * x, axis=-1, keepdims=True) + EPS) * g


def _silu(x):
    return x * jax.nn.sigmoid(x)


def _softplus(x):
    return jnp.maximum(x, 0.0) + jnp.log1p(jnp.exp(-jnp.abs(x)))


def _gelu_tanh(x):
    return 0.5 * x * (1.0 + jnp.tanh(math.sqrt(2.0 / math.pi) * (x + 0.044715 * (x * x * x))))


def _split3(x):
    x1 = x.astype(BF16)
    r1 = x - x1.astype(F32)
    x2 = r1.astype(BF16)
    x3 = (r1 - x2.astype(F32)).astype(BF16)
    return x1, x2, x3


def _log2(n):
    assert n & (n - 1) == 0
    return n.bit_length() - 1


def _cumsum_rows(x, row):
    s = 1
    while s < x.shape[0]:
        x = x + jnp.where(row >= s, pltpu.roll(x, s, 0), 0.0)
        s *= 2
    return x


def _block_row_bcast(b, row, h):
    L, n = b.shape
    blk = 2 * h
    if blk >= 8:
        b3 = b.reshape(L // blk, blk, n)
        return jnp.broadcast_to(b3[:, h - 1:h, :], (L // blk, blk, n)).reshape(L, n)
    pos = row & (blk - 1)
    x0 = jnp.where(pos == h - 1, b, 0.0)
    out = x0
    for j in range(1, h + 1):
        out = out + pltpu.roll(x0, j, 0)
    for j in range(1, h):
        out = out + pltpu.roll(x0, L - j, 0)
    return out


def _ada_kernel(c_ref, w_ref, b_ref, o_ref):
    c = _silu(c_ref[...]).astype(BF16)
    o_ref[0] = _dot(c, w_ref[0].astype(BF16)) + b_ref[0]


def _ada_call(c_all, ada_w, ada_b):
    rows = c_all.shape[0]
    tn = 2048
    return pl.pallas_call(
        _ada_kernel,
        out_shape=jax.ShapeDtypeStruct((DEPTH, rows, 6 * D_MODEL), F32),
        grid=(DEPTH, 6 * D_MODEL // tn),
        in_specs=[
            pl.BlockSpec((rows, D_MODEL), lambda l, j: (0, 0)),
            pl.BlockSpec((1, D_MODEL, tn), lambda l, j: (l, 0, j)),
            pl.BlockSpec((1, 1, tn), lambda l, j: (l, 0, j)),
        ],
        out_specs=pl.BlockSpec((1, rows, tn), lambda l, j: (l, 0, j)),
        compiler_params=_cparams("arbitrary", "arbitrary"),
        name="ada_mod",
    )(c_all, ada_w, ada_b.reshape(DEPTH, 1, 6 * D_MODEL))


def _nmm_kernel(x_ref, g_ref, sc_ref, sh_ref, w_ref, o_ref):
    nb, tt, d = x_ref.shape
    h = _rms(x_ref[...], g_ref[...]) * (1.0 + sc_ref[...]) + sh_ref[...]
    y = _dot(h.reshape(nb * tt, d).astype(BF16), w_ref[...])
    o_ref[...] = y.reshape(nb, tt, y.shape[-1])


def _tiles(B, T):
    if T >= 512:
        return 1, 512
    assert B * T <= 512
    return B, T


def _nmm_call(x, g, scale, shift, w):
    B, T, D = x.shape
    N = w.shape[1]
    nb, tt = _tiles(B, T)
    return pl.pallas_call(
        _nmm_kernel,
        out_shape=jax.ShapeDtypeStruct((B, T, N), F32),
        grid=(B // nb, T // tt),
        in_specs=[
            pl.BlockSpec((nb, tt, D), lambda b, i: (b, i, 0)),
            pl.BlockSpec((1, D), lambda b, i: (0, 0)),
            pl.BlockSpec((nb, 1, D), lambda b, i: (b, 0, 0)),
            pl.BlockSpec((nb, 1, D), lambda b, i: (b, 0, 0)),
            pl.BlockSpec((D, N), lambda b, i: (0, 0)),
        ],
        out_specs=pl.BlockSpec((nb, tt, N), lambda b, i: (b, i, 0)),
        compiler_params=_cparams("arbitrary", "arbitrary"),
        name="norm_mod_proj",
    )(x, g.reshape(1, D), scale[:, None, :], shift[:, None, :], w)


def _out_kernel(x_ref, oa_ref, ob_ref, w_ref, g_ref, gate_ref, o_ref):
    nb, tt, d = x_ref.shape
    oa = oa_ref[...].reshape(nb * tt, HALF)
    ob = ob_ref[...].reshape(nb * tt, HALF)
    y = _dot(oa, w_ref[0:HALF, :]) + _dot(ob, w_ref[HALF:2 * HALF, :])
    y = _rms(y, g_ref[...]).reshape(nb, tt, d)
    o_ref[...] = x_ref[...] + gate_ref[...] * y


def _out_call(x, oa, ob, w, g, gate):
    B, T, D = x.shape
    nb, tt = _tiles(B, T)
    return pl.pallas_call(
        _out_kernel,
        out_shape=jax.ShapeDtypeStruct((B, T, D), F32),
        grid=(B // nb, T // tt),
        in_specs=[
            pl.BlockSpec((nb, tt, D), lambda b, i: (b, i, 0)),
            pl.BlockSpec((nb, tt, HALF), lambda b, i: (b, i, 0)),
            pl.BlockSpec((nb, tt, HALF), lambda b, i: (b, i, 0)),
            pl.BlockSpec((D, D), lambda b, i: (0, 0)),
            pl.BlockSpec((1, D), lambda b, i: (0, 0)),
            pl.BlockSpec((nb, 1, D), lambda b, i: (b, 0, 0)),
        ],
        out_specs=pl.BlockSpec((nb, tt, D), lambda b, i: (b, i, 0)),
        compiler_params=_cparams("arbitrary", "arbitrary"),
        name="out_proj_residual",
    )(x, oa, ob, w, g.reshape(1, D), gate[:, None, :])


def _ffn_kernel(x_ref, g1_ref, sc_ref, sh_ref, gate_ref, g2_ref, buf0_ref, wg_ref, wv_ref, wd_ref,
                cw_ref, o_ref, st_ref, carry_ref, ubuf_ref):
    nb, tt, d = x_ref.shape
    i = pl.program_id(1)
    x = x_ref[...]
    h = (_rms(x, g1_ref[...]) * (1.0 + sc_ref[...]) + sh_ref[...]).reshape(nb * tt, d).astype(BF16)

    @pl.when(i == 0)
    def _():
        for c in range(2 * N_FF_TILES):
            carry_ref[c] = buf0_ref[:, :, c * FF_TILE:(c + 1) * FF_TILE]

    def conv(u, slot, c):
        ubuf_ref[slot, :, 8:8 + tt, :] = u
        ubuf_ref[slot, :, 6:8, :] = carry_ref[c]
        cw = cw_ref[c]
        y = (cw[0:1, :] * ubuf_ref[slot, :, 6:6 + tt, :] + cw[1:2, :] * ubuf_ref[slot, :, 7:7 + tt, :]
             + cw[2:3, :] * u)
        tail = ubuf_ref[slot, :, 6 + tt:8 + tt, :]
        carry_ref[c] = tail
        st_ref[:, :, c * FF_TILE:(c + 1) * FF_TILE] = tail
        return y

    def up_proj(c):
        return (_dot(h, wg_ref[c]).reshape(nb, tt, FF_TILE), _dot(h, wv_ref[c]).reshape(nb, tt, FF_TILE))

    acc = None
    nxt = up_proj(0)
    for g0 in range(0, N_FF_TILES, FF_GROUP):
        acts = []
        for c in range(g0, min(g0 + FF_GROUP, N_FF_TILES)):
            ug, uv = nxt
            if c + 1 < N_FF_TILES:
                nxt = up_proj(c + 1)
            slot = 2 * (c % FF_SLOTS)
            yg = conv(ug, slot, c)
            yv = conv(uv, slot + 1, N_FF_TILES + c)
            acts.append((_silu(yg) * yv).reshape(nb * tt, FF_TILE).astype(BF16))
        a = jnp.concatenate(acts, axis=-1) if len(acts) > 1 else acts[0]
        part = _dot(a, wd_ref[g0 * FF_TILE:g0 * FF_TILE + a.shape[-1], :])
        acc = part if acc is None else acc + part
    y = _rms(acc, g2_ref[...]).reshape(nb, tt, d)
    o_ref[...] = x + gate_ref[...] * y


def _ffn_call(x, g1, scale, shift, gate, g2, buf0, wg, wv, wd, cw):
    B, T, D = x.shape
    nb, tt = _tiles(B, T)
    assert T >= FFN_CONV - 1
    const3 = lambda b, i: (0, 0, 0)
    return pl.pallas_call(
        _ffn_kernel,
        out_shape=(jax.ShapeDtypeStruct((B, T, D), F32),
                   jax.ShapeDtypeStruct((B, FFN_CONV - 1, 2 * D_FF), F32)),
        grid=(B // nb, T // tt),
        in_specs=[
            pl.BlockSpec((nb, tt, D), lambda b, i: (b, i, 0)),
            pl.BlockSpec((1, D), lambda b, i: (0, 0)),
            pl.BlockSpec((nb, 1, D), lambda b, i: (b, 0, 0)),
            pl.BlockSpec((nb, 1, D), lambda b, i: (b, 0, 0)),
            pl.BlockSpec((nb, 1, D), lambda b, i: (b, 0, 0)),
            pl.BlockSpec((1, D), lambda b, i: (0, 0)),
            pl.BlockSpec((nb, FFN_CONV - 1, 2 * D_FF), lambda b, i: (b, 0, 0)),
            pl.BlockSpec((N_FF_TILES, D, FF_TILE), const3, pipeline_mode=pl.Buffered(1)),
            pl.BlockSpec((N_FF_TILES, D, FF_TILE), const3, pipeline_mode=pl.Buffered(1)),
            pl.BlockSpec((D_FF, D), lambda b, i: (0, 0), pipeline_mode=pl.Buffered(1)),
            pl.BlockSpec((2 * N_FF_TILES, FFN_CONV, FF_TILE), const3),
        ],
        out_specs=(pl.BlockSpec((nb, tt, D), lambda b, i: (b, i, 0)),
                   pl.BlockSpec((nb, FFN_CONV - 1, 2 * D_FF), lambda b, i: (b, 0, 0))),
        scratch_shapes=[
            pltpu.VMEM((2 * N_FF_TILES, nb, FFN_CONV - 1, FF_TILE), F32),
            pltpu.VMEM((2 * FF_SLOTS, nb, 8 + tt, FF_TILE), F32),
        ],
        compiler_params=_cparams("arbitrary", "arbitrary"),
        name="conv_ffn",
    )(x, g1.reshape(1, D), scale[:, None, :], shift[:, None, :], gate[:, None, :], g2.reshape(1, D),
      buf0, wg, wv, wd, cw)


def _hgrn_kernel(hq_ref, hf_ref, hi_ref, hz_ref, lb_ref, ng_ref, s0_ref, o_ref, st_ref):
    L = hq_ref.shape[1]
    i = pl.program_id(1)

    @pl.when(i == 0)
    def _():
        st_ref[...] = s0_ref[...]

    row = lax.broadcasted_iota(jnp.int32, (L, HEAD_DIM), 0)
    r2 = lax.broadcasted_iota(jnp.int32, (L, L), 0)
    c2 = lax.broadcasted_iota(jnp.int32, (L, L), 1)
    heads = range(HEADS)
    sls = [slice(hd * HEAD_DIM, (hd + 1) * HEAD_DIM) for hd in heads]
    q, k, v, b = ([None] * HEADS for _ in range(4))
    for hd in heads:
        z = hf_ref[0, :, sls[hd]]
        lb = lb_ref[:, sls[hd]]
        g = jnp.log(lb + (1.0 - lb) * jax.nn.sigmoid(z))
        k[hd] = (1.0 - lb) * jax.nn.sigmoid(-z)
        q[hd] = _silu(hq_ref[0, :, sls[hd]])
        v[hd] = hi_ref[0, :, sls[hd]].astype(BF16)
        b[hd] = _cumsum_rows(g, row)
    att = [jnp.where(r2 == c2, _dot_nt(q[hd].astype(BF16), k[hd].astype(BF16)), 0.0) for hd in heads]
    h = L // 2
    while h >= 1:
        upper = (row & (2 * h - 1)) >= h
        sh = _log2(2 * h)
        same_block = (r2 >> sh) == (c2 >> sh)
        for hd in heads:
            r = _block_row_bcast(b[hd], row, h)
            e = jnp.exp(jnp.where(upper, b[hd] - r, r - b[hd]))
            qt = jnp.where(upper, q[hd] * e, 0.0).astype(BF16)
            kt = jnp.where(upper, 0.0, k[hd] * e).astype(BF16)
            att[hd] = att[hd] + jnp.where(same_block, _dot_nt(qt, kt), 0.0)
        h //= 2
    for hd in heads:
        st = st_ref[0, hd]
        o = _dot(att[hd].astype(BF16), v[hd]) + _dot_nt((q[hd] * jnp.exp(b[hd])).astype(BF16), st.astype(BF16))
        b_last = b[hd][L - 1:L, :]
        kd = (k[hd] * jnp.exp(b_last - b[hd])).astype(BF16)
        st_ref[0, hd] = jnp.exp(b_last) * st + _dot_tn(v[hd], kd)
        o = _rms(o, ng_ref[...]) * _silu(hz_ref[0, :, sls[hd]])
        o_ref[0, :, sls[hd]] = o.astype(BF16)


def _hgrn_call(proj, lb, ng, s0t):
    B, T, _ = proj.shape
    L = 128 if T % 128 == 0 else T
    assert T % L == 0 and L & (L - 1) == 0 and L >= 8
    col = lambda c: pl.BlockSpec((1, L, HALF), lambda b, i, c=c: (b, i, c))
    return pl.pallas_call(
        _hgrn_kernel,
        out_shape=(jax.ShapeDtypeStruct((B, T, HALF), BF16),
                   jax.ShapeDtypeStruct((B, HEADS, HEAD_DIM, HEAD_DIM), F32)),
        grid=(B, T // L),
        in_specs=[col(0), col(1), col(2), col(3),
                  pl.BlockSpec((1, HALF), lambda b, i: (0, 0)),
                  pl.BlockSpec((1, HEAD_DIM), lambda b, i: (0, 0)),
                  pl.BlockSpec((1, HEADS, HEAD_DIM, HEAD_DIM), lambda b, i: (b, 0, 0, 0))],
        out_specs=(pl.BlockSpec((1, L, HALF), lambda b, i: (b, i, 0)),
                   pl.BlockSpec((1, HEADS, HEAD_DIM, HEAD_DIM), lambda b, i: (b, 0, 0, 0))),
        compiler_params=_cparams("arbitrary", "arbitrary"),
        name="hgrn2",
    )(proj, proj, proj, proj, lb.reshape(1, HALF), ng.reshape(1, HEAD_DIM), s0t)


def _lru_kernel(lx_ref, ly_ref, buf0_ref, h0_ref, cw_ref, cb_ref, wa_ref, wx_ref, ba_ref, bx_ref,
                lam_ref, o_ref, hl_ref, bufo_ref, xp_ref):
    tt = lx_ref.shape[1]
    i = pl.program_id(1)
    npad = LRU_CONV - 1

    @pl.when(i == 0)
    def _():
        xp_ref[8 - npad:8, :] = buf0_ref[0]
        hl_ref[0] = h0_ref[0]

    x = lx_ref[0]
    xp_ref[8:8 + tt, :] = x
    xc = cb_ref[...] + cw_ref[npad:npad + 1, :] * x
    for tap in range(npad):
        xc = xc + cw_ref[tap:tap + 1, :] * xp_ref[8 - npad + tap:8 - npad + tap + tt, :]
    tail = xp_ref[8 + tt - npad:8 + tt, :]
    xp_ref[8 - npad:8, :] = tail
    bufo_ref[0] = tail

    xb = xc.astype(BF16)
    half = HALF // 2
    rpre = jnp.concatenate([_dot(xb[:, 0:half], wa_ref[0]), _dot(xb[:, half:HALF], wa_ref[1])], axis=-1)
    ipre = jnp.concatenate([_dot(xb[:, 0:half], wx_ref[0]), _dot(xb[:, half:HALF], wx_ref[1])], axis=-1)
    r = jax.nn.sigmoid(rpre + ba_ref[...])
    ig = jax.nn.sigmoid(ipre + bx_ref[...])
    log_a = -LRU_C * r * _softplus(-lam_ref[...])
    a = jnp.exp(log_a)
    u = jnp.sqrt(jnp.maximum(-jnp.tanh(log_a) * (1.0 + a * a), SQRT_FLOOR)) * ig * xc

    row = lax.broadcasted_iota(jnp.int32, (tt, HALF), 0)
    s = 1
    while s < tt:
        keep = row >= s
        a_sh = jnp.where(keep, pltpu.roll(a, s, 0), 1.0)
        u_sh = jnp.where(keep, pltpu.roll(u, s, 0), 0.0)
        u = a * u_sh + u
        a = a * a_sh
        s *= 2
    hseq = u + a * hl_ref[0]
    hl_ref[0] = hseq[tt - 1:tt, :]
    o_ref[0] = (hseq * _gelu_tanh(ly_ref[0])).astype(BF16)


def _lru_call(proj, buf0, h0, cw, cb, wa_bd, wx_bd, ba, bx, lam):
    B, T, _ = proj.shape
    tt = 256 if T % 256 == 0 else T
    assert T % tt == 0 and T >= LRU_CONV - 1 and tt % 8 == 0
    vec = pl.BlockSpec((1, HALF), lambda b, i: (0, 0))
    wspec = pl.BlockSpec((2, HALF // 2, HALF // 2), lambda b, i: (0, 0, 0))
    return pl.pallas_call(
        _lru_kernel,
        out_shape=(jax.ShapeDtypeStruct((B, T, HALF), BF16),
                   jax.ShapeDtypeStruct((B, 1, HALF), F32),
                   jax.ShapeDtypeStruct((B, LRU_CONV - 1, HALF), F32)),
        grid=(B, T // tt),
        in_specs=[pl.BlockSpec((1, tt, HALF), lambda b, i: (b, i, 4)),
                  pl.BlockSpec((1, tt, HALF), lambda b, i: (b, i, 5)),
                  pl.BlockSpec((1, LRU_CONV - 1, HALF), lambda b, i: (b, 0, 0)),
                  pl.BlockSpec((1, 1, HALF), lambda b, i: (b, 0, 0)),
                  pl.BlockSpec((LRU_CONV, HALF), lambda b, i: (0, 0)),
                  vec, wspec, wspec, vec, vec, vec],
        out_specs=(pl.BlockSpec((1, tt, HALF), lambda b, i: (b, i, 0)),
                   pl.BlockSpec((1, 1, HALF), lambda b, i: (b, 0, 0)),
                   pl.BlockSpec((1, LRU_CONV - 1, HALF), lambda b, i: (b, 0, 0))),
        scratch_shapes=[pltpu.VMEM((8 + tt, HALF), F32)],
        compiler_params=_cparams("arbitrary", "arbitrary"),
        name="rglru",
    )(proj, proj, buf0, h0[:, None, :], cw, cb.reshape(1, HALF), wa_bd, wx_bd,
      ba.reshape(1, HALF), bx.reshape(1, HALF), lam.reshape(1, HALF))


def _gdn_kernel(qkv_ref, gz_ref, sm_ref, smt_ref, buf0_ref, cw_ref, pcol_ref, prow_ref, ng_ref, s0_ref,
                o_ref, st_ref, bufo_ref, xp_ref, *, L):
    tt = qkv_ref.shape[1]
    nc = tt // L
    sh = _log2(L)
    i = pl.program_id(1)
    npad = GD_CONV - 1

    @pl.when(i == 0)
    def _():
        xp_ref[8 - npad:8, :] = buf0_ref[0]
        st_ref[...] = s0_ref[...]

    x = qkv_ref[0]
    xp_ref[8:8 + tt, :] = x
    xc = cw_ref[npad:npad + 1, :] * x
    for tap in range(npad):
        xc = xc + cw_ref[tap:tap + 1, :] * xp_ref[8 - npad + tap:8 - npad + tap + tt, :]
    tail = xp_ref[8 + tt - npad:8 + tt, :]
    xp_ref[8 - npad:8, :] = tail
    bufo_ref[0] = tail
    xc = _silu(xc)

    r2 = lax.broadcasted_iota(jnp.int32, (tt, tt), 0)
    c2 = lax.broadcasted_iota(jnp.int32, (tt, tt), 1)
    same = (r2 >> sh) == (c2 >> sh)
    incl = jnp.logical_and(same, c2 <= r2)
    strict = jnp.logical_and(same, c2 < r2)
    tri_lo = jnp.where(incl, 1.0, 0.0).astype(BF16)
    tri_up = jnp.where(jnp.logical_and(same, r2 <= c2), 1.0, 0.0).astype(BF16)
    eye = jnp.where(r2 == c2, 1.0, 0.0)

    sm = sm_ref[0]
    beta_cols = jax.nn.sigmoid(sm)
    la_cols = -jnp.exp(pcol_ref[0:1, :]) * _softplus(sm + pcol_ref[1:2, :])
    c1, c2_, c3 = _split3(la_cols)
    g_cols = _dot(tri_lo, c1) + _dot(tri_lo, c2_) + _dot(tri_lo, c3)
    la_rows = -jnp.exp(prow_ref[:, 0:1]) * _softplus(smt_ref[0] + prow_ref[:, 1:2])
    w1, w2, w3 = _split3(la_rows)
    g_rows = _dot(w1, tri_up) + _dot(w2, tri_up) + _dot(w3, tri_up)

    heads = range(HEADS)
    q, k, v, beta, gcol, kb, dec, m = ([None] * HEADS for _ in range(8))
    for hd in heads:
        qh = xc[:, hd * HEAD_DIM:(hd + 1) * HEAD_DIM]
        kh = xc[:, HALF + hd * HEAD_DIM:HALF + (hd + 1) * HEAD_DIM]
        v[hd] = xc[:, 2 * HALF + hd * HEAD_DIM:2 * HALF + (hd + 1) * HEAD_DIM]
        q[hd] = qh * lax.rsqrt(jnp.sum(qh * qh, axis=-1, keepdims=True) + EPS) * (HEAD_DIM ** -0.5)
        k[hd] = kh * lax.rsqrt(jnp.sum(kh * kh, axis=-1, keepdims=True) + EPS)
        beta[hd] = beta_cols[:, SMALL_GB + hd:SMALL_GB + hd + 1]
        gcol[hd] = g_cols[:, SMALL_GA + hd:SMALL_GA + hd + 1]
        grow = g_rows[HEADS + hd:HEADS + hd + 1, :]
        kb[hd] = k[hd].astype(BF16)
        dec[hd] = jnp.exp(jnp.where(incl, gcol[hd] - grow, NEG_BIG))
        m[hd] = beta[hd] * _dot_nt(kb[hd], kb[hd]) * jnp.where(strict, dec[hd], 0.0)

    pair = (r2 >> 1) == (c2 >> 1)
    tinv = [eye - jnp.where(pair, m[hd], 0.0) for hd in heads]
    s = 2
    while s < L:
        ssh = _log2(s)
        lower_left = jnp.logical_and((r2 >> (ssh + 1)) == (c2 >> (ssh + 1)), (r2 >> ssh) != (c2 >> ssh))
        tb = [tinv[hd].astype(BF16) for hd in heads]
        tc = [_dot(tb[hd], jnp.where(lower_left, m[hd], 0.0).astype(BF16)).astype(BF16) for hd in heads]
        tinv = [tinv[hd] - _dot(tc[hd], tb[hd]) for hd in heads]
        s *= 2

    eg = [jnp.exp(gcol[hd]) for hd in heads]
    sol = [_dot(tinv[hd].astype(BF16),
                jnp.concatenate([beta[hd] * v[hd], (beta[hd] * eg[hd]) * k[hd]], axis=-1).astype(BF16))
           for hd in heads]
    u_v = [sol[hd][:, 0:HEAD_DIM] for hd in heads]
    w_k = [sol[hd][:, HEAD_DIM:2 * HEAD_DIM].astype(BF16) for hd in heads]
    qb = [q[hd].astype(BF16) for hd in heads]
    qk = [(_dot_nt(qb[hd], kb[hd]) * dec[hd]).astype(BF16) for hd in heads]

    S = [st_ref[0, hd] for hd in heads]
    us = [[] for _ in heads]
    inters = [[] for _ in heads]
    for c in range(nc):
        rs = slice(c * L, (c + 1) * L)
        for hd in heads:
            Sb = S[hd].astype(BF16)
            u = u_v[hd][rs] - _dot(w_k[hd][rs], Sb)
            inters[hd].append(eg[hd][rs] * _dot(qb[hd][rs], Sb))
            g_last = gcol[hd][(c + 1) * L - 1:(c + 1) * L, :]
            kd = (k[hd][rs] * jnp.exp(g_last - gcol[hd][rs])).astype(BF16)
            S[hd] = jnp.exp(g_last) * S[hd] + _dot_tn(kd, u.astype(BF16))
            us[hd].append(u)
    for hd in heads:
        sl = slice(hd * HEAD_DIM, (hd + 1) * HEAD_DIM)
        st_ref[0, hd] = S[hd]
        u_all = jnp.concatenate(us[hd], axis=0) if nc > 1 else us[hd][0]
        inter = jnp.concatenate(inters[hd], axis=0) if nc > 1 else inters[hd][0]
        o = _dot(qk[hd], u_all.astype(BF16)) + inter
        o = _rms(o, ng_ref[...]) * _silu(gz_ref[0, :, sl])
        o_ref[0, :, sl] = o.astype(BF16)


def _gdn_call(proj, small_t, buf0, cw, pcol, prow, ng, s0):
    B, T, _ = proj.shape
    L = CHUNK if T % CHUNK == 0 else T
    tt = 256 if T % 256 == 0 else T
    assert T % tt == 0 and tt % L == 0 and T >= GD_CONV - 1 and L >= 2
    W = 3 * HALF
    return pl.pallas_call(
        functools.partial(_gdn_kernel, L=L),
        out_shape=(jax.ShapeDtypeStruct((B, T, HALF), BF16),
                   jax.ShapeDtypeStruct((B, HEADS, HEAD_DIM, HEAD_DIM), F32),
                   jax.ShapeDtypeStruct((B, GD_CONV - 1, W), F32)),
        grid=(B, T // tt),
        in_specs=[pl.BlockSpec((1, tt, W), lambda b, i: (b, i, 0)),
                  pl.BlockSpec((1, tt, HALF), lambda b, i: (b, i, 3)),
                  pl.BlockSpec((1, tt, 128), lambda b, i: (b, i, 21)),
                  pl.BlockSpec((1, 2 * HEADS, tt), lambda b, i: (b, 0, i)),
                  pl.BlockSpec((1, GD_CONV - 1, W), lambda b, i: (b, 0, 0)),
                  pl.BlockSpec((GD_CONV, W), lambda b, i: (0, 0)),
                  pl.BlockSpec((2, 128), lambda b, i: (0, 0)),
                  pl.BlockSpec((2 * HEADS, 2), lambda b, i: (0, 0)),
                  pl.BlockSpec((1, HEAD_DIM), lambda b, i: (0, 0)),
                  pl.BlockSpec((1, HEADS, HEAD_DIM, HEAD_DIM), lambda b, i: (b, 0, 0, 0))],
        out_specs=(pl.BlockSpec((1, tt, HALF), lambda b, i: (b, i, 0)),
                   pl.BlockSpec((1, HEADS, HEAD_DIM, HEAD_DIM), lambda b, i: (b, 0, 0, 0)),
                   pl.BlockSpec((1, GD_CONV - 1, W), lambda b, i: (b, 0, 0))),
        scratch_shapes=[pltpu.VMEM((8 + tt, W), F32)],
        compiler_params=_cparams("arbitrary", "arbitrary"),
        name="gated_deltanet",
    )(proj, proj, proj, small_t, buf0, cw, pcol, prow, ng.reshape(1, HEAD_DIM), s0)


def _rope64(x, cs, sn):
    half = MLA_ROPE // 2
    swapped = jnp.concatenate([x[:, half:], x[:, :half]], axis=-1)
    return x * cs + swapped * sn


def _expand_kv(c_kv, k_r, wkvb_ref, kf_ref, v_ref):
    kv = _dot(c_kv.astype(BF16), wkvb_ref[...])
    for hd in range(HEADS):
        base = hd * 2 * HEAD_DIM
        kf_ref[0, hd] = jnp.concatenate([kv[:, base:base + MLA_NOPE], k_r], axis=-1).astype(BF16)
        v = kv[:, base + MLA_NOPE:base + 2 * HEAD_DIM]
        v_ref[0, hd] = jnp.concatenate([v, jnp.ones_like(v)], axis=-1).astype(BF16)


def _mla_prep_kernel(qa_ref, kva_ref, sm_ref, cs_ref, sn_ref, qng_ref, wqb_ref, kvng_ref, wkvb_ref,
                     q_ref, ckv_ref, kr_ref, kf_ref, v_ref):
    cs = cs_ref[...]
    sn = sn_ref[...]
    qn = _rms(qa_ref[0], qng_ref[...]).astype(BF16)
    qh = _dot(qn, wqb_ref[...]) * (MLA_SCALE * math.log2(math.e))
    for hd in range(HEADS):
        nope = qh[:, hd * MLA_NOPE:(hd + 1) * MLA_NOPE]
        off = HEADS * MLA_NOPE + hd * MLA_ROPE
        rot = _rope64(qh[:, off:off + MLA_ROPE], cs, sn)
        q_ref[0, hd] = jnp.concatenate([nope, rot], axis=-1).astype(BF16)
    c_kv = _rms(kva_ref[0], kvng_ref[...])
    ckv_ref[0] = c_kv
    k_r = _rope64(sm_ref[0, :, SMALL_KR:SMALL_KR + MLA_ROPE], cs, sn)
    kr_ref[0] = k_r
    _expand_kv(c_kv, k_r, wkvb_ref, kf_ref, v_ref)


def _mla_prep_call(proj, cs, sn, qng, wqb, kvng, wkvb):
    B, T, _ = proj.shape
    tt = 512 if T % 512 == 0 else T
    c2 = lambda b, i: (0, 0)
    return pl.pallas_call(
        _mla_prep_kernel,
        out_shape=(jax.ShapeDtypeStruct((B, HEADS, T, MLA_QK), BF16),
                   jax.ShapeDtypeStruct((B, T, MLA_KV_RANK), F32),
                   jax.ShapeDtypeStruct((B, T, MLA_ROPE), F32),
                   jax.ShapeDtypeStruct((B, HEADS, T, MLA_QK), BF16),
                   jax.ShapeDtypeStruct((B, HEADS, T, 2 * HEAD_DIM), BF16)),
        grid=(B, T // tt),
        in_specs=[pl.BlockSpec((1, tt, MLA_Q_RANK), lambda b, i: (b, i, 6)),
                  pl.BlockSpec((1, tt, MLA_KV_RANK), lambda b, i: (b, i, 8)),
                  pl.BlockSpec((1, tt, 128), lambda b, i: (b, i, 21)),
                  pl.BlockSpec((tt, MLA_ROPE), lambda b, i: (i, 0)),
                  pl.BlockSpec((tt, MLA_ROPE), lambda b, i: (i, 0)),
                  pl.BlockSpec((1, MLA_Q_RANK), c2),
                  pl.BlockSpec((MLA_Q_RANK, HEADS * MLA_QK), c2),
                  pl.BlockSpec((1, MLA_KV_RANK), c2),
                  pl.BlockSpec((MLA_KV_RANK, HEADS * 2 * HEAD_DIM), c2)],
        out_specs=(pl.BlockSpec((1, HEADS, tt, MLA_QK), lambda b, i: (b, 0, i, 0)),
                   pl.BlockSpec((1, tt, MLA_KV_RANK), lambda b, i: (b, i, 0)),
                   pl.BlockSpec((1, tt, MLA_ROPE), lambda b, i: (b, i, 0)),
                   pl.BlockSpec((1, HEADS, tt, MLA_QK), lambda b, i: (b, 0, i, 0)),
                   pl.BlockSpec((1, HEADS, tt, 2 * HEAD_DIM), lambda b, i: (b, 0, i, 0))),
        compiler_params=_cparams("arbitrary", "arbitrary"),
        name="mla_prep",
    )(proj, proj, proj, cs, sn, qng.reshape(1, MLA_Q_RANK), wqb, kvng.reshape(1, MLA_KV_RANK), wkvb)


def _mla_past_kernel(lat_ref, kr_ref, wkvb_ref, kf_ref, v_ref):
    _expand_kv(lat_ref[0], kr_ref[0], wkvb_ref, kf_ref, v_ref)


def _mla_past_call(lat, kr, wkvb):
    B, P, _ = lat.shape
    tt = 512 if P % 512 == 0 else P
    return pl.pallas_call(
        _mla_past_kernel,
        out_shape=(jax.ShapeDtypeStruct((B, HEADS, P, MLA_QK), BF16),
                   jax.ShapeDtypeStruct((B, HEADS, P, 2 * HEAD_DIM), BF16)),
        grid=(B, P // tt),
        in_specs=[pl.BlockSpec((1, tt, MLA_KV_RANK), lambda b, i: (b, i, 0)),
                  pl.BlockSpec((1, tt, MLA_ROPE), lambda b, i: (b, i, 0)),
                  pl.BlockSpec((MLA_KV_RANK, HEADS * 2 * HEAD_DIM), lambda b, i: (0, 0))],
        out_specs=(pl.BlockSpec((1, HEADS, tt, MLA_QK), lambda b, i: (b, 0, i, 0)),
                   pl.BlockSpec((1, HEADS, tt, 2 * HEAD_DIM), lambda b, i: (b, 0, i, 0))),
        compiler_params=_cparams("arbitrary", "arbitrary"),
        name="mla_past_kv",
    )(lat, kr, wkvb)


def _attn_kernel(q_ref, k_ref, v_ref, o_ref, m_ref, acc_ref, sa_ref, sb_ref, *, past_len, tq, tk, nk):
    i = pl.program_id(2)
    sh = _log2(CHUNK)
    lanes = HEAD_DIM
    m_ref[...] = jnp.full(m_ref.shape, NEG_BIG, F32)
    acc_ref[...] = jnp.zeros(acc_ref.shape, F32)
    q = q_ref[0, 0]
    q_lo = past_len + i * tq
    n_full = jnp.minimum(nk, (((q_lo >> sh) + 1) * CHUNK) // tk)
    n_need = jnp.minimum(nk, ((((q_lo + tq - 1) >> sh) + 1) * CHUNK + tk - 1) // tk)

    def scores(j):
        return _dot_nt(q, k_ref[0, 0, pl.ds(pl.multiple_of(j * tk, tk), tk), :])

    def masked_scores(j):
        k_lo = j * tk
        qc = (q_lo + lax.broadcasted_iota(jnp.int32, (tq, tk), 0)) >> sh
        kc = (k_lo + lax.broadcasted_iota(jnp.int32, (tq, tk), 1)) >> sh
        return jnp.where(kc <= qc, scores(j), NEG_BIG)

    def update(j, s_ref):
        k_lo = pl.multiple_of(j * tk, tk)
        m_prev = m_ref[...]
        m_new = jnp.maximum(m_prev, jnp.max(s_ref[...], axis=-1, keepdims=True))
        alpha = jnp.exp2(m_prev - m_new)
        if tk % lanes == 0:
            p = jnp.exp2(s_ref[...] - jnp.tile(m_new, (1, tk // lanes)))
        else:
            p = jnp.exp2(s_ref[...] - m_new[:, 0:1])
        pv = _dot(p.astype(BF16), v_ref[0, 0, pl.ds(k_lo, tk), :])
        acc_ref[...] = jnp.tile(alpha, (1, 2)) * acc_ref[...] + pv
        m_ref[...] = m_new

    def body_pair(g, carry):
        sb_ref[...] = scores(2 * g + 1)
        update(2 * g, sa_ref)
        sa_ref[...] = scores(2 * g + 2)
        update(2 * g + 1, sb_ref)
        return carry

    sa_ref[...] = scores(0)
    n_pairs = jnp.maximum(n_full - 1, 0) // 2
    lax.fori_loop(0, n_pairs, body_pair, 0)

    left = n_full - 2 * n_pairs
    has_masked = n_need > n_full
    j_masked = jnp.minimum(n_full, nk - 1)

    @pl.when(left == 1)
    def _():
        sb_ref[...] = masked_scores(j_masked)
        update(n_full - 1, sa_ref)

    @pl.when(jnp.logical_and(left == 1, has_masked))
    def _():
        update(n_full, sb_ref)

    @pl.when(left == 2)
    def _():
        sb_ref[...] = scores(n_full - 1)
        update(n_full - 2, sa_ref)
        sa_ref[...] = masked_scores(j_masked)
        update(n_full - 1, sb_ref)

    @pl.when(jnp.logical_and(left == 2, has_masked))
    def _():
        update(n_full, sa_ref)

    def body_masked(j, carry):
        sa_ref[...] = masked_scores(j)
        update(j, sa_ref)
        return carry

    lax.fori_loop(jnp.where(n_full > 0, n_full + 1, 0), n_need, body_masked, 0)
    o_ref[0] = (acc_ref[:, 0:HEAD_DIM] / acc_ref[:, HEAD_DIM:2 * HEAD_DIM]).astype(BF16)


def _attn_call(q, kf, v, past_len):
    B, H, Tq, _ = q.shape
    Tk = kf.shape[2]
    tq = 1024 if Tq % 1024 == 0 else Tq
    tk = 1024 if Tk % 1024 == 0 else Tk
    nk = Tk // tk
    return pl.pallas_call(
        functools.partial(_attn_kernel, past_len=past_len, tq=tq, tk=tk, nk=nk),
        out_shape=jax.ShapeDtypeStruct((B, Tq, H * HEAD_DIM), BF16),
        grid=(B, H, Tq // tq),
        in_specs=[pl.BlockSpec((1, 1, tq, MLA_QK), lambda b, h, i: (b, h, i, 0)),
                  pl.BlockSpec((1, 1, Tk, MLA_QK), lambda b, h, i: (b, h, 0, 0)),
                  pl.BlockSpec((1, 1, Tk, 2 * HEAD_DIM), lambda b, h, i: (b, h, 0, 0))],
        out_specs=pl.BlockSpec((1, tq, HEAD_DIM), lambda b, h, i: (b, i, h)),
        scratch_shapes=[pltpu.VMEM((tq, HEAD_DIM), F32), pltpu.VMEM((tq, 2 * HEAD_DIM), F32),
                        pltpu.VMEM((tq, tk), F32), pltpu.VMEM((tq, tk), F32)],
        compiler_params=_cparams("arbitrary", "arbitrary", "arbitrary"),
        name="mla_attention",
    )(q, kf, v)


def _block_diag_pairs(w):
    per = (HALF // 2) // LRU_BLOCK
    out = jnp.zeros((2, HALF // 2, HALF // 2), w.dtype)
    for p in range(2):
        for q in range(per):
            blk = w[p * per + q]
            out = out.at[p, q * LRU_BLOCK:(q + 1) * LRU_BLOCK, q * LRU_BLOCK:(q + 1) * LRU_BLOCK].set(blk)
    return out.astype(BF16)


def _prep_cd_w_in(w):
    o = 3 * HALF
    qkv, gz = w[:, :o], w[:, o:o + HALF]
    o += HALF
    gb, ga = w[:, o:o + HEADS], w[:, o + HEADS:o + 2 * HEADS]
    o += 2 * HEADS
    qa, kva = w[:, o:o + MLA_Q_RANK], w[:, o + MLA_Q_RANK:o + MLA_Q_RANK + MLA_KV_RANK]
    o += MLA_Q_RANK + MLA_KV_RANK
    kr = w[:, o:o + MLA_ROPE]
    small = jnp.zeros((w.shape[0], 128), w.dtype)
    small = small.at[:, SMALL_KR:SMALL_KR + MLA_ROPE].set(kr)
    small = small.at[:, SMALL_GB:SMALL_GB + HEADS].set(gb)
    small = small.at[:, SMALL_GA:SMALL_GA + HEADS].set(ga)
    out = jnp.concatenate([qkv, gz, kva, qa, small], axis=-1)
    assert out.shape[1] == CD_COLS
    return out.astype(BF16)


def _prep_wqb(w):
    w4 = w.reshape(MLA_Q_RANK, HEADS, MLA_QK)
    nope = w4[:, :, :MLA_NOPE].reshape(MLA_Q_RANK, HEADS * MLA_NOPE)
    rope = w4[:, :, MLA_NOPE:].reshape(MLA_Q_RANK, HEADS * MLA_ROPE)
    return jnp.concatenate([nope, rope], axis=-1).astype(BF16)


def _rope_tables(T, past_len):
    half = MLA_ROPE // 2
    freqs = jnp.exp(-math.log(ROPE_THETA) * jnp.arange(half, dtype=F32) / half)
    pos = past_len + jnp.arange(T, dtype=jnp.int32)
    ang = pos.astype(F32)[:, None] * freqs
    cos, sin = jnp.cos(ang), jnp.sin(ang)
    return jnp.concatenate([cos, cos], axis=-1), jnp.concatenate([-sin, sin], axis=-1)


def _run_group(x, mods, hg_s, lru_h, lru_buf, gd_s, gd_buf, lat_past, kr_past, ffn_buf, W):
    B, T, _ = x.shape
    n_hg, n_lru, n_lrub, n_gd, n_gdb, n_lat, n_kr, n_ffn = ([] for _ in range(8))
    for l in range(DEPTH):
        j = l // 2
        shift1, scale1, gate1, shift2, scale2, gate2 = jnp.split(mods[l], 6, axis=-1)
        g = W['norm_g'][l]
        if l % 2 == 0:
            proj = _nmm_call(x, g[0], scale1, shift1, W['ab_w_in'][j])
            o_a, s_hg_t = _hgrn_call(proj, W['lower_bounds'][j], W['hgrn_norm_g'][j],
                                     jnp.swapaxes(hg_s[j], -1, -2))
            o_b, s_lru, s_lrub = _lru_call(proj, lru_buf[j], lru_h[j], W['lru_conv_w'][j], W['lru_conv_b'][j],
                                           W['lru_wa_bd'][j], W['lru_wx_bd'][j], W['lru_b_a'][j],
                                           W['lru_b_x'][j], W['lru_lambda'][j])
            n_hg.append(jnp.swapaxes(s_hg_t, -1, -2))
            n_lru.append(s_lru[:, 0, :])
            n_lrub.append(s_lrub)
            x = _out_call(x, o_a, o_b, W['ab_w_out'][j], g[1], gate1)
        else:
            past_len = lat_past.shape[2]
            proj = _nmm_call(x, g[0], scale1, shift1, W['cd_w_in'][j])
            small_t = jnp.swapaxes(proj[:, :, CD_COLS - 128 + SMALL_GB:CD_COLS - 128 + SMALL_GB + 2 * HEADS], 1, 2)
            o_c, s_gd, s_gdb = _gdn_call(proj, small_t, gd_buf[j], W['gdn_conv_w'][j], W['gdn_pcol'][j],
                                         W['gdn_prow'][j], W['gdn_norm_g'][j], gd_s[j])
            cs, sn = _rope_tables(T, past_len)
            q, c_kv, k_r, kf, v = _mla_prep_call(proj, cs, sn, W['mla_q_norm_g'][j], W['mla_w_qb'][j],
                                                 W['mla_kv_norm_g'][j], W['mla_w_kvb'][j])
            if past_len > 0:
                kf_p, v_p = _mla_past_call(lat_past[j], kr_past[j], W['mla_w_kvb'][j])
                kf = jnp.concatenate([kf_p, kf], axis=2)
                v = jnp.concatenate([v_p, v], axis=2)
            o_d = _attn_call(q, kf, v, past_len)
            n_gd.append(s_gd)
            n_gdb.append(s_gdb)
            n_lat.append(c_kv)
            n_kr.append(k_r)
            x = _out_call(x, o_c, o_d, W['cd_w_out'][j], g[1], gate1)
        x, s_ffn = _ffn_call(x, g[2], scale2, shift2, gate2, g[3], ffn_buf[l], W['ffn_wg'][l], W['ffn_wv'][l],
                             W['ffn_wd'][l], W['ffn_cw'][l])
        n_ffn.append(s_ffn)
    return x, (jnp.stack(n_hg), jnp.stack(n_lru), jnp.stack(n_lrub), jnp.stack(n_gd), jnp.stack(n_gdb),
               jnp.stack(n_lat), jnp.stack(n_kr), jnp.stack(n_ffn))


def _prep_weights(norm_g, ab_w_in, ab_w_out, hgrn_lb_logits, hgrn_norm_g, lru_conv_w, lru_conv_b, lru_w_a, lru_b_a,
                  lru_w_x, lru_b_x, lru_lambda, cd_w_in, cd_w_out, gdn_conv_w, gdn_a_log, gdn_dt_bias, gdn_norm_g,
                  mla_q_norm_g, mla_w_qb, mla_kv_norm_g, mla_w_kvb, ffn_w_up, ffn_conv_w, ffn_w_down):
    lb_p = jax.nn.softmax(hgrn_lb_logits.astype(F32), axis=0)
    pcol = jnp.zeros((N_CD, 2, 128), F32)
    pcol = pcol.at[:, 0, SMALL_GA:SMALL_GA + HEADS].set(gdn_a_log).at[:, 1, SMALL_GA:SMALL_GA + HEADS].set(gdn_dt_bias)
    prow = jnp.zeros((N_CD, 2 * HEADS, 2), F32)
    prow = prow.at[:, HEADS:, 0].set(gdn_a_log).at[:, HEADS:, 1].set(gdn_dt_bias)
    ffn_cw = jnp.swapaxes(ffn_conv_w.reshape(DEPTH, FFN_CONV, 2 * N_FF_TILES, FF_TILE), 1, 2)
    W = dict(
        norm_g=norm_g,
        ab_w_in=ab_w_in.astype(BF16), ab_w_out=ab_w_out.astype(BF16),
        lower_bounds=jnp.cumsum(lb_p, axis=0) - lb_p[0:1],
        hgrn_norm_g=hgrn_norm_g, lru_conv_w=lru_conv_w, lru_conv_b=lru_conv_b,
        lru_wa_bd=jax.vmap(_block_diag_pairs)(lru_w_a), lru_wx_bd=jax.vmap(_block_diag_pairs)(lru_w_x),
        lru_b_a=lru_b_a, lru_b_x=lru_b_x, lru_lambda=lru_lambda,
        cd_w_in=jax.vmap(_prep_cd_w_in)(cd_w_in), cd_w_out=cd_w_out.astype(BF16),
        gdn_conv_w=gdn_conv_w, gdn_pcol=pcol, gdn_prow=prow, gdn_norm_g=gdn_norm_g,
        mla_q_norm_g=mla_q_norm_g, mla_w_qb=jax.vmap(_prep_wqb)(mla_w_qb),
        mla_kv_norm_g=mla_kv_norm_g, mla_w_kvb=mla_w_kvb.astype(BF16),
        ffn_wg=jnp.swapaxes(ffn_w_up[:, :, :D_FF].reshape(DEPTH, D_MODEL, N_FF_TILES, FF_TILE), 1, 2).astype(BF16),
        ffn_wv=jnp.swapaxes(ffn_w_up[:, :, D_FF:].reshape(DEPTH, D_MODEL, N_FF_TILES, FF_TILE), 1, 2).astype(BF16),
        ffn_wd=ffn_w_down.astype(BF16),
        ffn_cw=ffn_cw,
    )
    return W


def kernel(x_prompt, x_sample, c_prompt, c_sample, state_hgrn, state_rglru, state_rglru_conv, state_gdn, state_gdn_conv, cache_mla_latent, cache_mla_krope, state_ffn_conv, ada_w, ada_b, norm_g, ab_w_in, ab_w_out, hgrn_lb_logits, hgrn_norm_g, lru_conv_w, lru_conv_b, lru_w_a, lru_b_a, lru_w_x, lru_b_x, lru_lambda, cd_w_in, cd_w_out, gdn_conv_w, gdn_a_log, gdn_dt_bias, gdn_norm_g, mla_q_norm_g, mla_w_qb, mla_kv_norm_g, mla_w_kvb, ffn_w_up, ffn_conv_w, ffn_w_down):
    bp, bs = x_prompt.shape[0], x_sample.shape[0]
    W = _prep_weights(norm_g, ab_w_in, ab_w_out, hgrn_lb_logits, hgrn_norm_g, lru_conv_w, lru_conv_b, lru_w_a, lru_b_a,
                      lru_w_x, lru_b_x, lru_lambda, cd_w_in, cd_w_out, gdn_conv_w, gdn_a_log, gdn_dt_bias,
                      gdn_norm_g, mla_q_norm_g, mla_w_qb, mla_kv_norm_g, mla_w_kvb, ffn_w_up, ffn_conv_w,
                      ffn_w_down)
    rows = bp + bs
    rows_pad = -(-rows // 8) * 8
    c_all = jnp.concatenate([c_prompt, c_sample, jnp.zeros((rows_pad - rows, D_MODEL), F32)], axis=0)
    mods = _ada_call(c_all, ada_w, ada_b)
    dt_ = x_prompt.dtype
    y_prompt, p_states = _run_group(
        x_prompt, mods[:, :bp],
        jnp.zeros((N_AB, bp, HEADS, HEAD_DIM, HEAD_DIM), F32),
        jnp.zeros((N_AB, bp, HALF), F32),
        jnp.zeros((N_AB, bp, LRU_CONV - 1, HALF), dt_),
        jnp.zeros((N_CD, bp, HEADS, HEAD_DIM, HEAD_DIM), F32),
        jnp.zeros((N_CD, bp, GD_CONV - 1, 3 * HALF), dt_),
        jnp.zeros((N_CD, bp, 0, MLA_KV_RANK), dt_),
        jnp.zeros((N_CD, bp, 0, MLA_ROPE), dt_),
        jnp.zeros((DEPTH, bp, FFN_CONV - 1, 2 * D_FF), dt_),
        W)
    y_sample, s_states = _run_group(
        x_sample, mods[:, bp:rows], state_hgrn, state_rglru, state_rglru_conv, state_gdn, state_gdn_conv,
        cache_mla_latent, cache_mla_krope, state_ffn_conv, W)
    return (y_prompt, y_sample) + tuple(p_states) + tuple(s_states)
```

```python
import functools
import math

import jax
import jax.numpy as jnp
from jax import lax
from jax.experimental import pallas as pl
from jax.experimental.pallas import tpu as pltpu

F32 = jnp.float32
BF16 = jnp.bfloat16

D_MODEL = 1024
DEPTH = 4
CHUNK = 64
HALF = D_MODEL // 2
N_AB = (DEPTH + 1) // 2
N_CD = DEPTH // 2
HEADS = 4
HEAD_DIM = HALF // HEADS
LRU_BLOCKS = 8
LRU_BLOCK = HALF // LRU_BLOCKS
LRU_CONV = 4
LRU_C = 8.0
GD_CONV = 4
MLA_NOPE = 128
MLA_ROPE = 64
MLA_QK = MLA_NOPE + MLA_ROPE
MLA_Q_RANK = 384
MLA_KV_RANK = 256
MLA_SCALE = (MLA_NOPE + MLA_ROPE) ** -0.5
ROPE_THETA = 10000.0
D_FF = 2816
FFN_CONV = 3
FF_TILE = 256
N_FF_TILES = D_FF // FF_TILE
FF_GROUP = 4
FF_SLOTS = 4
EPS = 1e-6
NEG_BIG = -1e30
SQRT_FLOOR = 1e-12
CD_COLS = 2816
SMALL_KR = 0
SMALL_GB = 64
SMALL_GA = 68


VMEM_LIMIT_BYTES =56 * 1024 * 1024


def _cparams(*sem):
    return pltpu.CompilerParams(dimension_semantics=sem, vmem_limit_bytes=VMEM_LIMIT_BYTES)


def _dot(a, b):
    return jnp.dot(a, b, preferred_element_type=F32)


def _dot_nt(a, b):
    return lax.dot_general(a, b, (((1,), (1,)), ((), ())), preferred_element_type=F32)


def _dot_tn(a, b):
    return lax.dot_general(a, b, (((0,), (0,)), ((), ())), preferred_element_type=F32)


def _rms(x, g):
    return x * lax.rsqrt(jnp.mean(x * x, axis=-1, keepdims=True) + EPS) * g


def _silu(x):
    return x * jax.nn.sigmoid(x)


def _softplus(x):
    return jnp.maximum(x, 0.0) + jnp.log1p(jnp.exp(-jnp.abs(x)))


def _gelu_tanh(x):
    return 0.5 * x * (1.0 + jnp.tanh(math.sqrt(2.0 / math.pi) * (x + 0.044715 * (x * x * x))))


def _split3(x):
    x1 = x.astype(BF16)
    r1 = x - x1.astype(F32)
    x2 = r1.astype(BF16)
    x3 = (r1 - x2.astype(F32)).astype(BF16)
    return x1, x2, x3


def _log2(n):
    assert n & (n - 1) == 0
    return n.bit_length() - 1


def _cumsum_rows(x, row):
    s = 1
    while s < x.shape[0]:
        x = x + jnp.where(row >= s, pltpu.roll(x, s, 0), 0.0)
        s *= 2
    return x


def _block_row_bcast(b, row, h):
    L, n = b.shape
    blk = 2 * h
    if blk >= 8:
        b3 = b.reshape(L // blk, blk, n)
        return jnp.broadcast_to(b3[:, h - 1:h, :], (L // blk, blk, n)).reshape(L, n)
    pos = row & (blk - 1)
    x0 = jnp.where(pos == h - 1, b, 0.0)
    out = x0
    for j in range(1, h + 1):
        out = out + pltpu.roll(x0, j, 0)
    for j in range(1, h):
        out = out + pltpu.roll(x0, L - j, 0)
    return out


def _ada_kernel(c_ref, w_ref, b_ref, o_ref):
    c = _silu(c_ref[...]).astype(BF16)
    o_ref[0] = _dot(c, w_ref[0].astype(BF16)) + b_ref[0]


def _ada_call(c_all, ada_w, ada_b):
    rows = c_all.shape[0]
    tn = 2048
    return pl.pallas_call(
        _ada_kernel,
        out_shape=jax.ShapeDtypeStruct((DEPTH, rows, 6 * D_MODEL), F32),
        grid=(DEPTH, 6 * D_MODEL // tn),
        in_specs=[
            pl.BlockSpec((rows, D_MODEL), lambda l, j: (0, 0)),
            pl.BlockSpec((1, D_MODEL, tn), lambda l, j: (l, 0, j)),
            pl.BlockSpec((1, 1, tn), lambda l, j: (l, 0, j)),
        ],
        out_specs=pl.BlockSpec((1, rows, tn), lambda l, j: (l, 0, j)),
        compiler_params=_cparams("arbitrary", "arbitrary"),
        name="ada_mod",
    )(c_all, ada_w, ada_b.reshape(DEPTH, 1, 6 * D_MODEL))


def _nmm_kernel(x_ref, g_ref, sc_ref, sh_ref, w_ref, o_ref, *maybe_t_ref):
    nb, tt, d = x_ref.shape
    h = _rms(x_ref[...], g_ref[...]) * (1.0 + sc_ref[...]) + sh_ref[...]
    y = _dot(h.reshape(nb * tt, d).astype(BF16), w_ref[...])
    o_ref[...] = y.reshape(nb, tt, y.shape[-1])
    if maybe_t_ref:
        (t_ref,) = maybe_t_ref
        yt = y[:, y.shape[-1] - 128:].T
        t_ref[0] = yt[SMALL_GB:SMALL_GB + 2 * HEADS, :]


def _tiles(B, T):
    if T >= 512:
        return 1, 512
    assert B * T <= 512
    return B, T


def _nmm_call(x, g, scale, shift, w, gates_t=False):
    B, T, D = x.shape
    N = w.shape[1]
    nb, tt = _tiles(B, T)
    out_shape = jax.ShapeDtypeStruct((B, T, N), F32)
    out_specs = pl.BlockSpec((nb, tt, N), lambda b, i: (b, i, 0))
    if gates_t:
        assert nb == 1 and tt % 128 == 0
        out_shape = (out_shape, jax.ShapeDtypeStruct((B, 2 * HEADS, T), F32))
        out_specs = (out_specs, pl.BlockSpec((1, 2 * HEADS, tt), lambda b, i: (b, 0, i)))
    return pl.pallas_call(
        _nmm_kernel,
        out_shape=out_shape,
        grid=(B // nb, T // tt),
        in_specs=[
            pl.BlockSpec((nb, tt, D), lambda b, i: (b, i, 0)),
            pl.BlockSpec((1, D), lambda b, i: (0, 0)),
            pl.BlockSpec((nb, 1, D), lambda b, i: (b, 0, 0)),
            pl.BlockSpec((nb, 1, D), lambda b, i: (b, 0, 0)),
            pl.BlockSpec((D, N), lambda b, i: (0, 0)),
        ],
        out_specs=out_specs,
        compiler_params=_cparams("arbitrary", "arbitrary"),
        name="norm_mod_proj",
    )(x, g.reshape(1, D), scale[:, None, :], shift[:, None, :], w)


def _out_kernel(x_ref, oa_ref, ob_ref, w_ref, g_ref, gate_ref, o_ref):
    nb, tt, d = x_ref.shape
    oa = oa_ref[...].reshape(nb * tt, HALF)
    ob = ob_ref[...].reshape(nb * tt, HALF)
    y = _dot(oa, w_ref[0:HALF, :]) + _dot(ob, w_ref[HALF:2 * HALF, :])
    y = _rms(y, g_ref[...]).reshape(nb, tt, d)
    o_ref[...] = x_ref[...] + gate_ref[...] * y


def _out_call(x, oa, ob, w, g, gate):
    B, T, D = x.shape
    nb, tt = _tiles(B, T)
    return pl.pallas_call(
        _out_kernel,
        out_shape=jax.ShapeDtypeStruct((B, T, D), F32),
        grid=(B // nb, T // tt),
        in_specs=[
            pl.BlockSpec((nb, tt, D), lambda b, i: (b, i, 0)),
            pl.BlockSpec((nb, tt, HALF), lambda b, i: (b, i, 0)),
            pl.BlockSpec((nb, tt, HALF), lambda b, i: (b, i, 0)),
            pl.BlockSpec((D, D), lambda b, i: (0, 0)),
            pl.BlockSpec((1, D), lambda b, i: (0, 0)),
            pl.BlockSpec((nb, 1, D), lambda b, i: (b, 0, 0)),
        ],
        out_specs=pl.BlockSpec((nb, tt, D), lambda b, i: (b, i, 0)),
        compiler_params=_cparams("arbitrary", "arbitrary"),
        name="out_proj_residual",
    )(x, oa, ob, w, g.reshape(1, D), gate[:, None, :])


def _ffn_kernel(x_ref, g1_ref, sc_ref, sh_ref, gate_ref, g2_ref, buf0_ref, wu_ref, wd_ref,
                cw_ref, o_ref, st_ref, carry_ref, ubuf_ref):
    nb, tt, d = x_ref.shape
    i = pl.program_id(1)
    x = x_ref[...]
    h = (_rms(x, g1_ref[...]) * (1.0 + sc_ref[...]) + sh_ref[...]).reshape(nb * tt, d).astype(BF16)

    @pl.when(i == 0)
    def _():
        for c in range(2 * N_FF_TILES):
            carry_ref[c] = buf0_ref[:, :, c * FF_TILE:(c + 1) * FF_TILE]

    def conv(u, slot, c):
        ubuf_ref[slot, :, 8:8 + tt, :] = u
        ubuf_ref[slot, :, 6:8, :] = carry_ref[c]
        cw = cw_ref[:, c * FF_TILE:(c + 1) * FF_TILE]
        y = (cw[0:1, :] * ubuf_ref[slot, :, 6:6 + tt, :] + cw[1:2, :] * ubuf_ref[slot, :, 7:7 + tt, :]
             + cw[2:3, :] * u)
        tail = ubuf_ref[slot, :, 6 + tt:8 + tt, :]
        carry_ref[c] = tail
        st_ref[:, :, c * FF_TILE:(c + 1) * FF_TILE] = tail
        return y

    def up_proj(c):
        wg = wu_ref[:, c * FF_TILE:(c + 1) * FF_TILE]
        wv = wu_ref[:, D_FF + c * FF_TILE:D_FF + (c + 1) * FF_TILE]
        return _dot(h, wg).reshape(nb, tt, FF_TILE), _dot(h, wv).reshape(nb, tt, FF_TILE)

    acc = None
    nxt = up_proj(0)
    for g0 in range(0, N_FF_TILES, FF_GROUP):
        acts = []
        for c in range(g0, min(g0 + FF_GROUP, N_FF_TILES)):
            ug, uv = nxt
            if c + 1 < N_FF_TILES:
                nxt = up_proj(c + 1)
            slot = 2 * (c % FF_SLOTS)
            yg = conv(ug, slot, c)
            yv = conv(uv, slot + 1, N_FF_TILES + c)
            acts.append((_silu(yg) * yv).reshape(nb * tt, FF_TILE).astype(BF16))
        a = jnp.concatenate(acts, axis=-1) if len(acts) > 1 else acts[0]
        part = _dot(a, wd_ref[g0 * FF_TILE:g0 * FF_TILE + a.shape[-1], :])
        acc = part if acc is None else acc + part
    y = _rms(acc, g2_ref[...]).reshape(nb, tt, d)
    o_ref[...] = x + gate_ref[...] * y


def _ffn_call(x, g1, scale, shift, gate, g2, buf0, wu, wd, cw):
    B, T, D = x.shape
    nb, tt = _tiles(B, T)
    assert T >= FFN_CONV - 1
    return pl.pallas_call(
        _ffn_kernel,
        out_shape=(jax.ShapeDtypeStruct((B, T, D), F32),
                   jax.ShapeDtypeStruct((B, FFN_CONV - 1, 2 * D_FF), F32)),
        grid=(B // nb, T // tt),
        in_specs=[
            pl.BlockSpec((nb, tt, D), lambda b, i: (b, i, 0)),
            pl.BlockSpec((1, D), lambda b, i: (0, 0)),
            pl.BlockSpec((nb, 1, D), lambda b, i: (b, 0, 0)),
            pl.BlockSpec((nb, 1, D), lambda b, i: (b, 0, 0)),
            pl.BlockSpec((nb, 1, D), lambda b, i: (b, 0, 0)),
            pl.BlockSpec((1, D), lambda b, i: (0, 0)),
            pl.BlockSpec((nb, FFN_CONV - 1, 2 * D_FF), lambda b, i: (b, 0, 0)),
            pl.BlockSpec((D, 2 * D_FF), lambda b, i: (0, 0), pipeline_mode=pl.Buffered(1)),
            pl.BlockSpec((D_FF, D), lambda b, i: (0, 0), pipeline_mode=pl.Buffered(1)),
            pl.BlockSpec((FFN_CONV, 2 * D_FF), lambda b, i: (0, 0)),
        ],
        out_specs=(pl.BlockSpec((nb, tt, D), lambda b, i: (b, i, 0)),
                   pl.BlockSpec((nb, FFN_CONV - 1, 2 * D_FF), lambda b, i: (b, 0, 0))),
        scratch_shapes=[
            pltpu.VMEM((2 * N_FF_TILES, nb, FFN_CONV - 1, FF_TILE), F32),
            pltpu.VMEM((2 * FF_SLOTS, nb, 8 + tt, FF_TILE), F32),
        ],
        compiler_params=_cparams("arbitrary", "arbitrary"),
        name="conv_ffn",
    )(x, g1.reshape(1, D), scale[:, None, :], shift[:, None, :], gate[:, None, :], g2.reshape(1, D),
      buf0, wu, wd, cw)


def _hgrn_kernel(hq_ref, hf_ref, hi_ref, hz_ref, lb_ref, ng_ref, s0_ref, o_ref, st_ref):
    L = hq_ref.shape[1]
    i = pl.program_id(1)

    @pl.when(i == 0)
    def _():
        st_ref[...] = s0_ref[...]

    row = lax.broadcasted_iota(jnp.int32, (L, HEAD_DIM), 0)
    r2 = lax.broadcasted_iota(jnp.int32, (L, L), 0)
    c2 = lax.broadcasted_iota(jnp.int32, (L, L), 1)
    heads = range(HEADS)
    sls = [slice(hd * HEAD_DIM, (hd + 1) * HEAD_DIM) for hd in heads]
    q, k, v, b = ([None] * HEADS for _ in range(4))
    for hd in heads:
        z = hf_ref[0, :, sls[hd]]
        lb = lb_ref[:, sls[hd]]
        g = jnp.log(lb + (1.0 - lb) * jax.nn.sigmoid(z))
        k[hd] = (1.0 - lb) * jax.nn.sigmoid(-z)
        q[hd] = _silu(hq_ref[0, :, sls[hd]])
        v[hd] = hi_ref[0, :, sls[hd]].astype(BF16)
        b[hd] = _cumsum_rows(g, row)
    att = [jnp.where(r2 == c2, _dot_nt(q[hd].astype(BF16), k[hd].astype(BF16)), 0.0) for hd in heads]
    h = L // 2
    while h >= 1:
        upper = (row & (2 * h - 1)) >= h
        sh = _log2(2 * h)
        same_block = (r2 >> sh) == (c2 >> sh)
        for hd in heads:
            r = _block_row_bcast(b[hd], row, h)
            e = jnp.exp(jnp.where(upper, b[hd] - r, r - b[hd]))
            qt = jnp.where(upper, q[hd] * e, 0.0).astype(BF16)
            kt = jnp.where(upper, 0.0, k[hd] * e).astype(BF16)
            att[hd] = att[hd] + jnp.where(same_block, _dot_nt(qt, kt), 0.0)
        h //= 2
    for hd in heads:
        st = st_ref[0, hd]
        o = _dot(att[hd].astype(BF16), v[hd]) + _dot_nt((q[hd] * jnp.exp(b[hd])).astype(BF16), st.astype(BF16))
        b_last = b[hd][L - 1:L, :]
        kd = (k[hd] * jnp.exp(b_last - b[hd])).astype(BF16)
        st_ref[0, hd] = jnp.exp(b_last) * st + _dot_tn(v[hd], kd)
        o = _rms(o, ng_ref[...]) * _silu(hz_ref[0, :, sls[hd]])
        o_ref[0, :, sls[hd]] = o.astype(BF16)


def _hgrn_call(proj, lb, ng, s0t):
    B, T, _ = proj.shape
    L = 128 if T % 128 == 0 else T
    assert T % L == 0 and L & (L - 1) == 0 and L >= 8
    col = lambda c: pl.BlockSpec((1, L, HALF), lambda b, i, c=c: (b, i, c))
    return pl.pallas_call(
        _hgrn_kernel,
        out_shape=(jax.ShapeDtypeStruct((B, T, HALF), BF16),
                   jax.ShapeDtypeStruct((B, HEADS, HEAD_DIM, HEAD_DIM), F32)),
        grid=(B, T // L),
        in_specs=[col(0), col(1), col(2), col(3),
                  pl.BlockSpec((1, HALF), lambda b, i: (0, 0)),
                  pl.BlockSpec((1, HEAD_DIM), lambda b, i: (0, 0)),
                  pl.BlockSpec((1, HEADS, HEAD_DIM, HEAD_DIM), lambda b, i: (b, 0, 0, 0))],
        out_specs=(pl.BlockSpec((1, L, HALF), lambda b, i: (b, i, 0)),
                   pl.BlockSpec((1, HEADS, HEAD_DIM, HEAD_DIM), lambda b, i: (b, 0, 0, 0))),
        compiler_params=_cparams("arbitrary", "arbitrary"),
        name="hgrn2",
    )(proj, proj, proj, proj, lb.reshape(1, HALF), ng.reshape(1, HEAD_DIM), s0t)


def _lru_kernel(lx_ref, ly_ref, buf0_ref, h0_ref, cw_ref, cb_ref, wa_ref, wx_ref, ba_ref, bx_ref,
                lam_ref, o_ref, hl_ref, bufo_ref, xp_ref):
    tt = lx_ref.shape[1]
    i = pl.program_id(1)
    npad = LRU_CONV - 1

    @pl.when(i == 0)
    def _():
        xp_ref[8 - npad:8, :] = buf0_ref[0]
        hl_ref[0] = h0_ref[0]

    x = lx_ref[0]
    xp_ref[8:8 + tt, :] = x
    xc = cb_ref[...] + cw_ref[npad:npad + 1, :] * x
    for tap in range(npad):
        xc = xc + cw_ref[tap:tap + 1, :] * xp_ref[8 - npad + tap:8 - npad + tap + tt, :]
    tail = xp_ref[8 + tt - npad:8 + tt, :]
    xp_ref[8 - npad:8, :] = tail
    bufo_ref[0] = tail

    xb = xc.astype(BF16)
    half = HALF // 2
    rpre = jnp.concatenate([_dot(xb[:, 0:half], wa_ref[0]), _dot(xb[:, half:HALF], wa_ref[1])], axis=-1)
    ipre = jnp.concatenate([_dot(xb[:, 0:half], wx_ref[0]), _dot(xb[:, half:HALF], wx_ref[1])], axis=-1)
    r = jax.nn.sigmoid(rpre + ba_ref[...])
    ig = jax.nn.sigmoid(ipre + bx_ref[...])
    log_a = -LRU_C * r * _softplus(-lam_ref[...])
    a = jnp.exp(log_a)
    u = jnp.sqrt(jnp.maximum(-jnp.tanh(log_a) * (1.0 + a * a), SQRT_FLOOR)) * ig * xc

    row = lax.broadcasted_iota(jnp.int32, (tt, HALF), 0)
    s = 1
    while s < tt:
        keep = row >= s
        a_sh = jnp.where(keep, pltpu.roll(a, s, 0), 1.0)
        u_sh = jnp.where(keep, pltpu.roll(u, s, 0), 0.0)
        u = a * u_sh + u
        a = a * a_sh
        s *= 2
    hseq = u + a * hl_ref[0]
    hl_ref[0] = hseq[tt - 1:tt, :]
    o_ref[0] = (hseq * _gelu_tanh(ly_ref[0])).astype(BF16)


def _lru_call(proj, buf0, h0, cw, cb, wa_bd, wx_bd, ba, bx, lam):
    B, T, _ = proj.shape
    tt = 256 if T % 256 == 0 else T
    assert T % tt == 0 and T >= LRU_CONV - 1 and tt % 8 == 0
    vec = pl.BlockSpec((1, HALF), lambda b, i: (0, 0))
    wspec = pl.BlockSpec((2, HALF // 2, HALF // 2), lambda b, i: (0, 0, 0))
    return pl.pallas_call(
        _lru_kernel,
        out_shape=(jax.ShapeDtypeStruct((B, T, HALF), BF16),
                   jax.ShapeDtypeStruct((B, 1, HALF), F32),
                   jax.ShapeDtypeStruct((B, LRU_CONV - 1, HALF), F32)),
        grid=(B, T // tt),
        in_specs=[pl.BlockSpec((1, tt, HALF), lambda b, i: (b, i, 4)),
                  pl.BlockSpec((1, tt, HALF), lambda b, i: (b, i, 5)),
                  pl.BlockSpec((1, LRU_CONV - 1, HALF), lambda b, i: (b, 0, 0)),
                  pl.BlockSpec((1, 1, HALF), lambda b, i: (b, 0, 0)),
                  pl.BlockSpec((LRU_CONV, HALF), lambda b, i: (0, 0)),
                  vec, wspec, wspec, vec, vec, vec],
        out_specs=(pl.BlockSpec((1, tt, HALF), lambda b, i: (b, i, 0)),
                   pl.BlockSpec((1, 1, HALF), lambda b, i: (b, 0, 0)),
                   pl.BlockSpec((1, LRU_CONV - 1, HALF), lambda b, i: (b, 0, 0))),
        scratch_shapes=[pltpu.VMEM((8 + tt, HALF), F32)],
        compiler_params=_cparams("arbitrary", "arbitrary"),
        name="rglru",
    )(proj, proj, buf0, h0[:, None, :], cw, cb.reshape(1, HALF), wa_bd, wx_bd,
      ba.reshape(1, HALF), bx.reshape(1, HALF), lam.reshape(1, HALF))


def _gdn_kernel(qkv_ref, gz_ref, sm_ref, smt_ref, buf0_ref, cw_ref, pcol_ref, prow_ref, ng_ref, s0_ref,
                o_ref, st_ref, bufo_ref, xp_ref, *, L):
    tt = qkv_ref.shape[1]
    nc = tt // L
    sh = _log2(L)
    i = pl.program_id(1)
    npad = GD_CONV - 1

    @pl.when(i == 0)
    def _():
        xp_ref[8 - npad:8, :] = buf0_ref[0]
        st_ref[...] = s0_ref[...]

    x = qkv_ref[0]
    xp_ref[8:8 + tt, :] = x
    xc = cw_ref[npad:npad + 1, :] * x
    for tap in range(npad):
        xc = xc + cw_ref[tap:tap + 1, :] * xp_ref[8 - npad + tap:8 - npad + tap + tt, :]
    tail = xp_ref[8 + tt - npad:8 + tt, :]
    xp_ref[8 - npad:8, :] = tail
    bufo_ref[0] = tail
    xc = _silu(xc)

    r2 = lax.broadcasted_iota(jnp.int32, (tt, tt), 0)
    c2 = lax.broadcasted_iota(jnp.int32, (tt, tt), 1)
    same = (r2 >> sh) == (c2 >> sh)
    incl = jnp.logical_and(same, c2 <= r2)
    strict = jnp.logical_and(same, c2 < r2)
    tri_lo = jnp.where(incl, 1.0, 0.0).astype(BF16)
    tri_up = jnp.where(jnp.logical_and(same, r2 <= c2), 1.0, 0.0).astype(BF16)
    eye = jnp.where(r2 == c2, 1.0, 0.0)

    sm = sm_ref[0]
    beta_cols = jax.nn.sigmoid(sm)
    la_cols = -jnp.exp(pcol_ref[0:1, :]) * _softplus(sm + pcol_ref[1:2, :])
    c1, c2_, c3 = _split3(la_cols)
    g_cols = _dot(tri_lo, c1) + _dot(tri_lo, c2_) + _dot(tri_lo, c3)
    la_rows = -jnp.exp(prow_ref[:, 0:1]) * _softplus(smt_ref[0] + prow_ref[:, 1:2])
    w1, w2, w3 = _split3(la_rows)
    g_rows = _dot(w1, tri_up) + _dot(w2, tri_up) + _dot(w3, tri_up)

    heads = range(HEADS)
    q, k, v, beta, gcol, kb, dec, m = ([None] * HEADS for _ in range(8))
    for hd in heads:
        qh = xc[:, hd * HEAD_DIM:(hd + 1) * HEAD_DIM]
        kh = xc[:, HALF + hd * HEAD_DIM:HALF + (hd + 1) * HEAD_DIM]
        v[hd] = xc[:, 2 * HALF + hd * HEAD_DIM:2 * HALF + (hd + 1) * HEAD_DIM]
        q[hd] = qh * lax.rsqrt(jnp.sum(qh * qh, axis=-1, keepdims=True) + EPS) * (HEAD_DIM ** -0.5)
        k[hd] = kh * lax.rsqrt(jnp.sum(kh * kh, axis=-1, keepdims=True) + EPS)
        beta[hd] = beta_cols[:, SMALL_GB + hd:SMALL_GB + hd + 1]
        gcol[hd] = g_cols[:, SMALL_GA + hd:SMALL_GA + hd + 1]
        grow = g_rows[HEADS + hd:HEADS + hd + 1, :]
        kb[hd] = k[hd].astype(BF16)
        dec[hd] = jnp.exp(jnp.where(incl, gcol[hd] - grow, NEG_BIG))
        m[hd] = beta[hd] * _dot_nt(kb[hd], kb[hd]) * jnp.where(strict, dec[hd], 0.0)

    pair = (r2 >> 1) == (c2 >> 1)
    tinv = [eye - jnp.where(pair, m[hd], 0.0) for hd in heads]
    s = 2
    while s < L:
        ssh = _log2(s)
        lower_left = jnp.logical_and((r2 >> (ssh + 1)) == (c2 >> (ssh + 1)), (r2 >> ssh) != (c2 >> ssh))
        tb = [tinv[hd].astype(BF16) for hd in heads]
        tc = [_dot(tb[hd], jnp.where(lower_left, m[hd], 0.0).astype(BF16)).astype(BF16) for hd in heads]
        tinv = [tinv[hd] - _dot(tc[hd], tb[hd]) for hd in heads]
        s *= 2

    eg = [jnp.exp(gcol[hd]) for hd in heads]
    sol = [_dot(tinv[hd].astype(BF16),
                jnp.concatenate([beta[hd] * v[hd], (beta[hd] * eg[hd]) * k[hd]], axis=-1).astype(BF16))
           for hd in heads]
    u_v = [sol[hd][:, 0:HEAD_DIM] for hd in heads]
    w_k = [sol[hd][:, HEAD_DIM:2 * HEAD_DIM].astype(BF16) for hd in heads]
    qb = [q[hd].astype(BF16) for hd in heads]
    qk = [(_dot_nt(qb[hd], kb[hd]) * dec[hd]).astype(BF16) for hd in heads]

    S = [st_ref[0, hd] for hd in heads]
    us = [[] for _ in heads]
    inters = [[] for _ in heads]
    for c in range(nc):
        rs = slice(c * L, (c + 1) * L)
        for hd in heads:
            Sb = S[hd].astype(BF16)
            u = u_v[hd][rs] - _dot(w_k[hd][rs], Sb)
            inters[hd].append(eg[hd][rs] * _dot(qb[hd][rs], Sb))
            g_last = gcol[hd][(c + 1) * L - 1:(c + 1) * L, :]
            kd = (k[hd][rs] * jnp.exp(g_last - gcol[hd][rs])).astype(BF16)
            S[hd] = jnp.exp(g_last) * S[hd] + _dot_tn(kd, u.astype(BF16))
            us[hd].append(u)
    for hd in heads:
        sl = slice(hd * HEAD_DIM, (hd + 1) * HEAD_DIM)
        st_ref[0, hd] = S[hd]
        u_all = jnp.concatenate(us[hd], axis=0) if nc > 1 else us[hd][0]
        inter = jnp.concatenate(inters[hd], axis=0) if nc > 1 else inters[hd][0]
        o = _dot(qk[hd], u_all.astype(BF16)) + inter
        o = _rms(o, ng_ref[...]) * _silu(gz_ref[0, :, sl])
        o_ref[0, :, sl] = o.astype(BF16)


def _gdn_call(proj, small_t, buf0, cw, pcol, prow, ng, s0):
    B, T, _ = proj.shape
    L = CHUNK if T % CHUNK == 0 else T
    tt = 256 if T % 256 == 0 else T
    assert T % tt == 0 and tt % L == 0 and T >= GD_CONV - 1 and L >= 2
    W = 3 * HALF
    return pl.pallas_call(
        functools.partial(_gdn_kernel, L=L),
        out_shape=(jax.ShapeDtypeStruct((B, T, HALF), BF16),
                   jax.ShapeDtypeStruct((B, HEADS, HEAD_DIM, HEAD_DIM), F32),
                   jax.ShapeDtypeStruct((B, GD_CONV - 1, W), F32)),
        grid=(B, T // tt),
        in_specs=[pl.BlockSpec((1, tt, W), lambda b, i: (b, i, 0)),
                  pl.BlockSpec((1, tt, HALF), lambda b, i: (b, i, 3)),
                  pl.BlockSpec((1, tt, 128), lambda b, i: (b, i, 21)),
                  pl.BlockSpec((1, 2 * HEADS, tt), lambda b, i: (b, 0, i)),
                  pl.BlockSpec((1, GD_CONV - 1, W), lambda b, i: (b, 0, 0)),
                  pl.BlockSpec((GD_CONV, W), lambda b, i: (0, 0)),
                  pl.BlockSpec((2, 128), lambda b, i: (0, 0)),
                  pl.BlockSpec((2 * HEADS, 2), lambda b, i: (0, 0)),
                  pl.BlockSpec((1, HEAD_DIM), lambda b, i: (0, 0)),
                  pl.BlockSpec((1, HEADS, HEAD_DIM, HEAD_DIM), lambda b, i: (b, 0, 0, 0))],
        out_specs=(pl.BlockSpec((1, tt, HALF), lambda b, i: (b, i, 0)),
                   pl.BlockSpec((1, HEADS, HEAD_DIM, HEAD_DIM), lambda b, i: (b, 0, 0, 0)),
                   pl.BlockSpec((1, GD_CONV - 1, W), lambda b, i: (b, 0, 0))),
        scratch_shapes=[pltpu.VMEM((8 + tt, W), F32)],
        compiler_params=_cparams("arbitrary", "arbitrary"),
        name="gated_deltanet",
    )(proj, proj, proj, small_t, buf0, cw, pcol, prow, ng.reshape(1, HEAD_DIM), s0)


def _rope64(x, cs, sn):
    half = MLA_ROPE // 2
    swapped = jnp.concatenate([x[:, half:], x[:, :half]], axis=-1)
    return x * cs + swapped * sn


def _expand_kv(c_kv, k_r, wkvb_ref, kf_ref, v_ref):
    kv = _dot(c_kv.astype(BF16), wkvb_ref[...])
    for hd in range(HEADS):
        base = hd * 2 * HEAD_DIM
        kf_ref[0, hd] = jnp.concatenate([kv[:, base:base + MLA_NOPE], k_r], axis=-1).astype(BF16)
        v = kv[:, base + MLA_NOPE:base + 2 * HEAD_DIM]
        v_ref[0, hd] = jnp.concatenate([v, jnp.ones_like(v)], axis=-1).astype(BF16)


def _mla_prep_kernel(qa_ref, kva_ref, sm_ref, cs_ref, sn_ref, qng_ref, wqb_ref, kvng_ref, wkvb_ref, *rest):
    q_ref, ckv_ref, kr_ref, kf_ref, v_ref = rest[-5:]
    ckv_ref = ckv_ref.at[0]
    kr_ref = kr_ref.at[0]
    cs = cs_ref[...]
    sn = sn_ref[...]
    qn = _rms(qa_ref[0], qng_ref[...]).astype(BF16)
    qh = _dot(qn, wqb_ref[...]) * (MLA_SCALE * math.log2(math.e))
    for hd in range(HEADS):
        nope = qh[:, hd * MLA_NOPE:(hd + 1) * MLA_NOPE]
        off = HEADS * MLA_NOPE + hd * MLA_ROPE
        rot = _rope64(qh[:, off:off + MLA_ROPE], cs, sn)
        q_ref[0, hd] = jnp.concatenate([nope, rot], axis=-1).astype(BF16)
    c_kv = _rms(kva_ref[0], kvng_ref[...])
    ckv_ref[0] = c_kv
    k_r = _rope64(sm_ref[0, :, SMALL_KR:SMALL_KR + MLA_ROPE], cs, sn)
    kr_ref[0] = k_r
    _expand_kv(c_kv, k_r, wkvb_ref, kf_ref, v_ref)


def _mla_prep_call(proj, cs, sn, qng, wqb, kvng, wkvb, j, state_bufs, past_bufs):
    B, T, _ = proj.shape
    tt = 512 if T % 512 == 0 else T
    c2 = lambda b, i: (0, 0)
    past_len = 0 if past_bufs is None else past_bufs[0].shape[2] - T
    assert past_len % tt == 0
    off = past_len // tt
    any_spec = pl.BlockSpec(memory_space=pl.ANY)
    extra, aliases = [], {}
    n_fixed = 9
    if state_bufs is not None:
        aliases[n_fixed + len(extra)] = 1
        aliases[n_fixed + len(extra) + 1] = 2
        extra += list(state_bufs)
    if past_bufs is not None:
        aliases[n_fixed + len(extra)] = 3
        aliases[n_fixed + len(extra) + 1] = 4
        extra += list(past_bufs)
    return pl.pallas_call(
        _mla_prep_kernel,
        out_shape=(jax.ShapeDtypeStruct((B, HEADS, T, MLA_QK), BF16),
                   jax.ShapeDtypeStruct((N_CD, B, T, MLA_KV_RANK), F32),
                   jax.ShapeDtypeStruct((N_CD, B, T, MLA_ROPE), F32),
                   jax.ShapeDtypeStruct((B, HEADS, past_len + T, MLA_QK), BF16),
                   jax.ShapeDtypeStruct((B, HEADS, past_len + T, 2 * HEAD_DIM), BF16)),
        grid=(B, T // tt),
        in_specs=[pl.BlockSpec((1, tt, MLA_Q_RANK), lambda b, i: (b, i, 6)),
                  pl.BlockSpec((1, tt, MLA_KV_RANK), lambda b, i: (b, i, 8)),
                  pl.BlockSpec((1, tt, 128), lambda b, i: (b, i, 21)),
                  pl.BlockSpec((tt, MLA_ROPE), lambda b, i: (i, 0)),
                  pl.BlockSpec((tt, MLA_ROPE), lambda b, i: (i, 0)),
                  pl.BlockSpec((1, MLA_Q_RANK), c2),
                  pl.BlockSpec((MLA_Q_RANK, HEADS * MLA_QK), c2),
                  pl.BlockSpec((1, MLA_KV_RANK), c2),
                  pl.BlockSpec((MLA_KV_RANK, HEADS * 2 * HEAD_DIM), c2)] + [any_spec] * len(extra),
        out_specs=(pl.BlockSpec((1, HEADS, tt, MLA_QK), lambda b, i: (b, 0, i, 0)),
                   pl.BlockSpec((1, 1, tt, MLA_KV_RANK), lambda b, i: (j, b, i, 0)),
                   pl.BlockSpec((1, 1, tt, MLA_ROPE), lambda b, i: (j, b, i, 0)),
                   pl.BlockSpec((1, HEADS, tt, MLA_QK), lambda b, i: (b, 0, i + off, 0)),
                   pl.BlockSpec((1, HEADS, tt, 2 * HEAD_DIM), lambda b, i: (b, 0, i + off, 0))),
        input_output_aliases=aliases,
        compiler_params=_cparams("arbitrary", "arbitrary"),
        name="mla_prep",
    )(proj, proj, proj, cs, sn, qng.reshape(1, MLA_Q_RANK), wqb, kvng.reshape(1, MLA_KV_RANK), wkvb, *extra)


def _mla_past_kernel(lat_ref, kr_ref, wkvb_ref, kf_ref, v_ref):
    _expand_kv(lat_ref[0], kr_ref[0], wkvb_ref, kf_ref, v_ref)


def _mla_past_call(lat, kr, wkvb, n_new):
    B, P, _ = lat.shape
    tt = 512 if P % 512 == 0 else P
    return pl.pallas_call(
        _mla_past_kernel,
        out_shape=(jax.ShapeDtypeStruct((B, HEADS, P + n_new, MLA_QK), BF16),
                   jax.ShapeDtypeStruct((B, HEADS, P + n_new, 2 * HEAD_DIM), BF16)),
        grid=(B, P // tt),
        in_specs=[pl.BlockSpec((1, tt, MLA_KV_RANK), lambda b, i: (b, i, 0)),
                  pl.BlockSpec((1, tt, MLA_ROPE), lambda b, i: (b, i, 0)),
                  pl.BlockSpec((MLA_KV_RANK, HEADS * 2 * HEAD_DIM), lambda b, i: (0, 0))],
        out_specs=(pl.BlockSpec((1, HEADS, tt, MLA_QK), lambda b, i: (b, 0, i, 0)),
                   pl.BlockSpec((1, HEADS, tt, 2 * HEAD_DIM), lambda b, i: (b, 0, i, 0))),
        compiler_params=_cparams("arbitrary", "arbitrary"),
        name="mla_past_kv",
    )(lat, kr, wkvb)


def _attn_kernel(q_ref, k_ref, v_ref, o_ref, m_ref, acc_ref, sa_ref, sb_ref, *, past_len, tq, tk, nk):
    i = pl.program_id(2)
    sh = _log2(CHUNK)
    lanes = HEAD_DIM
    m_ref[...] = jnp.full(m_ref.shape, NEG_BIG, F32)
    acc_ref[...] = jnp.zeros(acc_ref.shape, F32)
    q = q_ref[0, 0]
    q_lo = past_len + i * tq
    n_full = jnp.minimum(nk, (((q_lo >> sh) + 1) * CHUNK) // tk)
    n_need = jnp.minimum(nk, ((((q_lo + tq - 1) >> sh) + 1) * CHUNK + tk - 1) // tk)

    def scores(j):
        return _dot_nt(q, k_ref[0, 0, pl.ds(pl.multiple_of(j * tk, tk), tk), :])

    def masked_scores(j):
        k_lo = j * tk
        qc = (q_lo + lax.broadcasted_iota(jnp.int32, (tq, tk), 0)) >> sh
        kc = (k_lo + lax.broadcasted_iota(jnp.int32, (tq, tk), 1)) >> sh
        return jnp.where(kc <= qc, scores(j), NEG_BIG)

    def update(j, s_ref):
        k_lo = pl.multiple_of(j * tk, tk)
        m_prev = m_ref[...]
        m_new = jnp.maximum(m_prev, jnp.max(s_ref[...], axis=-1, keepdims=True))
        alpha = jnp.exp2(m_prev - m_new)
        if tk % lanes == 0:
            p = jnp.exp2(s_ref[...] - jnp.tile(m_new, (1, tk // lanes)))
        else:
            p = jnp.exp2(s_ref[...] - m_new[:, 0:1])
        pv = _dot(p.astype(BF16), v_ref[0, 0, pl.ds(k_lo, tk), :])
        acc_ref[...] = jnp.tile(alpha, (1, 2)) * acc_ref[...] + pv
        m_ref[...] = m_new

    def body_pair(g, carry):
        sb_ref[...] = scores(2 * g + 1)
        update(2 * g, sa_ref)
        sa_ref[...] = scores(2 * g + 2)
        update(2 * g + 1, sb_ref)
        return carry

    sa_ref[...] = scores(0)
    n_pairs = jnp.maximum(n_full - 1, 0) // 2
    lax.fori_loop(0, n_pairs, body_pair, 0)

    left = n_full - 2 * n_pairs
    has_masked = n_need > n_full
    j_masked = jnp.minimum(n_full, nk - 1)

    @pl.when(left == 1)
    def _():
        sb_ref[...] = masked_scores(j_masked)
        update(n_full - 1, sa_ref)

    @pl.when(jnp.logical_and(left == 1, has_masked))
    def _():
        update(n_full, sb_ref)

    @pl.when(left == 2)
    def _():
        sb_ref[...] = scores(n_full - 1)
        update(n_full - 2, sa_ref)
        sa_ref[...] = masked_scores(j_masked)
        update(n_full - 1, sb_ref)

    @pl.when(jnp.logical_and(left == 2, has_masked))
    def _():
        update(n_full, sa_ref)

    def body_masked(j, carry):
        sa_ref[...] = masked_scores(j)
        update(j, sa_ref)
        return carry

    lax.fori_loop(jnp.where(n_full > 0, n_full + 1, 0), n_need, body_masked, 0)
    o_ref[0] = (acc_ref[:, 0:HEAD_DIM] / acc_ref[:, HEAD_DIM:2 * HEAD_DIM]).astype(BF16)


def _attn_call(q, kf, v, past_len):
    B, H, Tq, _ = q.shape
    Tk = kf.shape[2]
    tq = 1024 if Tq % 1024 == 0 else Tq
    tk = 1024 if Tk % 1024 == 0 else Tk
    nk = Tk // tk
    return pl.pallas_call(
        functools.partial(_attn_kernel, past_len=past_len, tq=tq, tk=tk, nk=nk),
        out_shape=jax.ShapeDtypeStruct((B, Tq, H * HEAD_DIM), BF16),
        grid=(B, H, Tq // tq),
        in_specs=[pl.BlockSpec((1, 1, tq, MLA_QK), lambda b, h, i: (b, h, i, 0)),
                  pl.BlockSpec((1, 1, Tk, MLA_QK), lambda b, h, i: (b, h, 0, 0)),
                  pl.BlockSpec((1, 1, Tk, 2 * HEAD_DIM), lambda b, h, i: (b, h, 0, 0))],
        out_specs=pl.BlockSpec((1, tq, HEAD_DIM), lambda b, h, i: (b, i, h)),
        scratch_shapes=[pltpu.VMEM((tq, HEAD_DIM), F32), pltpu.VMEM((tq, 2 * HEAD_DIM), F32),
                        pltpu.VMEM((tq, tk), F32), pltpu.VMEM((tq, tk), F32)],
        compiler_params=_cparams("arbitrary", "arbitrary", "arbitrary"),
        name="mla_attention",
    )(q, kf, v)


def _block_diag_pairs(w):
    per = (HALF // 2) // LRU_BLOCK
    w4 = w.reshape(2, per, LRU_BLOCK, LRU_BLOCK)
    eye = jnp.eye(per, dtype=w.dtype)
    out = w4[:, :, :, None, :] * eye[None, :, None, :, None]
    return out.reshape(2, HALF // 2, HALF // 2).astype(BF16)


def _prep_cd_w_in(w):
    o = 3 * HALF
    qkv, gz = w[:, :o], w[:, o:o + HALF]
    o += HALF
    gb, ga = w[:, o:o + HEADS], w[:, o + HEADS:o + 2 * HEADS]
    o += 2 * HEADS
    qa, kva = w[:, o:o + MLA_Q_RANK], w[:, o + MLA_Q_RANK:o + MLA_Q_RANK + MLA_KV_RANK]
    o += MLA_Q_RANK + MLA_KV_RANK
    kr = w[:, o:o + MLA_ROPE]
    assert (SMALL_KR, SMALL_GB, SMALL_GA) == (0, MLA_ROPE, MLA_ROPE + HEADS)
    fill = jnp.zeros((w.shape[0], 128 - MLA_ROPE - 2 * HEADS), w.dtype)
    out = jnp.concatenate([qkv, gz, kva, qa, kr, gb, ga, fill], axis=-1)
    assert out.shape[1] == CD_COLS
    return out.astype(BF16)


def _prep_wqb(w):
    w4 = w.reshape(MLA_Q_RANK, HEADS, MLA_QK)
    nope = w4[:, :, :MLA_NOPE].reshape(MLA_Q_RANK, HEADS * MLA_NOPE)
    rope = w4[:, :, MLA_NOPE:].reshape(MLA_Q_RANK, HEADS * MLA_ROPE)
    return jnp.concatenate([nope, rope], axis=-1).astype(BF16)


def _rope_tables(T, past_len):
    half = MLA_ROPE // 2
    freqs = jnp.exp(-math.log(ROPE_THETA) * jnp.arange(half, dtype=F32) / half)
    pos = past_len + jnp.arange(T, dtype=jnp.int32)
    ang = pos.astype(F32)[:, None] * freqs
    cos, sin = jnp.cos(ang), jnp.sin(ang)
    return jnp.concatenate([cos, cos], axis=-1), jnp.concatenate([-sin, sin], axis=-1)


def _run_group(x, mods, hg_s, lru_h, lru_buf, gd_s, gd_buf, lat_past, kr_past, ffn_buf, W):
    B, T, _ = x.shape
    n_hg, n_lru, n_lrub, n_gd, n_gdb, n_ffn = ([] for _ in range(6))
    mla_state = None
    for l in range(DEPTH):
        j = l // 2
        shift1, scale1, gate1, shift2, scale2, gate2 = jnp.split(mods[l], 6, axis=-1)
        g = W['norm_g'][l]
        if l % 2 == 0:
            proj = _nmm_call(x, g[0], scale1, shift1, W['ab_w_in'][j])
            o_a, s_hg_t = _hgrn_call(proj, W['lower_bounds'][j], W['hgrn_norm_g'][j],
                                     jnp.swapaxes(hg_s[j], -1, -2))
            o_b, s_lru, s_lrub = _lru_call(proj, lru_buf[j], lru_h[j], W['lru_conv_w'][j], W['lru_conv_b'][j],
                                           W['lru_wa_bd'][j], W['lru_wx_bd'][j], W['lru_b_a'][j],
                                           W['lru_b_x'][j], W['lru_lambda'][j])
            n_hg.append(jnp.swapaxes(s_hg_t, -1, -2))
            n_lru.append(s_lru[:, 0, :])
            n_lrub.append(s_lrub)
            x = _out_call(x, o_a, o_b, W['ab_w_out'][j], g[1], gate1)
        else:
            past_len = lat_past.shape[2]
            if _tiles(B, T)[0] == 1:
                proj, small_t = _nmm_call(x, g[0], scale1, shift1, W['cd_w_in'][j], gates_t=True)
            else:
                proj = _nmm_call(x, g[0], scale1, shift1, W['cd_w_in'][j])
                small_t = jnp.swapaxes(proj[:, :, CD_COLS - 128 + SMALL_GB:CD_COLS - 128 + SMALL_GB + 2 * HEADS], 1, 2)
            o_c, s_gd, s_gdb = _gdn_call(proj, small_t, gd_buf[j], W['gdn_conv_w'][j], W['gdn_pcol'][j],
                                         W['gdn_prow'][j], W['gdn_norm_g'][j], gd_s[j])
            cs, sn = _rope_tables(T, past_len)
            past_bufs = _mla_past_call(lat_past[j], kr_past[j], W['mla_w_kvb'][j], T) if past_len > 0 else None
            q, lat_all, kr_all, kf, v = _mla_prep_call(proj, cs, sn, W['mla_q_norm_g'][j], W['mla_w_qb'][j],
                                                       W['mla_kv_norm_g'][j], W['mla_w_kvb'][j], j, mla_state,
                                                       past_bufs)
            mla_state = (lat_all, kr_all)
            o_d = _attn_call(q, kf, v, past_len)
            n_gd.append(s_gd)
            n_gdb.append(s_gdb)
            x = _out_call(x, o_c, o_d, W['cd_w_out'][j], g[1], gate1)
        x, s_ffn = _ffn_call(x, g[2], scale2, shift2, gate2, g[3], ffn_buf[l], W['ffn_wu'][l], W['ffn_wd'][l],
                             W['ffn_cw'][l])
        n_ffn.append(s_ffn)
    return x, (jnp.stack(n_hg), jnp.stack(n_lru), jnp.stack(n_lrub), jnp.stack(n_gd), jnp.stack(n_gdb),
               mla_state[0], mla_state[1], jnp.stack(n_ffn))


def _prep_weights(norm_g, ab_w_in, ab_w_out, hgrn_lb_logits, hgrn_norm_g, lru_conv_w, lru_conv_b, lru_w_a, lru_b_a,
                  lru_w_x, lru_b_x, lru_lambda, cd_w_in, cd_w_out, gdn_conv_w, gdn_a_log, gdn_dt_bias, gdn_norm_g,
                  mla_q_norm_g, mla_w_qb, mla_kv_norm_g, mla_w_kvb, ffn_w_up, ffn_conv_w, ffn_w_down):
    lb_p = jax.nn.softmax(hgrn_lb_logits.astype(F32), axis=0)
    pcol = jnp.pad(jnp.stack([gdn_a_log, gdn_dt_bias], axis=1),
                   ((0, 0), (0, 0), (SMALL_GA, 128 - SMALL_GA - HEADS)))
    prow = jnp.pad(jnp.stack([gdn_a_log, gdn_dt_bias], axis=-1), ((0, 0), (HEADS, 0), (0, 0)))
    W = dict(
        norm_g=norm_g,
        ab_w_in=ab_w_in.astype(BF16), ab_w_out=ab_w_out.astype(BF16),
        lower_bounds=jnp.cumsum(lb_p, axis=0) - lb_p[0:1],
        hgrn_norm_g=hgrn_norm_g, lru_conv_w=lru_conv_w, lru_conv_b=lru_conv_b,
        lru_wa_bd=jax.vmap(_block_diag_pairs)(lru_w_a), lru_wx_bd=jax.vmap(_block_diag_pairs)(lru_w_x),
        lru_b_a=lru_b_a, lru_b_x=lru_b_x, lru_lambda=lru_lambda,
        cd_w_in=jax.vmap(_prep_cd_w_in)(cd_w_in), cd_w_out=cd_w_out.astype(BF16),
        gdn_conv_w=gdn_conv_w, gdn_pcol=pcol, gdn_prow=prow, gdn_norm_g=gdn_norm_g,
        mla_q_norm_g=mla_q_norm_g, mla_w_qb=jax.vmap(_prep_wqb)(mla_w_qb),
        mla_kv_norm_g=mla_kv_norm_g, mla_w_kvb=mla_w_kvb.astype(BF16),
        ffn_wu=ffn_w_up.astype(BF16), ffn_wd=ffn_w_down.astype(BF16), ffn_cw=ffn_conv_w,
    )
    return W


def kernel(x_prompt, x_sample, c_prompt, c_sample, state_hgrn, state_rglru, state_rglru_conv, state_gdn, state_gdn_conv, cache_mla_latent, cache_mla_krope, state_ffn_conv, ada_w, ada_b, norm_g, ab_w_in, ab_w_out, hgrn_lb_logits, hgrn_norm_g, lru_conv_w, lru_conv_b, lru_w_a, lru_b_a, lru_w_x, lru_b_x, lru_lambda, cd_w_in, cd_w_out, gdn_conv_w, gdn_a_log, gdn_dt_bias, gdn_norm_g, mla_q_norm_g, mla_w_qb, mla_kv_norm_g, mla_w_kvb, ffn_w_up, ffn_conv_w, ffn_w_down):
    bp, bs = x_prompt.shape[0], x_sample.shape[0]
    W = _prep_weights(norm_g, ab_w_in, ab_w_out, hgrn_lb_logits, hgrn_norm_g, lru_conv_w, lru_conv_b, lru_w_a, lru_b_a,
                      lru_w_x, lru_b_x, lru_lambda, cd_w_in, cd_w_out, gdn_conv_w, gdn_a_log, gdn_dt_bias,
                      gdn_norm_g, mla_q_norm_g, mla_w_qb, mla_kv_norm_g, mla_w_kvb, ffn_w_up, ffn_conv_w,
                      ffn_w_down)
    rows = bp + bs
    rows_pad = -(-rows // 8) * 8
    c_all = jnp.concatenate([c_prompt, c_sample, jnp.zeros((rows_pad - rows, D_MODEL), F32)], axis=0)
    mods = _ada_call(c_all, ada_w, ada_b)
    dt_ = x_prompt.dtype
    y_prompt, p_states = _run_group(
        x_prompt, mods[:, :bp],
        jnp.zeros((N_AB, bp, HEADS, HEAD_DIM, HEAD_DIM), F32),
        jnp.zeros((N_AB, bp, HALF), F32),
        jnp.zeros((N_AB, bp, LRU_CONV - 1, HALF), dt_),
        jnp.zeros((N_CD, bp, HEADS, HEAD_DIM, HEAD_DIM), F32),
        jnp.zeros((N_CD, bp, GD_CONV - 1, 3 * HALF), dt_),
        jnp.zeros((N_CD, bp, 0, MLA_KV_RANK), dt_),
        jnp.zeros((N_CD, bp, 0, MLA_ROPE), dt_),
        jnp.zeros((DEPTH, bp, FFN_CONV - 1, 2 * D_FF), dt_),
        W)
    y_sample, s_states = _run_group(
        x_sample, mods[:, bp:rows], state_hgrn, state_rglru, state_rglru_conv, state_gdn, state_gdn_conv,
        cache_mla_latent, cache_mla_krope, state_ffn_conv, W)
    return (y_prompt, y_sample) + tuple(p_states) + tuple(s_states)
```

```python
import functools
import math

import jax
import jax.numpy as jnp
from jax import lax
from jax.experimental import pallas as pl
from jax.experimental.pallas import tpu as pltpu

F32 = jnp.float32
BF16 = jnp.bfloat16

D_MODEL = 1024
DEPTH = 4
CHUNK = 64
HALF = D_MODEL // 2
N_AB = (DEPTH + 1) // 2
N_CD = DEPTH // 2
HEADS = 4
HEAD_DIM = HALF // HEADS
LRU_BLOCKS = 8
LRU_BLOCK = HALF // LRU_BLOCKS
LRU_CONV = 4
LRU_C = 8.0
LRU_GROUP = 8
GD_CONV = 4
MLA_NOPE = 128
MLA_ROPE = 64
MLA_QK = MLA_NOPE + MLA_ROPE
MLA_Q_RANK = 384
MLA_KV_RANK = 256
MLA_SCALE = (MLA_NOPE + MLA_ROPE) ** -0.5
ROPE_THETA = 10000.0
D_FF = 2816
FFN_CONV = 3
FF_TILE = 256
N_FF_TILES = D_FF // FF_TILE
FF_GROUP = 4
FF_SLOTS = 4
FF_AHEAD = 2
EPS = 1e-6
NEG_BIG = -1e30
SQRT_FLOOR = 1e-12
CD_COLS = 2816
SMALL_KR = 0
SMALL_GB = 64
SMALL_GA = 68

VMEM_LIMIT_BYTES = 56 * 1024 * 1024


def _cparams(*sem):
    return pltpu.CompilerParams(dimension_semantics=sem, vmem_limit_bytes=VMEM_LIMIT_BYTES)


def _dot(a, b):
    return jnp.dot(a, b, preferred_element_type=F32)


def _dot_nt(a, b):
    return lax.dot_general(a, b, (((1,), (1,)), ((), ())), preferred_element_type=F32)


def _dot_tn(a, b):
    return lax.dot_general(a, b, (((0,), (0,)), ((), ())), preferred_element_type=F32)


def _rms(x, g):
    return x * lax.rsqrt(jnp.mean(x * x, axis=-1, keepdims=True) + EPS) * g


def _silu(x):
    return x * jax.nn.sigmoid(x)


def _softplus(x):
    return jnp.maximum(x, 0.0) + jnp.log1p(jnp.exp(-jnp.abs(x)))


def _gelu_tanh(x):
    return 0.5 * x * (1.0 + jnp.tanh(math.sqrt(2.0 / math.pi) * (x + 0.044715 * (x * x * x))))


def _split3(x):
    x1 = x.astype(BF16)
    r1 = x - x1.astype(F32)
    x2 = r1.astype(BF16)
    x3 = (r1 - x2.astype(F32)).astype(BF16)
    return x1, x2, x3


def _log2(n):
    assert n & (n - 1) == 0
    return n.bit_length() - 1


def _cumsum_rows(x, row):
    s = 1
    while s < x.shape[0]:
        x = x + jnp.where(row >= s, pltpu.roll(x, s, 0), 0.0)
        s *= 2
    return x


def _block_row_bcast(b, row, h):
    L, n = b.shape
    blk = 2 * h
    if blk >= 8:
        b3 = b.reshape(L // blk, blk, n)
        return jnp.broadcast_to(b3[:, h - 1:h, :], (L // blk, blk, n)).reshape(L, n)
    pos = row & (blk - 1)
    x0 = jnp.where(pos == h - 1, b, 0.0)
    out = x0
    for j in range(1, h + 1):
        out = out + pltpu.roll(x0, j, 0)
    for j in range(1, h):
        out = out + pltpu.roll(x0, L - j, 0)
    return out


def _ada_kernel(c_ref, w_ref, b_ref, o_ref):
    c = _silu(c_ref[...]).astype(BF16)
    o_ref[0] = _dot(c, w_ref[0].astype(BF16)) + b_ref[0]


def _ada_call(c_all, ada_w, ada_b):
    rows = c_all.shape[0]
    tn = 2048
    return pl.pallas_call(
        _ada_kernel,
        out_shape=jax.ShapeDtypeStruct((DEPTH, rows, 6 * D_MODEL), F32),
        grid=(DEPTH, 6 * D_MODEL // tn),
        in_specs=[
            pl.BlockSpec((rows, D_MODEL), lambda l, j: (0, 0)),
            pl.BlockSpec((1, D_MODEL, tn), lambda l, j: (l, 0, j)),
            pl.BlockSpec((1, 1, tn), lambda l, j: (l, 0, j)),
        ],
        out_specs=pl.BlockSpec((1, rows, tn), lambda l, j: (l, 0, j)),
        compiler_params=_cparams("arbitrary", "arbitrary"),
        name="ada_mod",
    )(c_all, ada_w, ada_b.reshape(DEPTH, 1, 6 * D_MODEL))


def _nmm_kernel(x_ref, g_ref, sc_ref, sh_ref, w_ref, o_ref, *maybe_t_ref):
    nb, tt, d = x_ref.shape
    h = _rms(x_ref[...], g_ref[...]) * (1.0 + sc_ref[...]) + sh_ref[...]
    y = _dot(h.reshape(nb * tt, d).astype(BF16), w_ref[...])
    o_ref[...] = y.reshape(nb, tt, y.shape[-1])
    if maybe_t_ref:
        (t_ref,) = maybe_t_ref
        yt = y[:, y.shape[-1] - 128:].T
        t_ref[0] = yt[SMALL_GB:SMALL_GB + 2 * HEADS, :]


def _tiles(B, T):
    if T >= 512:
        return 1, 512
    assert B * T <= 512
    return B, T


def _nmm_call(x, g, scale, shift, w, gates_t=False):
    B, T, D = x.shape
    N = w.shape[1]
    nb, tt = _tiles(B, T)
    out_shape = jax.ShapeDtypeStruct((B, T, N), F32)
    out_specs = pl.BlockSpec((nb, tt, N), lambda b, i: (b, i, 0))
    if gates_t:
        assert nb == 1 and tt % 128 == 0
        out_shape = (out_shape, jax.ShapeDtypeStruct((B, 2 * HEADS, T), F32))
        out_specs = (out_specs, pl.BlockSpec((1, 2 * HEADS, tt), lambda b, i: (b, 0, i)))
    return pl.pallas_call(
        _nmm_kernel,
        out_shape=out_shape,
        grid=(B // nb, T // tt),
        in_specs=[
            pl.BlockSpec((nb, tt, D), lambda b, i: (b, i, 0)),
            pl.BlockSpec((1, D), lambda b, i: (0, 0)),
            pl.BlockSpec((nb, 1, D), lambda b, i: (b, 0, 0)),
            pl.BlockSpec((nb, 1, D), lambda b, i: (b, 0, 0)),
            pl.BlockSpec((D, N), lambda b, i: (0, 0)),
        ],
        out_specs=out_specs,
        compiler_params=_cparams("arbitrary", "arbitrary"),
        name="norm_mod_proj",
    )(x, g.reshape(1, D), scale[:, None, :], shift[:, None, :], w)


def _out_kernel(x_ref, oa_ref, ob_ref, w_ref, g_ref, gate_ref, o_ref):
    nb, tt, d = x_ref.shape
    oa = oa_ref[...].reshape(nb * tt, HALF)
    ob = ob_ref[...].reshape(nb * tt, HALF)
    y = _dot(oa, w_ref[0:HALF, :]) + _dot(ob, w_ref[HALF:2 * HALF, :])
    y = _rms(y, g_ref[...]).reshape(nb, tt, d)
    o_ref[...] = x_ref[...] + gate_ref[...] * y


def _out_call(x, oa, ob, w, g, gate):
    B, T, D = x.shape
    nb, tt = _tiles(B, T)
    return pl.pallas_call(
        _out_kernel,
        out_shape=jax.ShapeDtypeStruct((B, T, D), F32),
        grid=(B // nb, T // tt),
        in_specs=[
            pl.BlockSpec((nb, tt, D), lambda b, i: (b, i, 0)),
            pl.BlockSpec((nb, tt, HALF), lambda b, i: (b, i, 0)),
            pl.BlockSpec((nb, tt, HALF), lambda b, i: (b, i, 0)),
            pl.BlockSpec((D, D), lambda b, i: (0, 0)),
            pl.BlockSpec((1, D), lambda b, i: (0, 0)),
            pl.BlockSpec((nb, 1, D), lambda b, i: (b, 0, 0)),
        ],
        out_specs=pl.BlockSpec((nb, tt, D), lambda b, i: (b, i, 0)),
        compiler_params=_cparams("arbitrary", "arbitrary"),
        name="out_proj_residual",
    )(x, oa, ob, w, g.reshape(1, D), gate[:, None, :])


def _ffn_kernel(x_ref, g1_ref, sc_ref, sh_ref, gate_ref, g2_ref, buf0_ref, wu_ref, wd_ref,
                cw_ref, o_ref, st_ref, carry_ref, ubuf_ref):
    nb, tt, d = x_ref.shape
    i = pl.program_id(1)

    @pl.when(i == 0)
    def _():
        for c in range(2 * N_FF_TILES):
            carry_ref[c] = buf0_ref[:, :, c * FF_TILE:(c + 1) * FF_TILE]

    x = x_ref[...]
    h = (_rms(x, g1_ref[...]) * (1.0 + sc_ref[...]) + sh_ref[...]).reshape(nb * tt, d).astype(BF16)

    def conv(u, slot, c):
        ubuf_ref[slot, :, 8:8 + tt, :] = u
        ubuf_ref[slot, :, 6:8, :] = carry_ref[c]
        cw = cw_ref[:, c * FF_TILE:(c + 1) * FF_TILE]
        y = (cw[0:1, :] * ubuf_ref[slot, :, 6:6 + tt, :] + cw[1:2, :] * ubuf_ref[slot, :, 7:7 + tt, :]
             + cw[2:3, :] * u)
        tail = ubuf_ref[slot, :, 6 + tt:8 + tt, :]
        carry_ref[c] = tail
        st_ref[:, :, c * FF_TILE:(c + 1) * FF_TILE] = tail
        return y

    def up_proj(c):
        wg = wu_ref[:, c * FF_TILE:(c + 1) * FF_TILE]
        wv = wu_ref[:, D_FF + c * FF_TILE:D_FF + (c + 1) * FF_TILE]
        return _dot(h, wg).reshape(nb, tt, FF_TILE), _dot(h, wv).reshape(nb, tt, FF_TILE)

    acc = None
    ahead = [up_proj(c) for c in range(min(FF_AHEAD, N_FF_TILES))]
    for g0 in range(0, N_FF_TILES, FF_GROUP):
        acts = []
        for c in range(g0, min(g0 + FF_GROUP, N_FF_TILES)):
            ug, uv = ahead.pop(0)
            if c + FF_AHEAD < N_FF_TILES:
                ahead.append(up_proj(c + FF_AHEAD))
            slot = 2 * (c % FF_SLOTS)
            yg = conv(ug, slot, c)
            yv = conv(uv, slot + 1, N_FF_TILES + c)
            acts.append((_silu(yg) * yv).reshape(nb * tt, FF_TILE).astype(BF16))
        a = jnp.concatenate(acts, axis=-1) if len(acts) > 1 else acts[0]
        part = _dot(a, wd_ref[g0 * FF_TILE:g0 * FF_TILE + a.shape[-1], :])
        acc = part if acc is None else acc + part
    y = _rms(acc, g2_ref[...]).reshape(nb, tt, d)
    o_ref[...] = x + gate_ref[...] * y


def _ffn_call(x, g1, scale, shift, gate, g2, buf0, wu, wd, cw):
    B, T, D = x.shape
    nb, tt = _tiles(B, T)
    assert T >= FFN_CONV - 1
    return pl.pallas_call(
        _ffn_kernel,
        out_shape=(jax.ShapeDtypeStruct((B, T, D), F32),
                   jax.ShapeDtypeStruct((B, FFN_CONV - 1, 2 * D_FF), F32)),
        grid=(B // nb, T // tt),
        in_specs=[
            pl.BlockSpec((nb, tt, D), lambda b, i: (b, i, 0)),
            pl.BlockSpec((1, D), lambda b, i: (0, 0)),
            pl.BlockSpec((nb, 1, D), lambda b, i: (b, 0, 0)),
            pl.BlockSpec((nb, 1, D), lambda b, i: (b, 0, 0)),
            pl.BlockSpec((nb, 1, D), lambda b, i: (b, 0, 0)),
            pl.BlockSpec((1, D), lambda b, i: (0, 0)),
            pl.BlockSpec((nb, FFN_CONV - 1, 2 * D_FF), lambda b, i: (b, 0, 0)),
            pl.BlockSpec((D, 2 * D_FF), lambda b, i: (0, 0), pipeline_mode=pl.Buffered(1)),
            pl.BlockSpec((D_FF, D), lambda b, i: (0, 0), pipeline_mode=pl.Buffered(1)),
            pl.BlockSpec((FFN_CONV, 2 * D_FF), lambda b, i: (0, 0)),
        ],
        out_specs=(pl.BlockSpec((nb, tt, D), lambda b, i: (b, i, 0)),
                   pl.BlockSpec((nb, FFN_CONV - 1, 2 * D_FF), lambda b, i: (b, 0, 0))),
        scratch_shapes=[
            pltpu.VMEM((2 * N_FF_TILES, nb, FFN_CONV - 1, FF_TILE), F32),
            pltpu.VMEM((2 * FF_SLOTS, nb, 8 + tt, FF_TILE), F32),
        ],
        compiler_params=_cparams("arbitrary", "arbitrary"),
        name="conv_ffn",
    )(x, g1.reshape(1, D), scale[:, None, :], shift[:, None, :], gate[:, None, :], g2.reshape(1, D),
      buf0, wu, wd, cw)


def _hgrn_kernel(hq_ref, hf_ref, hi_ref, hz_ref, lb_ref, ng_ref, s0_ref, o_ref, st_ref):
    L = hq_ref.shape[1]
    i = pl.program_id(1)

    @pl.when(i == 0)
    def _():
        st_ref[...] = s0_ref[...]

    row = lax.broadcasted_iota(jnp.int32, (L, HEAD_DIM), 0)
    r2 = lax.broadcasted_iota(jnp.int32, (L, L), 0)
    c2 = lax.broadcasted_iota(jnp.int32, (L, L), 1)
    heads = range(HEADS)
    sls = [slice(hd * HEAD_DIM, (hd + 1) * HEAD_DIM) for hd in heads]
    q, k, v, b = ([None] * HEADS for _ in range(4))
    for hd in heads:
        z = hf_ref[0, :, sls[hd]]
        lb = lb_ref[:, sls[hd]]
        g = jnp.log(lb + (1.0 - lb) * jax.nn.sigmoid(z))
        k[hd] = (1.0 - lb) * jax.nn.sigmoid(-z)
        q[hd] = _silu(hq_ref[0, :, sls[hd]])
        v[hd] = hi_ref[0, :, sls[hd]].astype(BF16)
        b[hd] = _cumsum_rows(g, row)
    att = [jnp.where(r2 == c2, _dot_nt(q[hd].astype(BF16), k[hd].astype(BF16)), 0.0) for hd in heads]
    h = L // 2
    while h >= 1:
        upper = (row & (2 * h - 1)) >= h
        sh = _log2(2 * h)
        same_block = (r2 >> sh) == (c2 >> sh)
        for hd in heads:
            r = _block_row_bcast(b[hd], row, h)
            e = jnp.exp(jnp.where(upper, b[hd] - r, r - b[hd]))
            qt = jnp.where(upper, q[hd] * e, 0.0).astype(BF16)
            kt = jnp.where(upper, 0.0, k[hd] * e).astype(BF16)
            att[hd] = att[hd] + jnp.where(same_block, _dot_nt(qt, kt), 0.0)
        h //= 2
    for hd in heads:
        st = st_ref[0, hd]
        o = _dot(att[hd].astype(BF16), v[hd]) + _dot_nt((q[hd] * jnp.exp(b[hd])).astype(BF16), st.astype(BF16))
        b_last = b[hd][L - 1:L, :]
        kd = (k[hd] * jnp.exp(b_last - b[hd])).astype(BF16)
        st_ref[0, hd] = jnp.exp(b_last) * st + _dot_tn(v[hd], kd)
        o = _rms(o, ng_ref[...]) * _silu(hz_ref[0, :, sls[hd]])
        o_ref[0, :, sls[hd]] = o.astype(BF16)


def _hgrn_call(proj, lb, ng, s0t):
    B, T, _ = proj.shape
    L = 128 if T % 128 == 0 else T
    assert T % L == 0 and L & (L - 1) == 0 and L >= 8
    col = lambda c: pl.BlockSpec((1, L, HALF), lambda b, i, c=c: (b, i, c))
    return pl.pallas_call(
        _hgrn_kernel,
        out_shape=(jax.ShapeDtypeStruct((B, T, HALF), BF16),
                   jax.ShapeDtypeStruct((B, HEADS, HEAD_DIM, HEAD_DIM), F32)),
        grid=(B, T // L),
        in_specs=[col(0), col(1), col(2), col(3),
                  pl.BlockSpec((1, HALF), lambda b, i: (0, 0)),
                  pl.BlockSpec((1, HEAD_DIM), lambda b, i: (0, 0)),
                  pl.BlockSpec((1, HEADS, HEAD_DIM, HEAD_DIM), lambda b, i: (b, 0, 0, 0))],
        out_specs=(pl.BlockSpec((1, L, HALF), lambda b, i: (b, i, 0)),
                   pl.BlockSpec((1, HEADS, HEAD_DIM, HEAD_DIM), lambda b, i: (b, 0, 0, 0))),
        compiler_params=_cparams("arbitrary", "arbitrary"),
        name="hgrn2",
    )(proj, proj, proj, proj, lb.reshape(1, HALF), ng.reshape(1, HEAD_DIM), s0t)


def _lru_kernel(lx_ref, ly_ref, buf0_ref, h0_ref, cw_ref, cb_ref, wa_ref, wx_ref, ba_ref, bx_ref,
                lam_ref, o_ref, hl_ref, bufo_ref, xp_ref):
    tt = lx_ref.shape[1]
    i = pl.program_id(1)
    npad = LRU_CONV - 1

    @pl.when(i == 0)
    def _():
        xp_ref[8 - npad:8, :] = buf0_ref[0]
        hl_ref[0] = h0_ref[0]

    x = lx_ref[0]
    xp_ref[8:8 + tt, :] = x
    xc = cb_ref[...] + cw_ref[npad:npad + 1, :] * x
    for tap in range(npad):
        xc = xc + cw_ref[tap:tap + 1, :] * xp_ref[8 - npad + tap:8 - npad + tap + tt, :]
    tail = xp_ref[8 + tt - npad:8 + tt, :]
    xp_ref[8 - npad:8, :] = tail
    bufo_ref[0] = tail

    xb = xc.astype(BF16)
    half = HALF // 2
    rpre = jnp.concatenate([_dot(xb[:, 0:half], wa_ref[0]), _dot(xb[:, half:HALF], wa_ref[1])], axis=-1)
    ipre = jnp.concatenate([_dot(xb[:, 0:half], wx_ref[0]), _dot(xb[:, half:HALF], wx_ref[1])], axis=-1)
    r = jax.nn.sigmoid(rpre + ba_ref[...])
    ig = jax.nn.sigmoid(ipre + bx_ref[...])
    log_a = -LRU_C * r * _softplus(-lam_ref[...])
    a = jnp.exp(log_a)
    u = jnp.sqrt(jnp.maximum(-jnp.tanh(log_a) * (1.0 + a * a), SQRT_FLOOR)) * ig * xc

    pos = lax.broadcasted_iota(jnp.int32, (tt, HALF), 0) & (LRU_GROUP - 1)
    s = 1
    while s < LRU_GROUP:
        keep = pos >= s
        a_sh = jnp.where(keep, pltpu.roll(a, s, 0), 1.0)
        u_sh = jnp.where(keep, pltpu.roll(u, s, 0), 0.0)
        u = a * u_sh + u
        a = a * a_sh
        s *= 2
    carry = hl_ref[0]
    groups = []
    for gi in range(tt // LRU_GROUP):
        rs = slice(gi * LRU_GROUP, (gi + 1) * LRU_GROUP)
        hg = u[rs] + a[rs] * carry
        carry = hg[LRU_GROUP - 1:LRU_GROUP, :]
        groups.append(hg)
    hseq = jnp.concatenate(groups, axis=0) if len(groups) > 1 else groups[0]
    hl_ref[0] = carry
    o_ref[0] = (hseq * _gelu_tanh(ly_ref[0])).astype(BF16)


def _lru_call(proj, buf0, h0, cw, cb, wa_bd, wx_bd, ba, bx, lam):
    B, T, _ = proj.shape
    tt = 256 if T % 256 == 0 else T
    assert T % tt == 0 and T >= LRU_CONV - 1 and tt % 8 == 0
    vec = pl.BlockSpec((1, HALF), lambda b, i: (0, 0))
    wspec = pl.BlockSpec((2, HALF // 2, HALF // 2), lambda b, i: (0, 0, 0))
    return pl.pallas_call(
        _lru_kernel,
        out_shape=(jax.ShapeDtypeStruct((B, T, HALF), BF16),
                   jax.ShapeDtypeStruct((B, 1, HALF), F32),
                   jax.ShapeDtypeStruct((B, LRU_CONV - 1, HALF), F32)),
        grid=(B, T // tt),
        in_specs=[pl.BlockSpec((1, tt, HALF), lambda b, i: (b, i, 4)),
                  pl.BlockSpec((1, tt, HALF), lambda b, i: (b, i, 5)),
                  pl.BlockSpec((1, LRU_CONV - 1, HALF), lambda b, i: (b, 0, 0)),
                  pl.BlockSpec((1, 1, HALF), lambda b, i: (b, 0, 0)),
                  pl.BlockSpec((LRU_CONV, HALF), lambda b, i: (0, 0)),
                  vec, wspec, wspec, vec, vec, vec],
        out_specs=(pl.BlockSpec((1, tt, HALF), lambda b, i: (b, i, 0)),
                   pl.BlockSpec((1, 1, HALF), lambda b, i: (b, 0, 0)),
                   pl.BlockSpec((1, LRU_CONV - 1, HALF), lambda b, i: (b, 0, 0))),
        scratch_shapes=[pltpu.VMEM((8 + tt, HALF), F32)],
        compiler_params=_cparams("arbitrary", "arbitrary"),
        name="rglru",
    )(proj, proj, buf0, h0[:, None, :], cw, cb.reshape(1, HALF), wa_bd, wx_bd,
      ba.reshape(1, HALF), bx.reshape(1, HALF), lam.reshape(1, HALF))


def _gdn_kernel(qkv_ref, gz_ref, sm_ref, smt_ref, buf0_ref, cw_ref, pcol_ref, prow_ref, ng_ref, s0_ref,
                o_ref, st_ref, bufo_ref, xp_ref, *, L):
    tt = qkv_ref.shape[1]
    nc = tt // L
    sh = _log2(L)
    i = pl.program_id(1)
    npad = GD_CONV - 1

    @pl.when(i == 0)
    def _():
        xp_ref[8 - npad:8, :] = buf0_ref[0]
        st_ref[...] = s0_ref[...]

    x = qkv_ref[0]
    xp_ref[8:8 + tt, :] = x
    xc = cw_ref[npad:npad + 1, :] * x
    for tap in range(npad):
        xc = xc + cw_ref[tap:tap + 1, :] * xp_ref[8 - npad + tap:8 - npad + tap + tt, :]
    tail = xp_ref[8 + tt - npad:8 + tt, :]
    xp_ref[8 - npad:8, :] = tail
    bufo_ref[0] = tail
    xc = _silu(xc)

    r2 = lax.broadcasted_iota(jnp.int32, (tt, tt), 0)
    c2 = lax.broadcasted_iota(jnp.int32, (tt, tt), 1)
    same = (r2 >> sh) == (c2 >> sh)
    incl = jnp.logical_and(same, c2 <= r2)
    strict = jnp.logical_and(same, c2 < r2)
    tri_lo = jnp.where(incl, 1.0, 0.0).astype(BF16)
    tri_up = jnp.where(jnp.logical_and(same, r2 <= c2), 1.0, 0.0).astype(BF16)
    eye = jnp.where(r2 == c2, 1.0, 0.0)

    sm = sm_ref[0]
    beta_cols = jax.nn.sigmoid(sm)
    la_cols = -jnp.exp(pcol_ref[0:1, :]) * _softplus(sm + pcol_ref[1:2, :])
    c1, c2_, c3 = _split3(la_cols)
    g_cols = _dot(tri_lo, c1) + _dot(tri_lo, c2_) + _dot(tri_lo, c3)
    la_rows = -jnp.exp(prow_ref[:, 0:1]) * _softplus(smt_ref[0] + prow_ref[:, 1:2])
    w1, w2, w3 = _split3(la_rows)
    g_rows = _dot(w1, tri_up) + _dot(w2, tri_up) + _dot(w3, tri_up)

    heads = range(HEADS)
    q, k, v, beta, gcol, kb, dec, m = ([None] * HEADS for _ in range(8))
    for hd in heads:
        qh = xc[:, hd * HEAD_DIM:(hd + 1) * HEAD_DIM]
        kh = xc[:, HALF + hd * HEAD_DIM:HALF + (hd + 1) * HEAD_DIM]
        v[hd] = xc[:, 2 * HALF + hd * HEAD_DIM:2 * HALF + (hd + 1) * HEAD_DIM]
        q[hd] = qh * lax.rsqrt(jnp.sum(qh * qh, axis=-1, keepdims=True) + EPS) * (HEAD_DIM ** -0.5)
        k[hd] = kh * lax.rsqrt(jnp.sum(kh * kh, axis=-1, keepdims=True) + EPS)
        beta[hd] = beta_cols[:, SMALL_GB + hd:SMALL_GB + hd + 1]
        gcol[hd] = g_cols[:, SMALL_GA + hd:SMALL_GA + hd + 1]
        grow = g_rows[HEADS + hd:HEADS + hd + 1, :]
        kb[hd] = k[hd].astype(BF16)
        dec[hd] = jnp.exp(jnp.where(incl, gcol[hd] - grow, NEG_BIG))
        m[hd] = beta[hd] * _dot_nt(kb[hd], kb[hd]) * jnp.where(strict, dec[hd], 0.0)

    pair = (r2 >> 1) == (c2 >> 1)
    tinv = [eye - jnp.where(pair, m[hd], 0.0) for hd in heads]
    s = 2
    while s < L:
        ssh = _log2(s)
        lower_left = jnp.logical_and((r2 >> (ssh + 1)) == (c2 >> (ssh + 1)), (r2 >> ssh) != (c2 >> ssh))
        tb = [tinv[hd].astype(BF16) for hd in heads]
        tc = [_dot(tb[hd], jnp.where(lower_left, m[hd], 0.0).astype(BF16)).astype(BF16) for hd in heads]
        tinv = [tinv[hd] - _dot(tc[hd], tb[hd]) for hd in heads]
        s *= 2

    eg = [jnp.exp(gcol[hd]) for hd in heads]
    sol = [_dot(tinv[hd].astype(BF16),
                jnp.concatenate([beta[hd] * v[hd], (beta[hd] * eg[hd]) * k[hd]], axis=-1).astype(BF16))
           for hd in heads]
    u_v = [sol[hd][:, 0:HEAD_DIM] for hd in heads]
    w_k = [sol[hd][:, HEAD_DIM:2 * HEAD_DIM].astype(BF16) for hd in heads]
    qb = [q[hd].astype(BF16) for hd in heads]
    qk = [(_dot_nt(qb[hd], kb[hd]) * dec[hd]).astype(BF16) for hd in heads]

    a_c, p_c, n_c = ([[None] * nc for _ in heads] for _ in range(3))
    for c in range(nc):
        rs = slice(c * L, (c + 1) * L)
        for hd in heads:
            g_last = gcol[hd][(c + 1) * L - 1:(c + 1) * L, :]
            kd = (k[hd][rs] * jnp.exp(g_last - gcol[hd][rs])).astype(BF16)
            a_c[hd][c] = jnp.exp(g_last)
            p_c[hd][c] = (-_dot_tn(kd, w_k[hd][rs])).astype(BF16)
            n_c[hd][c] = _dot_tn(kd, u_v[hd][rs].astype(BF16))
    S = [[st_ref[0, hd]] for hd in heads]
    for c in range(nc):
        for hd in heads:
            s_cur = S[hd][c]
            S[hd].append(a_c[hd][c] * s_cur + _dot(p_c[hd][c], s_cur.astype(BF16)) + n_c[hd][c])
    us = [[] for _ in heads]
    inters = [[] for _ in heads]
    for c in range(nc):
        rs = slice(c * L, (c + 1) * L)
        for hd in heads:
            Sb = S[hd][c].astype(BF16)
            us[hd].append(u_v[hd][rs] - _dot(w_k[hd][rs], Sb))
            inters[hd].append(eg[hd][rs] * _dot(qb[hd][rs], Sb))
    for hd in heads:
        sl = slice(hd * HEAD_DIM, (hd + 1) * HEAD_DIM)
        st_ref[0, hd] = S[hd][nc]
        u_all = jnp.concatenate(us[hd], axis=0) if nc > 1 else us[hd][0]
        inter = jnp.concatenate(inters[hd], axis=0) if nc > 1 else inters[hd][0]
        o = _dot(qk[hd], u_all.astype(BF16)) + inter
        o = _rms(o, ng_ref[...]) * _silu(gz_ref[0, :, sl])
        o_ref[0, :, sl] = o.astype(BF16)


def _gdn_call(proj, small_t, buf0, cw, pcol, prow, ng, s0):
    B, T, _ = proj.shape
    L = CHUNK if T % CHUNK == 0 else T
    tt = 256 if T % 256 == 0 else T
    assert T % tt == 0 and tt % L == 0 and T >= GD_CONV - 1 and L >= 2
    W = 3 * HALF
    return pl.pallas_call(
        functools.partial(_gdn_kernel, L=L),
        out_shape=(jax.ShapeDtypeStruct((B, T, HALF), BF16),
                   jax.ShapeDtypeStruct((B, HEADS, HEAD_DIM, HEAD_DIM), F32),
                   jax.ShapeDtypeStruct((B, GD_CONV - 1, W), F32)),
        grid=(B, T // tt),
        in_specs=[pl.BlockSpec((1, tt, W), lambda b, i: (b, i, 0)),
                  pl.BlockSpec((1, tt, HALF), lambda b, i: (b, i, 3)),
                  pl.BlockSpec((1, tt, 128), lambda b, i: (b, i, 21)),
                  pl.BlockSpec((1, 2 * HEADS, tt), lambda b, i: (b, 0, i)),
                  pl.BlockSpec((1, GD_CONV - 1, W), lambda b, i: (b, 0, 0)),
                  pl.BlockSpec((GD_CONV, W), lambda b, i: (0, 0)),
                  pl.BlockSpec((2, 128), lambda b, i: (0, 0)),
                  pl.BlockSpec((2 * HEADS, 2), lambda b, i: (0, 0)),
                  pl.BlockSpec((1, HEAD_DIM), lambda b, i: (0, 0)),
                  pl.BlockSpec((1, HEADS, HEAD_DIM, HEAD_DIM), lambda b, i: (b, 0, 0, 0))],
        out_specs=(pl.BlockSpec((1, tt, HALF), lambda b, i: (b, i, 0)),
                   pl.BlockSpec((1, HEADS, HEAD_DIM, HEAD_DIM), lambda b, i: (b, 0, 0, 0)),
                   pl.BlockSpec((1, GD_CONV - 1, W), lambda b, i: (b, 0, 0))),
        scratch_shapes=[pltpu.VMEM((8 + tt, W), F32)],
        compiler_params=_cparams("arbitrary", "arbitrary"),
        name="gated_deltanet",
    )(proj, proj, proj, small_t, buf0, cw, pcol, prow, ng.reshape(1, HEAD_DIM), s0)


def _rope64(x, cs, sn):
    half = MLA_ROPE // 2
    swapped = jnp.concatenate([x[:, half:], x[:, :half]], axis=-1)
    return x * cs + swapped * sn


def _expand_kv(c_kv, k_r, wkvb_ref, kf_ref, v_ref):
    kv = _dot(c_kv.astype(BF16), wkvb_ref[...])
    for hd in range(HEADS):
        base = hd * 2 * HEAD_DIM
        kf_ref[0, hd] = jnp.concatenate([kv[:, base:base + MLA_NOPE], k_r], axis=-1).astype(BF16)
        v = kv[:, base + MLA_NOPE:base + 2 * HEAD_DIM]
        v_ref[0, hd] = jnp.concatenate([v, jnp.ones_like(v)], axis=-1).astype(BF16)


def _mla_prep_kernel(qa_ref, kva_ref, sm_ref, cs_ref, sn_ref, qng_ref, wqb_ref, kvng_ref, wkvb_ref, *rest):
    q_ref, ckv_ref, kr_ref, kf_ref, v_ref = rest[-5:]
    ckv_ref = ckv_ref.at[0]
    kr_ref = kr_ref.at[0]
    cs = cs_ref[...]
    sn = sn_ref[...]
    qn = _rms(qa_ref[0], qng_ref[...]).astype(BF16)
    qh = _dot(qn, wqb_ref[...]) * (MLA_SCALE * math.log2(math.e))
    for hd in range(HEADS):
        nope = qh[:, hd * MLA_NOPE:(hd + 1) * MLA_NOPE]
        off = HEADS * MLA_NOPE + hd * MLA_ROPE
        rot = _rope64(qh[:, off:off + MLA_ROPE], cs, sn)
        q_ref[0, hd] = jnp.concatenate([nope, rot], axis=-1).astype(BF16)
    c_kv = _rms(kva_ref[0], kvng_ref[...])
    ckv_ref[0] = c_kv
    k_r = _rope64(sm_ref[0, :, SMALL_KR:SMALL_KR + MLA_ROPE], cs, sn)
    kr_ref[0] = k_r
    _expand_kv(c_kv, k_r, wkvb_ref, kf_ref, v_ref)


def _mla_prep_call(proj, cs, sn, qng, wqb, kvng, wkvb, j, state_bufs, past_bufs):
    B, T, _ = proj.shape
    tt = 512 if T % 512 == 0 else T
    c2 = lambda b, i: (0, 0)
    past_len = 0 if past_bufs is None else past_bufs[0].shape[2] - T
    assert past_len % tt == 0
    off = past_len // tt
    any_spec = pl.BlockSpec(memory_space=pl.ANY)
    extra, aliases = [], {}
    n_fixed = 9
    if state_bufs is not None:
        aliases[n_fixed + len(extra)] = 1
        aliases[n_fixed + len(extra) + 1] = 2
        extra += list(state_bufs)
    if past_bufs is not None:
        aliases[n_fixed + len(extra)] = 3
        aliases[n_fixed + len(extra) + 1] = 4
        extra += list(past_bufs)
    return pl.pallas_call(
        _mla_prep_kernel,
        out_shape=(jax.ShapeDtypeStruct((B, HEADS, T, MLA_QK), BF16),
                   jax.ShapeDtypeStruct((N_CD, B, T, MLA_KV_RANK), F32),
                   jax.ShapeDtypeStruct((N_CD, B, T, MLA_ROPE), F32),
                   jax.ShapeDtypeStruct((B, HEADS, past_len + T, MLA_QK), BF16),
                   jax.ShapeDtypeStruct((B, HEADS, past_len + T, 2 * HEAD_DIM), BF16)),
        grid=(B, T // tt),
        in_specs=[pl.BlockSpec((1, tt, MLA_Q_RANK), lambda b, i: (b, i, 6)),
                  pl.BlockSpec((1, tt, MLA_KV_RANK), lambda b, i: (b, i, 8)),
                  pl.BlockSpec((1, tt, 128), lambda b, i: (b, i, 21)),
                  pl.BlockSpec((tt, MLA_ROPE), lambda b, i: (i, 0)),
                  pl.BlockSpec((tt, MLA_ROPE), lambda b, i: (i, 0)),
                  pl.BlockSpec((1, MLA_Q_RANK), c2),
                  pl.BlockSpec((MLA_Q_RANK, HEADS * MLA_QK), c2),
                  pl.BlockSpec((1, MLA_KV_RANK), c2),
                  pl.BlockSpec((MLA_KV_RANK, HEADS * 2 * HEAD_DIM), c2)] + [any_spec] * len(extra),
        out_specs=(pl.BlockSpec((1, HEADS, tt, MLA_QK), lambda b, i: (b, 0, i, 0)),
                   pl.BlockSpec((1, 1, tt, MLA_KV_RANK), lambda b, i: (j, b, i, 0)),
                   pl.BlockSpec((1, 1, tt, MLA_ROPE), lambda b, i: (j, b, i, 0)),
                   pl.BlockSpec((1, HEADS, tt, MLA_QK), lambda b, i: (b, 0, i + off, 0)),
                   pl.BlockSpec((1, HEADS, tt, 2 * HEAD_DIM), lambda b, i: (b, 0, i + off, 0))),
        input_output_aliases=aliases,
        compiler_params=_cparams("arbitrary", "arbitrary"),
        name="mla_prep",
    )(proj, proj, proj, cs, sn, qng.reshape(1, MLA_Q_RANK), wqb, kvng.reshape(1, MLA_KV_RANK), wkvb, *extra)


def _mla_past_kernel(lat_ref, kr_ref, wkvb_ref, kf_ref, v_ref):
    _expand_kv(lat_ref[0], kr_ref[0], wkvb_ref, kf_ref, v_ref)


def _mla_past_call(lat, kr, wkvb, n_new):
    B, P, _ = lat.shape
    tt = 512 if P % 512 == 0 else P
    return pl.pallas_call(
        _mla_past_kernel,
        out_shape=(jax.ShapeDtypeStruct((B, HEADS, P + n_new, MLA_QK), BF16),
                   jax.ShapeDtypeStruct((B, HEADS, P + n_new, 2 * HEAD_DIM), BF16)),
        grid=(B, P // tt),
        in_specs=[pl.BlockSpec((1, tt, MLA_KV_RANK), lambda b, i: (b, i, 0)),
                  pl.BlockSpec((1, tt, MLA_ROPE), lambda b, i: (b, i, 0)),
                  pl.BlockSpec((MLA_KV_RANK, HEADS * 2 * HEAD_DIM), lambda b, i: (0, 0))],
        out_specs=(pl.BlockSpec((1, HEADS, tt, MLA_QK), lambda b, i: (b, 0, i, 0)),
                   pl.BlockSpec((1, HEADS, tt, 2 * HEAD_DIM), lambda b, i: (b, 0, i, 0))),
        compiler_params=_cparams("arbitrary", "arbitrary"),
        name="mla_past_kv",
    )(lat, kr, wkvb)


def _attn_kernel(q_ref, k_ref, v_ref, o_ref, m_ref, acc_ref, sa_ref, sb_ref, *, past_len, tq, tk, nk):
    i = pl.program_id(2)
    sh = _log2(CHUNK)
    lanes = HEAD_DIM
    m_ref[...] = jnp.full(m_ref.shape, NEG_BIG, F32)
    acc_ref[...] = jnp.zeros(acc_ref.shape, F32)
    q = q_ref[0, 0]
    q_lo = past_len + i * tq
    n_full = jnp.minimum(nk, (((q_lo >> sh) + 1) * CHUNK) // tk)
    n_need = jnp.minimum(nk, ((((q_lo + tq - 1) >> sh) + 1) * CHUNK + tk - 1) // tk)

    def scores(j):
        return _dot_nt(q, k_ref[0, 0, pl.ds(pl.multiple_of(j * tk, tk), tk), :])

    def masked_scores(j):
        k_lo = j * tk
        qc = (q_lo + lax.broadcasted_iota(jnp.int32, (tq, tk), 0)) >> sh
        kc = (k_lo + lax.broadcasted_iota(jnp.int32, (tq, tk), 1)) >> sh
        return jnp.where(kc <= qc, scores(j), NEG_BIG)

    def update(j, s_ref):
        k_lo = pl.multiple_of(j * tk, tk)
        m_prev = m_ref[...]
        m_new = jnp.maximum(m_prev, jnp.max(s_ref[...], axis=-1, keepdims=True))
        alpha = jnp.exp2(m_prev - m_new)
        if tk % lanes == 0:
            p = jnp.exp2(s_ref[...] - jnp.tile(m_new, (1, tk // lanes)))
        else:
            p = jnp.exp2(s_ref[...] - m_new[:, 0:1])
        pv = _dot(p.astype(BF16), v_ref[0, 0, pl.ds(k_lo, tk), :])
        acc_ref[...] = jnp.tile(alpha, (1, 2)) * acc_ref[...] + pv
        m_ref[...] = m_new

    def body_pair(g, carry):
        sb_ref[...] = scores(2 * g + 1)
        update(2 * g, sa_ref)
        sa_ref[...] = scores(2 * g + 2)
        update(2 * g + 1, sb_ref)
        return carry

    sa_ref[...] = scores(0)
    n_pairs = jnp.maximum(n_full - 1, 0) // 2
    lax.fori_loop(0, n_pairs, body_pair, 0)

    left = n_full - 2 * n_pairs
    has_masked = n_need > n_full
    j_masked = jnp.minimum(n_full, nk - 1)

    @pl.when(left == 1)
    def _():
        sb_ref[...] = masked_scores(j_masked)
        update(n_full - 1, sa_ref)

    @pl.when(jnp.logical_and(left == 1, has_masked))
    def _():
        update(n_full, sb_ref)

    @pl.when(left == 2)
    def _():
        sb_ref[...] = scores(n_full - 1)
        update(n_full - 2, sa_ref)
        sa_ref[...] = masked_scores(j_masked)
        update(n_full - 1, sb_ref)

    @pl.when(jnp.logical_and(left == 2, has_masked))
    def _():
        update(n_full, sa_ref)

    def body_masked(j, carry):
        sa_ref[...] = masked_scores(j)
        update(j, sa_ref)
        return carry

    lax.fori_loop(jnp.where(n_full > 0, n_full + 1, 0), n_need, body_masked, 0)
    o_ref[0] = (acc_ref[:, 0:HEAD_DIM] / acc_ref[:, HEAD_DIM:2 * HEAD_DIM]).astype(BF16)


def _attn_call(q, kf, v, past_len):
    B, H, Tq, _ = q.shape
    Tk = kf.shape[2]
    tq = 1024 if Tq % 1024 == 0 else Tq
    tk = 1024 if Tk % 1024 == 0 else Tk
    nk = Tk // tk
    return pl.pallas_call(
        functools.partial(_attn_kernel, past_len=past_len, tq=tq, tk=tk, nk=nk),
        out_shape=jax.ShapeDtypeStruct((B, Tq, H * HEAD_DIM), BF16),
        grid=(B, H, Tq // tq),
        in_specs=[pl.BlockSpec((1, 1, tq, MLA_QK), lambda b, h, i: (b, h, i, 0)),
                  pl.BlockSpec((1, 1, Tk, MLA_QK), lambda b, h, i: (b, h, 0, 0)),
                  pl.BlockSpec((1, 1, Tk, 2 * HEAD_DIM), lambda b, h, i: (b, h, 0, 0))],
        out_specs=pl.BlockSpec((1, tq, HEAD_DIM), lambda b, h, i: (b, i, h)),
        scratch_shapes=[pltpu.VMEM((tq, HEAD_DIM), F32), pltpu.VMEM((tq, 2 * HEAD_DIM), F32),
                        pltpu.VMEM((tq, tk), F32), pltpu.VMEM((tq, tk), F32)],
        compiler_params=_cparams("arbitrary", "arbitrary", "arbitrary"),
        name="mla_attention",
    )(q, kf, v)


def _block_diag_pairs(w):
    per = (HALF // 2) // LRU_BLOCK
    w4 = w.reshape(2, per, LRU_BLOCK, LRU_BLOCK)
    eye = jnp.eye(per, dtype=w.dtype)
    out = w4[:, :, :, None, :] * eye[None, :, None, :, None]
    return out.reshape(2, HALF // 2, HALF // 2).astype(BF16)


def _prep_cd_w_in(w):
    o = 3 * HALF
    qkv, gz = w[:, :o], w[:, o:o + HALF]
    o += HALF
    gb, ga = w[:, o:o + HEADS], w[:, o + HEADS:o + 2 * HEADS]
    o += 2 * HEADS
    qa, kva = w[:, o:o + MLA_Q_RANK], w[:, o + MLA_Q_RANK:o + MLA_Q_RANK + MLA_KV_RANK]
    o += MLA_Q_RANK + MLA_KV_RANK
    kr = w[:, o:o + MLA_ROPE]
    assert (SMALL_KR, SMALL_GB, SMALL_GA) == (0, MLA_ROPE, MLA_ROPE + HEADS)
    fill = jnp.zeros((w.shape[0], 128 - MLA_ROPE - 2 * HEADS), w.dtype)
    out = jnp.concatenate([qkv, gz, kva, qa, kr, gb, ga, fill], axis=-1)
    assert out.shape[1] == CD_COLS
    return out.astype(BF16)


def _prep_wqb(w):
    w4 = w.reshape(MLA_Q_RANK, HEADS, MLA_QK)
    nope = w4[:, :, :MLA_NOPE].reshape(MLA_Q_RANK, HEADS * MLA_NOPE)
    rope = w4[:, :, MLA_NOPE:].reshape(MLA_Q_RANK, HEADS * MLA_ROPE)
    return jnp.concatenate([nope, rope], axis=-1).astype(BF16)


def _rope_tables(T, past_len):
    half = MLA_ROPE // 2
    freqs = jnp.exp(-math.log(ROPE_THETA) * jnp.arange(half, dtype=F32) / half)
    pos = past_len + jnp.arange(T, dtype=jnp.int32)
    ang = pos.astype(F32)[:, None] * freqs
    cos, sin = jnp.cos(ang), jnp.sin(ang)
    return jnp.concatenate([cos, cos], axis=-1), jnp.concatenate([-sin, sin], axis=-1)


def _run_group(x, mods, hg_s, lru_h, lru_buf, gd_s, gd_buf, lat_past, kr_past, ffn_buf, W):
    B, T, _ = x.shape
    n_hg, n_lru, n_lrub, n_gd, n_gdb, n_ffn = ([] for _ in range(6))
    mla_state = None
    for l in range(DEPTH):
        j = l // 2
        shift1, scale1, gate1, shift2, scale2, gate2 = jnp.split(mods[l], 6, axis=-1)
        g = W['norm_g'][l]
        if l % 2 == 0:
            proj = _nmm_call(x, g[0], scale1, shift1, W['ab_w_in'][j])
            o_a, s_hg_t = _hgrn_call(proj, W['lower_bounds'][j], W['hgrn_norm_g'][j],
                                     jnp.swapaxes(hg_s[j], -1, -2))
            o_b, s_lru, s_lrub = _lru_call(proj, lru_buf[j], lru_h[j], W['lru_conv_w'][j], W['lru_conv_b'][j],
                                           W['lru_wa_bd'][j], W['lru_wx_bd'][j], W['lru_b_a'][j],
                                           W['lru_b_x'][j], W['lru_lambda'][j])
            n_hg.append(jnp.swapaxes(s_hg_t, -1, -2))
            n_lru.append(s_lru[:, 0, :])
            n_lrub.append(s_lrub)
            x = _out_call(x, o_a, o_b, W['ab_w_out'][j], g[1], gate1)
        else:
            past_len = lat_past.shape[2]
            if _tiles(B, T)[0] == 1:
                proj, small_t = _nmm_call(x, g[0], scale1, shift1, W['cd_w_in'][j], gates_t=True)
            else:
                proj = _nmm_call(x, g[0], scale1, shift1, W['cd_w_in'][j])
                small_t = jnp.swapaxes(proj[:, :, CD_COLS - 128 + SMALL_GB:CD_COLS - 128 + SMALL_GB + 2 * HEADS], 1, 2)
            o_c, s_gd, s_gdb = _gdn_call(proj, small_t, gd_buf[j], W['gdn_conv_w'][j], W['gdn_pcol'][j],
                                         W['gdn_prow'][j], W['gdn_norm_g'][j], gd_s[j])
            cs, sn = _rope_tables(T, past_len)
            past_bufs = _mla_past_call(lat_past[j], kr_past[j], W['mla_w_kvb'][j], T) if past_len > 0 else None
            q, lat_all, kr_all, kf, v = _mla_prep_call(proj, cs, sn, W['mla_q_norm_g'][j], W['mla_w_qb'][j],
                                                       W['mla_kv_norm_g'][j], W['mla_w_kvb'][j], j, mla_state,
                                                       past_bufs)
            mla_state = (lat_all, kr_all)
            o_d = _attn_call(q, kf, v, past_len)
            n_gd.append(s_gd)
            n_gdb.append(s_gdb)
            x = _out_call(x, o_c, o_d, W['cd_w_out'][j], g[1], gate1)
        x, s_ffn = _ffn_call(x, g[2], scale2, shift2, gate2, g[3], ffn_buf[l], W['ffn_wu'][l], W['ffn_wd'][l],
                             W['ffn_cw'][l])
        n_ffn.append(s_ffn)
    return x, (jnp.stack(n_hg), jnp.stack(n_lru), jnp.stack(n_lrub), jnp.stack(n_gd), jnp.stack(n_gdb),
               mla_state[0], mla_state[1], jnp.stack(n_ffn))


def _prep_weights(norm_g, ab_w_in, ab_w_out, hgrn_lb_logits, hgrn_norm_g, lru_conv_w, lru_conv_b, lru_w_a, lru_b_a,
                  lru_w_x, lru_b_x, lru_lambda, cd_w_in, cd_w_out, gdn_conv_w, gdn_a_log, gdn_dt_bias, gdn_norm_g,
                  mla_q_norm_g, mla_w_qb, mla_kv_norm_g, mla_w_kvb, ffn_w_up, ffn_conv_w, ffn_w_down):
    lb_p = jax.nn.softmax(hgrn_lb_logits.astype(F32), axis=0)
    pcol = jnp.pad(jnp.stack([gdn_a_log, gdn_dt_bias], axis=1),
                   ((0, 0), (0, 0), (SMALL_GA, 128 - SMALL_GA - HEADS)))
    prow = jnp.pad(jnp.stack([gdn_a_log, gdn_dt_bias], axis=-1), ((0, 0), (HEADS, 0), (0, 0)))
    W = dict(
        norm_g=norm_g,
        ab_w_in=ab_w_in.astype(BF16), ab_w_out=ab_w_out.astype(BF16),
        lower_bounds=jnp.cumsum(lb_p, axis=0) - lb_p[0:1],
        hgrn_norm_g=hgrn_norm_g, lru_conv_w=lru_conv_w, lru_conv_b=lru_conv_b,
        lru_wa_bd=jax.vmap(_block_diag_pairs)(lru_w_a), lru_wx_bd=jax.vmap(_block_diag_pairs)(lru_w_x),
        lru_b_a=lru_b_a, lru_b_x=lru_b_x, lru_lambda=lru_lambda,
        cd_w_in=jax.vmap(_prep_cd_w_in)(cd_w_in), cd_w_out=cd_w_out.astype(BF16),
        gdn_conv_w=gdn_conv_w, gdn_pcol=pcol, gdn_prow=prow, gdn_norm_g=gdn_norm_g,
        mla_q_norm_g=mla_q_norm_g, mla_w_qb=jax.vmap(_prep_wqb)(mla_w_qb),
        mla_kv_norm_g=mla_kv_norm_g, mla_w_kvb=mla_w_kvb.astype(BF16),
        ffn_wu=ffn_w_up.astype(BF16), ffn_wd=ffn_w_down.astype(BF16), ffn_cw=ffn_conv_w,
    )
    return W


def kernel(x_prompt, x_sample, c_prompt, c_sample, state_hgrn, state_rglru, state_rglru_conv, state_gdn, state_gdn_conv, cache_mla_latent, cache_mla_krope, state_ffn_conv, ada_w, ada_b, norm_g, ab_w_in, ab_w_out, hgrn_lb_logits, hgrn_norm_g, lru_conv_w, lru_conv_b, lru_w_a, lru_b_a, lru_w_x, lru_b_x, lru_lambda, cd_w_in, cd_w_out, gdn_conv_w, gdn_a_log, gdn_dt_bias, gdn_norm_g, mla_q_norm_g, mla_w_qb, mla_kv_norm_g, mla_w_kvb, ffn_w_up, ffn_conv_w, ffn_w_down):
    bp, bs = x_prompt.shape[0], x_sample.shape[0]
    W = _prep_weights(norm_g, ab_w_in, ab_w_out, hgrn_lb_logits, hgrn_norm_g, lru_conv_w, lru_conv_b, lru_w_a, lru_b_a,
                      lru_w_x, lru_b_x, lru_lambda, cd_w_in, cd_w_out, gdn_conv_w, gdn_a_log, gdn_dt_bias,
                      gdn_norm_g, mla_q_norm_g, mla_w_qb, mla_kv_norm_g, mla_w_kvb, ffn_w_up, ffn_conv_w,
                      ffn_w_down)
    rows = bp + bs
    rows_pad = -(-rows // 8) * 8
    c_all = jnp.concatenate([c_prompt, c_sample, jnp.zeros((rows_pad - rows, D_MODEL), F32)], axis=0)
    mods = _ada_call(c_all, ada_w, ada_b)
    dt_ = x_prompt.dtype
    y_prompt, p_states = _run_group(
        x_prompt, mods[:, :bp],
        jnp.zeros((N_AB, bp, HEADS, HEAD_DIM, HEAD_DIM), F32),
        jnp.zeros((N_AB, bp, HALF), F32),
        jnp.zeros((N_AB, bp, LRU_CONV - 1, HALF), dt_),
        jnp.zeros((N_CD, bp, HEADS, HEAD_DIM, HEAD_DIM), F32),
        jnp.zeros((N_CD, bp, GD_CONV - 1, 3 * HALF), dt_),
        jnp.zeros((N_CD, bp, 0, MLA_KV_RANK), dt_),
        jnp.zeros((N_CD, bp, 0, MLA_ROPE), dt_),
        jnp.zeros((DEPTH, bp, FFN_CONV - 1, 2 * D_FF), dt_),
        W)
    y_sample, s_states = _run_group(
        x_sample, mods[:, bp:rows], state_hgrn, state_rglru, state_rglru_conv, state_gdn, state_gdn_conv,
        cache_mla_latent, cache_mla_krope, state_ffn_conv, W)
    return (y_prompt, y_sample) + tuple(p_states) + tuple(s_states)
```

```python
import functools
import math

import jax
import jax.numpy as jnp
from jax import lax
from jax.experimental import pallas as pl
from jax.experimental.pallas import tpu as pltpu

F32 = jnp.float32
BF16 = jnp.bfloat16

D_MODEL = 1024
DEPTH = 4
CHUNK = 64
HALF = D_MODEL // 2
N_AB = (DEPTH + 1) // 2
N_CD = DEPTH // 2
HEADS = 4
HEAD_DIM = HALF // HEADS
LRU_BLOCKS = 8
LRU_BLOCK = HALF // LRU_BLOCKS
LRU_CONV = 4
LRU_C = 8.0
LRU_GROUP = 8
GD_CONV = 4
MLA_NOPE = 128
MLA_ROPE = 64
MLA_QK = MLA_NOPE + MLA_ROPE
MLA_Q_RANK = 384
MLA_KV_RANK = 256
MLA_SCALE = (MLA_NOPE + MLA_ROPE) ** -0.5
ROPE_THETA = 10000.0
D_FF = 2816
FFN_CONV = 3
FF_TILE = 256
N_FF_TILES = D_FF // FF_TILE
FF_GROUP = 4
FF_SLOTS = 4
FF_AHEAD = 2
EPS = 1e-6
NEG_BIG = -1e30
SQRT_FLOOR = 1e-12
CD_COLS = 2816
SMALL_KR = 0
SMALL_GB = 64
SMALL_GA = 68

VMEM_LIMIT_BYTES = 56 * 1024 * 1024
ATTN_KV_BLOCK_BYTES = 8 * 1024 * 1024


def _cparams(*sem):
    return pltpu.CompilerParams(dimension_semantics=sem, vmem_limit_bytes=VMEM_LIMIT_BYTES)


def _dot(a, b):
    return jnp.dot(a, b, preferred_element_type=F32)


def _dot_nt(a, b):
    return lax.dot_general(a, b, (((1,), (1,)), ((), ())), preferred_element_type=F32)


def _dot_tn(a, b):
    return lax.dot_general(a, b, (((0,), (0,)), ((), ())), preferred_element_type=F32)


def _rms(x, g):
    return x * lax.rsqrt(jnp.mean(x * x, axis=-1, keepdims=True) + EPS) * g


def _silu(x):
    return x * jax.nn.sigmoid(x)


def _softplus(x):
    return jnp.maximum(x, 0.0) + jnp.log1p(jnp.exp(-jnp.abs(x)))


def _gelu_tanh(x):
    return 0.5 * x * (1.0 + jnp.tanh(math.sqrt(2.0 / math.pi) * (x + 0.044715 * (x * x * x))))


def _split3(x):
    x1 = x.astype(BF16)
    r1 = x - x1.astype(F32)
    x2 = r1.astype(BF16)
    x3 = (r1 - x2.astype(F32)).astype(BF16)
    return x1, x2, x3


def _log2(n):
    assert n & (n - 1) == 0
    return n.bit_length() - 1


def _cumsum_rows(x, row):
    n, c = x.shape
    group = min(n, 8)
    x3 = x.reshape(n // group, group, c)
    pos = lax.broadcasted_iota(jnp.int32, x3.shape, 1)
    s = 1
    while s < group:
        x3 = x3 + jnp.where(pos >= s, pltpu.roll(x3, s, 1), 0.0)
        s *= 2
    if n == group:
        return x3.reshape(n, c)
    groups = [x3[0]]
    for gi in range(1, n // group):
        groups.append(x3[gi] + groups[-1][group - 1:group, :])
    return jnp.concatenate(groups, axis=0)


def _block_row_bcast(b, row, h):
    L, n = b.shape
    blk = 2 * h
    if blk >= 8:
        b3 = b.reshape(L // blk, blk, n)
        return jnp.broadcast_to(b3[:, h - 1:h, :], (L // blk, blk, n)).reshape(L, n)
    pos = row & (blk - 1)
    x0 = jnp.where(pos == h - 1, b, 0.0)
    out = x0
    for j in range(1, h + 1):
        out = out + pltpu.roll(x0, j, 0)
    for j in range(1, h):
        out = out + pltpu.roll(x0, L - j, 0)
    return out


def _ada_kernel(c_ref, w_ref, b_ref, o_ref):
    c = _silu(c_ref[...]).astype(BF16)
    o_ref[0] = _dot(c, w_ref[0].astype(BF16)) + b_ref[0]


def _ada_call(c_all, ada_w, ada_b):
    rows = c_all.shape[0]
    tn = 2048
    return pl.pallas_call(
        _ada_kernel,
        out_shape=jax.ShapeDtypeStruct((DEPTH, rows, 6 * D_MODEL), F32),
        grid=(DEPTH, 6 * D_MODEL // tn),
        in_specs=[
            pl.BlockSpec((rows, D_MODEL), lambda l, j: (0, 0)),
            pl.BlockSpec((1, D_MODEL, tn), lambda l, j: (l, 0, j)),
            pl.BlockSpec((1, 1, tn), lambda l, j: (l, 0, j)),
        ],
        out_specs=pl.BlockSpec((1, rows, tn), lambda l, j: (l, 0, j)),
        compiler_params=_cparams("arbitrary", "arbitrary"),
        name="ada_mod",
    )(c_all, ada_w, ada_b.reshape(DEPTH, 1, 6 * D_MODEL))


def _nmm_kernel(x_ref, g_ref, sc_ref, sh_ref, w_ref, o_ref, *maybe_t_ref):
    nb, tt, d = x_ref.shape
    h = _rms(x_ref[...], g_ref[...]) * (1.0 + sc_ref[...]) + sh_ref[...]
    y = _dot(h.reshape(nb * tt, d).astype(BF16), w_ref[...])
    o_ref[...] = y.reshape(nb, tt, y.shape[-1])
    if maybe_t_ref:
        (t_ref,) = maybe_t_ref
        yt = y[:, y.shape[-1] - 128:].T
        t_ref[0] = yt[SMALL_GB:SMALL_GB + 2 * HEADS, :]


def _tiles(B, T):
    if T >= 512:
        return 1, 512
    assert B * T <= 512
    return B, T


def _nmm_call(x, g, scale, shift, w, gates_t=False):
    B, T, D = x.shape
    N = w.shape[1]
    nb, tt = _tiles(B, T)
    out_shape = jax.ShapeDtypeStruct((B, T, N), F32)
    out_specs = pl.BlockSpec((nb, tt, N), lambda b, i: (b, i, 0))
    if gates_t:
        assert nb == 1 and tt % 128 == 0
        out_shape = (out_shape, jax.ShapeDtypeStruct((B, 2 * HEADS, T), F32))
        out_specs = (out_specs, pl.BlockSpec((1, 2 * HEADS, tt), lambda b, i: (b, 0, i)))
    return pl.pallas_call(
        _nmm_kernel,
        out_shape=out_shape,
        grid=(B // nb, T // tt),
        in_specs=[
            pl.BlockSpec((nb, tt, D), lambda b, i: (b, i, 0)),
            pl.BlockSpec((1, D), lambda b, i: (0, 0)),
            pl.BlockSpec((nb, 1, D), lambda b, i: (b, 0, 0)),
            pl.BlockSpec((nb, 1, D), lambda b, i: (b, 0, 0)),
            pl.BlockSpec((D, N), lambda b, i: (0, 0)),
        ],
        out_specs=out_specs,
        compiler_params=_cparams("arbitrary", "arbitrary"),
        name="norm_mod_proj",
    )(x, g.reshape(1, D), scale[:, None, :], shift[:, None, :], w)


def _out_kernel(x_ref, oa_ref, ob_ref, w_ref, g_ref, gate_ref, o_ref):
    nb, tt, d = x_ref.shape
    oa = oa_ref[...].reshape(nb * tt, HALF)
    ob = ob_ref[...].reshape(nb * tt, HALF)
    y = _dot(oa, w_ref[0:HALF, :]) + _dot(ob, w_ref[HALF:2 * HALF, :])
    y = _rms(y, g_ref[...]).reshape(nb, tt, d)
    o_ref[...] = x_ref[...] + gate_ref[...] * y


def _out_call(x, oa, ob, w, g, gate):
    B, T, D = x.shape
    nb, tt = _tiles(B, T)
    return pl.pallas_call(
        _out_kernel,
        out_shape=jax.ShapeDtypeStruct((B, T, D), F32),
        grid=(B // nb, T // tt),
        in_specs=[
            pl.BlockSpec((nb, tt, D), lambda b, i: (b, i, 0)),
            pl.BlockSpec((nb, tt, HALF), lambda b, i: (b, i, 0)),
            pl.BlockSpec((nb, tt, HALF), lambda b, i: (b, i, 0)),
            pl.BlockSpec((D, D), lambda b, i: (0, 0)),
            pl.BlockSpec((1, D), lambda b, i: (0, 0)),
            pl.BlockSpec((nb, 1, D), lambda b, i: (b, 0, 0)),
        ],
        out_specs=pl.BlockSpec((nb, tt, D), lambda b, i: (b, i, 0)),
        compiler_params=_cparams("arbitrary", "arbitrary"),
        name="out_proj_residual",
    )(x, oa, ob, w, g.reshape(1, D), gate[:, None, :])


def _ffn_kernel(x_ref, g1_ref, sc_ref, sh_ref, gate_ref, g2_ref, buf0_ref, wu_ref, wd_ref,
                cw_ref, o_ref, st_ref, carry_ref, ubuf_ref):
    nb, tt, d = x_ref.shape
    i = pl.program_id(1)

    @pl.when(i == 0)
    def _():
        for c in range(2 * N_FF_TILES):
            carry_ref[c] = buf0_ref[:, :, c * FF_TILE:(c + 1) * FF_TILE]

    x = x_ref[...]
    h = (_rms(x, g1_ref[...]) * (1.0 + sc_ref[...]) + sh_ref[...]).reshape(nb * tt, d).astype(BF16)

    def conv(u, slot, c):
        ubuf_ref[slot, :, 8:8 + tt, :] = u
        ubuf_ref[slot, :, 6:8, :] = carry_ref[c]
        cw = cw_ref[:, c * FF_TILE:(c + 1) * FF_TILE]
        y = (cw[0:1, :] * ubuf_ref[slot, :, 6:6 + tt, :] + cw[1:2, :] * ubuf_ref[slot, :, 7:7 + tt, :]
             + cw[2:3, :] * u)
        tail = ubuf_ref[slot, :, 6 + tt:8 + tt, :]
        carry_ref[c] = tail
        st_ref[:, :, c * FF_TILE:(c + 1) * FF_TILE] = tail
        return y

    def up_proj(c):
        wg = wu_ref[:, c * FF_TILE:(c + 1) * FF_TILE]
        wv = wu_ref[:, D_FF + c * FF_TILE:D_FF + (c + 1) * FF_TILE]
        return _dot(h, wg).reshape(nb, tt, FF_TILE), _dot(h, wv).reshape(nb, tt, FF_TILE)

    acc = None
    ahead = [up_proj(c) for c in range(min(FF_AHEAD, N_FF_TILES))]
    for g0 in range(0, N_FF_TILES, FF_GROUP):
        acts = []
        for c in range(g0, min(g0 + FF_GROUP, N_FF_TILES)):
            ug, uv = ahead.pop(0)
            if c + FF_AHEAD < N_FF_TILES:
                ahead.append(up_proj(c + FF_AHEAD))
            slot = 2 * (c % FF_SLOTS)
            yg = conv(ug, slot, c)
            yv = conv(uv, slot + 1, N_FF_TILES + c)
            acts.append((_silu(yg) * yv).reshape(nb * tt, FF_TILE).astype(BF16))
        a = jnp.concatenate(acts, axis=-1) if len(acts) > 1 else acts[0]
        part = _dot(a, wd_ref[g0 * FF_TILE:g0 * FF_TILE + a.shape[-1], :])
        acc = part if acc is None else acc + part
    y = _rms(acc, g2_ref[...]).reshape(nb, tt, d)
    o_ref[...] = x + gate_ref[...] * y


def _ffn_call(x, g1, scale, shift, gate, g2, buf0, wu, wd, cw):
    B, T, D = x.shape
    nb, tt = _tiles(B, T)
    assert T >= FFN_CONV - 1
    return pl.pallas_call(
        _ffn_kernel,
        out_shape=(jax.ShapeDtypeStruct((B, T, D), F32),
                   jax.ShapeDtypeStruct((B, FFN_CONV - 1, 2 * D_FF), F32)),
        grid=(B // nb, T // tt),
        in_specs=[
            pl.BlockSpec((nb, tt, D), lambda b, i: (b, i, 0)),
            pl.BlockSpec((1, D), lambda b, i: (0, 0)),
            pl.BlockSpec((nb, 1, D), lambda b, i: (b, 0, 0)),
            pl.BlockSpec((nb, 1, D), lambda b, i: (b, 0, 0)),
            pl.BlockSpec((nb, 1, D), lambda b, i: (b, 0, 0)),
            pl.BlockSpec((1, D), lambda b, i: (0, 0)),
            pl.BlockSpec((nb, FFN_CONV - 1, 2 * D_FF), lambda b, i: (b, 0, 0)),
            pl.BlockSpec((D, 2 * D_FF), lambda b, i: (0, 0), pipeline_mode=pl.Buffered(1)),
            pl.BlockSpec((D_FF, D), lambda b, i: (0, 0), pipeline_mode=pl.Buffered(1)),
            pl.BlockSpec((FFN_CONV, 2 * D_FF), lambda b, i: (0, 0)),
        ],
        out_specs=(pl.BlockSpec((nb, tt, D), lambda b, i: (b, i, 0)),
                   pl.BlockSpec((nb, FFN_CONV - 1, 2 * D_FF), lambda b, i: (b, 0, 0))),
        scratch_shapes=[
            pltpu.VMEM((2 * N_FF_TILES, nb, FFN_CONV - 1, FF_TILE), F32),
            pltpu.VMEM((2 * FF_SLOTS, nb, 8 + tt, FF_TILE), F32),
        ],
        compiler_params=_cparams("arbitrary", "arbitrary"),
        name="conv_ffn",
    )(x, g1.reshape(1, D), scale[:, None, :], shift[:, None, :], gate[:, None, :], g2.reshape(1, D),
      buf0, wu, wd, cw)


def _hgrn_kernel(hq_ref, hf_ref, hi_ref, hz_ref, lb_ref, ng_ref, s0_ref, o_ref, st_ref):
    L = hq_ref.shape[1]
    i = pl.program_id(1)

    @pl.when(i == 0)
    def _():
        st_ref[...] = s0_ref[...]

    row = lax.broadcasted_iota(jnp.int32, (L, HEAD_DIM), 0)
    r2 = lax.broadcasted_iota(jnp.int32, (L, L), 0)
    c2 = lax.broadcasted_iota(jnp.int32, (L, L), 1)
    heads = range(HEADS)
    sls = [slice(hd * HEAD_DIM, (hd + 1) * HEAD_DIM) for hd in heads]
    q, k, v, b = ([None] * HEADS for _ in range(4))
    for hd in heads:
        z = hf_ref[0, :, sls[hd]]
        lb = lb_ref[:, sls[hd]]
        en = jnp.exp(-jnp.abs(z))
        inv = 1.0 / (1.0 + en)
        sig_pos = jnp.where(z >= 0.0, inv, en * inv)
        sig_neg = jnp.where(z >= 0.0, en * inv, inv)
        g = jnp.log(lb + (1.0 - lb) * sig_pos)
        k[hd] = (1.0 - lb) * sig_neg
        q[hd] = _silu(hq_ref[0, :, sls[hd]])
        v[hd] = hi_ref[0, :, sls[hd]].astype(BF16)
        b[hd] = _cumsum_rows(g, row)
    att = [jnp.where(r2 == c2, _dot_nt(q[hd].astype(BF16), k[hd].astype(BF16)), 0.0) for hd in heads]
    h = L // 2
    while h >= 1:
        upper = (row & (2 * h - 1)) >= h
        sh = _log2(2 * h)
        same_block = (r2 >> sh) == (c2 >> sh)
        for hd in heads:
            r = _block_row_bcast(b[hd], row, h)
            e = jnp.exp2(jnp.abs(b[hd] - r) * (-math.log2(math.e)))
            qt = jnp.where(upper, q[hd] * e, 0.0).astype(BF16)
            kt = jnp.where(upper, 0.0, k[hd] * e).astype(BF16)
            att[hd] = att[hd] + jnp.where(same_block, _dot_nt(qt, kt), 0.0)
        h //= 2
    for hd in heads:
        st = st_ref[0, hd]
        o = _dot(att[hd].astype(BF16), v[hd]) + _dot_nt((q[hd] * jnp.exp(b[hd])).astype(BF16), st.astype(BF16))
        b_last = b[hd][L - 1:L, :]
        kd = (k[hd] * jnp.exp(b_last - b[hd])).astype(BF16)
        st_ref[0, hd] = jnp.exp(b_last) * st + _dot_tn(v[hd], kd)
        o = _rms(o, ng_ref[...]) * _silu(hz_ref[0, :, sls[hd]])
        o_ref[0, :, sls[hd]] = o.astype(BF16)


def _hgrn_call(proj, lb, ng, s0t):
    B, T, _ = proj.shape
    L = 128 if T % 128 == 0 else T
    assert T % L == 0 and L & (L - 1) == 0 and L >= 8
    col = lambda c: pl.BlockSpec((1, L, HALF), lambda b, i, c=c: (b, i, c))
    return pl.pallas_call(
        _hgrn_kernel,
        out_shape=(jax.ShapeDtypeStruct((B, T, HALF), BF16),
                   jax.ShapeDtypeStruct((B, HEADS, HEAD_DIM, HEAD_DIM), F32)),
        grid=(B, T // L),
        in_specs=[col(0), col(1), col(2), col(3),
                  pl.BlockSpec((1, HALF), lambda b, i: (0, 0)),
                  pl.BlockSpec((1, HEAD_DIM), lambda b, i: (0, 0)),
                  pl.BlockSpec((1, HEADS, HEAD_DIM, HEAD_DIM), lambda b, i: (b, 0, 0, 0))],
        out_specs=(pl.BlockSpec((1, L, HALF), lambda b, i: (b, i, 0)),
                   pl.BlockSpec((1, HEADS, HEAD_DIM, HEAD_DIM), lambda b, i: (b, 0, 0, 0))),
        compiler_params=_cparams("arbitrary", "arbitrary"),
        name="hgrn2",
    )(proj, proj, proj, proj, lb.reshape(1, HALF), ng.reshape(1, HEAD_DIM), s0t)


def _lru_kernel(lx_ref, ly_ref, buf0_ref, h0_ref, cw_ref, cb_ref, wa_ref, wx_ref, ba_ref, bx_ref,
                lam_ref, o_ref, hl_ref, bufo_ref, xp_ref):
    tt = lx_ref.shape[1]
    i = pl.program_id(1)
    npad = LRU_CONV - 1

    @pl.when(i == 0)
    def _():
        xp_ref[8 - npad:8, :] = buf0_ref[0]
        hl_ref[0] = h0_ref[0]

    x = lx_ref[0]
    xp_ref[8:8 + tt, :] = x
    xc = cb_ref[...] + cw_ref[npad:npad + 1, :] * x
    for tap in range(npad):
        xc = xc + cw_ref[tap:tap + 1, :] * xp_ref[8 - npad + tap:8 - npad + tap + tt, :]
    tail = xp_ref[8 + tt - npad:8 + tt, :]
    xp_ref[8 - npad:8, :] = tail
    bufo_ref[0] = tail

    xb = xc.astype(BF16)
    half = HALF // 2
    rpre = jnp.concatenate([_dot(xb[:, 0:half], wa_ref[0]), _dot(xb[:, half:HALF], wa_ref[1])], axis=-1)
    ipre = jnp.concatenate([_dot(xb[:, 0:half], wx_ref[0]), _dot(xb[:, half:HALF], wx_ref[1])], axis=-1)
    r = jax.nn.sigmoid(rpre + ba_ref[...])
    ig = jax.nn.sigmoid(ipre + bx_ref[...])
    log_a = -LRU_C * r * _softplus(-lam_ref[...])
    a = jnp.exp(log_a)
    u = jnp.sqrt(jnp.maximum(-jnp.tanh(log_a) * (1.0 + a * a), SQRT_FLOOR)) * ig * xc

    a = a.reshape(tt // LRU_GROUP, LRU_GROUP, HALF)
    u = u.reshape(tt // LRU_GROUP, LRU_GROUP, HALF)
    pos = lax.broadcasted_iota(jnp.int32, a.shape, 1)
    s = 1
    while s < LRU_GROUP:
        keep = pos >= s
        a_sh = jnp.where(keep, pltpu.roll(a, s, 1), 1.0)
        u_sh = jnp.where(keep, pltpu.roll(u, s, 1), 0.0)
        u = a * u_sh + u
        a = a * a_sh
        s *= 2
    carry = hl_ref[0]
    groups = []
    for gi in range(tt // LRU_GROUP):
        hg = u[gi] + a[gi] * carry
        carry = hg[LRU_GROUP - 1:LRU_GROUP, :]
        groups.append(hg)
    hseq = jnp.concatenate(groups, axis=0) if len(groups) > 1 else groups[0]
    hl_ref[0] = carry
    o_ref[0] = (hseq * _gelu_tanh(ly_ref[0])).astype(BF16)


def _lru_call(proj, buf0, h0, cw, cb, wa_bd, wx_bd, ba, bx, lam):
    B, T, _ = proj.shape
    tt = 256 if T % 256 == 0 else T
    assert T % tt == 0 and T >= LRU_CONV - 1 and tt % 8 == 0
    vec = pl.BlockSpec((1, HALF), lambda b, i: (0, 0))
    wspec = pl.BlockSpec((2, HALF // 2, HALF // 2), lambda b, i: (0, 0, 0))
    return pl.pallas_call(
        _lru_kernel,
        out_shape=(jax.ShapeDtypeStruct((B, T, HALF), BF16),
                   jax.ShapeDtypeStruct((B, 1, HALF), F32),
                   jax.ShapeDtypeStruct((B, LRU_CONV - 1, HALF), F32)),
        grid=(B, T // tt),
        in_specs=[pl.BlockSpec((1, tt, HALF), lambda b, i: (b, i, 4)),
                  pl.BlockSpec((1, tt, HALF), lambda b, i: (b, i, 5)),
                  pl.BlockSpec((1, LRU_CONV - 1, HALF), lambda b, i: (b, 0, 0)),
                  pl.BlockSpec((1, 1, HALF), lambda b, i: (b, 0, 0)),
                  pl.BlockSpec((LRU_CONV, HALF), lambda b, i: (0, 0)),
                  vec, wspec, wspec, vec, vec, vec],
        out_specs=(pl.BlockSpec((1, tt, HALF), lambda b, i: (b, i, 0)),
                   pl.BlockSpec((1, 1, HALF), lambda b, i: (b, 0, 0)),
                   pl.BlockSpec((1, LRU_CONV - 1, HALF), lambda b, i: (b, 0, 0))),
        scratch_shapes=[pltpu.VMEM((8 + tt, HALF), F32)],
        compiler_params=_cparams("arbitrary", "arbitrary"),
        name="rglru",
    )(proj, proj, buf0, h0[:, None, :], cw, cb.reshape(1, HALF), wa_bd, wx_bd,
      ba.reshape(1, HALF), bx.reshape(1, HALF), lam.reshape(1, HALF))


def _gdn_kernel(qkv_ref, gz_ref, sm_ref, smt_ref, buf0_ref, cw_ref, pcol_ref, prow_ref, ng_ref, s0_ref,
                o_ref, st_ref, bufo_ref, xp_ref, *, L):
    tt = qkv_ref.shape[1]
    nc = tt // L
    sh = _log2(L)
    i = pl.program_id(1)
    npad = GD_CONV - 1

    @pl.when(i == 0)
    def _():
        xp_ref[8 - npad:8, :] = buf0_ref[0]
        st_ref[...] = s0_ref[...]

    x = qkv_ref[0]
    xp_ref[8:8 + tt, :] = x
    xc = cw_ref[npad:npad + 1, :] * x
    for tap in range(npad):
        xc = xc + cw_ref[tap:tap + 1, :] * xp_ref[8 - npad + tap:8 - npad + tap + tt, :]
    tail = xp_ref[8 + tt - npad:8 + tt, :]
    xp_ref[8 - npad:8, :] = tail
    bufo_ref[0] = tail
    xc = _silu(xc)

    r2 = lax.broadcasted_iota(jnp.int32, (tt, tt), 0)
    c2 = lax.broadcasted_iota(jnp.int32, (tt, tt), 1)
    same = (r2 >> sh) == (c2 >> sh)
    incl = jnp.logical_and(same, c2 <= r2)
    strict = jnp.logical_and(same, c2 < r2)
    tri_lo = jnp.where(incl, 1.0, 0.0).astype(BF16)
    tri_up = jnp.where(jnp.logical_and(same, r2 <= c2), 1.0, 0.0).astype(BF16)
    eye = jnp.where(r2 == c2, 1.0, 0.0)

    sm = sm_ref[0]
    beta_cols = jax.nn.sigmoid(sm)
    la_cols = -jnp.exp(pcol_ref[0:1, :]) * _softplus(sm + pcol_ref[1:2, :])
    c1, c2_, c3 = _split3(la_cols)
    g_cols = _dot(tri_lo, c1) + _dot(tri_lo, c2_) + _dot(tri_lo, c3)
    la_rows = -jnp.exp(prow_ref[:, 0:1]) * _softplus(smt_ref[0] + prow_ref[:, 1:2])
    w1, w2, w3 = _split3(la_rows)
    g_rows = _dot(w1, tri_up) + _dot(w2, tri_up) + _dot(w3, tri_up)

    heads = range(HEADS)
    q, k, v, beta, gcol, kb, dec, m = ([None] * HEADS for _ in range(8))
    for hd in heads:
        qh = xc[:, hd * HEAD_DIM:(hd + 1) * HEAD_DIM]
        kh = xc[:, HALF + hd * HEAD_DIM:HALF + (hd + 1) * HEAD_DIM]
        v[hd] = xc[:, 2 * HALF + hd * HEAD_DIM:2 * HALF + (hd + 1) * HEAD_DIM]
        q[hd] = qh * lax.rsqrt(jnp.sum(qh * qh, axis=-1, keepdims=True) + EPS) * (HEAD_DIM ** -0.5)
        k[hd] = kh * lax.rsqrt(jnp.sum(kh * kh, axis=-1, keepdims=True) + EPS)
        beta[hd] = beta_cols[:, SMALL_GB + hd:SMALL_GB + hd + 1]
        gcol[hd] = g_cols[:, SMALL_GA + hd:SMALL_GA + hd + 1]
        grow = g_rows[HEADS + hd:HEADS + hd + 1, :]
        kb[hd] = k[hd].astype(BF16)
        dec[hd] = jnp.exp(jnp.where(incl, gcol[hd] - grow, NEG_BIG))
        m[hd] = beta[hd] * _dot_nt(kb[hd], kb[hd]) * jnp.where(strict, dec[hd], 0.0)

    pair = (r2 >> 1) == (c2 >> 1)
    tinv = [eye - jnp.where(pair, m[hd], 0.0) for hd in heads]
    s = 2
    while s < L:
        ssh = _log2(s)
        lower_left = jnp.logical_and((r2 >> (ssh + 1)) == (c2 >> (ssh + 1)), (r2 >> ssh) != (c2 >> ssh))
        tb = [tinv[hd].astype(BF16) for hd in heads]
        tc = [_dot(tb[hd], jnp.where(lower_left, m[hd], 0.0).astype(BF16)).astype(BF16) for hd in heads]
        tinv = [tinv[hd] - _dot(tc[hd], tb[hd]) for hd in heads]
        s *= 2

    eg = [jnp.exp(gcol[hd]) for hd in heads]
    sol = [_dot(tinv[hd].astype(BF16),
                jnp.concatenate([beta[hd] * v[hd], (beta[hd] * eg[hd]) * k[hd]], axis=-1).astype(BF16))
           for hd in heads]
    u_v = [sol[hd][:, 0:HEAD_DIM] for hd in heads]
    w_k = [sol[hd][:, HEAD_DIM:2 * HEAD_DIM].astype(BF16) for hd in heads]
    qb = [q[hd].astype(BF16) for hd in heads]
    qk = [(_dot_nt(qb[hd], kb[hd]) * dec[hd]).astype(BF16) for hd in heads]

    a_c, p_c, n_c = ([[None] * nc for _ in heads] for _ in range(3))
    for c in range(nc):
        rs = slice(c * L, (c + 1) * L)
        for hd in heads:
            g_last = gcol[hd][(c + 1) * L - 1:(c + 1) * L, :]
            kd = (k[hd][rs] * jnp.exp(g_last - gcol[hd][rs])).astype(BF16)
            a_c[hd][c] = jnp.exp(g_last)
            p_c[hd][c] = (-_dot_tn(kd, w_k[hd][rs])).astype(BF16)
            n_c[hd][c] = _dot_tn(kd, u_v[hd][rs].astype(BF16))
    S = [[st_ref[0, hd]] for hd in heads]
    for c in range(nc):
        for hd in heads:
            s_cur = S[hd][c]
            S[hd].append(a_c[hd][c] * s_cur + _dot(p_c[hd][c], s_cur.astype(BF16)) + n_c[hd][c])
    us = [[] for _ in heads]
    inters = [[] for _ in heads]
    for c in range(nc):
        rs = slice(c * L, (c + 1) * L)
        for hd in heads:
            Sb = S[hd][c].astype(BF16)
            us[hd].append(u_v[hd][rs] - _dot(w_k[hd][rs], Sb))
            inters[hd].append(eg[hd][rs] * _dot(qb[hd][rs], Sb))
    for hd in heads:
        sl = slice(hd * HEAD_DIM, (hd + 1) * HEAD_DIM)
        st_ref[0, hd] = S[hd][nc]
        u_all = jnp.concatenate(us[hd], axis=0) if nc > 1 else us[hd][0]
        inter = jnp.concatenate(inters[hd], axis=0) if nc > 1 else inters[hd][0]
        o = _dot(qk[hd], u_all.astype(BF16)) + inter
        o = _rms(o, ng_ref[...]) * _silu(gz_ref[0, :, sl])
        o_ref[0, :, sl] = o.astype(BF16)


def _gdn_call(proj, small_t, buf0, cw, pcol, prow, ng, s0):
    B, T, _ = proj.shape
    L = CHUNK if T % CHUNK == 0 else T
    tt = 256 if T % 256 == 0 else T
    assert T % tt == 0 and tt % L == 0 and T >= GD_CONV - 1 and L >= 2
    W = 3 * HALF
    return pl.pallas_call(
        functools.partial(_gdn_kernel, L=L),
        out_shape=(jax.ShapeDtypeStruct((B, T, HALF), BF16),
                   jax.ShapeDtypeStruct((B, HEADS, HEAD_DIM, HEAD_DIM), F32),
                   jax.ShapeDtypeStruct((B, GD_CONV - 1, W), F32)),
        grid=(B, T // tt),
        in_specs=[pl.BlockSpec((1, tt, W), lambda b, i: (b, i, 0)),
                  pl.BlockSpec((1, tt, HALF), lambda b, i: (b, i, 3)),
                  pl.BlockSpec((1, tt, 128), lambda b, i: (b, i, 21)),
                  pl.BlockSpec((1, 2 * HEADS, tt), lambda b, i: (b, 0, i)),
                  pl.BlockSpec((1, GD_CONV - 1, W), lambda b, i: (b, 0, 0)),
                  pl.BlockSpec((GD_CONV, W), lambda b, i: (0, 0)),
                  pl.BlockSpec((2, 128), lambda b, i: (0, 0)),
                  pl.BlockSpec((2 * HEADS, 2), lambda b, i: (0, 0)),
                  pl.BlockSpec((1, HEAD_DIM), lambda b, i: (0, 0)),
                  pl.BlockSpec((1, HEADS, HEAD_DIM, HEAD_DIM), lambda b, i: (b, 0, 0, 0))],
        out_specs=(pl.BlockSpec((1, tt, HALF), lambda b, i: (b, i, 0)),
                   pl.BlockSpec((1, HEADS, HEAD_DIM, HEAD_DIM), lambda b, i: (b, 0, 0, 0)),
                   pl.BlockSpec((1, GD_CONV - 1, W), lambda b, i: (b, 0, 0))),
        scratch_shapes=[pltpu.VMEM((8 + tt, W), F32)],
        compiler_params=_cparams("arbitrary", "arbitrary"),
        name="gated_deltanet",
    )(proj, proj, proj, small_t, buf0, cw, pcol, prow, ng.reshape(1, HEAD_DIM), s0)


def _rope64(x, cs, sn):
    half = MLA_ROPE // 2
    swapped = jnp.concatenate([x[:, half:], x[:, :half]], axis=-1)
    return x * cs + swapped * sn


def _expand_kv(c_kv, k_r, wkvb_ref, kf_ref, v_ref):
    kv = _dot(c_kv.astype(BF16), wkvb_ref[...])
    for hd in range(HEADS):
        base = hd * 2 * HEAD_DIM
        kf_ref[0, hd] = jnp.concatenate([kv[:, base:base + MLA_NOPE], k_r], axis=-1).astype(BF16)
        v = kv[:, base + MLA_NOPE:base + 2 * HEAD_DIM]
        v_ref[0, hd] = jnp.concatenate([v, jnp.ones_like(v)], axis=-1).astype(BF16)


def _mla_prep_kernel(qa_ref, kva_ref, sm_ref, cs_ref, sn_ref, qng_ref, wqb_ref, kvng_ref, wkvb_ref, *rest):
    q_ref, ckv_ref, kr_ref, kf_ref, v_ref = rest[-5:]
    ckv_ref = ckv_ref.at[0]
    kr_ref = kr_ref.at[0]
    cs = cs_ref[...]
    sn = sn_ref[...]
    qn = _rms(qa_ref[0], qng_ref[...]).astype(BF16)
    qh = _dot(qn, wqb_ref[...]) * (MLA_SCALE * math.log2(math.e))
    for hd in range(HEADS):
        nope = qh[:, hd * MLA_NOPE:(hd + 1) * MLA_NOPE]
        off = HEADS * MLA_NOPE + hd * MLA_ROPE
        rot = _rope64(qh[:, off:off + MLA_ROPE], cs, sn)
        q_ref[0, hd] = jnp.concatenate([nope, rot], axis=-1).astype(BF16)
    c_kv = _rms(kva_ref[0], kvng_ref[...])
    ckv_ref[0] = c_kv
    k_r = _rope64(sm_ref[0, :, SMALL_KR:SMALL_KR + MLA_ROPE], cs, sn)
    kr_ref[0] = k_r
    _expand_kv(c_kv, k_r, wkvb_ref, kf_ref, v_ref)


def _mla_prep_call(proj, cs, sn, qng, wqb, kvng, wkvb, j, state_bufs, past_bufs):
    B, T, _ = proj.shape
    tt = 512 if T % 512 == 0 else T
    c2 = lambda b, i: (0, 0)
    past_len = 0 if past_bufs is None else past_bufs[0].shape[2] - T
    assert past_len % tt == 0
    off = past_len // tt
    any_spec = pl.BlockSpec(memory_space=pl.ANY)
    extra, aliases = [], {}
    n_fixed = 9
    if state_bufs is not None:
        aliases[n_fixed + len(extra)] = 1
        aliases[n_fixed + len(extra) + 1] = 2
        extra += list(state_bufs)
    if past_bufs is not None:
        aliases[n_fixed + len(extra)] = 3
        aliases[n_fixed + len(extra) + 1] = 4
        extra += list(past_bufs)
    return pl.pallas_call(
        _mla_prep_kernel,
        out_shape=(jax.ShapeDtypeStruct((B, HEADS, T, MLA_QK), BF16),
                   jax.ShapeDtypeStruct((N_CD, B, T, MLA_KV_RANK), F32),
                   jax.ShapeDtypeStruct((N_CD, B, T, MLA_ROPE), F32),
                   jax.ShapeDtypeStruct((B, HEADS, past_len + T, MLA_QK), BF16),
                   jax.ShapeDtypeStruct((B, HEADS, past_len + T, 2 * HEAD_DIM), BF16)),
        grid=(B, T // tt),
        in_specs=[pl.BlockSpec((1, tt, MLA_Q_RANK), lambda b, i: (b, i, 6)),
                  pl.BlockSpec((1, tt, MLA_KV_RANK), lambda b, i: (b, i, 8)),
                  pl.BlockSpec((1, tt, 128), lambda b, i: (b, i, 21)),
                  pl.BlockSpec((tt, MLA_ROPE), lambda b, i: (i, 0)),
                  pl.BlockSpec((tt, MLA_ROPE), lambda b, i: (i, 0)),
                  pl.BlockSpec((1, MLA_Q_RANK), c2),
                  pl.BlockSpec((MLA_Q_RANK, HEADS * MLA_QK), c2),
                  pl.BlockSpec((1, MLA_KV_RANK), c2),
                  pl.BlockSpec((MLA_KV_RANK, HEADS * 2 * HEAD_DIM), c2)] + [any_spec] * len(extra),
        out_specs=(pl.BlockSpec((1, HEADS, tt, MLA_QK), lambda b, i: (b, 0, i, 0)),
                   pl.BlockSpec((1, 1, tt, MLA_KV_RANK), lambda b, i: (j, b, i, 0)),
                   pl.BlockSpec((1, 1, tt, MLA_ROPE), lambda b, i: (j, b, i, 0)),
                   pl.BlockSpec((1, HEADS, tt, MLA_QK), lambda b, i: (b, 0, i + off, 0)),
                   pl.BlockSpec((1, HEADS, tt, 2 * HEAD_DIM), lambda b, i: (b, 0, i + off, 0))),
        input_output_aliases=aliases,
        compiler_params=_cparams("arbitrary", "arbitrary"),
        name="mla_prep",
    )(proj, proj, proj, cs, sn, qng.reshape(1, MLA_Q_RANK), wqb, kvng.reshape(1, MLA_KV_RANK), wkvb, *extra)


def _mla_past_kernel(lat_ref, kr_ref, wkvb_ref, kf_ref, v_ref):
    _expand_kv(lat_ref[0], kr_ref[0], wkvb_ref, kf_ref, v_ref)


def _mla_past_call(lat, kr, wkvb, n_new):
    B, P, _ = lat.shape
    tt = 512 if P % 512 == 0 else P
    return pl.pallas_call(
        _mla_past_kernel,
        out_shape=(jax.ShapeDtypeStruct((B, HEADS, P + n_new, MLA_QK), BF16),
                   jax.ShapeDtypeStruct((B, HEADS, P + n_new, 2 * HEAD_DIM), BF16)),
        grid=(B, P // tt),
        in_specs=[pl.BlockSpec((1, tt, MLA_KV_RANK), lambda b, i: (b, i, 0)),
                  pl.BlockSpec((1, tt, MLA_ROPE), lambda b, i: (b, i, 0)),
                  pl.BlockSpec((MLA_KV_RANK, HEADS * 2 * HEAD_DIM), lambda b, i: (0, 0))],
        out_specs=(pl.BlockSpec((1, HEADS, tt, MLA_QK), lambda b, i: (b, 0, i, 0)),
                   pl.BlockSpec((1, HEADS, tt, 2 * HEAD_DIM), lambda b, i: (b, 0, i, 0))),
        compiler_params=_cparams("arbitrary", "arbitrary"),
        name="mla_past_kv",
    )(lat, kr, wkvb)


def _attn_kernel(q_ref, k_ref, v_ref, o_ref, m_ref, acc_ref, sa_ref, sb_ref, **static):
    for hh in range(q_ref.shape[1]):
        _attn_head(q_ref.at[0, hh], k_ref.at[0, hh], v_ref.at[0, hh], o_ref.at[0, :, hh * HEAD_DIM:(hh + 1) * HEAD_DIM],
                   m_ref, acc_ref, sa_ref, sb_ref, **static)


def _attn_head(q_ref, k_ref, v_ref, o_ref, m_ref, acc_ref, sa_ref, sb_ref, *, past_len, tq, tk, nk):
    i = pl.program_id(2)
    sh = _log2(CHUNK)
    lanes = HEAD_DIM
    m_ref[...] = jnp.full(m_ref.shape, NEG_BIG, F32)
    acc_ref[...] = jnp.zeros(acc_ref.shape, F32)
    q = q_ref[...]
    q_lo = past_len + i * tq
    n_full = jnp.minimum(nk, (((q_lo >> sh) + 1) * CHUNK) // tk)
    n_need = jnp.minimum(nk, ((((q_lo + tq - 1) >> sh) + 1) * CHUNK + tk - 1) // tk)

    def scores(j):
        return _dot_nt(q, k_ref[pl.ds(pl.multiple_of(j * tk, tk), tk), :])

    def masked_scores(j):
        k_lo = j * tk
        qc = (q_lo + lax.broadcasted_iota(jnp.int32, (tq, tk), 0)) >> sh
        kc = (k_lo + lax.broadcasted_iota(jnp.int32, (tq, tk), 1)) >> sh
        return jnp.where(kc <= qc, scores(j), NEG_BIG)

    def update(j, s_ref):
        k_lo = pl.multiple_of(j * tk, tk)
        m_prev = m_ref[...]
        m_new = jnp.maximum(m_prev, jnp.max(s_ref[...], axis=-1, keepdims=True))
        alpha = jnp.exp2(m_prev - m_new)
        if tk % lanes == 0:
            p = jnp.exp2(s_ref[...] - jnp.tile(m_new, (1, tk // lanes)))
        else:
            p = jnp.exp2(s_ref[...] - m_new[:, 0:1])
        pv = _dot(p.astype(BF16), v_ref[pl.ds(k_lo, tk), :])
        acc_ref[...] = jnp.tile(alpha, (1, 2)) * acc_ref[...] + pv
        m_ref[...] = m_new

    def body_pair(g, carry):
        sb_ref[...] = scores(2 * g + 1)
        update(2 * g, sa_ref)
        sa_ref[...] = scores(2 * g + 2)
        update(2 * g + 1, sb_ref)
        return carry

    sa_ref[...] = scores(0)
    n_pairs = jnp.maximum(n_full - 1, 0) // 2
    lax.fori_loop(0, n_pairs, body_pair, 0)

    left = n_full - 2 * n_pairs
    has_masked = n_need > n_full
    j_masked = jnp.minimum(n_full, nk - 1)

    @pl.when(left == 1)
    def _():
        sb_ref[...] = masked_scores(j_masked)
        update(n_full - 1, sa_ref)

    @pl.when(jnp.logical_and(left == 1, has_masked))
    def _():
        update(n_full, sb_ref)

    @pl.when(left == 2)
    def _():
        sb_ref[...] = scores(n_full - 1)
        update(n_full - 2, sa_ref)
        sa_ref[...] = masked_scores(j_masked)
        update(n_full - 1, sb_ref)

    @pl.when(jnp.logical_and(left == 2, has_masked))
    def _():
        update(n_full, sa_ref)

    def body_masked(j, carry):
        sa_ref[...] = masked_scores(j)
        update(j, sa_ref)
        return carry

    lax.fori_loop(jnp.where(n_full > 0, n_full + 1, 0), n_need, body_masked, 0)
    o_ref[...] = (acc_ref[:, 0:HEAD_DIM] / acc_ref[:, HEAD_DIM:2 * HEAD_DIM]).astype(BF16)


def _attn_call(q, kf, v, past_len):
    B, H, Tq, _ = q.shape
    Tk = kf.shape[2]
    tq = 1024 if Tq % 1024 == 0 else Tq
    tk = 1024 if Tk % 1024 == 0 else Tk
    nk = Tk // tk
    kv_bytes = Tk * (MLA_QK + 2 * HEAD_DIM) * 2
    hb = H if H * kv_bytes <= ATTN_KV_BLOCK_BYTES else 1
    return pl.pallas_call(
        functools.partial(_attn_kernel, past_len=past_len, tq=tq, tk=tk, nk=nk),
        out_shape=jax.ShapeDtypeStruct((B, Tq, H * HEAD_DIM), BF16),
        grid=(B, H // hb, Tq // tq),
        in_specs=[pl.BlockSpec((1, hb, tq, MLA_QK), lambda b, h, i: (b, h, i, 0)),
                  pl.BlockSpec((1, hb, Tk, MLA_QK), lambda b, h, i: (b, h, 0, 0)),
                  pl.BlockSpec((1, hb, Tk, 2 * HEAD_DIM), lambda b, h, i: (b, h, 0, 0))],
        out_specs=pl.BlockSpec((1, tq, hb * HEAD_DIM), lambda b, h, i: (b, i, h)),
        scratch_shapes=[pltpu.VMEM((tq, HEAD_DIM), F32), pltpu.VMEM((tq, 2 * HEAD_DIM), F32),
                        pltpu.VMEM((tq, tk), F32), pltpu.VMEM((tq, tk), F32)],
        compiler_params=_cparams("arbitrary", "arbitrary", "arbitrary"),
        name="mla_attention",
    )(q, kf, v)


def _block_diag_pairs(w):
    per = (HALF // 2) // LRU_BLOCK
    w4 = w.reshape(2, per, LRU_BLOCK, LRU_BLOCK)
    eye = jnp.eye(per, dtype=w.dtype)
    out = w4[:, :, :, None, :] * eye[None, :, None, :, None]
    return out.reshape(2, HALF // 2, HALF // 2).astype(BF16)


def _prep_cd_w_in(w):
    o = 3 * HALF
    qkv, gz = w[:, :o], w[:, o:o + HALF]
    o += HALF
    gb, ga = w[:, o:o + HEADS], w[:, o + HEADS:o + 2 * HEADS]
    o += 2 * HEADS
    qa, kva = w[:, o:o + MLA_Q_RANK], w[:, o + MLA_Q_RANK:o + MLA_Q_RANK + MLA_KV_RANK]
    o += MLA_Q_RANK + MLA_KV_RANK
    kr = w[:, o:o + MLA_ROPE]
    assert (SMALL_KR, SMALL_GB, SMALL_GA) == (0, MLA_ROPE, MLA_ROPE + HEADS)
    fill = jnp.zeros((w.shape[0], 128 - MLA_ROPE - 2 * HEADS), w.dtype)
    out = jnp.concatenate([qkv, gz, kva, qa, kr, gb, ga, fill], axis=-1)
    assert out.shape[1] == CD_COLS
    return out.astype(BF16)


def _prep_wqb(w):
    w4 = w.reshape(MLA_Q_RANK, HEADS, MLA_QK)
    nope = w4[:, :, :MLA_NOPE].reshape(MLA_Q_RANK, HEADS * MLA_NOPE)
    rope = w4[:, :, MLA_NOPE:].reshape(MLA_Q_RANK, HEADS * MLA_ROPE)
    return jnp.concatenate([nope, rope], axis=-1).astype(BF16)


def _rope_tables(T, past_len):
    half = MLA_ROPE // 2
    freqs = jnp.exp(-math.log(ROPE_THETA) * jnp.arange(half, dtype=F32) / half)
    pos = past_len + jnp.arange(T, dtype=jnp.int32)
    ang = pos.astype(F32)[:, None] * freqs
    cos, sin = jnp.cos(ang), jnp.sin(ang)
    return jnp.concatenate([cos, cos], axis=-1), jnp.concatenate([-sin, sin], axis=-1)


def _run_group(x, mods, hg_s, lru_h, lru_buf, gd_s, gd_buf, lat_past, kr_past, ffn_buf, W):
    B, T, _ = x.shape
    n_hg, n_lru, n_lrub, n_gd, n_gdb, n_ffn = ([] for _ in range(6))
    mla_state = None
    for l in range(DEPTH):
        j = l // 2
        shift1, scale1, gate1, shift2, scale2, gate2 = jnp.split(mods[l], 6, axis=-1)
        g = W['norm_g'][l]
        if l % 2 == 0:
            proj = _nmm_call(x, g[0], scale1, shift1, W['ab_w_in'][j])
            o_a, s_hg_t = _hgrn_call(proj, W['lower_bounds'][j], W['hgrn_norm_g'][j],
                                     jnp.swapaxes(hg_s[j], -1, -2))
            o_b, s_lru, s_lrub = _lru_call(proj, lru_buf[j], lru_h[j], W['lru_conv_w'][j], W['lru_conv_b'][j],
                                           W['lru_wa_bd'][j], W['lru_wx_bd'][j], W['lru_b_a'][j],
                                           W['lru_b_x'][j], W['lru_lambda'][j])
            n_hg.append(jnp.swapaxes(s_hg_t, -1, -2))
            n_lru.append(s_lru[:, 0, :])
            n_lrub.append(s_lrub)
            x = _out_call(x, o_a, o_b, W['ab_w_out'][j], g[1], gate1)
        else:
            past_len = lat_past.shape[2]
            if _tiles(B, T)[0] == 1:
                proj, small_t = _nmm_call(x, g[0], scale1, shift1, W['cd_w_in'][j], gates_t=True)
            else:
                proj = _nmm_call(x, g[0], scale1, shift1, W['cd_w_in'][j])
                small_t = jnp.swapaxes(proj[:, :, CD_COLS - 128 + SMALL_GB:CD_COLS - 128 + SMALL_GB + 2 * HEADS], 1, 2)
            o_c, s_gd, s_gdb = _gdn_call(proj, small_t, gd_buf[j], W['gdn_conv_w'][j], W['gdn_pcol'][j],
                                         W['gdn_prow'][j], W['gdn_norm_g'][j], gd_s[j])
            cs, sn = _rope_tables(T, past_len)
            past_bufs = _mla_past_call(lat_past[j], kr_past[j], W['mla_w_kvb'][j], T) if past_len > 0 else None
            q, lat_all, kr_all, kf, v = _mla_prep_call(proj, cs, sn, W['mla_q_norm_g'][j], W['mla_w_qb'][j],
                                                       W['mla_kv_norm_g'][j], W['mla_w_kvb'][j], j, mla_state,
                                                       past_bufs)
            mla_state = (lat_all, kr_all)
            o_d = _attn_call(q, kf, v, past_len)
            n_gd.append(s_gd)
            n_gdb.append(s_gdb)
            x = _out_call(x, o_c, o_d, W['cd_w_out'][j], g[1], gate1)
        x, s_ffn = _ffn_call(x, g[2], scale2, shift2, gate2, g[3], ffn_buf[l], W['ffn_wu'][l], W['ffn_wd'][l],
                             W['ffn_cw'][l])
        n_ffn.append(s_ffn)
    return x, (jnp.stack(n_hg), jnp.stack(n_lru), jnp.stack(n_lrub), jnp.stack(n_gd), jnp.stack(n_gdb),
               mla_state[0], mla_state[1], jnp.stack(n_ffn))


def _prep_weights(norm_g, ab_w_in, ab_w_out, hgrn_lb_logits, hgrn_norm_g, lru_conv_w, lru_conv_b, lru_w_a, lru_b_a,
                  lru_w_x, lru_b_x, lru_lambda, cd_w_in, cd_w_out, gdn_conv_w, gdn_a_log, gdn_dt_bias, gdn_norm_g,
                  mla_q_norm_g, mla_w_qb, mla_kv_norm_g, mla_w_kvb, ffn_w_up, ffn_conv_w, ffn_w_down):
    lb_p = jax.nn.softmax(hgrn_lb_logits.astype(F32), axis=0)
    pcol = jnp.pad(jnp.stack([gdn_a_log, gdn_dt_bias], axis=1),
                   ((0, 0), (0, 0), (SMALL_GA, 128 - SMALL_GA - HEADS)))
    prow = jnp.pad(jnp.stack([gdn_a_log, gdn_dt_bias], axis=-1), ((0, 0), (HEADS, 0), (0, 0)))
    W = dict(
        norm_g=norm_g,
        ab_w_in=ab_w_in.astype(BF16), ab_w_out=ab_w_out.astype(BF16),
        lower_bounds=jnp.cumsum(lb_p, axis=0) - lb_p[0:1],
        hgrn_norm_g=hgrn_norm_g, lru_conv_w=lru_conv_w, lru_conv_b=lru_conv_b,
        lru_wa_bd=jax.vmap(_block_diag_pairs)(lru_w_a), lru_wx_bd=jax.vmap(_block_diag_pairs)(lru_w_x),
        lru_b_a=lru_b_a, lru_b_x=lru_b_x, lru_lambda=lru_lambda,
        cd_w_in=jax.vmap(_prep_cd_w_in)(cd_w_in), cd_w_out=cd_w_out.astype(BF16),
        gdn_conv_w=gdn_conv_w, gdn_pcol=pcol, gdn_prow=prow, gdn_norm_g=gdn_norm_g,
        mla_q_norm_g=mla_q_norm_g, mla_w_qb=jax.vmap(_prep_wqb)(mla_w_qb),
        mla_kv_norm_g=mla_kv_norm_g, mla_w_kvb=mla_w_kvb.astype(BF16),
        ffn_wu=ffn_w_up.astype(BF16), ffn_wd=ffn_w_down.astype(BF16), ffn_cw=ffn_conv_w,
    )
    return W


def kernel(x_prompt, x_sample, c_prompt, c_sample, state_hgrn, state_rglru, state_rglru_conv, state_gdn, state_gdn_conv, cache_mla_latent, cache_mla_krope, state_ffn_conv, ada_w, ada_b, norm_g, ab_w_in, ab_w_out, hgrn_lb_logits, hgrn_norm_g, lru_conv_w, lru_conv_b, lru_w_a, lru_b_a, lru_w_x, lru_b_x, lru_lambda, cd_w_in, cd_w_out, gdn_conv_w, gdn_a_log, gdn_dt_bias, gdn_norm_g, mla_q_norm_g, mla_w_qb, mla_kv_norm_g, mla_w_kvb, ffn_w_up, ffn_conv_w, ffn_w_down):
    bp, bs = x_prompt.shape[0], x_sample.shape[0]
    W = _prep_weights(norm_g, ab_w_in, ab_w_out, hgrn_lb_logits, hgrn_norm_g, lru_conv_w, lru_conv_b, lru_w_a, lru_b_a,
                      lru_w_x, lru_b_x, lru_lambda, cd_w_in, cd_w_out, gdn_conv_w, gdn_a_log, gdn_dt_bias,
                      gdn_norm_g, mla_q_norm_g, mla_w_qb, mla_kv_norm_g, mla_w_kvb, ffn_w_up, ffn_conv_w,
                      ffn_w_down)
    rows = bp + bs
    rows_pad = -(-rows // 8) * 8
    c_all = jnp.concatenate([c_prompt, c_sample, jnp.zeros((rows_pad - rows, D_MODEL), F32)], axis=0)
    mods = _ada_call(c_all, ada_w, ada_b)
    dt_ = x_prompt.dtype
    y_prompt, p_states = _run_group(
        x_prompt, mods[:, :bp],
        jnp.zeros((N_AB, bp, HEADS, HEAD_DIM, HEAD_DIM), F32),
        jnp.zeros((N_AB, bp, HALF), F32),
        jnp.zeros((N_AB, bp, LRU_CONV - 1, HALF), dt_),
        jnp.zeros((N_CD, bp, HEADS, HEAD_DIM, HEAD_DIM), F32),
        jnp.zeros((N_CD, bp, GD_CONV - 1, 3 * HALF), dt_),
        jnp.zeros((N_CD, bp, 0, MLA_KV_RANK), dt_),
        jnp.zeros((N_CD, bp, 0, MLA_ROPE), dt_),
        jnp.zeros((DEPTH, bp, FFN_CONV - 1, 2 * D_FF), dt_),
        W)
    y_sample, s_states = _run_group(
        x_sample, mods[:, bp:rows], state_hgrn, state_rglru, state_rglru_conv, state_gdn, state_gdn_conv,
        cache_mla_latent, cache_mla_krope, state_ffn_conv, W)
    return (y_prompt, y_sample) + tuple(p_states) + tuple(s_states)
```

```python
import functools
import math

import jax
import jax.numpy as jnp
from jax import lax
from jax.experimental import pallas as pl
from jax.experimental.pallas import tpu as pltpu

F32 = jnp.float32
BF16 = jnp.bfloat16

D_MODEL = 1024
DEPTH = 4
CHUNK = 64
HALF = D_MODEL // 2
N_AB = (DEPTH + 1) // 2
N_CD = DEPTH // 2
HEADS = 4
HEAD_DIM = HALF // HEADS
LRU_BLOCKS = 8
LRU_BLOCK = HALF // LRU_BLOCKS
LRU_CONV = 4
LRU_C = 8.0
LRU_GROUP = 8
GD_CONV = 4
MLA_NOPE = 128
MLA_ROPE = 64
MLA_QK = MLA_NOPE + MLA_ROPE
MLA_Q_RANK = 384
MLA_KV_RANK = 256
MLA_SCALE = (MLA_NOPE + MLA_ROPE) ** -0.5
ROPE_THETA = 10000.0
D_FF = 2816
FFN_CONV = 3
FF_TILE = 256
N_FF_TILES = D_FF // FF_TILE
FF_GROUP = 4
FF_SLOTS = 4
FF_AHEAD = 2
EPS = 1e-6
NEG_BIG = -1e30
SQRT_FLOOR = 1e-12
CD_COLS = 2816
SMALL_KR = 0
SMALL_GB = 64
SMALL_GA = 68

VMEM_LIMIT_BYTES = 56 * 1024 * 1024
ATTN_KV_BLOCK_BYTES = 8 * 1024 * 1024


def _cparams(*sem):
    return pltpu.CompilerParams(dimension_semantics=sem, vmem_limit_bytes=VMEM_LIMIT_BYTES)


def _dot(a, b):
    return jnp.dot(a, b, preferred_element_type=F32)


def _dot_nt(a, b):
    return lax.dot_general(a, b, (((1,), (1,)), ((), ())), preferred_element_type=F32)


def _dot_tn(a, b):
    return lax.dot_general(a, b, (((0,), (0,)), ((), ())), preferred_element_type=F32)


def _rms(x, g):
    return x * lax.rsqrt(jnp.mean(x * x, axis=-1, keepdims=True) + EPS) * g


def _silu(x):
    return x * jax.nn.sigmoid(x)


def _softplus(x):
    return jnp.maximum(x, 0.0) + jnp.log1p(jnp.exp(-jnp.abs(x)))


def _gelu_tanh(x):
    return 0.5 * x * (1.0 + jnp.tanh(math.sqrt(2.0 / math.pi) * (x + 0.044715 * (x * x * x))))


def _split3(x):
    x1 = x.astype(BF16)
    r1 = x - x1.astype(F32)
    x2 = r1.astype(BF16)
    x3 = (r1 - x2.astype(F32)).astype(BF16)
    return x1, x2, x3


def _log2(n):
    assert n & (n - 1) == 0
    return n.bit_length() - 1


def _cumsum_rows(x, row):
    n, c = x.shape
    group = min(n, 8)
    x3 = x.reshape(n // group, group, c)
    pos = lax.broadcasted_iota(jnp.int32, x3.shape, 1)
    s = 1
    while s < group:
        x3 = x3 + jnp.where(pos >= s, pltpu.roll(x3, s, 1), 0.0)
        s *= 2
    if n == group:
        return x3.reshape(n, c)
    groups = [x3[0]]
    for gi in range(1, n // group):
        groups.append(x3[gi] + groups[-1][group - 1:group, :])
    return jnp.concatenate(groups, axis=0)


def _block_row_bcast(b, row, h):
    L, n = b.shape
    blk = 2 * h
    if blk >= 8:
        b3 = b.reshape(L // blk, blk, n)
        return jnp.broadcast_to(b3[:, h - 1:h, :], (L // blk, blk, n)).reshape(L, n)
    pos = row & (blk - 1)
    x0 = jnp.where(pos == h - 1, b, 0.0)
    out = x0
    for j in range(1, h + 1):
        out = out + pltpu.roll(x0, j, 0)
    for j in range(1, h):
        out = out + pltpu.roll(x0, L - j, 0)
    return out


def _ada_kernel(c_ref, w_ref, b_ref, o_ref):
    c = _silu(c_ref[...]).astype(BF16)
    o_ref[0] = _dot(c, w_ref[0].astype(BF16)) + b_ref[0]


def _ada_call(c_all, ada_w, ada_b):
    rows = c_all.shape[0]
    tn = 2048
    return pl.pallas_call(
        _ada_kernel,
        out_shape=jax.ShapeDtypeStruct((DEPTH, rows, 6 * D_MODEL), F32),
        grid=(DEPTH, 6 * D_MODEL // tn),
        in_specs=[
            pl.BlockSpec((rows, D_MODEL), lambda l, j: (0, 0)),
            pl.BlockSpec((1, D_MODEL, tn), lambda l, j: (l, 0, j)),
            pl.BlockSpec((1, 1, tn), lambda l, j: (l, 0, j)),
        ],
        out_specs=pl.BlockSpec((1, rows, tn), lambda l, j: (l, 0, j)),
        compiler_params=_cparams("arbitrary", "arbitrary"),
        name="ada_mod",
    )(c_all, ada_w, ada_b.reshape(DEPTH, 1, 6 * D_MODEL))


def _nmm_kernel(x_ref, g_ref, sc_ref, sh_ref, w_ref, o_ref, *maybe_t_ref):
    nb, tt, d = x_ref.shape
    h = _rms(x_ref[...], g_ref[...]) * (1.0 + sc_ref[...]) + sh_ref[...]
    y = _dot(h.reshape(nb * tt, d).astype(BF16), w_ref[...])
    o_ref[...] = y.reshape(nb, tt, y.shape[-1])
    if maybe_t_ref:
        (t_ref,) = maybe_t_ref
        yt = y[:, y.shape[-1] - 128:].T
        t_ref[0] = yt[SMALL_GB:SMALL_GB + 2 * HEADS, :]


def _tiles(B, T):
    if T >= 512:
        return 1, 512
    assert B * T <= 512
    return B, T


def _nmm_call(x, g, scale, shift, w, gates_t=False):
    B, T, D = x.shape
    N = w.shape[1]
    nb, tt = _tiles(B, T)
    out_shape = jax.ShapeDtypeStruct((B, T, N), F32)
    out_specs = pl.BlockSpec((nb, tt, N), lambda b, i: (b, i, 0))
    if gates_t:
        assert nb == 1 and tt % 128 == 0
        out_shape = (out_shape, jax.ShapeDtypeStruct((B, 2 * HEADS, T), F32))
        out_specs = (out_specs, pl.BlockSpec((1, 2 * HEADS, tt), lambda b, i: (b, 0, i)))
    return pl.pallas_call(
        _nmm_kernel,
        out_shape=out_shape,
        grid=(B // nb, T // tt),
        in_specs=[
            pl.BlockSpec((nb, tt, D), lambda b, i: (b, i, 0)),
            pl.BlockSpec((1, D), lambda b, i: (0, 0)),
            pl.BlockSpec((nb, 1, D), lambda b, i: (b, 0, 0)),
            pl.BlockSpec((nb, 1, D), lambda b, i: (b, 0, 0)),
            pl.BlockSpec((D, N), lambda b, i: (0, 0)),
        ],
        out_specs=out_specs,
        compiler_params=_cparams("arbitrary", "arbitrary"),
        name="norm_mod_proj",
    )(x, g.reshape(1, D), scale[:, None, :], shift[:, None, :], w)


def _ffn_kernel(x_ref, oa_ref, ob_ref, wo_ref, go_ref, gateo_ref, g1_ref, sc_ref, sh_ref, gate_ref, g2_ref,
                buf0_ref, wu_ref, wd_ref, cw_ref, o_ref, st_ref, carry_ref, ubuf_ref):
    nb, tt, d = x_ref.shape
    i = pl.program_id(1)

    @pl.when(i == 0)
    def _():
        for c in range(2 * N_FF_TILES):
            carry_ref[c] = buf0_ref[:, :, c * FF_TILE:(c + 1) * FF_TILE]

    oa = oa_ref[...].reshape(nb * tt, HALF)
    ob = ob_ref[...].reshape(nb * tt, HALF)
    mix = _dot(oa, wo_ref[0:HALF, :]) + _dot(ob, wo_ref[HALF:2 * HALF, :])
    x = x_ref[...] + gateo_ref[...] * _rms(mix, go_ref[...]).reshape(nb, tt, d)
    h = (_rms(x, g1_ref[...]) * (1.0 + sc_ref[...]) + sh_ref[...]).reshape(nb * tt, d).astype(BF16)

    def conv(u, slot, c):
        ubuf_ref[slot, :, 8:8 + tt, :] = u
        ubuf_ref[slot, :, 6:8, :] = carry_ref[c]
        cw = cw_ref[:, c * FF_TILE:(c + 1) * FF_TILE]
        y = (cw[0:1, :] * ubuf_ref[slot, :, 6:6 + tt, :] + cw[1:2, :] * ubuf_ref[slot, :, 7:7 + tt, :]
             + cw[2:3, :] * u)
        tail = ubuf_ref[slot, :, 6 + tt:8 + tt, :]
        carry_ref[c] = tail
        st_ref[:, :, c * FF_TILE:(c + 1) * FF_TILE] = tail
        return y

    def up_proj(c):
        wg = wu_ref[:, c * FF_TILE:(c + 1) * FF_TILE]
        wv = wu_ref[:, D_FF + c * FF_TILE:D_FF + (c + 1) * FF_TILE]
        return _dot(h, wg).reshape(nb, tt, FF_TILE), _dot(h, wv).reshape(nb, tt, FF_TILE)

    acc = None
    ahead = [up_proj(c) for c in range(min(FF_AHEAD, N_FF_TILES))]
    for g0 in range(0, N_FF_TILES, FF_GROUP):
        acts = []
        for c in range(g0, min(g0 + FF_GROUP, N_FF_TILES)):
            ug, uv = ahead.pop(0)
            if c + FF_AHEAD < N_FF_TILES:
                ahead.append(up_proj(c + FF_AHEAD))
            slot = 2 * (c % FF_SLOTS)
            yg = conv(ug, slot, c)
            yv = conv(uv, slot + 1, N_FF_TILES + c)
            acts.append((_silu(yg) * yv).reshape(nb * tt, FF_TILE).astype(BF16))
        a = jnp.concatenate(acts, axis=-1) if len(acts) > 1 else acts[0]
        part = _dot(a, wd_ref[g0 * FF_TILE:g0 * FF_TILE + a.shape[-1], :])
        acc = part if acc is None else acc + part
    y = _rms(acc, g2_ref[...]).reshape(nb, tt, d)
    o_ref[...] = x + gate_ref[...] * y


def _ffn_call(x, oa, ob, wo, go, gateo, g1, scale, shift, gate, g2, buf0, wu, wd, cw):
    B, T, D = x.shape
    nb, tt = _tiles(B, T)
    assert T >= FFN_CONV - 1
    return pl.pallas_call(
        _ffn_kernel,
        out_shape=(jax.ShapeDtypeStruct((B, T, D), F32),
                   jax.ShapeDtypeStruct((B, FFN_CONV - 1, 2 * D_FF), F32)),
        grid=(B // nb, T // tt),
        in_specs=[
            pl.BlockSpec((nb, tt, D), lambda b, i: (b, i, 0)),
            pl.BlockSpec((nb, tt, HALF), lambda b, i: (b, i, 0)),
            pl.BlockSpec((nb, tt, HALF), lambda b, i: (b, i, 0)),
            pl.BlockSpec((D, D), lambda b, i: (0, 0), pipeline_mode=pl.Buffered(1)),
            pl.BlockSpec((1, D), lambda b, i: (0, 0)),
            pl.BlockSpec((nb, 1, D), lambda b, i: (b, 0, 0)),
            pl.BlockSpec((1, D), lambda b, i: (0, 0)),
            pl.BlockSpec((nb, 1, D), lambda b, i: (b, 0, 0)),
            pl.BlockSpec((nb, 1, D), lambda b, i: (b, 0, 0)),
            pl.BlockSpec((nb, 1, D), lambda b, i: (b, 0, 0)),
            pl.BlockSpec((1, D), lambda b, i: (0, 0)),
            pl.BlockSpec((nb, FFN_CONV - 1, 2 * D_FF), lambda b, i: (b, 0, 0)),
            pl.BlockSpec((D, 2 * D_FF), lambda b, i: (0, 0), pipeline_mode=pl.Buffered(1)),
            pl.BlockSpec((D_FF, D), lambda b, i: (0, 0), pipeline_mode=pl.Buffered(1)),
            pl.BlockSpec((FFN_CONV, 2 * D_FF), lambda b, i: (0, 0)),
        ],
        out_specs=(pl.BlockSpec((nb, tt, D), lambda b, i: (b, i, 0)),
                   pl.BlockSpec((nb, FFN_CONV - 1, 2 * D_FF), lambda b, i: (b, 0, 0))),
        scratch_shapes=[
            pltpu.VMEM((2 * N_FF_TILES, nb, FFN_CONV - 1, FF_TILE), F32),
            pltpu.VMEM((2 * FF_SLOTS, nb, 8 + tt, FF_TILE), F32),
        ],
        compiler_params=_cparams("arbitrary", "arbitrary"),
        name="conv_ffn",
    )(x, oa, ob, wo, go.reshape(1, D), gateo[:, None, :], g1.reshape(1, D), scale[:, None, :], shift[:, None, :],
      gate[:, None, :], g2.reshape(1, D), buf0, wu, wd, cw)


def _hgrn_kernel(hq_ref, hf_ref, hi_ref, hz_ref, lb_ref, ng_ref, s0_ref, o_ref, st_ref):
    L = hq_ref.shape[1]
    i = pl.program_id(1)

    @pl.when(i == 0)
    def _():
        st_ref[...] = s0_ref[...]

    row = lax.broadcasted_iota(jnp.int32, (L, HEAD_DIM), 0)
    r2 = lax.broadcasted_iota(jnp.int32, (L, L), 0)
    c2 = lax.broadcasted_iota(jnp.int32, (L, L), 1)
    heads = range(HEADS)
    sls = [slice(hd * HEAD_DIM, (hd + 1) * HEAD_DIM) for hd in heads]
    q, k, v, b = ([None] * HEADS for _ in range(4))
    for hd in heads:
        z = hf_ref[0, :, sls[hd]]
        lb = lb_ref[:, sls[hd]]
        en = jnp.exp(-jnp.abs(z))
        inv = 1.0 / (1.0 + en)
        sig_pos = jnp.where(z >= 0.0, inv, en * inv)
        sig_neg = jnp.where(z >= 0.0, en * inv, inv)
        g = jnp.log(lb + (1.0 - lb) * sig_pos)
        k[hd] = (1.0 - lb) * sig_neg
        q[hd] = _silu(hq_ref[0, :, sls[hd]])
        v[hd] = hi_ref[0, :, sls[hd]].astype(BF16)
        b[hd] = _cumsum_rows(g, row)
    att = [jnp.where(r2 == c2, _dot_nt(q[hd].astype(BF16), k[hd].astype(BF16)), 0.0) for hd in heads]
    h = L // 2
    while h >= 1:
        upper = (row & (2 * h - 1)) >= h
        sh = _log2(2 * h)
        same_block = (r2 >> sh) == (c2 >> sh)
        for hd in heads:
            r = _block_row_bcast(b[hd], row, h)
            e = jnp.exp2(jnp.abs(b[hd] - r) * (-math.log2(math.e)))
            qt = jnp.where(upper, q[hd] * e, 0.0).astype(BF16)
            kt = jnp.where(upper, 0.0, k[hd] * e).astype(BF16)
            att[hd] = att[hd] + jnp.where(same_block, _dot_nt(qt, kt), 0.0)
        h //= 2
    for hd in heads:
        st = st_ref[0, hd]
        o = _dot(att[hd].astype(BF16), v[hd]) + _dot_nt((q[hd] * jnp.exp(b[hd])).astype(BF16), st.astype(BF16))
        b_last = b[hd][L - 1:L, :]
        kd = (k[hd] * jnp.exp(b_last - b[hd])).astype(BF16)
        st_ref[0, hd] = jnp.exp(b_last) * st + _dot_tn(v[hd], kd)
        o = _rms(o, ng_ref[...]) * _silu(hz_ref[0, :, sls[hd]])
        o_ref[0, :, sls[hd]] = o.astype(BF16)


def _hgrn_call(proj, lb, ng, s0t):
    B, T, _ = proj.shape
    L = 128 if T % 128 == 0 else T
    assert T % L == 0 and L & (L - 1) == 0 and L >= 8
    col = lambda c: pl.BlockSpec((1, L, HALF), lambda b, i, c=c: (b, i, c))
    return pl.pallas_call(
        _hgrn_kernel,
        out_shape=(jax.ShapeDtypeStruct((B, T, HALF), BF16),
                   jax.ShapeDtypeStruct((B, HEADS, HEAD_DIM, HEAD_DIM), F32)),
        grid=(B, T // L),
        in_specs=[col(0), col(1), col(2), col(3),
                  pl.BlockSpec((1, HALF), lambda b, i: (0, 0)),
                  pl.BlockSpec((1, HEAD_DIM), lambda b, i: (0, 0)),
                  pl.BlockSpec((1, HEADS, HEAD_DIM, HEAD_DIM), lambda b, i: (b, 0, 0, 0))],
        out_specs=(pl.BlockSpec((1, L, HALF), lambda b, i: (b, i, 0)),
                   pl.BlockSpec((1, HEADS, HEAD_DIM, HEAD_DIM), lambda b, i: (b, 0, 0, 0))),
        compiler_params=_cparams("arbitrary", "arbitrary"),
        name="hgrn2",
    )(proj, proj, proj, proj, lb.reshape(1, HALF), ng.reshape(1, HEAD_DIM), s0t)


def _lru_kernel(lx_ref, ly_ref, buf0_ref, h0_ref, cw_ref, cb_ref, wa_ref, wx_ref, ba_ref, bx_ref,
                lam_ref, o_ref, hl_ref, bufo_ref, xp_ref):
    tt = lx_ref.shape[1]
    i = pl.program_id(1)
    npad = LRU_CONV - 1

    @pl.when(i == 0)
    def _():
        xp_ref[8 - npad:8, :] = buf0_ref[0]
        hl_ref[0] = h0_ref[0]

    x = lx_ref[0]
    xp_ref[8:8 + tt, :] = x
    xc = cb_ref[...] + cw_ref[npad:npad + 1, :] * x
    for tap in range(npad):
        xc = xc + cw_ref[tap:tap + 1, :] * xp_ref[8 - npad + tap:8 - npad + tap + tt, :]
    tail = xp_ref[8 + tt - npad:8 + tt, :]
    xp_ref[8 - npad:8, :] = tail
    bufo_ref[0] = tail

    xb = xc.astype(BF16)
    half = HALF // 2
    rpre = jnp.concatenate([_dot(xb[:, 0:half], wa_ref[0]), _dot(xb[:, half:HALF], wa_ref[1])], axis=-1)
    ipre = jnp.concatenate([_dot(xb[:, 0:half], wx_ref[0]), _dot(xb[:, half:HALF], wx_ref[1])], axis=-1)
    r = jax.nn.sigmoid(rpre + ba_ref[...])
    ig = jax.nn.sigmoid(ipre + bx_ref[...])
    log_a = -LRU_C * r * _softplus(-lam_ref[...])
    a = jnp.exp(log_a)
    u = jnp.sqrt(jnp.maximum(-jnp.tanh(log_a) * (1.0 + a * a), SQRT_FLOOR)) * ig * xc

    a = a.reshape(tt // LRU_GROUP, LRU_GROUP, HALF)
    u = u.reshape(tt // LRU_GROUP, LRU_GROUP, HALF)
    pos = lax.broadcasted_iota(jnp.int32, a.shape, 1)
    s = 1
    while s < LRU_GROUP:
        keep = pos >= s
        a_sh = jnp.where(keep, pltpu.roll(a, s, 1), 1.0)
        u_sh = jnp.where(keep, pltpu.roll(u, s, 1), 0.0)
        u = a * u_sh + u
        a = a * a_sh
        s *= 2
    carry = hl_ref[0]
    groups = []
    for gi in range(tt // LRU_GROUP):
        hg = u[gi] + a[gi] * carry
        carry = hg[LRU_GROUP - 1:LRU_GROUP, :]
        groups.append(hg)
    hseq = jnp.concatenate(groups, axis=0) if len(groups) > 1 else groups[0]
    hl_ref[0] = carry
    o_ref[0] = (hseq * _gelu_tanh(ly_ref[0])).astype(BF16)


def _lru_call(proj, buf0, h0, cw, cb, wa_bd, wx_bd, ba, bx, lam):
    B, T, _ = proj.shape
    tt = 256 if T % 256 == 0 else T
    assert T % tt == 0 and T >= LRU_CONV - 1 and tt % 8 == 0
    vec = pl.BlockSpec((1, HALF), lambda b, i: (0, 0))
    wspec = pl.BlockSpec((2, HALF // 2, HALF // 2), lambda b, i: (0, 0, 0))
    return pl.pallas_call(
        _lru_kernel,
        out_shape=(jax.ShapeDtypeStruct((B, T, HALF), BF16),
                   jax.ShapeDtypeStruct((B, 1, HALF), F32),
                   jax.ShapeDtypeStruct((B, LRU_CONV - 1, HALF), F32)),
        grid=(B, T // tt),
        in_specs=[pl.BlockSpec((1, tt, HALF), lambda b, i: (b, i, 4)),
                  pl.BlockSpec((1, tt, HALF), lambda b, i: (b, i, 5)),
                  pl.BlockSpec((1, LRU_CONV - 1, HALF), lambda b, i: (b, 0, 0)),
                  pl.BlockSpec((1, 1, HALF), lambda b, i: (b, 0, 0)),
                  pl.BlockSpec((LRU_CONV, HALF), lambda b, i: (0, 0)),
                  vec, wspec, wspec, vec, vec, vec],
        out_specs=(pl.BlockSpec((1, tt, HALF), lambda b, i: (b, i, 0)),
                   pl.BlockSpec((1, 1, HALF), lambda b, i: (b, 0, 0)),
                   pl.BlockSpec((1, LRU_CONV - 1, HALF), lambda b, i: (b, 0, 0))),
        scratch_shapes=[pltpu.VMEM((8 + tt, HALF), F32)],
        compiler_params=_cparams("arbitrary", "arbitrary"),
        name="rglru",
    )(proj, proj, buf0, h0[:, None, :], cw, cb.reshape(1, HALF), wa_bd, wx_bd,
      ba.reshape(1, HALF), bx.reshape(1, HALF), lam.reshape(1, HALF))


def _gdn_kernel(qkv_ref, gz_ref, sm_ref, smt_ref, buf0_ref, cw_ref, pcol_ref, prow_ref, ng_ref, s0_ref,
                o_ref, st_ref, bufo_ref, xp_ref, *, L):
    tt = qkv_ref.shape[1]
    nc = tt // L
    sh = _log2(L)
    i = pl.program_id(1)
    npad = GD_CONV - 1

    @pl.when(i == 0)
    def _():
        xp_ref[8 - npad:8, :] = buf0_ref[0]
        st_ref[...] = s0_ref[...]

    x = qkv_ref[0]
    xp_ref[8:8 + tt, :] = x
    xc = cw_ref[npad:npad + 1, :] * x
    for tap in range(npad):
        xc = xc + cw_ref[tap:tap + 1, :] * xp_ref[8 - npad + tap:8 - npad + tap + tt, :]
    tail = xp_ref[8 + tt - npad:8 + tt, :]
    xp_ref[8 - npad:8, :] = tail
    bufo_ref[0] = tail
    xc = _silu(xc)

    r2 = lax.broadcasted_iota(jnp.int32, (tt, tt), 0)
    c2 = lax.broadcasted_iota(jnp.int32, (tt, tt), 1)
    same = (r2 >> sh) == (c2 >> sh)
    incl = jnp.logical_and(same, c2 <= r2)
    strict = jnp.logical_and(same, c2 < r2)
    tri_lo = jnp.where(incl, 1.0, 0.0).astype(BF16)
    tri_up = jnp.where(jnp.logical_and(same, r2 <= c2), 1.0, 0.0).astype(BF16)
    eye = jnp.where(r2 == c2, 1.0, 0.0)

    sm = sm_ref[0]
    beta_cols = jax.nn.sigmoid(sm)
    la_cols = -jnp.exp(pcol_ref[0:1, :]) * _softplus(sm + pcol_ref[1:2, :])
    c1, c2_, c3 = _split3(la_cols)
    g_cols = _dot(tri_lo, c1) + _dot(tri_lo, c2_) + _dot(tri_lo, c3)
    la_rows = -jnp.exp(prow_ref[:, 0:1]) * _softplus(smt_ref[0] + prow_ref[:, 1:2])
    w1, w2, w3 = _split3(la_rows)
    g_rows = _dot(w1, tri_up) + _dot(w2, tri_up) + _dot(w3, tri_up)

    heads = range(HEADS)
    q, k, v, beta, gcol, kb, dec, m = ([None] * HEADS for _ in range(8))
    for hd in heads:
        qh = xc[:, hd * HEAD_DIM:(hd + 1) * HEAD_DIM]
        kh = xc[:, HALF + hd * HEAD_DIM:HALF + (hd + 1) * HEAD_DIM]
        v[hd] = xc[:, 2 * HALF + hd * HEAD_DIM:2 * HALF + (hd + 1) * HEAD_DIM]
        q[hd] = qh * lax.rsqrt(jnp.sum(qh * qh, axis=-1, keepdims=True) + EPS) * (HEAD_DIM ** -0.5)
        k[hd] = kh * lax.rsqrt(jnp.sum(kh * kh, axis=-1, keepdims=True) + EPS)
        beta[hd] = beta_cols[:, SMALL_GB + hd:SMALL_GB + hd + 1]
        gcol[hd] = g_cols[:, SMALL_GA + hd:SMALL_GA + hd + 1]
        grow = g_rows[HEADS + hd:HEADS + hd + 1, :]
        kb[hd] = k[hd].astype(BF16)
        dec[hd] = jnp.exp(jnp.where(incl, gcol[hd] - grow, NEG_BIG))
        m[hd] = beta[hd] * _dot_nt(kb[hd], kb[hd]) * jnp.where(strict, dec[hd], 0.0)

    pair = (r2 >> 1) == (c2 >> 1)
    tinv = [eye - jnp.where(pair, m[hd], 0.0) for hd in heads]
    s = 2
    while s < L:
        ssh = _log2(s)
        lower_left = jnp.logical_and((r2 >> (ssh + 1)) == (c2 >> (ssh + 1)), (r2 >> ssh) != (c2 >> ssh))
        tb = [tinv[hd].astype(BF16) for hd in heads]
        tc = [_dot(tb[hd], jnp.where(lower_left, m[hd], 0.0).astype(BF16)).astype(BF16) for hd in heads]
        tinv = [tinv[hd] - _dot(tc[hd], tb[hd]) for hd in heads]
        s *= 2

    eg = [jnp.exp(gcol[hd]) for hd in heads]
    sol = [_dot(tinv[hd].astype(BF16),
                jnp.concatenate([beta[hd] * v[hd], (beta[hd] * eg[hd]) * k[hd]], axis=-1).astype(BF16))
           for hd in heads]
    u_v = [sol[hd][:, 0:HEAD_DIM] for hd in heads]
    w_k = [sol[hd][:, HEAD_DIM:2 * HEAD_DIM].astype(BF16) for hd in heads]
    qb = [q[hd].astype(BF16) for hd in heads]
    qk = [(_dot_nt(qb[hd], kb[hd]) * dec[hd]).astype(BF16) for hd in heads]

    a_c, p_c, n_c = ([[None] * nc for _ in heads] for _ in range(3))
    for c in range(nc):
        rs = slice(c * L, (c + 1) * L)
        for hd in heads:
            g_last = gcol[hd][(c + 1) * L - 1:(c + 1) * L, :]
            kd = (k[hd][rs] * jnp.exp(g_last - gcol[hd][rs])).astype(BF16)
            a_c[hd][c] = jnp.exp(g_last)
            p_c[hd][c] = (-_dot_tn(kd, w_k[hd][rs])).astype(BF16)
            n_c[hd][c] = _dot_tn(kd, u_v[hd][rs].astype(BF16))
    S = [[st_ref[0, hd]] for hd in heads]
    for c in range(nc):
        for hd in heads:
            s_cur = S[hd][c]
            S[hd].append(a_c[hd][c] * s_cur + _dot(p_c[hd][c], s_cur.astype(BF16)) + n_c[hd][c])
    us = [[] for _ in heads]
    inters = [[] for _ in heads]
    for c in range(nc):
        rs = slice(c * L, (c + 1) * L)
        for hd in heads:
            Sb = S[hd][c].astype(BF16)
            us[hd].append(u_v[hd][rs] - _dot(w_k[hd][rs], Sb))
            inters[hd].append(eg[hd][rs] * _dot(qb[hd][rs], Sb))
    for hd in heads:
        sl = slice(hd * HEAD_DIM, (hd + 1) * HEAD_DIM)
        st_ref[0, hd] = S[hd][nc]
        u_all = jnp.concatenate(us[hd], axis=0) if nc > 1 else us[hd][0]
        inter = jnp.concatenate(inters[hd], axis=0) if nc > 1 else inters[hd][0]
        o = _dot(qk[hd], u_all.astype(BF16)) + inter
        o = _rms(o, ng_ref[...]) * _silu(gz_ref[0, :, sl])
        o_ref[0, :, sl] = o.astype(BF16)


def _gdn_call(proj, small_t, buf0, cw, pcol, prow, ng, s0):
    B, T, _ = proj.shape
    L = CHUNK if T % CHUNK == 0 else T
    tt = 256 if T % 256 == 0 else T
    assert T % tt == 0 and tt % L == 0 and T >= GD_CONV - 1 and L >= 2
    W = 3 * HALF
    return pl.pallas_call(
        functools.partial(_gdn_kernel, L=L),
        out_shape=(jax.ShapeDtypeStruct((B, T, HALF), BF16),
                   jax.ShapeDtypeStruct((B, HEADS, HEAD_DIM, HEAD_DIM), F32),
                   jax.ShapeDtypeStruct((B, GD_CONV - 1, W), F32)),
        grid=(B, T // tt),
        in_specs=[pl.BlockSpec((1, tt, W), lambda b, i: (b, i, 0)),
                  pl.BlockSpec((1, tt, HALF), lambda b, i: (b, i, 3)),
                  pl.BlockSpec((1, tt, 128), lambda b, i: (b, i, 21)),
                  pl.BlockSpec((1, 2 * HEADS, tt), lambda b, i: (b, 0, i)),
                  pl.BlockSpec((1, GD_CONV - 1, W), lambda b, i: (b, 0, 0)),
                  pl.BlockSpec((GD_CONV, W), lambda b, i: (0, 0)),
                  pl.BlockSpec((2, 128), lambda b, i: (0, 0)),
                  pl.BlockSpec((2 * HEADS, 2), lambda b, i: (0, 0)),
                  pl.BlockSpec((1, HEAD_DIM), lambda b, i: (0, 0)),
                  pl.BlockSpec((1, HEADS, HEAD_DIM, HEAD_DIM), lambda b, i: (b, 0, 0, 0))],
        out_specs=(pl.BlockSpec((1, tt, HALF), lambda b, i: (b, i, 0)),
                   pl.BlockSpec((1, HEADS, HEAD_DIM, HEAD_DIM), lambda b, i: (b, 0, 0, 0)),
                   pl.BlockSpec((1, GD_CONV - 1, W), lambda b, i: (b, 0, 0))),
        scratch_shapes=[pltpu.VMEM((8 + tt, W), F32)],
        compiler_params=_cparams("arbitrary", "arbitrary"),
        name="gated_deltanet",
    )(proj, proj, proj, small_t, buf0, cw, pcol, prow, ng.reshape(1, HEAD_DIM), s0)


def _rope64(x, cs, sn):
    half = MLA_ROPE // 2
    swapped = jnp.concatenate([x[:, half:], x[:, :half]], axis=-1)
    return x * cs + swapped * sn


def _expand_kv(c_kv, k_r, wkvb_ref, kf_ref, v_ref):
    kv = _dot(c_kv.astype(BF16), wkvb_ref[...])
    for hd in range(HEADS):
        base = hd * 2 * HEAD_DIM
        kf_ref[0, hd] = jnp.concatenate([kv[:, base:base + MLA_NOPE], k_r], axis=-1).astype(BF16)
        v = kv[:, base + MLA_NOPE:base + 2 * HEAD_DIM]
        v_ref[0, hd] = jnp.concatenate([v, jnp.ones_like(v)], axis=-1).astype(BF16)


def _mla_prep_kernel(qa_ref, kva_ref, sm_ref, cs_ref, sn_ref, qng_ref, wqb_ref, kvng_ref, wkvb_ref, *rest):
    q_ref, ckv_ref, kr_ref, kf_ref, v_ref = rest[-5:]
    ckv_ref = ckv_ref.at[0]
    kr_ref = kr_ref.at[0]
    cs = cs_ref[...]
    sn = sn_ref[...]
    qn = _rms(qa_ref[0], qng_ref[...]).astype(BF16)
    qh = _dot(qn, wqb_ref[...]) * (MLA_SCALE * math.log2(math.e))
    for hd in range(HEADS):
        nope = qh[:, hd * MLA_NOPE:(hd + 1) * MLA_NOPE]
        off = HEADS * MLA_NOPE + hd * MLA_ROPE
        rot = _rope64(qh[:, off:off + MLA_ROPE], cs, sn)
        q_ref[0, hd] = jnp.concatenate([nope, rot], axis=-1).astype(BF16)
    c_kv = _rms(kva_ref[0], kvng_ref[...])
    ckv_ref[0] = c_kv
    k_r = _rope64(sm_ref[0, :, SMALL_KR:SMALL_KR + MLA_ROPE], cs, sn)
    kr_ref[0] = k_r
    _expand_kv(c_kv, k_r, wkvb_ref, kf_ref, v_ref)


def _mla_prep_call(proj, cs, sn, qng, wqb, kvng, wkvb, j, state_bufs, past_bufs):
    B, T, _ = proj.shape
    tt = 512 if T % 512 == 0 else T
    c2 = lambda b, i: (0, 0)
    past_len = 0 if past_bufs is None else past_bufs[0].shape[2] - T
    assert past_len % tt == 0
    off = past_len // tt
    any_spec = pl.BlockSpec(memory_space=pl.ANY)
    extra, aliases = [], {}
    n_fixed = 9
    if state_bufs is not None:
        aliases[n_fixed + len(extra)] = 1
        aliases[n_fixed + len(extra) + 1] = 2
        extra += list(state_bufs)
    if past_bufs is not None:
        aliases[n_fixed + len(extra)] = 3
        aliases[n_fixed + len(extra) + 1] = 4
        extra += list(past_bufs)
    return pl.pallas_call(
        _mla_prep_kernel,
        out_shape=(jax.ShapeDtypeStruct((B, HEADS, T, MLA_QK), BF16),
                   jax.ShapeDtypeStruct((N_CD, B, T, MLA_KV_RANK), F32),
                   jax.ShapeDtypeStruct((N_CD, B, T, MLA_ROPE), F32),
                   jax.ShapeDtypeStruct((B, HEADS, past_len + T, MLA_QK), BF16),
                   jax.ShapeDtypeStruct((B, HEADS, past_len + T, 2 * HEAD_DIM), BF16)),
        grid=(B, T // tt),
        in_specs=[pl.BlockSpec((1, tt, MLA_Q_RANK), lambda b, i: (b, i, 6)),
                  pl.BlockSpec((1, tt, MLA_KV_RANK), lambda b, i: (b, i, 8)),
                  pl.BlockSpec((1, tt, 128), lambda b, i: (b, i, 21)),
                  pl.BlockSpec((tt, MLA_ROPE), lambda b, i: (i, 0)),
                  pl.BlockSpec((tt, MLA_ROPE), lambda b, i: (i, 0)),
                  pl.BlockSpec((1, MLA_Q_RANK), c2),
                  pl.BlockSpec((MLA_Q_RANK, HEADS * MLA_QK), c2),
                  pl.BlockSpec((1, MLA_KV_RANK), c2),
                  pl.BlockSpec((MLA_KV_RANK, HEADS * 2 * HEAD_DIM), c2)] + [any_spec] * len(extra),
        out_specs=(pl.BlockSpec((1, HEADS, tt, MLA_QK), lambda b, i: (b, 0, i, 0)),
                   pl.BlockSpec((1, 1, tt, MLA_KV_RANK), lambda b, i: (j, b, i, 0)),
                   pl.BlockSpec((1, 1, tt, MLA_ROPE), lambda b, i: (j, b, i, 0)),
                   pl.BlockSpec((1, HEADS, tt, MLA_QK), lambda b, i: (b, 0, i + off, 0)),
                   pl.BlockSpec((1, HEADS, tt, 2 * HEAD_DIM), lambda b, i: (b, 0, i + off, 0))),
        input_output_aliases=aliases,
        compiler_params=_cparams("arbitrary", "arbitrary"),
        name="mla_prep",
    )(proj, proj, proj, cs, sn, qng.reshape(1, MLA_Q_RANK), wqb, kvng.reshape(1, MLA_KV_RANK), wkvb, *extra)


def _mla_past_kernel(lat_ref, kr_ref, wkvb_ref, kf_ref, v_ref):
    _expand_kv(lat_ref[0], kr_ref[0], wkvb_ref, kf_ref, v_ref)


def _mla_past_call(lat, kr, wkvb, n_new):
    B, P, _ = lat.shape
    tt = 512 if P % 512 == 0 else P
    return pl.pallas_call(
        _mla_past_kernel,
        out_shape=(jax.ShapeDtypeStruct((B, HEADS, P + n_new, MLA_QK), BF16),
                   jax.ShapeDtypeStruct((B, HEADS, P + n_new, 2 * HEAD_DIM), BF16)),
        grid=(B, P // tt),
        in_specs=[pl.BlockSpec((1, tt, MLA_KV_RANK), lambda b, i: (b, i, 0)),
                  pl.BlockSpec((1, tt, MLA_ROPE), lambda b, i: (b, i, 0)),
                  pl.BlockSpec((MLA_KV_RANK, HEADS * 2 * HEAD_DIM), lambda b, i: (0, 0))],
        out_specs=(pl.BlockSpec((1, HEADS, tt, MLA_QK), lambda b, i: (b, 0, i, 0)),
                   pl.BlockSpec((1, HEADS, tt, 2 * HEAD_DIM), lambda b, i: (b, 0, i, 0))),
        compiler_params=_cparams("arbitrary", "arbitrary"),
        name="mla_past_kv",
    )(lat, kr, wkvb)


def _attn_kernel(q_ref, k_ref, v_ref, o_ref, m_ref, acc_ref, sa_ref, sb_ref, **static):
    for hh in range(q_ref.shape[1]):
        _attn_head(q_ref.at[0, hh], k_ref.at[0, hh], v_ref.at[0, hh], o_ref.at[0, :, hh * HEAD_DIM:(hh + 1) * HEAD_DIM],
                   m_ref, acc_ref, sa_ref, sb_ref, **static)


def _attn_head(q_ref, k_ref, v_ref, o_ref, m_ref, acc_ref, sa_ref, sb_ref, *, past_len, tq, tk, nk):
    i = pl.program_id(2)
    sh = _log2(CHUNK)
    lanes = HEAD_DIM
    m_ref[...] = jnp.full(m_ref.shape, NEG_BIG, F32)
    acc_ref[...] = jnp.zeros(acc_ref.shape, F32)
    q = q_ref[...]
    q_lo = past_len + i * tq
    n_full = jnp.minimum(nk, (((q_lo >> sh) + 1) * CHUNK) // tk)
    n_need = jnp.minimum(nk, ((((q_lo + tq - 1) >> sh) + 1) * CHUNK + tk - 1) // tk)

    def scores(j):
        return _dot_nt(q, k_ref[pl.ds(pl.multiple_of(j * tk, tk), tk), :])

    def masked_scores(j):
        k_lo = j * tk
        qc = (q_lo + lax.broadcasted_iota(jnp.int32, (tq, tk), 0)) >> sh
        kc = (k_lo + lax.broadcasted_iota(jnp.int32, (tq, tk), 1)) >> sh
        return jnp.where(kc <= qc, scores(j), NEG_BIG)

    def update(j, s_ref):
        k_lo = pl.multiple_of(j * tk, tk)
        m_prev = m_ref[...]
        m_new = jnp.maximum(m_prev, jnp.max(s_ref[...], axis=-1, keepdims=True))
        alpha = jnp.exp2(m_prev - m_new)
        if tk % lanes == 0:
            p = jnp.exp2(s_ref[...] - jnp.tile(m_new, (1, tk // lanes)))
        else:
            p = jnp.exp2(s_ref[...] - m_new[:, 0:1])
        pv = _dot(p.astype(BF16), v_ref[pl.ds(k_lo, tk), :])
        acc_ref[...] = jnp.tile(alpha, (1, 2)) * acc_ref[...] + pv
        m_ref[...] = m_new

    def body_pair(g, carry):
        sb_ref[...] = scores(2 * g + 1)
        update(2 * g, sa_ref)
        sa_ref[...] = scores(2 * g + 2)
        update(2 * g + 1, sb_ref)
        return carry

    sa_ref[...] = scores(0)
    n_pairs = jnp.maximum(n_full - 1, 0) // 2
    lax.fori_loop(0, n_pairs, body_pair, 0)

    left = n_full - 2 * n_pairs
    has_masked = n_need > n_full
    j_masked = jnp.minimum(n_full, nk - 1)

    @pl.when(left == 1)
    def _():
        sb_ref[...] = masked_scores(j_masked)
        update(n_full - 1, sa_ref)

    @pl.when(jnp.logical_and(left == 1, has_masked))
    def _():
        update(n_full, sb_ref)

    @pl.when(left == 2)
    def _():
        sb_ref[...] = scores(n_full - 1)
        update(n_full - 2, sa_ref)
        sa_ref[...] = masked_scores(j_masked)
        update(n_full - 1, sb_ref)

    @pl.when(jnp.logical_and(left == 2, has_masked))
    def _():
        update(n_full, sa_ref)

    def body_masked(j, carry):
        sa_ref[...] = masked_scores(j)
        update(j, sa_ref)
        return carry

    lax.fori_loop(jnp.where(n_full > 0, n_full + 1, 0), n_need, body_masked, 0)
    o_ref[...] = (acc_ref[:, 0:HEAD_DIM] / acc_ref[:, HEAD_DIM:2 * HEAD_DIM]).astype(BF16)


def _attn_call(q, kf, v, past_len):
    B, H, Tq, _ = q.shape
    Tk = kf.shape[2]
    tq = 1024 if Tq % 1024 == 0 else Tq
    tk = 1024 if Tk % 1024 == 0 else Tk
    nk = Tk // tk
    kv_bytes = Tk * (MLA_QK + 2 * HEAD_DIM) * 2
    hb = H if H * kv_bytes <= ATTN_KV_BLOCK_BYTES else 1
    return pl.pallas_call(
        functools.partial(_attn_kernel, past_len=past_len, tq=tq, tk=tk, nk=nk),
        out_shape=jax.ShapeDtypeStruct((B, Tq, H * HEAD_DIM), BF16),
        grid=(B, H // hb, Tq // tq),
        in_specs=[pl.BlockSpec((1, hb, tq, MLA_QK), lambda b, h, i: (b, h, i, 0)),
                  pl.BlockSpec((1, hb, Tk, MLA_QK), lambda b, h, i: (b, h, 0, 0)),
                  pl.BlockSpec((1, hb, Tk, 2 * HEAD_DIM), lambda b, h, i: (b, h, 0, 0))],
        out_specs=pl.BlockSpec((1, tq, hb * HEAD_DIM), lambda b, h, i: (b, i, h)),
        scratch_shapes=[pltpu.VMEM((tq, HEAD_DIM), F32), pltpu.VMEM((tq, 2 * HEAD_DIM), F32),
                        pltpu.VMEM((tq, tk), F32), pltpu.VMEM((tq, tk), F32)],
        compiler_params=_cparams("arbitrary", "arbitrary", "arbitrary"),
        name="mla_attention",
    )(q, kf, v)


def _block_diag_pairs(w):
    per = (HALF // 2) // LRU_BLOCK
    w4 = w.reshape(2, per, LRU_BLOCK, LRU_BLOCK)
    eye = jnp.eye(per, dtype=w.dtype)
    out = w4[:, :, :, None, :] * eye[None, :, None, :, None]
    return out.reshape(2, HALF // 2, HALF // 2).astype(BF16)


def _prep_cd_w_in(w):
    o = 3 * HALF
    qkv, gz = w[:, :o], w[:, o:o + HALF]
    o += HALF
    gb, ga = w[:, o:o + HEADS], w[:, o + HEADS:o + 2 * HEADS]
    o += 2 * HEADS
    qa, kva = w[:, o:o + MLA_Q_RANK], w[:, o + MLA_Q_RANK:o + MLA_Q_RANK + MLA_KV_RANK]
    o += MLA_Q_RANK + MLA_KV_RANK
    kr = w[:, o:o + MLA_ROPE]
    assert (SMALL_KR, SMALL_GB, SMALL_GA) == (0, MLA_ROPE, MLA_ROPE + HEADS)
    fill = jnp.zeros((w.shape[0], 128 - MLA_ROPE - 2 * HEADS), w.dtype)
    out = jnp.concatenate([qkv, gz, kva, qa, kr, gb, ga, fill], axis=-1)
    assert out.shape[1] == CD_COLS
    return out.astype(BF16)


def _prep_wqb(w):
    w4 = w.reshape(MLA_Q_RANK, HEADS, MLA_QK)
    nope = w4[:, :, :MLA_NOPE].reshape(MLA_Q_RANK, HEADS * MLA_NOPE)
    rope = w4[:, :, MLA_NOPE:].reshape(MLA_Q_RANK, HEADS * MLA_ROPE)
    return jnp.concatenate([nope, rope], axis=-1).astype(BF16)


def _rope_tables(T, past_len):
    half = MLA_ROPE // 2
    freqs = jnp.exp(-math.log(ROPE_THETA) * jnp.arange(half, dtype=F32) / half)
    pos = past_len + jnp.arange(T, dtype=jnp.int32)
    ang = pos.astype(F32)[:, None] * freqs
    cos, sin = jnp.cos(ang), jnp.sin(ang)
    return jnp.concatenate([cos, cos], axis=-1), jnp.concatenate([-sin, sin], axis=-1)


def _run_group(x, mods, hg_s, lru_h, lru_buf, gd_s, gd_buf, lat_past, kr_past, ffn_buf, W):
    B, T, _ = x.shape
    n_hg, n_lru, n_lrub, n_gd, n_gdb, n_ffn = ([] for _ in range(6))
    mla_state = None
    for l in range(DEPTH):
        j = l // 2
        shift1, scale1, gate1, shift2, scale2, gate2 = jnp.split(mods[l], 6, axis=-1)
        g = W['norm_g'][l]
        if l % 2 == 0:
            proj = _nmm_call(x, g[0], scale1, shift1, W['ab_w_in'][j])
            o_a, s_hg_t = _hgrn_call(proj, W['lower_bounds'][j], W['hgrn_norm_g'][j],
                                     jnp.swapaxes(hg_s[j], -1, -2))
            o_b, s_lru, s_lrub = _lru_call(proj, lru_buf[j], lru_h[j], W['lru_conv_w'][j], W['lru_conv_b'][j],
                                           W['lru_wa_bd'][j], W['lru_wx_bd'][j], W['lru_b_a'][j],
                                           W['lru_b_x'][j], W['lru_lambda'][j])
            n_hg.append(jnp.swapaxes(s_hg_t, -1, -2))
            n_lru.append(s_lru[:, 0, :])
            n_lrub.append(s_lrub)
            mix_a, mix_b, w_out = o_a, o_b, W['ab_w_out'][j]
        else:
            past_len = lat_past.shape[2]
            if _tiles(B, T)[0] == 1:
                proj, small_t = _nmm_call(x, g[0], scale1, shift1, W['cd_w_in'][j], gates_t=True)
            else:
                proj = _nmm_call(x, g[0], scale1, shift1, W['cd_w_in'][j])
                small_t = jnp.swapaxes(proj[:, :, CD_COLS - 128 + SMALL_GB:CD_COLS - 128 + SMALL_GB + 2 * HEADS], 1, 2)
            o_c, s_gd, s_gdb = _gdn_call(proj, small_t, gd_buf[j], W['gdn_conv_w'][j], W['gdn_pcol'][j],
                                         W['gdn_prow'][j], W['gdn_norm_g'][j], gd_s[j])
            cs, sn = _rope_tables(T, past_len)
            past_bufs = _mla_past_call(lat_past[j], kr_past[j], W['mla_w_kvb'][j], T) if past_len > 0 else None
            q, lat_all, kr_all, kf, v = _mla_prep_call(proj, cs, sn, W['mla_q_norm_g'][j], W['mla_w_qb'][j],
                                                       W['mla_kv_norm_g'][j], W['mla_w_kvb'][j], j, mla_state,
                                                       past_bufs)
            mla_state = (lat_all, kr_all)
            o_d = _attn_call(q, kf, v, past_len)
            n_gd.append(s_gd)
            n_gdb.append(s_gdb)
            mix_a, mix_b, w_out = o_c, o_d, W['cd_w_out'][j]
        x, s_ffn = _ffn_call(x, mix_a, mix_b, w_out, g[1], gate1, g[2], scale2, shift2, gate2, g[3], ffn_buf[l],
                             W['ffn_wu'][l], W['ffn_wd'][l], W['ffn_cw'][l])
        n_ffn.append(s_ffn)
    return x, (jnp.stack(n_hg), jnp.stack(n_lru), jnp.stack(n_lrub), jnp.stack(n_gd), jnp.stack(n_gdb),
               mla_state[0], mla_state[1], jnp.stack(n_ffn))


def _prep_weights(norm_g, ab_w_in, ab_w_out, hgrn_lb_logits, hgrn_norm_g, lru_conv_w, lru_conv_b, lru_w_a, lru_b_a,
                  lru_w_x, lru_b_x, lru_lambda, cd_w_in, cd_w_out, gdn_conv_w, gdn_a_log, gdn_dt_bias, gdn_norm_g,
                  mla_q_norm_g, mla_w_qb, mla_kv_norm_g, mla_w_kvb, ffn_w_up, ffn_conv_w, ffn_w_down):
    lb_p = jax.nn.softmax(hgrn_lb_logits.astype(F32), axis=0)
    pcol = jnp.pad(jnp.stack([gdn_a_log, gdn_dt_bias], axis=1),
                   ((0, 0), (0, 0), (SMALL_GA, 128 - SMALL_GA - HEADS)))
    prow = jnp.pad(jnp.stack([gdn_a_log, gdn_dt_bias], axis=-1), ((0, 0), (HEADS, 0), (0, 0)))
    W = dict(
        norm_g=norm_g,
        ab_w_in=ab_w_in.astype(BF16), ab_w_out=ab_w_out.astype(BF16),
        lower_bounds=jnp.cumsum(lb_p, axis=0) - lb_p[0:1],
        hgrn_norm_g=hgrn_norm_g, lru_conv_w=lru_conv_w, lru_conv_b=lru_conv_b,
        lru_wa_bd=jax.vmap(_block_diag_pairs)(lru_w_a), lru_wx_bd=jax.vmap(_block_diag_pairs)(lru_w_x),
        lru_b_a=lru_b_a, lru_b_x=lru_b_x, lru_lambda=lru_lambda,
        cd_w_in=jax.vmap(_prep_cd_w_in)(cd_w_in), cd_w_out=cd_w_out.astype(BF16),
        gdn_conv_w=gdn_conv_w, gdn_pcol=pcol, gdn_prow=prow, gdn_norm_g=gdn_norm_g,
        mla_q_norm_g=mla_q_norm_g, mla_w_qb=jax.vmap(_prep_wqb)(mla_w_qb),
        mla_kv_norm_g=mla_kv_norm_g, mla_w_kvb=mla_w_kvb.astype(BF16),
        ffn_wu=ffn_w_up.astype(BF16), ffn_wd=ffn_w_down.astype(BF16), ffn_cw=ffn_conv_w,
    )
    return W


def kernel(x_prompt, x_sample, c_prompt, c_sample, state_hgrn, state_rglru, state_rglru_conv, state_gdn, state_gdn_conv, cache_mla_latent, cache_mla_krope, state_ffn_conv, ada_w, ada_b, norm_g, ab_w_in, ab_w_out, hgrn_lb_logits, hgrn_norm_g, lru_conv_w, lru_conv_b, lru_w_a, lru_b_a, lru_w_x, lru_b_x, lru_lambda, cd_w_in, cd_w_out, gdn_conv_w, gdn_a_log, gdn_dt_bias, gdn_norm_g, mla_q_norm_g, mla_w_qb, mla_kv_norm_g, mla_w_kvb, ffn_w_up, ffn_conv_w, ffn_w_down):
    bp, bs = x_prompt.shape[0], x_sample.shape[0]
    W = _prep_weights(norm_g, ab_w_in, ab_w_out, hgrn_lb_logits, hgrn_norm_g, lru_conv_w, lru_conv_b, lru_w_a, lru_b_a,
                      lru_w_x, lru_b_x, lru_lambda, cd_w_in, cd_w_out, gdn_conv_w, gdn_a_log, gdn_dt_bias,
                      gdn_norm_g, mla_q_norm_g, mla_w_qb, mla_kv_norm_g, mla_w_kvb, ffn_w_up, ffn_conv_w,
                      ffn_w_down)
    rows = bp + bs
    rows_pad = -(-rows // 8) * 8
    c_all = jnp.concatenate([c_prompt, c_sample, jnp.zeros((rows_pad - rows, D_MODEL), F32)], axis=0)
    mods = _ada_call(c_all, ada_w, ada_b)
    dt_ = x_prompt.dtype
    y_prompt, p_states = _run_group(
        x_prompt, mods[:, :bp],
        jnp.zeros((N_AB, bp, HEADS, HEAD_DIM, HEAD_DIM), F32),
        jnp.zeros((N_AB, bp, HALF), F32),
        jnp.zeros((N_AB, bp, LRU_CONV - 1, HALF), dt_),
        jnp.zeros((N_CD, bp, HEADS, HEAD_DIM, HEAD_DIM), F32),
        jnp.zeros((N_CD, bp, GD_CONV - 1, 3 * HALF), dt_),
        jnp.zeros((N_CD, bp, 0, MLA_KV_RANK), dt_),
        jnp.zeros((N_CD, bp, 0, MLA_ROPE), dt_),
        jnp.zeros((DEPTH, bp, FFN_CONV - 1, 2 * D_FF), dt_),
        W)
    y_sample, s_states = _run_group(
        x_sample, mods[:, bp:rows], state_hgrn, state_rglru, state_rglru_conv, state_gdn, state_gdn_conv,
        cache_mla_latent, cache_mla_krope, state_ffn_conv, W)
    return (y_prompt, y_sample) + tuple(p_states) + tuple(s_states)
```

```python
import functools
import math

import jax
import jax.numpy as jnp
from jax import lax
from jax.experimental import pallas as pl
from jax.experimental.pallas import tpu as pltpu

F32 = jnp.float32
BF16 = jnp.bfloat16

D_MODEL = 1024
DEPTH = 4
CHUNK = 64
HALF = D_MODEL // 2
N_AB = (DEPTH + 1) // 2
N_CD = DEPTH // 2
HEADS = 4
HEAD_DIM = HALF // HEADS
LRU_BLOCKS = 8
LRU_BLOCK = HALF // LRU_BLOCKS
LRU_CONV = 4
LRU_C = 8.0
LRU_GROUP = 8
GD_CONV = 4
MLA_NOPE = 128
MLA_ROPE = 64
MLA_QK = MLA_NOPE + MLA_ROPE
MLA_Q_RANK = 384
MLA_KV_RANK = 256
MLA_SCALE = (MLA_NOPE + MLA_ROPE) ** -0.5
ROPE_THETA = 10000.0
D_FF = 2816
FFN_CONV = 3
FF_TILE = 256
N_FF_TILES = D_FF // FF_TILE
FF_GROUP = 4
FF_SLOTS = 4
FF_AHEAD = 2
EPS = 1e-6
NEG_BIG = -1e30
SQRT_FLOOR = 1e-12
CD_COLS = 2816
SMALL_KR = 0
SMALL_GB = 64
SMALL_GA = 68

VMEM_LIMIT_BYTES = 56 * 1024 * 1024
ATTN_KV_BLOCK_BYTES = 8 * 1024 * 1024


def _cparams(*sem):
    return pltpu.CompilerParams(dimension_semantics=sem, vmem_limit_bytes=VMEM_LIMIT_BYTES)


def _dot(a, b):
    return jnp.dot(a, b, preferred_element_type=F32)


def _dot_nt(a, b):
    return lax.dot_general(a, b, (((1,), (1,)), ((), ())), preferred_element_type=F32)


def _dot_tn(a, b):
    return lax.dot_general(a, b, (((0,), (0,)), ((), ())), preferred_element_type=F32)


def _rms(x, g):
    return x * lax.rsqrt(jnp.mean(x * x, axis=-1, keepdims=True) + EPS) * g


def _silu(x):
    return x * jax.nn.sigmoid(x)


def _softplus(x):
    return jnp.maximum(x, 0.0) + jnp.log1p(jnp.exp(-jnp.abs(x)))


def _gelu_tanh(x):
    return 0.5 * x * (1.0 + jnp.tanh(math.sqrt(2.0 / math.pi) * (x + 0.044715 * (x * x * x))))


def _split3(x):
    x1 = x.astype(BF16)
    r1 = x - x1.astype(F32)
    x2 = r1.astype(BF16)
    x3 = (r1 - x2.astype(F32)).astype(BF16)
    return x1, x2, x3


def _log2(n):
    assert n & (n - 1) == 0
    return n.bit_length() - 1


def _cumsum_rows(x, row):
    n, c = x.shape
    group = min(n, 8)
    x3 = x.reshape(n // group, group, c)
    pos = lax.broadcasted_iota(jnp.int32, x3.shape, 1)
    s = 1
    while s < group:
        x3 = x3 + jnp.where(pos >= s, pltpu.roll(x3, s, 1), 0.0)
        s *= 2
    if n == group:
        return x3.reshape(n, c)
    groups = [x3[0]]
    for gi in range(1, n // group):
        groups.append(x3[gi] + groups[-1][group - 1:group, :])
    return jnp.concatenate(groups, axis=0)


def _block_row_bcast(b, row, h):
    L, n = b.shape
    blk = 2 * h
    if blk >= 8:
        b3 = b.reshape(L // blk, blk, n)
        return jnp.broadcast_to(b3[:, h - 1:h, :], (L // blk, blk, n)).reshape(L, n)
    pos = row & (blk - 1)
    x0 = jnp.where(pos == h - 1, b, 0.0)
    out = x0
    for j in range(1, h + 1):
        out = out + pltpu.roll(x0, j, 0)
    for j in range(1, h):
        out = out + pltpu.roll(x0, L - j, 0)
    return out


def _ada_kernel(c_ref, w_ref, b_ref, o_ref):
    c = _silu(c_ref[...]).astype(BF16)
    o_ref[0] = _dot(c, w_ref[0].astype(BF16)) + b_ref[0]


def _ada_call(c_all, ada_w, ada_b):
    rows = c_all.shape[0]
    tn = 2048
    return pl.pallas_call(
        _ada_kernel,
        out_shape=jax.ShapeDtypeStruct((DEPTH, rows, 6 * D_MODEL), F32),
        grid=(DEPTH, 6 * D_MODEL // tn),
        in_specs=[
            pl.BlockSpec((rows, D_MODEL), lambda l, j: (0, 0)),
            pl.BlockSpec((1, D_MODEL, tn), lambda l, j: (l, 0, j)),
            pl.BlockSpec((1, 1, tn), lambda l, j: (l, 0, j)),
        ],
        out_specs=pl.BlockSpec((1, rows, tn), lambda l, j: (l, 0, j)),
        compiler_params=_cparams("arbitrary", "arbitrary"),
        name="ada_mod",
    )(c_all, ada_w, ada_b.reshape(DEPTH, 1, 6 * D_MODEL))


def _nmm_kernel(x_ref, g_ref, sc_ref, sh_ref, w_ref, o_ref, *maybe_t_ref):
    nb, tt, d = x_ref.shape
    h = _rms(x_ref[...], g_ref[...]) * (1.0 + sc_ref[...]) + sh_ref[...]
    y = _dot(h.reshape(nb * tt, d).astype(BF16), w_ref[...])
    o_ref[...] = y.reshape(nb, tt, y.shape[-1])
    if maybe_t_ref:
        (t_ref,) = maybe_t_ref
        yt = y[:, y.shape[-1] - 128:].T
        t_ref[0] = yt[SMALL_GB:SMALL_GB + 2 * HEADS, :]


def _tiles(B, T):
    if T >= 512:
        return 1, 512
    assert B * T <= 512
    return B, T


def _nmm_call(x, g, scale, shift, w, gates_t=False):
    B, T, D = x.shape
    N = w.shape[1]
    nb, tt = _tiles(B, T)
    out_shape = jax.ShapeDtypeStruct((B, T, N), F32)
    out_specs = pl.BlockSpec((nb, tt, N), lambda b, i: (b, i, 0))
    if gates_t:
        assert nb == 1 and tt % 128 == 0
        out_shape = (out_shape, jax.ShapeDtypeStruct((B, 2 * HEADS, T), F32))
        out_specs = (out_specs, pl.BlockSpec((1, 2 * HEADS, tt), lambda b, i: (b, 0, i)))
    return pl.pallas_call(
        _nmm_kernel,
        out_shape=out_shape,
        grid=(B // nb, T // tt),
        in_specs=[
            pl.BlockSpec((nb, tt, D), lambda b, i: (b, i, 0)),
            pl.BlockSpec((1, D), lambda b, i: (0, 0)),
            pl.BlockSpec((nb, 1, D), lambda b, i: (b, 0, 0)),
            pl.BlockSpec((nb, 1, D), lambda b, i: (b, 0, 0)),
            pl.BlockSpec((D, N), lambda b, i: (0, 0)),
        ],
        out_specs=out_specs,
        compiler_params=_cparams("arbitrary", "arbitrary"),
        name="norm_mod_proj",
    )(x, g.reshape(1, D), scale[:, None, :], shift[:, None, :], w)


def _ffn_kernel(x_ref, oa_ref, ob_ref, wo_ref, go_ref, gateo_ref, g1_ref, sc_ref, sh_ref, gate_ref, g2_ref,
                buf0_ref, wu_ref, wd_ref, cw_ref, o_ref, st_ref, carry_ref, ubuf_ref):
    nb, tt, d = x_ref.shape
    i = pl.program_id(1)

    @pl.when(i == 0)
    def _():
        for c in range(2 * N_FF_TILES):
            carry_ref[c] = buf0_ref[:, :, c * FF_TILE:(c + 1) * FF_TILE]

    oa = oa_ref[...].reshape(nb * tt, HALF)
    ob = ob_ref[...].reshape(nb * tt, HALF)
    mix = _dot(oa, wo_ref[0:HALF, :]) + _dot(ob, wo_ref[HALF:2 * HALF, :])
    x = x_ref[...] + gateo_ref[...] * _rms(mix, go_ref[...]).reshape(nb, tt, d)
    h = (_rms(x, g1_ref[...]) * (1.0 + sc_ref[...]) + sh_ref[...]).reshape(nb * tt, d).astype(BF16)

    def conv(u, slot, c):
        ubuf_ref[slot, :, 8:8 + tt, :] = u
        ubuf_ref[slot, :, 6:8, :] = carry_ref[c]
        cw = cw_ref[:, c * FF_TILE:(c + 1) * FF_TILE]
        y = (cw[0:1, :] * ubuf_ref[slot, :, 6:6 + tt, :] + cw[1:2, :] * ubuf_ref[slot, :, 7:7 + tt, :]
             + cw[2:3, :] * u)
        tail = ubuf_ref[slot, :, 6 + tt:8 + tt, :]
        carry_ref[c] = tail
        st_ref[:, :, c * FF_TILE:(c + 1) * FF_TILE] = tail
        return y

    def up_proj(c):
        wg = wu_ref[:, c * FF_TILE:(c + 1) * FF_TILE]
        wv = wu_ref[:, D_FF + c * FF_TILE:D_FF + (c + 1) * FF_TILE]
        return _dot(h, wg).reshape(nb, tt, FF_TILE), _dot(h, wv).reshape(nb, tt, FF_TILE)

    acc = None
    ahead = [up_proj(c) for c in range(min(FF_AHEAD, N_FF_TILES))]
    for g0 in range(0, N_FF_TILES, FF_GROUP):
        acts = []
        for c in range(g0, min(g0 + FF_GROUP, N_FF_TILES)):
            ug, uv = ahead.pop(0)
            if c + FF_AHEAD < N_FF_TILES:
                ahead.append(up_proj(c + FF_AHEAD))
            slot = 2 * (c % FF_SLOTS)
            yg = conv(ug, slot, c)
            yv = conv(uv, slot + 1, N_FF_TILES + c)
            acts.append((_silu(yg) * yv).reshape(nb * tt, FF_TILE).astype(BF16))
        a = jnp.concatenate(acts, axis=-1) if len(acts) > 1 else acts[0]
        part = _dot(a, wd_ref[g0 * FF_TILE:g0 * FF_TILE + a.shape[-1], :])
        acc = part if acc is None else acc + part
    y = _rms(acc, g2_ref[...]).reshape(nb, tt, d)
    o_ref[...] = x + gate_ref[...] * y


def _ffn_call(x, oa, ob, wo, go, gateo, g1, scale, shift, gate, g2, buf0, wu, wd, cw):
    B, T, D = x.shape
    nb, tt = _tiles(B, T)
    assert T >= FFN_CONV - 1
    return pl.pallas_call(
        _ffn_kernel,
        out_shape=(jax.ShapeDtypeStruct((B, T, D), F32),
                   jax.ShapeDtypeStruct((B, FFN_CONV - 1, 2 * D_FF), F32)),
        grid=(B // nb, T // tt),
        in_specs=[
            pl.BlockSpec((nb, tt, D), lambda b, i: (b, i, 0)),
            pl.BlockSpec((nb, tt, HALF), lambda b, i: (b, i, 0)),
            pl.BlockSpec((nb, tt, HALF), lambda b, i: (b, i, 0)),
            pl.BlockSpec((D, D), lambda b, i: (0, 0), pipeline_mode=pl.Buffered(1)),
            pl.BlockSpec((1, D), lambda b, i: (0, 0)),
            pl.BlockSpec((nb, 1, D), lambda b, i: (b, 0, 0)),
            pl.BlockSpec((1, D), lambda b, i: (0, 0)),
            pl.BlockSpec((nb, 1, D), lambda b, i: (b, 0, 0)),
            pl.BlockSpec((nb, 1, D), lambda b, i: (b, 0, 0)),
            pl.BlockSpec((nb, 1, D), lambda b, i: (b, 0, 0)),
            pl.BlockSpec((1, D), lambda b, i: (0, 0)),
            pl.BlockSpec((nb, FFN_CONV - 1, 2 * D_FF), lambda b, i: (b, 0, 0)),
            pl.BlockSpec((D, 2 * D_FF), lambda b, i: (0, 0), pipeline_mode=pl.Buffered(1)),
            pl.BlockSpec((D_FF, D), lambda b, i: (0, 0), pipeline_mode=pl.Buffered(1)),
            pl.BlockSpec((FFN_CONV, 2 * D_FF), lambda b, i: (0, 0)),
        ],
        out_specs=(pl.BlockSpec((nb, tt, D), lambda b, i: (b, i, 0)),
                   pl.BlockSpec((nb, FFN_CONV - 1, 2 * D_FF), lambda b, i: (b, 0, 0))),
        scratch_shapes=[
            pltpu.VMEM((2 * N_FF_TILES, nb, FFN_CONV - 1, FF_TILE), F32),
            pltpu.VMEM((2 * FF_SLOTS, nb, 8 + tt, FF_TILE), F32),
        ],
        compiler_params=_cparams("arbitrary", "arbitrary"),
        name="conv_ffn",
    )(x, oa, ob, wo, go.reshape(1, D), gateo[:, None, :], g1.reshape(1, D), scale[:, None, :], shift[:, None, :],
      gate[:, None, :], g2.reshape(1, D), buf0, wu, wd, cw)


def _hgrn_kernel(hq_ref, hf_ref, hi_ref, hz_ref, lb_ref, ng_ref, s0_ref, o_ref, st_ref):
    L = hq_ref.shape[1]
    i = pl.program_id(1)

    @pl.when(i == 0)
    def _():
        st_ref[...] = s0_ref[...]

    row = lax.broadcasted_iota(jnp.int32, (L, HEAD_DIM), 0)
    r2 = lax.broadcasted_iota(jnp.int32, (L, L), 0)
    c2 = lax.broadcasted_iota(jnp.int32, (L, L), 1)
    heads = range(HEADS)
    sls = [slice(hd * HEAD_DIM, (hd + 1) * HEAD_DIM) for hd in heads]
    q, k, v, b = ([None] * HEADS for _ in range(4))
    for hd in heads:
        z = hf_ref[0, :, sls[hd]]
        lb = lb_ref[:, sls[hd]]
        en = jnp.exp(-jnp.abs(z))
        inv = 1.0 / (1.0 + en)
        sig_pos = jnp.where(z >= 0.0, inv, en * inv)
        sig_neg = jnp.where(z >= 0.0, en * inv, inv)
        g = jnp.log(lb + (1.0 - lb) * sig_pos)
        k[hd] = (1.0 - lb) * sig_neg
        q[hd] = _silu(hq_ref[0, :, sls[hd]])
        v[hd] = hi_ref[0, :, sls[hd]].astype(BF16)
        b[hd] = _cumsum_rows(g, row)
    att = [jnp.where(r2 == c2, _dot_nt(q[hd].astype(BF16), k[hd].astype(BF16)), 0.0) for hd in heads]
    h = L // 2
    while h >= 1:
        upper = (row & (2 * h - 1)) >= h
        sh = _log2(2 * h)
        same_block = (r2 >> sh) == (c2 >> sh)
        for hd in heads:
            r = _block_row_bcast(b[hd], row, h)
            e = jnp.exp2(jnp.abs(b[hd] - r) * (-math.log2(math.e)))
            qt = jnp.where(upper, q[hd] * e, 0.0).astype(BF16)
            kt = jnp.where(upper, 0.0, k[hd] * e).astype(BF16)
            att[hd] = att[hd] + jnp.where(same_block, _dot_nt(qt, kt), 0.0)
        h //= 2
    for hd in heads:
        st = st_ref[0, hd]
        o = _dot(att[hd].astype(BF16), v[hd]) + _dot_nt((q[hd] * jnp.exp(b[hd])).astype(BF16), st.astype(BF16))
        b_last = b[hd][L - 1:L, :]
        kd = (k[hd] * jnp.exp(b_last - b[hd])).astype(BF16)
        st_ref[0, hd] = jnp.exp(b_last) * st + _dot_tn(v[hd], kd)
        o = _rms(o, ng_ref[...]) * _silu(hz_ref[0, :, sls[hd]])
        o_ref[0, :, sls[hd]] = o.astype(BF16)


def _hgrn_call(proj, lb, ng, s0t):
    B, T, _ = proj.shape
    L = 128 if T % 128 == 0 else T
    assert T % L == 0 and L & (L - 1) == 0 and L >= 8
    col = lambda c: pl.BlockSpec((1, L, HALF), lambda b, i, c=c: (b, i, c))
    return pl.pallas_call(
        _hgrn_kernel,
        out_shape=(jax.ShapeDtypeStruct((B, T, HALF), BF16),
                   jax.ShapeDtypeStruct((B, HEADS, HEAD_DIM, HEAD_DIM), F32)),
        grid=(B, T // L),
        in_specs=[col(0), col(1), col(2), col(3),
                  pl.BlockSpec((1, HALF), lambda b, i: (0, 0)),
                  pl.BlockSpec((1, HEAD_DIM), lambda b, i: (0, 0)),
                  pl.BlockSpec((1, HEADS, HEAD_DIM, HEAD_DIM), lambda b, i: (b, 0, 0, 0))],
        out_specs=(pl.BlockSpec((1, L, HALF), lambda b, i: (b, i, 0)),
                   pl.BlockSpec((1, HEADS, HEAD_DIM, HEAD_DIM), lambda b, i: (b, 0, 0, 0))),
        compiler_params=_cparams("arbitrary", "arbitrary"),
        name="hgrn2",
    )(proj, proj, proj, proj, lb.reshape(1, HALF), ng.reshape(1, HEAD_DIM), s0t)


def _lru_kernel(lx_ref, ly_ref, buf0_ref, h0_ref, cw_ref, cb_ref, wa_ref, wx_ref, ba_ref, bx_ref,
                lam_ref, o_ref, hl_ref, bufo_ref, xp_ref):
    tt = lx_ref.shape[1]
    i = pl.program_id(1)
    npad = LRU_CONV - 1

    @pl.when(i == 0)
    def _():
        xp_ref[8 - npad:8, :] = buf0_ref[0]
        hl_ref[0] = h0_ref[0]

    x = lx_ref[0]
    xp_ref[8:8 + tt, :] = x
    xc = cb_ref[...] + cw_ref[npad:npad + 1, :] * x
    for tap in range(npad):
        xc = xc + cw_ref[tap:tap + 1, :] * xp_ref[8 - npad + tap:8 - npad + tap + tt, :]
    tail = xp_ref[8 + tt - npad:8 + tt, :]
    xp_ref[8 - npad:8, :] = tail
    bufo_ref[0] = tail

    xb = xc.astype(BF16)
    half = HALF // 2
    rpre = jnp.concatenate([_dot(xb[:, 0:half], wa_ref[0]), _dot(xb[:, half:HALF], wa_ref[1])], axis=-1)
    ipre = jnp.concatenate([_dot(xb[:, 0:half], wx_ref[0]), _dot(xb[:, half:HALF], wx_ref[1])], axis=-1)
    r = jax.nn.sigmoid(rpre + ba_ref[...])
    ig = jax.nn.sigmoid(ipre + bx_ref[...])
    log_a = -LRU_C * r * _softplus(-lam_ref[...])
    a = jnp.exp(log_a)
    u = jnp.sqrt(jnp.maximum(-jnp.tanh(log_a) * (1.0 + a * a), SQRT_FLOOR)) * ig * xc

    a = a.reshape(tt // LRU_GROUP, LRU_GROUP, HALF)
    u = u.reshape(tt // LRU_GROUP, LRU_GROUP, HALF)
    pos = lax.broadcasted_iota(jnp.int32, a.shape, 1)
    s = 1
    while s < LRU_GROUP:
        keep = pos >= s
        a_sh = jnp.where(keep, pltpu.roll(a, s, 1), 1.0)
        u_sh = jnp.where(keep, pltpu.roll(u, s, 1), 0.0)
        u = a * u_sh + u
        a = a * a_sh
        s *= 2
    carry = hl_ref[0]
    groups = []
    for gi in range(tt // LRU_GROUP):
        hg = u[gi] + a[gi] * carry
        carry = hg[LRU_GROUP - 1:LRU_GROUP, :]
        groups.append(hg)
    hseq = jnp.concatenate(groups, axis=0) if len(groups) > 1 else groups[0]
    hl_ref[0] = carry
    o_ref[0] = (hseq * _gelu_tanh(ly_ref[0])).astype(BF16)


def _lru_call(proj, buf0, h0, cw, cb, wa_bd, wx_bd, ba, bx, lam):
    B, T, _ = proj.shape
    tt = 256 if T % 256 == 0 else T
    assert T % tt == 0 and T >= LRU_CONV - 1 and tt % 8 == 0
    vec = pl.BlockSpec((1, HALF), lambda b, i: (0, 0))
    wspec = pl.BlockSpec((2, HALF // 2, HALF // 2), lambda b, i: (0, 0, 0))
    return pl.pallas_call(
        _lru_kernel,
        out_shape=(jax.ShapeDtypeStruct((B, T, HALF), BF16),
                   jax.ShapeDtypeStruct((B, 1, HALF), F32),
                   jax.ShapeDtypeStruct((B, LRU_CONV - 1, HALF), F32)),
        grid=(B, T // tt),
        in_specs=[pl.BlockSpec((1, tt, HALF), lambda b, i: (b, i, 4)),
                  pl.BlockSpec((1, tt, HALF), lambda b, i: (b, i, 5)),
                  pl.BlockSpec((1, LRU_CONV - 1, HALF), lambda b, i: (b, 0, 0)),
                  pl.BlockSpec((1, 1, HALF), lambda b, i: (b, 0, 0)),
                  pl.BlockSpec((LRU_CONV, HALF), lambda b, i: (0, 0)),
                  vec, wspec, wspec, vec, vec, vec],
        out_specs=(pl.BlockSpec((1, tt, HALF), lambda b, i: (b, i, 0)),
                   pl.BlockSpec((1, 1, HALF), lambda b, i: (b, 0, 0)),
                   pl.BlockSpec((1, LRU_CONV - 1, HALF), lambda b, i: (b, 0, 0))),
        scratch_shapes=[pltpu.VMEM((8 + tt, HALF), F32)],
        compiler_params=_cparams("arbitrary", "arbitrary"),
        name="rglru",
    )(proj, proj, buf0, h0[:, None, :], cw, cb.reshape(1, HALF), wa_bd, wx_bd,
      ba.reshape(1, HALF), bx.reshape(1, HALF), lam.reshape(1, HALF))


def _gdn_kernel(qkv_ref, gz_ref, sm_ref, smt_ref, buf0_ref, cw_ref, pcol_ref, prow_ref, ng_ref, s0_ref,
                o_ref, st_ref, bufo_ref, xp_ref, *, L):
    tt = qkv_ref.shape[1]
    nc = tt // L
    sh = _log2(L)
    i = pl.program_id(1)
    npad = GD_CONV - 1

    @pl.when(i == 0)
    def _():
        xp_ref[8 - npad:8, :] = buf0_ref[0]
        st_ref[...] = s0_ref[...]

    x = qkv_ref[0]
    xp_ref[8:8 + tt, :] = x
    xc = cw_ref[npad:npad + 1, :] * x
    for tap in range(npad):
        xc = xc + cw_ref[tap:tap + 1, :] * xp_ref[8 - npad + tap:8 - npad + tap + tt, :]
    tail = xp_ref[8 + tt - npad:8 + tt, :]
    xp_ref[8 - npad:8, :] = tail
    bufo_ref[0] = tail
    xc = _silu(xc)

    r2 = lax.broadcasted_iota(jnp.int32, (tt, tt), 0)
    c2 = lax.broadcasted_iota(jnp.int32, (tt, tt), 1)
    same = (r2 >> sh) == (c2 >> sh)
    incl = jnp.logical_and(same, c2 <= r2)
    strict = jnp.logical_and(same, c2 < r2)
    tri_lo = jnp.where(incl, 1.0, 0.0).astype(BF16)
    tri_up = jnp.where(jnp.logical_and(same, r2 <= c2), 1.0, 0.0).astype(BF16)
    eye = jnp.where(r2 == c2, 1.0, 0.0)

    sm = sm_ref[0]
    beta_cols = jax.nn.sigmoid(sm)
    la_cols = -jnp.exp(pcol_ref[0:1, :]) * _softplus(sm + pcol_ref[1:2, :])
    c1, c2_, c3 = _split3(la_cols)
    g_cols = _dot(tri_lo, c1) + _dot(tri_lo, c2_) + _dot(tri_lo, c3)
    la_rows = -jnp.exp(prow_ref[:, 0:1]) * _softplus(smt_ref[0] + prow_ref[:, 1:2])
    w1, w2, w3 = _split3(la_rows)
    g_rows = _dot(w1, tri_up) + _dot(w2, tri_up) + _dot(w3, tri_up)

    heads = range(HEADS)
    q, k, v, beta, gcol, kb, dec, m = ([None] * HEADS for _ in range(8))
    for hd in heads:
        qh = xc[:, hd * HEAD_DIM:(hd + 1) * HEAD_DIM]
        kh = xc[:, HALF + hd * HEAD_DIM:HALF + (hd + 1) * HEAD_DIM]
        v[hd] = xc[:, 2 * HALF + hd * HEAD_DIM:2 * HALF + (hd + 1) * HEAD_DIM]
        q[hd] = qh * lax.rsqrt(jnp.sum(qh * qh, axis=-1, keepdims=True) + EPS) * (HEAD_DIM ** -0.5)
        k[hd] = kh * lax.rsqrt(jnp.sum(kh * kh, axis=-1, keepdims=True) + EPS)
        beta[hd] = beta_cols[:, SMALL_GB + hd:SMALL_GB + hd + 1]
        gcol[hd] = g_cols[:, SMALL_GA + hd:SMALL_GA + hd + 1]
        grow = g_rows[HEADS + hd:HEADS + hd + 1, :]
        kb[hd] = k[hd].astype(BF16)
        dec[hd] = jnp.exp(jnp.where(incl, gcol[hd] - grow, NEG_BIG))
        m[hd] = beta[hd] * _dot_nt(kb[hd], kb[hd]) * jnp.where(strict, dec[hd], 0.0)

    pair = (r2 >> 1) == (c2 >> 1)
    tinv = [eye - jnp.where(pair, m[hd], 0.0) for hd in heads]
    s = 2
    while s < L:
        ssh = _log2(s)
        lower_left = jnp.logical_and((r2 >> (ssh + 1)) == (c2 >> (ssh + 1)), (r2 >> ssh) != (c2 >> ssh))
        tb = [tinv[hd].astype(BF16) for hd in heads]
        tc = [_dot(tb[hd], jnp.where(lower_left, m[hd], 0.0).astype(BF16)).astype(BF16) for hd in heads]
        tinv = [tinv[hd] - _dot(tc[hd], tb[hd]) for hd in heads]
        s *= 2

    eg = [jnp.exp(gcol[hd]) for hd in heads]
    sol = [_dot(tinv[hd].astype(BF16),
                jnp.concatenate([beta[hd] * v[hd], (beta[hd] * eg[hd]) * k[hd]], axis=-1).astype(BF16))
           for hd in heads]
    u_v = [sol[hd][:, 0:HEAD_DIM] for hd in heads]
    w_k = [sol[hd][:, HEAD_DIM:2 * HEAD_DIM].astype(BF16) for hd in heads]
    qb = [q[hd].astype(BF16) for hd in heads]
    qk = [(_dot_nt(qb[hd], kb[hd]) * dec[hd]).astype(BF16) for hd in heads]

    a_c, p_c, n_c = ([[None] * nc for _ in heads] for _ in range(3))
    for c in range(nc):
        rs = slice(c * L, (c + 1) * L)
        for hd in heads:
            g_last = gcol[hd][(c + 1) * L - 1:(c + 1) * L, :]
            kd = (k[hd][rs] * jnp.exp(g_last - gcol[hd][rs])).astype(BF16)
            a_c[hd][c] = jnp.exp(g_last)
            p_c[hd][c] = (-_dot_tn(kd, w_k[hd][rs])).astype(BF16)
            n_c[hd][c] = _dot_tn(kd, u_v[hd][rs].astype(BF16))
    S = [[st_ref[0, hd]] for hd in heads]
    for c in range(nc):
        for hd in heads:
            s_cur = S[hd][c]
            S[hd].append(a_c[hd][c] * s_cur + _dot(p_c[hd][c], s_cur.astype(BF16)) + n_c[hd][c])
    us = [[] for _ in heads]
    inters = [[] for _ in heads]
    for c in range(nc):
        rs = slice(c * L, (c + 1) * L)
        for hd in heads:
            Sb = S[hd][c].astype(BF16)
            us[hd].append(u_v[hd][rs] - _dot(w_k[hd][rs], Sb))
            inters[hd].append(eg[hd][rs] * _dot(qb[hd][rs], Sb))
    for hd in heads:
        sl = slice(hd * HEAD_DIM, (hd + 1) * HEAD_DIM)
        st_ref[0, hd] = S[hd][nc]
        u_all = jnp.concatenate(us[hd], axis=0) if nc > 1 else us[hd][0]
        inter = jnp.concatenate(inters[hd], axis=0) if nc > 1 else inters[hd][0]
        o = _dot(qk[hd], u_all.astype(BF16)) + inter
        o = _rms(o, ng_ref[...]) * _silu(gz_ref[0, :, sl])
        o_ref[0, :, sl] = o.astype(BF16)


def _gdn_call(proj, small_t, buf0, cw, pcol, prow, ng, s0):
    B, T, _ = proj.shape
    L = CHUNK if T % CHUNK == 0 else T
    tt = 256 if T % 256 == 0 else T
    assert T % tt == 0 and tt % L == 0 and T >= GD_CONV - 1 and L >= 2
    W = 3 * HALF
    return pl.pallas_call(
        functools.partial(_gdn_kernel, L=L),
        out_shape=(jax.ShapeDtypeStruct((B, T, HALF), BF16),
                   jax.ShapeDtypeStruct((B, HEADS, HEAD_DIM, HEAD_DIM), F32),
                   jax.ShapeDtypeStruct((B, GD_CONV - 1, W), F32)),
        grid=(B, T // tt),
        in_specs=[pl.BlockSpec((1, tt, W), lambda b, i: (b, i, 0)),
                  pl.BlockSpec((1, tt, HALF), lambda b, i: (b, i, 3)),
                  pl.BlockSpec((1, tt, 128), lambda b, i: (b, i, 21)),
                  pl.BlockSpec((1, 2 * HEADS, tt), lambda b, i: (b, 0, i)),
                  pl.BlockSpec((1, GD_CONV - 1, W), lambda b, i: (b, 0, 0)),
                  pl.BlockSpec((GD_CONV, W), lambda b, i: (0, 0)),
                  pl.BlockSpec((2, 128), lambda b, i: (0, 0)),
                  pl.BlockSpec((2 * HEADS, 2), lambda b, i: (0, 0)),
                  pl.BlockSpec((1, HEAD_DIM), lambda b, i: (0, 0)),
                  pl.BlockSpec((1, HEADS, HEAD_DIM, HEAD_DIM), lambda b, i: (b, 0, 0, 0))],
        out_specs=(pl.BlockSpec((1, tt, HALF), lambda b, i: (b, i, 0)),
                   pl.BlockSpec((1, HEADS, HEAD_DIM, HEAD_DIM), lambda b, i: (b, 0, 0, 0)),
                   pl.BlockSpec((1, GD_CONV - 1, W), lambda b, i: (b, 0, 0))),
        scratch_shapes=[pltpu.VMEM((8 + tt, W), F32)],
        compiler_params=_cparams("arbitrary", "arbitrary"),
        name="gated_deltanet",
    )(proj, proj, proj, small_t, buf0, cw, pcol, prow, ng.reshape(1, HEAD_DIM), s0)


def _rope64(x, cs, sn):
    half = MLA_ROPE // 2
    swapped = jnp.concatenate([x[:, half:], x[:, :half]], axis=-1)
    return x * cs + swapped * sn


def _expand_kv(c_kv, k_r, wkvb_ref, kf_ref, v_ref):
    kv = _dot(c_kv.astype(BF16), wkvb_ref[...])
    for hd in range(HEADS):
        base = hd * 2 * HEAD_DIM
        kf_ref[0, hd] = jnp.concatenate([kv[:, base:base + MLA_NOPE], k_r], axis=-1).astype(BF16)
        v = kv[:, base + MLA_NOPE:base + 2 * HEAD_DIM]
        v_ref[0, hd] = jnp.concatenate([v, jnp.ones_like(v)], axis=-1).astype(BF16)


def _mla_prep_kernel(qa_ref, kva_ref, sm_ref, cs_ref, sn_ref, qng_ref, wqb_ref, kvng_ref, wkvb_ref, *rest):
    q_ref, ckv_ref, kr_ref, kf_ref, v_ref = rest[-5:]
    ckv_ref = ckv_ref.at[0]
    kr_ref = kr_ref.at[0]
    cs = cs_ref[...]
    sn = sn_ref[...]
    qn = _rms(qa_ref[0], qng_ref[...]).astype(BF16)
    qh = _dot(qn, wqb_ref[...]) * (MLA_SCALE * math.log2(math.e))
    for hd in range(HEADS):
        nope = qh[:, hd * MLA_NOPE:(hd + 1) * MLA_NOPE]
        off = HEADS * MLA_NOPE + hd * MLA_ROPE
        rot = _rope64(qh[:, off:off + MLA_ROPE], cs, sn)
        q_ref[0, hd] = jnp.concatenate([nope, rot], axis=-1).astype(BF16)
    c_kv = _rms(kva_ref[0], kvng_ref[...])
    ckv_ref[0] = c_kv
    k_r = _rope64(sm_ref[0, :, SMALL_KR:SMALL_KR + MLA_ROPE], cs, sn)
    kr_ref[0] = k_r
    _expand_kv(c_kv, k_r, wkvb_ref, kf_ref, v_ref)


def _mla_prep_call(proj, cs, sn, qng, wqb, kvng, wkvb, j, state_bufs):
    B, T, _ = proj.shape
    tt = 512 if T % 512 == 0 else T
    c2 = lambda b, i: (0, 0)
    any_spec = pl.BlockSpec(memory_space=pl.ANY)
    extra, aliases = [], {}
    n_fixed = 9
    if state_bufs is not None:
        aliases[n_fixed] = 1
        aliases[n_fixed + 1] = 2
        extra += list(state_bufs)
    return pl.pallas_call(
        _mla_prep_kernel,
        out_shape=(jax.ShapeDtypeStruct((B, HEADS, T, MLA_QK), BF16),
                   jax.ShapeDtypeStruct((N_CD, B, T, MLA_KV_RANK), F32),
                   jax.ShapeDtypeStruct((N_CD, B, T, MLA_ROPE), F32),
                   jax.ShapeDtypeStruct((B, HEADS, T, MLA_QK), BF16),
                   jax.ShapeDtypeStruct((B, HEADS, T, 2 * HEAD_DIM), BF16)),
        grid=(B, T // tt),
        in_specs=[pl.BlockSpec((1, tt, MLA_Q_RANK), lambda b, i: (b, i, 6)),
                  pl.BlockSpec((1, tt, MLA_KV_RANK), lambda b, i: (b, i, 8)),
                  pl.BlockSpec((1, tt, 128), lambda b, i: (b, i, 21)),
                  pl.BlockSpec((tt, MLA_ROPE), lambda b, i: (i, 0)),
                  pl.BlockSpec((tt, MLA_ROPE), lambda b, i: (i, 0)),
                  pl.BlockSpec((1, MLA_Q_RANK), c2),
                  pl.BlockSpec((MLA_Q_RANK, HEADS * MLA_QK), c2),
                  pl.BlockSpec((1, MLA_KV_RANK), c2),
                  pl.BlockSpec((MLA_KV_RANK, HEADS * 2 * HEAD_DIM), c2)] + [any_spec] * len(extra),
        out_specs=(pl.BlockSpec((1, HEADS, tt, MLA_QK), lambda b, i: (b, 0, i, 0)),
                   pl.BlockSpec((1, 1, tt, MLA_KV_RANK), lambda b, i: (j, b, i, 0)),
                   pl.BlockSpec((1, 1, tt, MLA_ROPE), lambda b, i: (j, b, i, 0)),
                   pl.BlockSpec((1, HEADS, tt, MLA_QK), lambda b, i: (b, 0, i, 0)),
                   pl.BlockSpec((1, HEADS, tt, 2 * HEAD_DIM), lambda b, i: (b, 0, i, 0))),
        input_output_aliases=aliases,
        compiler_params=_cparams("arbitrary", "arbitrary"),
        name="mla_prep",
    )(proj, proj, proj, cs, sn, qng.reshape(1, MLA_Q_RANK), wqb, kvng.reshape(1, MLA_KV_RANK), wkvb, *extra)


def _attn_cached_kernel(q_ref, latp_ref, krp_ref, latn_ref, krn_ref, w_ref, o_ref):
    T = q_ref.shape[2]
    P = latp_ref.shape[1]
    sh = _log2(CHUNK)
    latp = latp_ref[0].astype(BF16)
    krp = krp_ref[0].astype(BF16)
    latn = latn_ref[0, 0].astype(BF16)
    krn = krn_ref[0, 0].astype(BF16)
    q_chunk = (P + lax.broadcasted_iota(jnp.int32, (T, 1), 0)) >> sh
    vis_p = (lax.broadcasted_iota(jnp.int32, (T, P), 1) >> sh) <= q_chunk
    vis_n = ((P + lax.broadcasted_iota(jnp.int32, (T, T), 1)) >> sh) <= q_chunk
    for hd in range(HEADS):
        base = hd * 2 * HEAD_DIM
        q = q_ref[0, hd]
        q_lat = _dot_nt(q[:, 0:MLA_NOPE], w_ref[:, base:base + MLA_NOPE]).astype(BF16)
        q_rope = q[:, MLA_NOPE:MLA_QK]
        s_p = jnp.where(vis_p, _dot_nt(q_lat, latp) + _dot_nt(q_rope, krp), NEG_BIG)
        s_n = jnp.where(vis_n, _dot_nt(q_lat, latn) + _dot_nt(q_rope, krn), NEG_BIG)
        m = jnp.maximum(jnp.max(s_p, axis=-1, keepdims=True), jnp.max(s_n, axis=-1, keepdims=True))
        p_p = jnp.exp2(s_p - m)
        p_n = jnp.exp2(s_n - m)
        denom = jnp.sum(p_p, axis=-1, keepdims=True) + jnp.sum(p_n, axis=-1, keepdims=True)
        o_lat = (_dot(p_p.astype(BF16), latp) + _dot(p_n.astype(BF16), latn)) / denom
        o = _dot(o_lat.astype(BF16), w_ref[:, base + MLA_NOPE:base + 2 * HEAD_DIM])
        o_ref[0, :, hd * HEAD_DIM:(hd + 1) * HEAD_DIM] = o.astype(BF16)


def _attn_cached_call(q, lat_past, kr_past, lat_all, kr_all, j, wkvb):
    B, H, T, _ = q.shape
    P = lat_past.shape[1]
    return pl.pallas_call(
        _attn_cached_kernel,
        out_shape=jax.ShapeDtypeStruct((B, T, H * HEAD_DIM), BF16),
        grid=(B,),
        in_specs=[pl.BlockSpec((1, H, T, MLA_QK), lambda b: (b, 0, 0, 0)),
                  pl.BlockSpec((1, P, MLA_KV_RANK), lambda b: (b, 0, 0)),
                  pl.BlockSpec((1, P, MLA_ROPE), lambda b: (b, 0, 0)),
                  pl.BlockSpec((1, 1, T, MLA_KV_RANK), lambda b: (j, b, 0, 0)),
                  pl.BlockSpec((1, 1, T, MLA_ROPE), lambda b: (j, b, 0, 0)),
                  pl.BlockSpec((MLA_KV_RANK, H * 2 * HEAD_DIM), lambda b: (0, 0))],
        out_specs=pl.BlockSpec((1, T, H * HEAD_DIM), lambda b: (b, 0, 0)),
        compiler_params=_cparams("arbitrary"),
        name="mla_attention_cached",
    )(q, lat_past, kr_past, lat_all, kr_all, wkvb)


def _attn_kernel(q_ref, k_ref, v_ref, o_ref, m_ref, acc_ref, sa_ref, sb_ref, **static):
    for hh in range(q_ref.shape[1]):
        _attn_head(q_ref.at[0, hh], k_ref.at[0, hh], v_ref.at[0, hh], o_ref.at[0, :, hh * HEAD_DIM:(hh + 1) * HEAD_DIM],
                   m_ref, acc_ref, sa_ref, sb_ref, **static)


def _attn_head(q_ref, k_ref, v_ref, o_ref, m_ref, acc_ref, sa_ref, sb_ref, *, past_len, tq, tk, nk):
    i = pl.program_id(2)
    sh = _log2(CHUNK)
    lanes = HEAD_DIM
    m_ref[...] = jnp.full(m_ref.shape, NEG_BIG, F32)
    acc_ref[...] = jnp.zeros(acc_ref.shape, F32)
    q = q_ref[...]
    q_lo = past_len + i * tq
    n_full = jnp.minimum(nk, (((q_lo >> sh) + 1) * CHUNK) // tk)
    n_need = jnp.minimum(nk, ((((q_lo + tq - 1) >> sh) + 1) * CHUNK + tk - 1) // tk)

    def scores(j):
        return _dot_nt(q, k_ref[pl.ds(pl.multiple_of(j * tk, tk), tk), :])

    def masked_scores(j):
        k_lo = j * tk
        qc = (q_lo + lax.broadcasted_iota(jnp.int32, (tq, tk), 0)) >> sh
        kc = (k_lo + lax.broadcasted_iota(jnp.int32, (tq, tk), 1)) >> sh
        return jnp.where(kc <= qc, scores(j), NEG_BIG)

    def update(j, s_ref):
        k_lo = pl.multiple_of(j * tk, tk)
        m_prev = m_ref[...]
        m_new = jnp.maximum(m_prev, jnp.max(s_ref[...], axis=-1, keepdims=True))
        alpha = jnp.exp2(m_prev - m_new)
        if tk % lanes == 0:
            p = jnp.exp2(s_ref[...] - jnp.tile(m_new, (1, tk // lanes)))
        else:
            p = jnp.exp2(s_ref[...] - m_new[:, 0:1])
        pv = _dot(p.astype(BF16), v_ref[pl.ds(k_lo, tk), :])
        acc_ref[...] = jnp.tile(alpha, (1, 2)) * acc_ref[...] + pv
        m_ref[...] = m_new

    def body_pair(g, carry):
        sb_ref[...] = scores(2 * g + 1)
        update(2 * g, sa_ref)
        sa_ref[...] = scores(2 * g + 2)
        update(2 * g + 1, sb_ref)
        return carry

    sa_ref[...] = scores(0)
    n_pairs = jnp.maximum(n_full - 1, 0) // 2
    lax.fori_loop(0, n_pairs, body_pair, 0)

    left = n_full - 2 * n_pairs
    has_masked = n_need > n_full
    j_masked = jnp.minimum(n_full, nk - 1)

    @pl.when(left == 1)
    def _():
        sb_ref[...] = masked_scores(j_masked)
        update(n_full - 1, sa_ref)

    @pl.when(jnp.logical_and(left == 1, has_masked))
    def _():
        update(n_full, sb_ref)

    @pl.when(left == 2)
    def _():
        sb_ref[...] = scores(n_full - 1)
        update(n_full - 2, sa_ref)
        sa_ref[...] = masked_scores(j_masked)
        update(n_full - 1, sb_ref)

    @pl.when(jnp.logical_and(left == 2, has_masked))
    def _():
        update(n_full, sa_ref)

    def body_masked(j, carry):
        sa_ref[...] = masked_scores(j)
        update(j, sa_ref)
        return carry

    lax.fori_loop(jnp.where(n_full > 0, n_full + 1, 0), n_need, body_masked, 0)
    o_ref[...] = (acc_ref[:, 0:HEAD_DIM] / acc_ref[:, HEAD_DIM:2 * HEAD_DIM]).astype(BF16)


def _attn_call(q, kf, v, past_len):
    B, H, Tq, _ = q.shape
    Tk = kf.shape[2]
    tq = 1024 if Tq % 1024 == 0 else Tq
    tk = 1024 if Tk % 1024 == 0 else Tk
    nk = Tk // tk
    kv_bytes = Tk * (MLA_QK + 2 * HEAD_DIM) * 2
    hb = H if H * kv_bytes <= ATTN_KV_BLOCK_BYTES else 1
    return pl.pallas_call(
        functools.partial(_attn_kernel, past_len=past_len, tq=tq, tk=tk, nk=nk),
        out_shape=jax.ShapeDtypeStruct((B, Tq, H * HEAD_DIM), BF16),
        grid=(B, H // hb, Tq // tq),
        in_specs=[pl.BlockSpec((1, hb, tq, MLA_QK), lambda b, h, i: (b, h, i, 0)),
                  pl.BlockSpec((1, hb, Tk, MLA_QK), lambda b, h, i: (b, h, 0, 0)),
                  pl.BlockSpec((1, hb, Tk, 2 * HEAD_DIM), lambda b, h, i: (b, h, 0, 0))],
        out_specs=pl.BlockSpec((1, tq, hb * HEAD_DIM), lambda b, h, i: (b, i, h)),
        scratch_shapes=[pltpu.VMEM((tq, HEAD_DIM), F32), pltpu.VMEM((tq, 2 * HEAD_DIM), F32),
                        pltpu.VMEM((tq, tk), F32), pltpu.VMEM((tq, tk), F32)],
        compiler_params=_cparams("arbitrary", "arbitrary", "arbitrary"),
        name="mla_attention",
    )(q, kf, v)


def _block_diag_pairs(w):
    per = (HALF // 2) // LRU_BLOCK
    w4 = w.reshape(2, per, LRU_BLOCK, LRU_BLOCK)
    eye = jnp.eye(per, dtype=w.dtype)
    out = w4[:, :, :, None, :] * eye[None, :, None, :, None]
    return out.reshape(2, HALF // 2, HALF // 2).astype(BF16)


def _prep_cd_w_in(w):
    o = 3 * HALF
    qkv, gz = w[:, :o], w[:, o:o + HALF]
    o += HALF
    gb, ga = w[:, o:o + HEADS], w[:, o + HEADS:o + 2 * HEADS]
    o += 2 * HEADS
    qa, kva = w[:, o:o + MLA_Q_RANK], w[:, o + MLA_Q_RANK:o + MLA_Q_RANK + MLA_KV_RANK]
    o += MLA_Q_RANK + MLA_KV_RANK
    kr = w[:, o:o + MLA_ROPE]
    assert (SMALL_KR, SMALL_GB, SMALL_GA) == (0, MLA_ROPE, MLA_ROPE + HEADS)
    fill = jnp.zeros((w.shape[0], 128 - MLA_ROPE - 2 * HEADS), w.dtype)
    out = jnp.concatenate([qkv, gz, kva, qa, kr, gb, ga, fill], axis=-1)
    assert out.shape[1] == CD_COLS
    return out.astype(BF16)


def _prep_wqb(w):
    w4 = w.reshape(MLA_Q_RANK, HEADS, MLA_QK)
    nope = w4[:, :, :MLA_NOPE].reshape(MLA_Q_RANK, HEADS * MLA_NOPE)
    rope = w4[:, :, MLA_NOPE:].reshape(MLA_Q_RANK, HEADS * MLA_ROPE)
    return jnp.concatenate([nope, rope], axis=-1).astype(BF16)


def _rope_tables(T, past_len):
    half = MLA_ROPE // 2
    freqs = jnp.exp(-math.log(ROPE_THETA) * jnp.arange(half, dtype=F32) / half)
    pos = past_len + jnp.arange(T, dtype=jnp.int32)
    ang = pos.astype(F32)[:, None] * freqs
    cos, sin = jnp.cos(ang), jnp.sin(ang)
    return jnp.concatenate([cos, cos], axis=-1), jnp.concatenate([-sin, sin], axis=-1)


def _run_group(x, mods, hg_s, lru_h, lru_buf, gd_s, gd_buf, lat_past, kr_past, ffn_buf, W):
    B, T, _ = x.shape
    n_hg, n_lru, n_lrub, n_gd, n_gdb, n_ffn = ([] for _ in range(6))
    mla_state = None
    for l in range(DEPTH):
        j = l // 2
        shift1, scale1, gate1, shift2, scale2, gate2 = jnp.split(mods[l], 6, axis=-1)
        g = W['norm_g'][l]
        if l % 2 == 0:
            proj = _nmm_call(x, g[0], scale1, shift1, W['ab_w_in'][j])
            o_a, s_hg_t = _hgrn_call(proj, W['lower_bounds'][j], W['hgrn_norm_g'][j],
                                     jnp.swapaxes(hg_s[j], -1, -2))
            o_b, s_lru, s_lrub = _lru_call(proj, lru_buf[j], lru_h[j], W['lru_conv_w'][j], W['lru_conv_b'][j],
                                           W['lru_wa_bd'][j], W['lru_wx_bd'][j], W['lru_b_a'][j],
                                           W['lru_b_x'][j], W['lru_lambda'][j])
            n_hg.append(jnp.swapaxes(s_hg_t, -1, -2))
            n_lru.append(s_lru[:, 0, :])
            n_lrub.append(s_lrub)
            mix_a, mix_b, w_out = o_a, o_b, W['ab_w_out'][j]
        else:
            past_len = lat_past.shape[2]
            if _tiles(B, T)[0] == 1:
                proj, small_t = _nmm_call(x, g[0], scale1, shift1, W['cd_w_in'][j], gates_t=True)
            else:
                proj = _nmm_call(x, g[0], scale1, shift1, W['cd_w_in'][j])
                small_t = jnp.swapaxes(proj[:, :, CD_COLS - 128 + SMALL_GB:CD_COLS - 128 + SMALL_GB + 2 * HEADS], 1, 2)
            o_c, s_gd, s_gdb = _gdn_call(proj, small_t, gd_buf[j], W['gdn_conv_w'][j], W['gdn_pcol'][j],
                                         W['gdn_prow'][j], W['gdn_norm_g'][j], gd_s[j])
            cs, sn = _rope_tables(T, past_len)
            q, lat_all, kr_all, kf, v = _mla_prep_call(proj, cs, sn, W['mla_q_norm_g'][j], W['mla_w_qb'][j],
                                                       W['mla_kv_norm_g'][j], W['mla_w_kvb'][j], j, mla_state)
            mla_state = (lat_all, kr_all)
            if past_len > 0:
                o_d = _attn_cached_call(q, lat_past[j], kr_past[j], lat_all, kr_all, j, W['mla_w_kvb'][j])
            else:
                o_d = _attn_call(q, kf, v, 0)
            n_gd.append(s_gd)
            n_gdb.append(s_gdb)
            mix_a, mix_b, w_out = o_c, o_d, W['cd_w_out'][j]
        x, s_ffn = _ffn_call(x, mix_a, mix_b, w_out, g[1], gate1, g[2], scale2, shift2, gate2, g[3], ffn_buf[l],
                             W['ffn_wu'][l], W['ffn_wd'][l], W['ffn_cw'][l])
        n_ffn.append(s_ffn)
    return x, (jnp.stack(n_hg), jnp.stack(n_lru), jnp.stack(n_lrub), jnp.stack(n_gd), jnp.stack(n_gdb),
               mla_state[0], mla_state[1], jnp.stack(n_ffn))


def _prep_weights(norm_g, ab_w_in, ab_w_out, hgrn_lb_logits, hgrn_norm_g, lru_conv_w, lru_conv_b, lru_w_a, lru_b_a,
                  lru_w_x, lru_b_x, lru_lambda, cd_w_in, cd_w_out, gdn_conv_w, gdn_a_log, gdn_dt_bias, gdn_norm_g,
                  mla_q_norm_g, mla_w_qb, mla_kv_norm_g, mla_w_kvb, ffn_w_up, ffn_conv_w, ffn_w_down):
    lb_p = jax.nn.softmax(hgrn_lb_logits.astype(F32), axis=0)
    pcol = jnp.pad(jnp.stack([gdn_a_log, gdn_dt_bias], axis=1),
                   ((0, 0), (0, 0), (SMALL_GA, 128 - SMALL_GA - HEADS)))
    prow = jnp.pad(jnp.stack([gdn_a_log, gdn_dt_bias], axis=-1), ((0, 0), (HEADS, 0), (0, 0)))
    W = dict(
        norm_g=norm_g,
        ab_w_in=ab_w_in.astype(BF16), ab_w_out=ab_w_out.astype(BF16),
        lower_bounds=jnp.cumsum(lb_p, axis=0) - lb_p[0:1],
        hgrn_norm_g=hgrn_norm_g, lru_conv_w=lru_conv_w, lru_conv_b=lru_conv_b,
        lru_wa_bd=jax.vmap(_block_diag_pairs)(lru_w_a), lru_wx_bd=jax.vmap(_block_diag_pairs)(lru_w_x),
        lru_b_a=lru_b_a, lru_b_x=lru_b_x, lru_lambda=lru_lambda,
        cd_w_in=jax.vmap(_prep_cd_w_in)(cd_w_in), cd_w_out=cd_w_out.astype(BF16),
        gdn_conv_w=gdn_conv_w, gdn_pcol=pcol, gdn_prow=prow, gdn_norm_g=gdn_norm_g,
        mla_q_norm_g=mla_q_norm_g, mla_w_qb=jax.vmap(_prep_wqb)(mla_w_qb),
        mla_kv_norm_g=mla_kv_norm_g, mla_w_kvb=mla_w_kvb.astype(BF16),
        ffn_wu=ffn_w_up.astype(BF16), ffn_wd=ffn_w_down.astype(BF16), ffn_cw=ffn_conv_w,
    )
    return W


def kernel(x_prompt, x_sample, c_prompt, c_sample, state_hgrn, state_rglru, state_rglru_conv, state_gdn, state_gdn_conv, cache_mla_latent, cache_mla_krope, state_ffn_conv, ada_w, ada_b, norm_g, ab_w_in, ab_w_out, hgrn_lb_logits, hgrn_norm_g, lru_conv_w, lru_conv_b, lru_w_a, lru_b_a, lru_w_x, lru_b_x, lru_lambda, cd_w_in, cd_w_out, gdn_conv_w, gdn_a_log, gdn_dt_bias, gdn_norm_g, mla_q_norm_g, mla_w_qb, mla_kv_norm_g, mla_w_kvb, ffn_w_up, ffn_conv_w, ffn_w_down):
    bp, bs = x_prompt.shape[0], x_sample.shape[0]
    W = _prep_weights(norm_g, ab_w_in, ab_w_out, hgrn_lb_logits, hgrn_norm_g, lru_conv_w, lru_conv_b, lru_w_a, lru_b_a,
                      lru_w_x, lru_b_x, lru_lambda, cd_w_in, cd_w_out, gdn_conv_w, gdn_a_log, gdn_dt_bias,
                      gdn_norm_g, mla_q_norm_g, mla_w_qb, mla_kv_norm_g, mla_w_kvb, ffn_w_up, ffn_conv_w,
                      ffn_w_down)
    rows = bp + bs
    rows_pad = -(-rows // 8) * 8
    c_all = jnp.concatenate([c_prompt, c_sample, jnp.zeros((rows_pad - rows, D_MODEL), F32)], axis=0)
    mods = _ada_call(c_all, ada_w, ada_b)
    dt_ = x_prompt.dtype
    y_prompt, p_states = _run_group(
        x_prompt, mods[:, :bp],
        jnp.zeros((N_AB, bp, HEADS, HEAD_DIM, HEAD_DIM), F32),
        jnp.zeros((N_AB, bp, HALF), F32),
        jnp.zeros((N_AB, bp, LRU_CONV - 1, HALF), dt_),
        jnp.zeros((N_CD, bp, HEADS, HEAD_DIM, HEAD_DIM), F32),
        jnp.zeros((N_CD, bp, GD_CONV - 1, 3 * HALF), dt_),
        jnp.zeros((N_CD, bp, 0, MLA_KV_RANK), dt_),
        jnp.zeros((N_CD, bp, 0, MLA_ROPE), dt_),
        jnp.zeros((DEPTH, bp, FFN_CONV - 1, 2 * D_FF), dt_),
        W)
    y_sample, s_states = _run_group(
        x_sample, mods[:, bp:rows], state_hgrn, state_rglru, state_rglru_conv, state_gdn, state_gdn_conv,
        cache_mla_latent, cache_mla_krope, state_ffn_conv, W)
    return (y_prompt, y_sample) + tuple(p_states) + tuple(s_states)
```

```python
import functools
import math

import jax
import jax.numpy as jnp
from jax import lax
from jax.experimental import pallas as pl
from jax.experimental.pallas import tpu as pltpu

F32 = jnp.float32
BF16 = jnp.bfloat16

D_MODEL = 1024
DEPTH = 4
CHUNK = 64
HALF = D_MODEL // 2
N_AB = (DEPTH + 1) // 2
N_CD = DEPTH // 2
HEADS = 4
HEAD_DIM = HALF // HEADS
LRU_BLOCKS = 8
LRU_BLOCK = HALF // LRU_BLOCKS
LRU_CONV = 4
LRU_C = 8.0
LRU_GROUP = 8
GD_CONV = 4
MLA_NOPE = 128
MLA_ROPE = 64
MLA_QK = MLA_NOPE + MLA_ROPE
MLA_Q_RANK = 384
MLA_KV_RANK = 256
MLA_SCALE = (MLA_NOPE + MLA_ROPE) ** -0.5
ROPE_THETA = 10000.0
D_FF = 2816
FFN_CONV = 3
FF_TILE = 256
N_FF_TILES = D_FF // FF_TILE
FF_GROUP = 4
FF_SLOTS = 4
FF_AHEAD = 2
EPS = 1e-6
NEG_BIG = -1e30
SQRT_FLOOR = 1e-12
CD_COLS = 2816
SMALL_KR = 0
SMALL_GB = 64
SMALL_GA = 68

VMEM_LIMIT_BYTES = 56 * 1024 * 1024
ATTN_KV_BLOCK_BYTES = 8 * 1024 * 1024


def _cparams(*sem):
    return pltpu.CompilerParams(dimension_semantics=sem, vmem_limit_bytes=VMEM_LIMIT_BYTES)


def _dot(a, b):
    return jnp.dot(a, b, preferred_element_type=F32)


def _dot_nt(a, b):
    return lax.dot_general(a, b, (((1,), (1,)), ((), ())), preferred_element_type=F32)


def _dot_tn(a, b):
    return lax.dot_general(a, b, (((0,), (0,)), ((), ())), preferred_element_type=F32)


def _rms(x, g):
    return x * lax.rsqrt(jnp.mean(x * x, axis=-1, keepdims=True) + EPS) * g


def _silu(x):
    return x * jax.nn.sigmoid(x)


def _softplus(x):
    return jnp.maximum(x, 0.0) + jnp.log1p(jnp.exp(-jnp.abs(x)))


def _gelu_tanh(x):
    return 0.5 * x * (1.0 + jnp.tanh(math.sqrt(2.0 / math.pi) * (x + 0.044715 * (x * x * x))))


def _split3(x):
    x1 = x.astype(BF16)
    r1 = x - x1.astype(F32)
    x2 = r1.astype(BF16)
    x3 = (r1 - x2.astype(F32)).astype(BF16)
    return x1, x2, x3


def _log2(n):
    assert n & (n - 1) == 0
    return n.bit_length() - 1


def _cumsum_rows(x):
    n, c = x.shape
    group = min(n, 8)
    x3 = x.reshape(n // group, group, c)
    pos = lax.broadcasted_iota(jnp.int32, x3.shape, 1)
    s = 1
    while s < group:
        x3 = x3 + jnp.where(pos >= s, pltpu.roll(x3, s, 1), 0.0)
        s *= 2
    if n == group:
        return x3.reshape(n, c)
    groups = [x3[0]]
    for gi in range(1, n // group):
        groups.append(x3[gi] + groups[-1][group - 1:group, :])
    return jnp.concatenate(groups, axis=0)


def _block_row_bcast(b, row, h):
    L, n = b.shape
    blk = 2 * h
    if blk >= 8:
        b3 = b.reshape(L // blk, blk, n)
        return jnp.broadcast_to(b3[:, h - 1:h, :], (L // blk, blk, n)).reshape(L, n)
    group = min(L, 8)
    x0 = jnp.where((row & (blk - 1)) == h - 1, b, 0.0).reshape(L // group, group, n)
    out = x0
    for j in range(1, h + 1):
        out = out + pltpu.roll(x0, j, 1)
    for j in range(1, h):
        out = out + pltpu.roll(x0, group - j, 1)
    return out.reshape(L, n)


def _ada_kernel(c_ref, w_ref, b_ref, o_ref):
    c = _silu(c_ref[...]).astype(BF16)
    o_ref[0] = _dot(c, w_ref[0].astype(BF16)) + b_ref[0]


def _ada_call(c_all, ada_w, ada_b):
    rows = c_all.shape[0]
    tn = 2048
    return pl.pallas_call(
        _ada_kernel,
        out_shape=jax.ShapeDtypeStruct((DEPTH, rows, 6 * D_MODEL), F32),
        grid=(DEPTH, 6 * D_MODEL // tn),
        in_specs=[
            pl.BlockSpec((rows, D_MODEL), lambda l, j: (0, 0)),
            pl.BlockSpec((1, D_MODEL, tn), lambda l, j: (l, 0, j)),
            pl.BlockSpec((1, 1, tn), lambda l, j: (l, 0, j)),
        ],
        out_specs=pl.BlockSpec((1, rows, tn), lambda l, j: (l, 0, j)),
        compiler_params=_cparams("arbitrary", "arbitrary"),
        name="ada_mod",
    )(c_all, ada_w, ada_b.reshape(DEPTH, 1, 6 * D_MODEL))


def _nmm_kernel(x_ref, g_ref, sc_ref, sh_ref, w_ref, o_ref, *maybe_t_ref):
    nb, tt, d = x_ref.shape
    h = _rms(x_ref[...], g_ref[...]) * (1.0 + sc_ref[...]) + sh_ref[...]
    y = _dot(h.reshape(nb * tt, d).astype(BF16), w_ref[...])
    o_ref[...] = y.reshape(nb, tt, y.shape[-1])
    if maybe_t_ref:
        (t_ref,) = maybe_t_ref
        yt = y[:, y.shape[-1] - 128:].T
        t_ref[0] = yt[SMALL_GB:SMALL_GB + 2 * HEADS, :]


def _tiles(B, T):
    if T >= 512:
        return 1, 512
    assert B * T <= 512
    return B, T


def _nmm_call(x, g, scale, shift, w, gates_t=False):
    B, T, D = x.shape
    N = w.shape[1]
    nb, tt = _tiles(B, T)
    out_shape = jax.ShapeDtypeStruct((B, T, N), F32)
    out_specs = pl.BlockSpec((nb, tt, N), lambda b, i: (b, i, 0))
    if gates_t:
        assert nb == 1 and tt % 128 == 0
        out_shape = (out_shape, jax.ShapeDtypeStruct((B, 2 * HEADS, T), F32))
        out_specs = (out_specs, pl.BlockSpec((1, 2 * HEADS, tt), lambda b, i: (b, 0, i)))
    return pl.pallas_call(
        _nmm_kernel,
        out_shape=out_shape,
        grid=(B // nb, T // tt),
        in_specs=[
            pl.BlockSpec((nb, tt, D), lambda b, i: (b, i, 0)),
            pl.BlockSpec((1, D), lambda b, i: (0, 0)),
            pl.BlockSpec((nb, 1, D), lambda b, i: (b, 0, 0)),
            pl.BlockSpec((nb, 1, D), lambda b, i: (b, 0, 0)),
            pl.BlockSpec((D, N), lambda b, i: (0, 0)),
        ],
        out_specs=out_specs,
        compiler_params=_cparams("arbitrary", "arbitrary"),
        name="norm_mod_proj",
    )(x, g.reshape(1, D), scale[:, None, :], shift[:, None, :], w)


def _ffn_kernel(x_ref, oa_ref, ob_ref, wo_ref, go_ref, gateo_ref, g1_ref, sc_ref, sh_ref, gate_ref, g2_ref,
                buf0_ref, wu_ref, wd_ref, cw_ref, o_ref, st_ref, carry_ref, ubuf_ref):
    nb, tt, d = x_ref.shape
    i = pl.program_id(1)

    @pl.when(i == 0)
    def _():
        for c in range(2 * N_FF_TILES):
            carry_ref[c] = buf0_ref[:, :, c * FF_TILE:(c + 1) * FF_TILE]

    oa = oa_ref[...].reshape(nb * tt, HALF)
    ob = ob_ref[...].reshape(nb * tt, HALF)
    mix = _dot(oa, wo_ref[0:HALF, :]) + _dot(ob, wo_ref[HALF:2 * HALF, :])
    x = x_ref[...] + gateo_ref[...] * _rms(mix, go_ref[...]).reshape(nb, tt, d)
    h = (_rms(x, g1_ref[...]) * (1.0 + sc_ref[...]) + sh_ref[...]).reshape(nb * tt, d).astype(BF16)

    def conv(u, slot, c):
        ubuf_ref[slot, :, 8:8 + tt, :] = u
        ubuf_ref[slot, :, 6:8, :] = carry_ref[c]
        cw = cw_ref[:, c * FF_TILE:(c + 1) * FF_TILE]
        y = (cw[0:1, :] * ubuf_ref[slot, :, 6:6 + tt, :] + cw[1:2, :] * ubuf_ref[slot, :, 7:7 + tt, :]
             + cw[2:3, :] * u)
        tail = ubuf_ref[slot, :, 6 + tt:8 + tt, :]
        carry_ref[c] = tail
        st_ref[:, :, c * FF_TILE:(c + 1) * FF_TILE] = tail
        return y

    def up_proj(c):
        wg = wu_ref[:, c * FF_TILE:(c + 1) * FF_TILE]
        wv = wu_ref[:, D_FF + c * FF_TILE:D_FF + (c + 1) * FF_TILE]
        return _dot(h, wg).reshape(nb, tt, FF_TILE), _dot(h, wv).reshape(nb, tt, FF_TILE)

    acc = None
    ahead = [up_proj(c) for c in range(min(FF_AHEAD, N_FF_TILES))]
    for g0 in range(0, N_FF_TILES, FF_GROUP):
        acts = []
        for c in range(g0, min(g0 + FF_GROUP, N_FF_TILES)):
            ug, uv = ahead.pop(0)
            if c + FF_AHEAD < N_FF_TILES:
                ahead.append(up_proj(c + FF_AHEAD))
            slot = 2 * (c % FF_SLOTS)
            yg = conv(ug, slot, c)
            yv = conv(uv, slot + 1, N_FF_TILES + c)
            acts.append((_silu(yg) * yv).reshape(nb * tt, FF_TILE).astype(BF16))
        a = jnp.concatenate(acts, axis=-1) if len(acts) > 1 else acts[0]
        part = _dot(a, wd_ref[g0 * FF_TILE:g0 * FF_TILE + a.shape[-1], :])
        acc = part if acc is None else acc + part
    y = _rms(acc, g2_ref[...]).reshape(nb, tt, d)
    o_ref[...] = x + gate_ref[...] * y


def _ffn_call(x, oa, ob, wo, go, gateo, g1, scale, shift, gate, g2, buf0, wu, wd, cw):
    B, T, D = x.shape
    nb, tt = _tiles(B, T)
    assert T >= FFN_CONV - 1
    return pl.pallas_call(
        _ffn_kernel,
        out_shape=(jax.ShapeDtypeStruct((B, T, D), F32),
                   jax.ShapeDtypeStruct((B, FFN_CONV - 1, 2 * D_FF), F32)),
        grid=(B // nb, T // tt),
        in_specs=[
            pl.BlockSpec((nb, tt, D), lambda b, i: (b, i, 0)),
            pl.BlockSpec((nb, tt, HALF), lambda b, i: (b, i, 0)),
            pl.BlockSpec((nb, tt, HALF), lambda b, i: (b, i, 0)),
            pl.BlockSpec((D, D), lambda b, i: (0, 0), pipeline_mode=pl.Buffered(1)),
            pl.BlockSpec((1, D), lambda b, i: (0, 0)),
            pl.BlockSpec((nb, 1, D), lambda b, i: (b, 0, 0)),
            pl.BlockSpec((1, D), lambda b, i: (0, 0)),
            pl.BlockSpec((nb, 1, D), lambda b, i: (b, 0, 0)),
            pl.BlockSpec((nb, 1, D), lambda b, i: (b, 0, 0)),
            pl.BlockSpec((nb, 1, D), lambda b, i: (b, 0, 0)),
            pl.BlockSpec((1, D), lambda b, i: (0, 0)),
            pl.BlockSpec((nb, FFN_CONV - 1, 2 * D_FF), lambda b, i: (b, 0, 0)),
            pl.BlockSpec((D, 2 * D_FF), lambda b, i: (0, 0), pipeline_mode=pl.Buffered(1)),
            pl.BlockSpec((D_FF, D), lambda b, i: (0, 0), pipeline_mode=pl.Buffered(1)),
            pl.BlockSpec((FFN_CONV, 2 * D_FF), lambda b, i: (0, 0)),
        ],
        out_specs=(pl.BlockSpec((nb, tt, D), lambda b, i: (b, i, 0)),
                   pl.BlockSpec((nb, FFN_CONV - 1, 2 * D_FF), lambda b, i: (b, 0, 0))),
        scratch_shapes=[
            pltpu.VMEM((2 * N_FF_TILES, nb, FFN_CONV - 1, FF_TILE), F32),
            pltpu.VMEM((2 * FF_SLOTS, nb, 8 + tt, FF_TILE), F32),
        ],
        compiler_params=_cparams("arbitrary", "arbitrary"),
        name="conv_ffn",
    )(x, oa, ob, wo, go.reshape(1, D), gateo[:, None, :], g1.reshape(1, D), scale[:, None, :], shift[:, None, :],
      gate[:, None, :], g2.reshape(1, D), buf0, wu, wd, cw)


def _hgrn_kernel(hq_ref, hf_ref, hi_ref, hz_ref, lb_ref, ng_ref, s0_ref, o_ref, st_ref):
    L = hq_ref.shape[1]
    i = pl.program_id(1)

    @pl.when(i == 0)
    def _():
        st_ref[...] = s0_ref[...]

    row = lax.broadcasted_iota(jnp.int32, (L, HEAD_DIM), 0)
    r2 = lax.broadcasted_iota(jnp.int32, (L, L), 0)
    c2 = lax.broadcasted_iota(jnp.int32, (L, L), 1)
    heads = range(HEADS)
    sls = [slice(hd * HEAD_DIM, (hd + 1) * HEAD_DIM) for hd in heads]
    q, k, v, b = ([None] * HEADS for _ in range(4))
    for hd in heads:
        z = hf_ref[0, :, sls[hd]]
        lb = lb_ref[:, sls[hd]]
        en = jnp.exp(-jnp.abs(z))
        inv = 1.0 / (1.0 + en)
        sig_pos = jnp.where(z >= 0.0, inv, en * inv)
        sig_neg = jnp.where(z >= 0.0, en * inv, inv)
        g = jnp.log(lb + (1.0 - lb) * sig_pos)
        k[hd] = (1.0 - lb) * sig_neg
        q[hd] = _silu(hq_ref[0, :, sls[hd]])
        v[hd] = hi_ref[0, :, sls[hd]].astype(BF16)
        b[hd] = _cumsum_rows(g)
    att = [jnp.where(r2 == c2, _dot_nt(q[hd].astype(BF16), k[hd].astype(BF16)), 0.0) for hd in heads]
    h = L // 2
    while h >= 1:
        sh = _log2(2 * h)
        keep = jnp.logical_and((r2 >> sh) == (c2 >> sh),
                               jnp.logical_and((r2 & (2 * h - 1)) >= h, (c2 & (2 * h - 1)) < h))
        for hd in heads:
            r = _block_row_bcast(b[hd], row, h)
            e = jnp.exp2(jnp.abs(b[hd] - r) * (-math.log2(math.e)))
            qt = (q[hd] * e).astype(BF16)
            kt = (k[hd] * e).astype(BF16)
            att[hd] = jnp.where(keep, _dot_nt(qt, kt), att[hd])
        h //= 2
    for hd in heads:
        st = st_ref[0, hd]
        o = _dot(att[hd].astype(BF16), v[hd]) + _dot_nt((q[hd] * jnp.exp(b[hd])).astype(BF16), st.astype(BF16))
        b_last = b[hd][L - 1:L, :]
        kd = (k[hd] * jnp.exp(b_last - b[hd])).astype(BF16)
        st_ref[0, hd] = jnp.exp(b_last) * st + _dot_tn(v[hd], kd)
        o = _rms(o, ng_ref[...]) * _silu(hz_ref[0, :, sls[hd]])
        o_ref[0, :, sls[hd]] = o.astype(BF16)


def _hgrn_call(proj, lb, ng, s0t):
    B, T, _ = proj.shape
    L = 128 if T % 128 == 0 else T
    assert T % L == 0 and L & (L - 1) == 0 and L >= 8
    col = lambda c: pl.BlockSpec((1, L, HALF), lambda b, i, c=c: (b, i, c))
    return pl.pallas_call(
        _hgrn_kernel,
        out_shape=(jax.ShapeDtypeStruct((B, T, HALF), BF16),
                   jax.ShapeDtypeStruct((B, HEADS, HEAD_DIM, HEAD_DIM), F32)),
        grid=(B, T // L),
        in_specs=[col(0), col(1), col(2), col(3),
                  pl.BlockSpec((1, HALF), lambda b, i: (0, 0)),
                  pl.BlockSpec((1, HEAD_DIM), lambda b, i: (0, 0)),
                  pl.BlockSpec((1, HEADS, HEAD_DIM, HEAD_DIM), lambda b, i: (b, 0, 0, 0))],
        out_specs=(pl.BlockSpec((1, L, HALF), lambda b, i: (b, i, 0)),
                   pl.BlockSpec((1, HEADS, HEAD_DIM, HEAD_DIM), lambda b, i: (b, 0, 0, 0))),
        compiler_params=_cparams("arbitrary", "arbitrary"),
        name="hgrn2",
    )(proj, proj, proj, proj, lb.reshape(1, HALF), ng.reshape(1, HEAD_DIM), s0t)


def _lru_kernel(lx_ref, ly_ref, buf0_ref, h0_ref, cw_ref, cb_ref, wa_ref, wx_ref, ba_ref, bx_ref,
                lam_ref, o_ref, hl_ref, bufo_ref, xp_ref):
    tt = lx_ref.shape[1]
    i = pl.program_id(1)
    npad = LRU_CONV - 1

    @pl.when(i == 0)
    def _():
        xp_ref[8 - npad:8, :] = buf0_ref[0]
        hl_ref[0] = h0_ref[0]

    x = lx_ref[0]
    xp_ref[8:8 + tt, :] = x
    xc = cb_ref[...] + cw_ref[npad:npad + 1, :] * x
    for tap in range(npad):
        xc = xc + cw_ref[tap:tap + 1, :] * xp_ref[8 - npad + tap:8 - npad + tap + tt, :]
    tail = xp_ref[8 + tt - npad:8 + tt, :]
    xp_ref[8 - npad:8, :] = tail
    bufo_ref[0] = tail

    xb = xc.astype(BF16)
    half = HALF // 2
    rpre = jnp.concatenate([_dot(xb[:, 0:half], wa_ref[0]), _dot(xb[:, half:HALF], wa_ref[1])], axis=-1)
    ipre = jnp.concatenate([_dot(xb[:, 0:half], wx_ref[0]), _dot(xb[:, half:HALF], wx_ref[1])], axis=-1)
    r = jax.nn.sigmoid(rpre + ba_ref[...])
    ig = jax.nn.sigmoid(ipre + bx_ref[...])
    log_a = -LRU_C * r * _softplus(-lam_ref[...])
    a = jnp.exp(log_a)
    u = jnp.sqrt(jnp.maximum(-jnp.tanh(log_a) * (1.0 + a * a), SQRT_FLOOR)) * ig * xc

    a = a.reshape(tt // LRU_GROUP, LRU_GROUP, HALF)
    u = u.reshape(tt // LRU_GROUP, LRU_GROUP, HALF)
    pos = lax.broadcasted_iota(jnp.int32, a.shape, 1)
    s = 1
    while s < LRU_GROUP:
        keep = pos >= s
        a_sh = jnp.where(keep, pltpu.roll(a, s, 1), 1.0)
        u_sh = jnp.where(keep, pltpu.roll(u, s, 1), 0.0)
        u = a * u_sh + u
        a = a * a_sh
        s *= 2
    carry = hl_ref[0]
    groups = []
    for gi in range(tt // LRU_GROUP):
        hg = u[gi] + a[gi] * carry
        carry = hg[LRU_GROUP - 1:LRU_GROUP, :]
        groups.append(hg)
    hseq = jnp.concatenate(groups, axis=0) if len(groups) > 1 else groups[0]
    hl_ref[0] = carry
    o_ref[0] = (hseq * _gelu_tanh(ly_ref[0])).astype(BF16)


def _lru_call(proj, buf0, h0, cw, cb, wa_bd, wx_bd, ba, bx, lam):
    B, T, _ = proj.shape
    tt = 256 if T % 256 == 0 else T
    assert T % tt == 0 and T >= LRU_CONV - 1 and tt % 8 == 0
    vec = pl.BlockSpec((1, HALF), lambda b, i: (0, 0))
    wspec = pl.BlockSpec((2, HALF // 2, HALF // 2), lambda b, i: (0, 0, 0))
    return pl.pallas_call(
        _lru_kernel,
        out_shape=(jax.ShapeDtypeStruct((B, T, HALF), BF16),
                   jax.ShapeDtypeStruct((B, 1, HALF), F32),
                   jax.ShapeDtypeStruct((B, LRU_CONV - 1, HALF), F32)),
        grid=(B, T // tt),
        in_specs=[pl.BlockSpec((1, tt, HALF), lambda b, i: (b, i, 4)),
                  pl.BlockSpec((1, tt, HALF), lambda b, i: (b, i, 5)),
                  pl.BlockSpec((1, LRU_CONV - 1, HALF), lambda b, i: (b, 0, 0)),
                  pl.BlockSpec((1, 1, HALF), lambda b, i: (b, 0, 0)),
                  pl.BlockSpec((LRU_CONV, HALF), lambda b, i: (0, 0)),
                  vec, wspec, wspec, vec, vec, vec],
        out_specs=(pl.BlockSpec((1, tt, HALF), lambda b, i: (b, i, 0)),
                   pl.BlockSpec((1, 1, HALF), lambda b, i: (b, 0, 0)),
                   pl.BlockSpec((1, LRU_CONV - 1, HALF), lambda b, i: (b, 0, 0))),
        scratch_shapes=[pltpu.VMEM((8 + tt, HALF), F32)],
        compiler_params=_cparams("arbitrary", "arbitrary"),
        name="rglru",
    )(proj, proj, buf0, h0[:, None, :], cw, cb.reshape(1, HALF), wa_bd, wx_bd,
      ba.reshape(1, HALF), bx.reshape(1, HALF), lam.reshape(1, HALF))


def _gdn_kernel(qkv_ref, gz_ref, sm_ref, smt_ref, buf0_ref, cw_ref, pcol_ref, prow_ref, ng_ref, s0_ref,
                o_ref, st_ref, bufo_ref, xp_ref, *, L):
    tt = qkv_ref.shape[1]
    nc = tt // L
    sh = _log2(L)
    i = pl.program_id(1)
    npad = GD_CONV - 1

    @pl.when(i == 0)
    def _():
        xp_ref[8 - npad:8, :] = buf0_ref[0]
        st_ref[...] = s0_ref[...]

    x = qkv_ref[0]
    xp_ref[8:8 + tt, :] = x
    xc = cw_ref[npad:npad + 1, :] * x
    for tap in range(npad):
        xc = xc + cw_ref[tap:tap + 1, :] * xp_ref[8 - npad + tap:8 - npad + tap + tt, :]
    tail = xp_ref[8 + tt - npad:8 + tt, :]
    xp_ref[8 - npad:8, :] = tail
    bufo_ref[0] = tail
    xc = _silu(xc)

    r2 = lax.broadcasted_iota(jnp.int32, (tt, tt), 0)
    c2 = lax.broadcasted_iota(jnp.int32, (tt, tt), 1)
    same = (r2 >> sh) == (c2 >> sh)
    incl = jnp.logical_and(same, c2 <= r2)
    strict = jnp.logical_and(same, c2 < r2)
    tri_lo = jnp.where(incl, 1.0, 0.0).astype(BF16)
    tri_up = jnp.where(jnp.logical_and(same, r2 <= c2), 1.0, 0.0).astype(BF16)
    eye = jnp.where(r2 == c2, 1.0, 0.0)

    sm = sm_ref[0]
    beta_cols = jax.nn.sigmoid(sm)
    la_cols = -jnp.exp(pcol_ref[0:1, :]) * _softplus(sm + pcol_ref[1:2, :])
    c1, c2_, c3 = _split3(la_cols)
    g_cols = _dot(tri_lo, c1) + _dot(tri_lo, c2_) + _dot(tri_lo, c3)
    la_rows = -jnp.exp(prow_ref[:, 0:1]) * _softplus(smt_ref[0] + prow_ref[:, 1:2])
    w1, w2, w3 = _split3(la_rows)
    g_rows = _dot(w1, tri_up) + _dot(w2, tri_up) + _dot(w3, tri_up)

    heads = range(HEADS)
    q, k, v, beta, gcol, kb, dec, m = ([None] * HEADS for _ in range(8))
    for hd in heads:
        qh = xc[:, hd * HEAD_DIM:(hd + 1) * HEAD_DIM]
        kh = xc[:, HALF + hd * HEAD_DIM:HALF + (hd + 1) * HEAD_DIM]
        v[hd] = xc[:, 2 * HALF + hd * HEAD_DIM:2 * HALF + (hd + 1) * HEAD_DIM]
        q[hd] = qh * lax.rsqrt(jnp.sum(qh * qh, axis=-1, keepdims=True) + EPS) * (HEAD_DIM ** -0.5)
        k[hd] = kh * lax.rsqrt(jnp.sum(kh * kh, axis=-1, keepdims=True) + EPS)
        beta[hd] = beta_cols[:, SMALL_GB + hd:SMALL_GB + hd + 1]
        gcol[hd] = g_cols[:, SMALL_GA + hd:SMALL_GA + hd + 1]
        grow = g_rows[HEADS + hd:HEADS + hd + 1, :]
        kb[hd] = k[hd].astype(BF16)
        dec[hd] = jnp.exp(jnp.where(incl, gcol[hd] - grow, NEG_BIG))
        m[hd] = beta[hd] * _dot_nt(kb[hd], kb[hd]) * jnp.where(strict, dec[hd], 0.0)

    pair = (r2 >> 1) == (c2 >> 1)
    tinv = [eye - jnp.where(pair, m[hd], 0.0) for hd in heads]
    s = 2
    while s < L:
        ssh = _log2(s)
        lower_left = jnp.logical_and((r2 >> (ssh + 1)) == (c2 >> (ssh + 1)), (r2 >> ssh) != (c2 >> ssh))
        tb = [tinv[hd].astype(BF16) for hd in heads]
        tc = [_dot(tb[hd], jnp.where(lower_left, m[hd], 0.0).astype(BF16)).astype(BF16) for hd in heads]
        tinv = [tinv[hd] - _dot(tc[hd], tb[hd]) for hd in heads]
        s *= 2

    eg = [jnp.exp(gcol[hd]) for hd in heads]
    sol = [_dot(tinv[hd].astype(BF16),
                jnp.concatenate([beta[hd] * v[hd], (beta[hd] * eg[hd]) * k[hd]], axis=-1).astype(BF16))
           for hd in heads]
    u_v = [sol[hd][:, 0:HEAD_DIM] for hd in heads]
    w_k = [sol[hd][:, HEAD_DIM:2 * HEAD_DIM].astype(BF16) for hd in heads]
    qb = [q[hd].astype(BF16) for hd in heads]
    qk = [(_dot_nt(qb[hd], kb[hd]) * dec[hd]).astype(BF16) for hd in heads]

    a_c, p_c, n_c = ([[None] * nc for _ in heads] for _ in range(3))
    for c in range(nc):
        rs = slice(c * L, (c + 1) * L)
        for hd in heads:
            g_last = gcol[hd][(c + 1) * L - 1:(c + 1) * L, :]
            kd = (k[hd][rs] * jnp.exp(g_last - gcol[hd][rs])).astype(BF16)
            a_c[hd][c] = jnp.exp(g_last)
            p_c[hd][c] = (-_dot_tn(kd, w_k[hd][rs])).astype(BF16)
            n_c[hd][c] = _dot_tn(kd, u_v[hd][rs].astype(BF16))
    S = [[st_ref[0, hd]] for hd in heads]
    for c in range(nc):
        for hd in heads:
            s_cur = S[hd][c]
            S[hd].append(a_c[hd][c] * s_cur + _dot(p_c[hd][c], s_cur.astype(BF16)) + n_c[hd][c])
    us = [[] for _ in heads]
    inters = [[] for _ in heads]
    for c in range(nc):
        rs = slice(c * L, (c + 1) * L)
        for hd in heads:
            Sb = S[hd][c].astype(BF16)
            us[hd].append(u_v[hd][rs] - _dot(w_k[hd][rs], Sb))
            inters[hd].append(eg[hd][rs] * _dot(qb[hd][rs], Sb))
    for hd in heads:
        sl = slice(hd * HEAD_DIM, (hd + 1) * HEAD_DIM)
        st_ref[0, hd] = S[hd][nc]
        u_all = jnp.concatenate(us[hd], axis=0) if nc > 1 else us[hd][0]
        inter = jnp.concatenate(inters[hd], axis=0) if nc > 1 else inters[hd][0]
        o = _dot(qk[hd], u_all.astype(BF16)) + inter
        o = _rms(o, ng_ref[...]) * _silu(gz_ref[0, :, sl])
        o_ref[0, :, sl] = o.astype(BF16)


def _gdn_call(proj, small_t, buf0, cw, pcol, prow, ng, s0):
    B, T, _ = proj.shape
    L = CHUNK if T % CHUNK == 0 else T
    tt = 256 if T % 256 == 0 else T
    assert T % tt == 0 and tt % L == 0 and T >= GD_CONV - 1 and L >= 2
    W = 3 * HALF
    return pl.pallas_call(
        functools.partial(_gdn_kernel, L=L),
        out_shape=(jax.ShapeDtypeStruct((B, T, HALF), BF16),
                   jax.ShapeDtypeStruct((B, HEADS, HEAD_DIM, HEAD_DIM), F32),
                   jax.ShapeDtypeStruct((B, GD_CONV - 1, W), F32)),
        grid=(B, T // tt),
        in_specs=[pl.BlockSpec((1, tt, W), lambda b, i: (b, i, 0)),
                  pl.BlockSpec((1, tt, HALF), lambda b, i: (b, i, 3)),
                  pl.BlockSpec((1, tt, 128), lambda b, i: (b, i, 21)),
                  pl.BlockSpec((1, 2 * HEADS, tt), lambda b, i: (b, 0, i)),
                  pl.BlockSpec((1, GD_CONV - 1, W), lambda b, i: (b, 0, 0)),
                  pl.BlockSpec((GD_CONV, W), lambda b, i: (0, 0)),
                  pl.BlockSpec((2, 128), lambda b, i: (0, 0)),
                  pl.BlockSpec((2 * HEADS, 2), lambda b, i: (0, 0)),
                  pl.BlockSpec((1, HEAD_DIM), lambda b, i: (0, 0)),
                  pl.BlockSpec((1, HEADS, HEAD_DIM, HEAD_DIM), lambda b, i: (b, 0, 0, 0))],
        out_specs=(pl.BlockSpec((1, tt, HALF), lambda b, i: (b, i, 0)),
                   pl.BlockSpec((1, HEADS, HEAD_DIM, HEAD_DIM), lambda b, i: (b, 0, 0, 0)),
                   pl.BlockSpec((1, GD_CONV - 1, W), lambda b, i: (b, 0, 0))),
        scratch_shapes=[pltpu.VMEM((8 + tt, W), F32)],
        compiler_params=_cparams("arbitrary", "arbitrary"),
        name="gated_deltanet",
    )(proj, proj, proj, small_t, buf0, cw, pcol, prow, ng.reshape(1, HEAD_DIM), s0)


def _rope64(x, cs, sn):
    half = MLA_ROPE // 2
    swapped = jnp.concatenate([x[:, half:], x[:, :half]], axis=-1)
    return x * cs + swapped * sn


def _expand_kv(c_kv, k_r, wkvb_ref, kf_ref, v_ref):
    kv = _dot(c_kv.astype(BF16), wkvb_ref[...])
    for hd in range(HEADS):
        base = hd * 2 * HEAD_DIM
        kf_ref[0, hd] = jnp.concatenate([kv[:, base:base + MLA_NOPE], k_r], axis=-1).astype(BF16)
        v = kv[:, base + MLA_NOPE:base + 2 * HEAD_DIM]
        v_ref[0, hd] = jnp.concatenate([v, jnp.ones_like(v)], axis=-1).astype(BF16)


def _mla_prep_kernel(qa_ref, kva_ref, sm_ref, cs_ref, sn_ref, qng_ref, wqb_ref, kvng_ref, wkvb_ref, *rest):
    q_ref, ckv_ref, kr_ref, kf_ref, v_ref = rest[-5:]
    ckv_ref = ckv_ref.at[0]
    kr_ref = kr_ref.at[0]
    cs = cs_ref[...]
    sn = sn_ref[...]
    qn = _rms(qa_ref[0], qng_ref[...]).astype(BF16)
    qh = _dot(qn, wqb_ref[...]) * (MLA_SCALE * math.log2(math.e))
    for hd in range(HEADS):
        nope = qh[:, hd * MLA_NOPE:(hd + 1) * MLA_NOPE]
        off = HEADS * MLA_NOPE + hd * MLA_ROPE
        rot = _rope64(qh[:, off:off + MLA_ROPE], cs, sn)
        q_ref[0, hd] = jnp.concatenate([nope, rot], axis=-1).astype(BF16)
    c_kv = _rms(kva_ref[0], kvng_ref[...])
    ckv_ref[0] = c_kv
    k_r = _rope64(sm_ref[0, :, SMALL_KR:SMALL_KR + MLA_ROPE], cs, sn)
    kr_ref[0] = k_r
    _expand_kv(c_kv, k_r, wkvb_ref, kf_ref, v_ref)


def _mla_prep_call(proj, cs, sn, qng, wqb, kvng, wkvb, j, state_bufs):
    B, T, _ = proj.shape
    tt = 512 if T % 512 == 0 else T
    c2 = lambda b, i: (0, 0)
    any_spec = pl.BlockSpec(memory_space=pl.ANY)
    extra, aliases = [], {}
    n_fixed = 9
    if state_bufs is not None:
        aliases[n_fixed] = 1
        aliases[n_fixed + 1] = 2
        extra += list(state_bufs)
    return pl.pallas_call(
        _mla_prep_kernel,
        out_shape=(jax.ShapeDtypeStruct((B, HEADS, T, MLA_QK), BF16),
                   jax.ShapeDtypeStruct((N_CD, B, T, MLA_KV_RANK), F32),
                   jax.ShapeDtypeStruct((N_CD, B, T, MLA_ROPE), F32),
                   jax.ShapeDtypeStruct((B, HEADS, T, MLA_QK), BF16),
                   jax.ShapeDtypeStruct((B, HEADS, T, 2 * HEAD_DIM), BF16)),
        grid=(B, T // tt),
        in_specs=[pl.BlockSpec((1, tt, MLA_Q_RANK), lambda b, i: (b, i, 6)),
                  pl.BlockSpec((1, tt, MLA_KV_RANK), lambda b, i: (b, i, 8)),
                  pl.BlockSpec((1, tt, 128), lambda b, i: (b, i, 21)),
                  pl.BlockSpec((tt, MLA_ROPE), lambda b, i: (i, 0)),
                  pl.BlockSpec((tt, MLA_ROPE), lambda b, i: (i, 0)),
                  pl.BlockSpec((1, MLA_Q_RANK), c2),
                  pl.BlockSpec((MLA_Q_RANK, HEADS * MLA_QK), c2),
                  pl.BlockSpec((1, MLA_KV_RANK), c2),
                  pl.BlockSpec((MLA_KV_RANK, HEADS * 2 * HEAD_DIM), c2)] + [any_spec] * len(extra),
        out_specs=(pl.BlockSpec((1, HEADS, tt, MLA_QK), lambda b, i: (b, 0, i, 0)),
                   pl.BlockSpec((1, 1, tt, MLA_KV_RANK), lambda b, i: (j, b, i, 0)),
                   pl.BlockSpec((1, 1, tt, MLA_ROPE), lambda b, i: (j, b, i, 0)),
                   pl.BlockSpec((1, HEADS, tt, MLA_QK), lambda b, i: (b, 0, i, 0)),
                   pl.BlockSpec((1, HEADS, tt, 2 * HEAD_DIM), lambda b, i: (b, 0, i, 0))),
        input_output_aliases=aliases,
        compiler_params=_cparams("arbitrary", "arbitrary"),
        name="mla_prep",
    )(proj, proj, proj, cs, sn, qng.reshape(1, MLA_Q_RANK), wqb, kvng.reshape(1, MLA_KV_RANK), wkvb, *extra)


def _attn_cached_kernel(q_ref, latp_ref, krp_ref, latn_ref, krn_ref, w_ref, o_ref):
    T = q_ref.shape[2]
    P = latp_ref.shape[1]
    sh = _log2(CHUNK)
    latp = latp_ref[0].astype(BF16)
    krp = krp_ref[0].astype(BF16)
    latn = latn_ref[0, 0].astype(BF16)
    krn = krn_ref[0, 0].astype(BF16)
    q_chunk = (P + lax.broadcasted_iota(jnp.int32, (T, 1), 0)) >> sh
    vis_p = (lax.broadcasted_iota(jnp.int32, (T, P), 1) >> sh) <= q_chunk
    vis_n = ((P + lax.broadcasted_iota(jnp.int32, (T, T), 1)) >> sh) <= q_chunk
    for hd in range(HEADS):
        base = hd * 2 * HEAD_DIM
        q = q_ref[0, hd]
        q_lat = _dot_nt(q[:, 0:MLA_NOPE], w_ref[:, base:base + MLA_NOPE]).astype(BF16)
        q_rope = q[:, MLA_NOPE:MLA_QK]
        s_p = jnp.where(vis_p, _dot_nt(q_lat, latp) + _dot_nt(q_rope, krp), NEG_BIG)
        s_n = jnp.where(vis_n, _dot_nt(q_lat, latn) + _dot_nt(q_rope, krn), NEG_BIG)
        m = jnp.maximum(jnp.max(s_p, axis=-1, keepdims=True), jnp.max(s_n, axis=-1, keepdims=True))
        p_p = jnp.exp2(s_p - m)
        p_n = jnp.exp2(s_n - m)
        denom = jnp.sum(p_p, axis=-1, keepdims=True) + jnp.sum(p_n, axis=-1, keepdims=True)
        o_lat = (_dot(p_p.astype(BF16), latp) + _dot(p_n.astype(BF16), latn)) / denom
        o = _dot(o_lat.astype(BF16), w_ref[:, base + MLA_NOPE:base + 2 * HEAD_DIM])
        o_ref[0, :, hd * HEAD_DIM:(hd + 1) * HEAD_DIM] = o.astype(BF16)


def _attn_cached_call(q, lat_past, kr_past, lat_all, kr_all, j, wkvb):
    B, H, T, _ = q.shape
    P = lat_past.shape[1]
    return pl.pallas_call(
        _attn_cached_kernel,
        out_shape=jax.ShapeDtypeStruct((B, T, H * HEAD_DIM), BF16),
        grid=(B,),
        in_specs=[pl.BlockSpec((1, H, T, MLA_QK), lambda b: (b, 0, 0, 0)),
                  pl.BlockSpec((1, P, MLA_KV_RANK), lambda b: (b, 0, 0)),
                  pl.BlockSpec((1, P, MLA_ROPE), lambda b: (b, 0, 0)),
                  pl.BlockSpec((1, 1, T, MLA_KV_RANK), lambda b: (j, b, 0, 0)),
                  pl.BlockSpec((1, 1, T, MLA_ROPE), lambda b: (j, b, 0, 0)),
                  pl.BlockSpec((MLA_KV_RANK, H * 2 * HEAD_DIM), lambda b: (0, 0))],
        out_specs=pl.BlockSpec((1, T, H * HEAD_DIM), lambda b: (b, 0, 0)),
        compiler_params=_cparams("arbitrary"),
        name="mla_attention_cached",
    )(q, lat_past, kr_past, lat_all, kr_all, wkvb)


def _attn_kernel(q_ref, k_ref, v_ref, o_ref, m_ref, acc_ref, sa_ref, sb_ref, **static):
    for hh in range(q_ref.shape[1]):
        _attn_head(q_ref.at[0, hh], k_ref.at[0, hh], v_ref.at[0, hh], o_ref.at[0, :, hh * HEAD_DIM:(hh + 1) * HEAD_DIM],
                   m_ref, acc_ref, sa_ref, sb_ref, **static)


def _attn_head(q_ref, k_ref, v_ref, o_ref, m_ref, acc_ref, sa_ref, sb_ref, *, past_len, tq, tk, nk):
    i = pl.program_id(2)
    sh = _log2(CHUNK)
    lanes = HEAD_DIM
    m_ref[...] = jnp.full(m_ref.shape, NEG_BIG, F32)
    acc_ref[...] = jnp.zeros(acc_ref.shape, F32)
    q = q_ref[...]
    q_lo = past_len + i * tq
    n_full = jnp.minimum(nk, (((q_lo >> sh) + 1) * CHUNK) // tk)
    n_need = jnp.minimum(nk, ((((q_lo + tq - 1) >> sh) + 1) * CHUNK + tk - 1) // tk)

    def scores(j):
        return _dot_nt(q, k_ref[pl.ds(pl.multiple_of(j * tk, tk), tk), :])

    def masked_scores(j):
        k_lo = j * tk
        qc = (q_lo + lax.broadcasted_iota(jnp.int32, (tq, tk), 0)) >> sh
        kc = (k_lo + lax.broadcasted_iota(jnp.int32, (tq, tk), 1)) >> sh
        return jnp.where(kc <= qc, scores(j), NEG_BIG)

    def update(j, s_ref):
        k_lo = pl.multiple_of(j * tk, tk)
        m_prev = m_ref[...]
        m_new = jnp.maximum(m_prev, jnp.max(s_ref[...], axis=-1, keepdims=True))
        alpha = jnp.exp2(m_prev - m_new)
        if tk % lanes == 0:
            p = jnp.exp2(s_ref[...] - jnp.tile(m_new, (1, tk // lanes)))
        else:
            p = jnp.exp2(s_ref[...] - m_new[:, 0:1])
        pv = _dot(p.astype(BF16), v_ref[pl.ds(k_lo, tk), :])
        acc_ref[...] = jnp.tile(alpha, (1, 2)) * acc_ref[...] + pv
        m_ref[...] = m_new

    def body_pair(g, carry):
        sb_ref[...] = scores(2 * g + 1)
        update(2 * g, sa_ref)
        sa_ref[...] = scores(2 * g + 2)
        update(2 * g + 1, sb_ref)
        return carry

    sa_ref[...] = scores(0)
    n_pairs = jnp.maximum(n_full - 1, 0) // 2
    lax.fori_loop(0, n_pairs, body_pair, 0)

    left = n_full - 2 * n_pairs
    has_masked = n_need > n_full
    j_masked = jnp.minimum(n_full, nk - 1)

    @pl.when(left == 1)
    def _():
        sb_ref[...] = masked_scores(j_masked)
        update(n_full - 1, sa_ref)

    @pl.when(jnp.logical_and(left == 1, has_masked))
    def _():
        update(n_full, sb_ref)

    @pl.when(left == 2)
    def _():
        sb_ref[...] = scores(n_full - 1)
        update(n_full - 2, sa_ref)
        sa_ref[...] = masked_scores(j_masked)
        update(n_full - 1, sb_ref)

    @pl.when(jnp.logical_and(left == 2, has_masked))
    def _():
        update(n_full, sa_ref)

    def body_masked(j, carry):
        sa_ref[...] = masked_scores(j)
        update(j, sa_ref)
        return carry

    lax.fori_loop(jnp.where(n_full > 0, n_full + 1, 0), n_need, body_masked, 0)
    o_ref[...] = (acc_ref[:, 0:HEAD_DIM] / acc_ref[:, HEAD_DIM:2 * HEAD_DIM]).astype(BF16)


def _attn_call(q, kf, v, past_len):
    B, H, Tq, _ = q.shape
    Tk = kf.shape[2]
    tq = 1024 if Tq % 1024 == 0 else Tq
    tk = 1024 if Tk % 1024 == 0 else Tk
    nk = Tk // tk
    kv_bytes = Tk * (MLA_QK + 2 * HEAD_DIM) * 2
    hb = H if H * kv_bytes <= ATTN_KV_BLOCK_BYTES else 1
    return pl.pallas_call(
        functools.partial(_attn_kernel, past_len=past_len, tq=tq, tk=tk, nk=nk),
        out_shape=jax.ShapeDtypeStruct((B, Tq, H * HEAD_DIM), BF16),
        grid=(B, H // hb, Tq // tq),
        in_specs=[pl.BlockSpec((1, hb, tq, MLA_QK), lambda b, h, i: (b, h, i, 0)),
                  pl.BlockSpec((1, hb, Tk, MLA_QK), lambda b, h, i: (b, h, 0, 0)),
                  pl.BlockSpec((1, hb, Tk, 2 * HEAD_DIM), lambda b, h, i: (b, h, 0, 0))],
        out_specs=pl.BlockSpec((1, tq, hb * HEAD_DIM), lambda b, h, i: (b, i, h)),
        scratch_shapes=[pltpu.VMEM((tq, HEAD_DIM), F32), pltpu.VMEM((tq, 2 * HEAD_DIM), F32),
                        pltpu.VMEM((tq, tk), F32), pltpu.VMEM((tq, tk), F32)],
        compiler_params=_cparams("arbitrary", "arbitrary", "arbitrary"),
        name="mla_attention",
    )(q, kf, v)


def _block_diag_pairs(w):
    per = (HALF // 2) // LRU_BLOCK
    w4 = w.reshape(2, per, LRU_BLOCK, LRU_BLOCK)
    eye = jnp.eye(per, dtype=w.dtype)
    out = w4[:, :, :, None, :] * eye[None, :, None, :, None]
    return out.reshape(2, HALF // 2, HALF // 2).astype(BF16)


def _prep_cd_w_in(w):
    o = 3 * HALF
    qkv, gz = w[:, :o], w[:, o:o + HALF]
    o += HALF
    gb, ga = w[:, o:o + HEADS], w[:, o + HEADS:o + 2 * HEADS]
    o += 2 * HEADS
    qa, kva = w[:, o:o + MLA_Q_RANK], w[:, o + MLA_Q_RANK:o + MLA_Q_RANK + MLA_KV_RANK]
    o += MLA_Q_RANK + MLA_KV_RANK
    kr = w[:, o:o + MLA_ROPE]
    assert (SMALL_KR, SMALL_GB, SMALL_GA) == (0, MLA_ROPE, MLA_ROPE + HEADS)
    fill = jnp.zeros((w.shape[0], 128 - MLA_ROPE - 2 * HEADS), w.dtype)
    out = jnp.concatenate([qkv, gz, kva, qa, kr, gb, ga, fill], axis=-1)
    assert out.shape[1] == CD_COLS
    return out.astype(BF16)


def _prep_wqb(w):
    w4 = w.reshape(MLA_Q_RANK, HEADS, MLA_QK)
    nope = w4[:, :, :MLA_NOPE].reshape(MLA_Q_RANK, HEADS * MLA_NOPE)
    rope = w4[:, :, MLA_NOPE:].reshape(MLA_Q_RANK, HEADS * MLA_ROPE)
    return jnp.concatenate([nope, rope], axis=-1).astype(BF16)


def _rope_tables(T, past_len):
    half = MLA_ROPE // 2
    freqs = jnp.exp(-math.log(ROPE_THETA) * jnp.arange(half, dtype=F32) / half)
    pos = past_len + jnp.arange(T, dtype=jnp.int32)
    ang = pos.astype(F32)[:, None] * freqs
    cos, sin = jnp.cos(ang), jnp.sin(ang)
    return jnp.concatenate([cos, cos], axis=-1), jnp.concatenate([-sin, sin], axis=-1)


def _run_group(x, mods, hg_s, lru_h, lru_buf, gd_s, gd_buf, lat_past, kr_past, ffn_buf, W):
    B, T, _ = x.shape
    n_hg, n_lru, n_lrub, n_gd, n_gdb, n_ffn = ([] for _ in range(6))
    mla_state = None
    for l in range(DEPTH):
        j = l // 2
        shift1, scale1, gate1, shift2, scale2, gate2 = jnp.split(mods[l], 6, axis=-1)
        g = W['norm_g'][l]
        if l % 2 == 0:
            proj = _nmm_call(x, g[0], scale1, shift1, W['ab_w_in'][j])
            o_a, s_hg_t = _hgrn_call(proj, W['lower_bounds'][j], W['hgrn_norm_g'][j],
                                     jnp.swapaxes(hg_s[j], -1, -2))
            o_b, s_lru, s_lrub = _lru_call(proj, lru_buf[j], lru_h[j], W['lru_conv_w'][j], W['lru_conv_b'][j],
                                           W['lru_wa_bd'][j], W['lru_wx_bd'][j], W['lru_b_a'][j],
                                           W['lru_b_x'][j], W['lru_lambda'][j])
            n_hg.append(jnp.swapaxes(s_hg_t, -1, -2))
            n_lru.append(s_lru[:, 0, :])
            n_lrub.append(s_lrub)
            mix_a, mix_b, w_out = o_a, o_b, W['ab_w_out'][j]
        else:
            past_len = lat_past.shape[2]
            if _tiles(B, T)[0] == 1:
                proj, small_t = _nmm_call(x, g[0], scale1, shift1, W['cd_w_in'][j], gates_t=True)
            else:
                proj = _nmm_call(x, g[0], scale1, shift1, W['cd_w_in'][j])
                small_t = jnp.swapaxes(proj[:, :, CD_COLS - 128 + SMALL_GB:CD_COLS - 128 + SMALL_GB + 2 * HEADS], 1, 2)
            o_c, s_gd, s_gdb = _gdn_call(proj, small_t, gd_buf[j], W['gdn_conv_w'][j], W['gdn_pcol'][j],
                                         W['gdn_prow'][j], W['gdn_norm_g'][j], gd_s[j])
            cs, sn = _rope_tables(T, past_len)
            q, lat_all, kr_all, kf, v = _mla_prep_call(proj, cs, sn, W['mla_q_norm_g'][j], W['mla_w_qb'][j],
                                                       W['mla_kv_norm_g'][j], W['mla_w_kvb'][j], j, mla_state)
            mla_state = (lat_all, kr_all)
            if past_len > 0:
                o_d = _attn_cached_call(q, lat_past[j], kr_past[j], lat_all, kr_all, j, W['mla_w_kvb'][j])
            else:
                o_d = _attn_call(q, kf, v, 0)
            n_gd.append(s_gd)
            n_gdb.append(s_gdb)
            mix_a, mix_b, w_out = o_c, o_d, W['cd_w_out'][j]
        x, s_ffn = _ffn_call(x, mix_a, mix_b, w_out, g[1], gate1, g[2], scale2, shift2, gate2, g[3], ffn_buf[l],
                             W['ffn_wu'][l], W['ffn_wd'][l], W['ffn_cw'][l])
        n_ffn.append(s_ffn)
    return x, (jnp.stack(n_hg), jnp.stack(n_lru), jnp.stack(n_lrub), jnp.stack(n_gd), jnp.stack(n_gdb),
               mla_state[0], mla_state[1], jnp.stack(n_ffn))


def _prep_weights(norm_g, ab_w_in, ab_w_out, hgrn_lb_logits, hgrn_norm_g, lru_conv_w, lru_conv_b, lru_w_a, lru_b_a,
                  lru_w_x, lru_b_x, lru_lambda, cd_w_in, cd_w_out, gdn_conv_w, gdn_a_log, gdn_dt_bias, gdn_norm_g,
                  mla_q_norm_g, mla_w_qb, mla_kv_norm_g, mla_w_kvb, ffn_w_up, ffn_conv_w, ffn_w_down):
    lb_p = jax.nn.softmax(hgrn_lb_logits.astype(F32), axis=0)
    pcol = jnp.pad(jnp.stack([gdn_a_log, gdn_dt_bias], axis=1),
                   ((0, 0), (0, 0), (SMALL_GA, 128 - SMALL_GA - HEADS)))
    prow = jnp.pad(jnp.stack([gdn_a_log, gdn_dt_bias], axis=-1), ((0, 0), (HEADS, 0), (0, 0)))
    W = dict(
        norm_g=norm_g,
        ab_w_in=ab_w_in.astype(BF16), ab_w_out=ab_w_out.astype(BF16),
        lower_bounds=jnp.cumsum(lb_p, axis=0) - lb_p[0:1],
        hgrn_norm_g=hgrn_norm_g, lru_conv_w=lru_conv_w, lru_conv_b=lru_conv_b,
        lru_wa_bd=jax.vmap(_block_diag_pairs)(lru_w_a), lru_wx_bd=jax.vmap(_block_diag_pairs)(lru_w_x),
        lru_b_a=lru_b_a, lru_b_x=lru_b_x, lru_lambda=lru_lambda,
        cd_w_in=jax.vmap(_prep_cd_w_in)(cd_w_in), cd_w_out=cd_w_out.astype(BF16),
        gdn_conv_w=gdn_conv_w, gdn_pcol=pcol, gdn_prow=prow, gdn_norm_g=gdn_norm_g,
        mla_q_norm_g=mla_q_norm_g, mla_w_qb=jax.vmap(_prep_wqb)(mla_w_qb),
        mla_kv_norm_g=mla_kv_norm_g, mla_w_kvb=mla_w_kvb.astype(BF16),
        ffn_wu=ffn_w_up.astype(BF16), ffn_wd=ffn_w_down.astype(BF16), ffn_cw=ffn_conv_w,
    )
    return W


def kernel(x_prompt, x_sample, c_prompt, c_sample, state_hgrn, state_rglru, state_rglru_conv, state_gdn, state_gdn_conv, cache_mla_latent, cache_mla_krope, state_ffn_conv, ada_w, ada_b, norm_g, ab_w_in, ab_w_out, hgrn_lb_logits, hgrn_norm_g, lru_conv_w, lru_conv_b, lru_w_a, lru_b_a, lru_w_x, lru_b_x, lru_lambda, cd_w_in, cd_w_out, gdn_conv_w, gdn_a_log, gdn_dt_bias, gdn_norm_g, mla_q_norm_g, mla_w_qb, mla_kv_norm_g, mla_w_kvb, ffn_w_up, ffn_conv_w, ffn_w_down):
    bp, bs = x_prompt.shape[0], x_sample.shape[0]
    W = _prep_weights(norm_g, ab_w_in, ab_w_out, hgrn_lb_logits, hgrn_norm_g, lru_conv_w, lru_conv_b, lru_w_a, lru_b_a,
                      lru_w_x, lru_b_x, lru_lambda, cd_w_in, cd_w_out, gdn_conv_w, gdn_a_log, gdn_dt_bias,
                      gdn_norm_g, mla_q_norm_g, mla_w_qb, mla_kv_norm_g, mla_w_kvb, ffn_w_up, ffn_conv_w,
                      ffn_w_down)
    rows = bp + bs
    rows_pad = -(-rows // 8) * 8
    c_all = jnp.concatenate([c_prompt, c_sample, jnp.zeros((rows_pad - rows, D_MODEL), F32)], axis=0)
    mods = _ada_call(c_all, ada_w, ada_b)
    dt_ = x_prompt.dtype
    y_prompt, p_states = _run_group(
        x_prompt, mods[:, :bp],
        jnp.zeros((N_AB, bp, HEADS, HEAD_DIM, HEAD_DIM), F32),
        jnp.zeros((N_AB, bp, HALF), F32),
        jnp.zeros((N_AB, bp, LRU_CONV - 1, HALF), dt_),
        jnp.zeros((N_CD, bp, HEADS, HEAD_DIM, HEAD_DIM), F32),
        jnp.zeros((N_CD, bp, GD_CONV - 1, 3 * HALF), dt_),
        jnp.zeros((N_CD, bp, 0, MLA_KV_RANK), dt_),
        jnp.zeros((N_CD, bp, 0, MLA_ROPE), dt_),
        jnp.zeros((DEPTH, bp, FFN_CONV - 1, 2 * D_FF), dt_),
        W)
    y_sample, s_states = _run_group(
        x_sample, mods[:, bp:rows], state_hgrn, state_rglru, state_rglru_conv, state_gdn, state_gdn_conv,
        cache_mla_latent, cache_mla_krope, state_ffn_conv, W)
    return (y_prompt, y_sample) + tuple(p_states) + tuple(s_states)
```

```python
import functools
import math

import jax
import jax.numpy as jnp
from jax import lax
from jax.experimental import pallas as pl
from jax.experimental.pallas import tpu as pltpu

F32 = jnp.float32
BF16 = jnp.bfloat16

D_MODEL = 1024
DEPTH = 4
CHUNK = 64
HALF = D_MODEL // 2
N_AB = (DEPTH + 1) // 2
N_CD = DEPTH // 2
HEADS = 4
HEAD_DIM = HALF // HEADS
LRU_BLOCKS = 8
LRU_BLOCK = HALF // LRU_BLOCKS
LRU_CONV = 4
LRU_C = 8.0
LRU_GROUP = 8
GD_CONV = 4
MLA_NOPE = 128
MLA_ROPE = 64
MLA_QK = MLA_NOPE + MLA_ROPE
MLA_Q_RANK = 384
MLA_KV_RANK = 256
MLA_SCALE = (MLA_NOPE + MLA_ROPE) ** -0.5
ROPE_THETA = 10000.0
D_FF = 2816
FFN_CONV = 3
FF_TILE = 256
N_FF_TILES = D_FF // FF_TILE
FF_GROUP = 6
FF_SLOTS = 4
FF_AHEAD = 2
EPS = 1e-6
NEG_BIG = -1e30
SQRT_FLOOR = 1e-12
CD_COLS = 2816
SMALL_KR = 0
SMALL_GB = 64
SMALL_GA = 68

VMEM_LIMIT_BYTES = 56 * 1024 * 1024
ATTN_KV_BLOCK_BYTES = 8 * 1024 * 1024


def _cparams(*sem):
    return pltpu.CompilerParams(dimension_semantics=sem, vmem_limit_bytes=VMEM_LIMIT_BYTES)


def _dot(a, b):
    return jnp.dot(a, b, preferred_element_type=F32)


def _dot_nt(a, b):
    return lax.dot_general(a, b, (((1,), (1,)), ((), ())), preferred_element_type=F32)


def _dot_tn(a, b):
    return lax.dot_general(a, b, (((0,), (0,)), ((), ())), preferred_element_type=F32)


def _rms(x, g):
    return x * lax.rsqrt(jnp.mean(x * x, axis=-1, keepdims=True) + EPS) * g


def _silu(x):
    return x * jax.nn.sigmoid(x)


def _softplus(x):
    return jnp.maximum(x, 0.0) + jnp.log1p(jnp.exp(-jnp.abs(x)))


def _gelu_tanh(x):
    return 0.5 * x * (1.0 + jnp.tanh(math.sqrt(2.0 / math.pi) * (x + 0.044715 * (x * x * x))))


def _split3(x):
    x1 = x.astype(BF16)
    r1 = x - x1.astype(F32)
    x2 = r1.astype(BF16)
    x3 = (r1 - x2.astype(F32)).astype(BF16)
    return x1, x2, x3


def _log2(n):
    assert n & (n - 1) == 0
    return n.bit_length() - 1


def _cumsum_rows(x):
    n, c = x.shape
    group = min(n, 8)
    x3 = x.reshape(n // group, group, c)
    pos = lax.broadcasted_iota(jnp.int32, x3.shape, 1)
    s = 1
    while s < group:
        x3 = x3 + jnp.where(pos >= s, pltpu.roll(x3, s, 1), 0.0)
        s *= 2
    if n == group:
        return x3.reshape(n, c)
    groups = [x3[0]]
    for gi in range(1, n // group):
        groups.append(x3[gi] + groups[-1][group - 1:group, :])
    return jnp.concatenate(groups, axis=0)


def _block_row_bcast(b, row, h):
    L, n = b.shape
    blk = 2 * h
    if blk >= 8:
        b3 = b.reshape(L // blk, blk, n)
        return jnp.broadcast_to(b3[:, h - 1:h, :], (L // blk, blk, n)).reshape(L, n)
    group = min(L, 8)
    x0 = jnp.where((row & (blk - 1)) == h - 1, b, 0.0).reshape(L // group, group, n)
    out = x0
    for j in range(1, h + 1):
        out = out + pltpu.roll(x0, j, 1)
    for j in range(1, h):
        out = out + pltpu.roll(x0, group - j, 1)
    return out.reshape(L, n)


def _ada_kernel(c_ref, w_ref, b_ref, o_ref):
    c = _silu(c_ref[...]).astype(BF16)
    o_ref[0] = _dot(c, w_ref[0].astype(BF16)) + b_ref[0]


def _ada_call(c_all, ada_w, ada_b):
    rows = c_all.shape[0]
    tn = 2048
    return pl.pallas_call(
        _ada_kernel,
        out_shape=jax.ShapeDtypeStruct((DEPTH, rows, 6 * D_MODEL), F32),
        grid=(DEPTH, 6 * D_MODEL // tn),
        in_specs=[
            pl.BlockSpec((rows, D_MODEL), lambda l, j: (0, 0)),
            pl.BlockSpec((1, D_MODEL, tn), lambda l, j: (l, 0, j)),
            pl.BlockSpec((1, 1, tn), lambda l, j: (l, 0, j)),
        ],
        out_specs=pl.BlockSpec((1, rows, tn), lambda l, j: (l, 0, j)),
        compiler_params=_cparams("arbitrary", "arbitrary"),
        name="ada_mod",
    )(c_all, ada_w, ada_b.reshape(DEPTH, 1, 6 * D_MODEL))


def _nmm_kernel(x_ref, g_ref, sc_ref, sh_ref, w_ref, o_ref, *maybe_t_ref):
    nb, tt, d = x_ref.shape
    h = _rms(x_ref[...], g_ref[...]) * (1.0 + sc_ref[...]) + sh_ref[...]
    y = _dot(h.reshape(nb * tt, d).astype(BF16), w_ref[...])
    o_ref[...] = y.reshape(nb, tt, y.shape[-1])
    if maybe_t_ref:
        (t_ref,) = maybe_t_ref
        yt = y[:, y.shape[-1] - 128:].T
        t_ref[0] = yt[SMALL_GB:SMALL_GB + 2 * HEADS, :]


def _tiles(B, T):
    if T >= 512:
        return 1, 512
    assert B * T <= 512
    return B, T


def _nmm_call(x, g, scale, shift, w, gates_t=False):
    B, T, D = x.shape
    N = w.shape[1]
    nb, tt = _tiles(B, T)
    out_shape = jax.ShapeDtypeStruct((B, T, N), F32)
    out_specs = pl.BlockSpec((nb, tt, N), lambda b, i: (b, i, 0))
    if gates_t:
        assert nb == 1 and tt % 128 == 0
        out_shape = (out_shape, jax.ShapeDtypeStruct((B, 2 * HEADS, T), F32))
        out_specs = (out_specs, pl.BlockSpec((1, 2 * HEADS, tt), lambda b, i: (b, 0, i)))
    return pl.pallas_call(
        _nmm_kernel,
        out_shape=out_shape,
        grid=(B // nb, T // tt),
        in_specs=[
            pl.BlockSpec((nb, tt, D), lambda b, i: (b, i, 0)),
            pl.BlockSpec((1, D), lambda b, i: (0, 0)),
            pl.BlockSpec((nb, 1, D), lambda b, i: (b, 0, 0)),
            pl.BlockSpec((nb, 1, D), lambda b, i: (b, 0, 0)),
            pl.BlockSpec((D, N), lambda b, i: (0, 0)),
        ],
        out_specs=out_specs,
        compiler_params=_cparams("arbitrary", "arbitrary"),
        name="norm_mod_proj",
    )(x, g.reshape(1, D), scale[:, None, :], shift[:, None, :], w)


def _ffn_kernel(x_ref, oa_ref, ob_ref, wo_ref, go_ref, gateo_ref, g1_ref, sc_ref, sh_ref, gate_ref, g2_ref,
                buf0_ref, wu_ref, wd_ref, cw_ref, o_ref, st_ref, carry_ref, ubuf_ref):
    nb, tt, d = x_ref.shape
    i = pl.program_id(1)

    @pl.when(i == 0)
    def _():
        for c in range(2 * N_FF_TILES):
            carry_ref[c] = buf0_ref[:, :, c * FF_TILE:(c + 1) * FF_TILE]

    oa = oa_ref[...].reshape(nb * tt, HALF)
    ob = ob_ref[...].reshape(nb * tt, HALF)
    mix = _dot(oa, wo_ref[0:HALF, :]) + _dot(ob, wo_ref[HALF:2 * HALF, :])
    x = x_ref[...] + gateo_ref[...] * _rms(mix, go_ref[...]).reshape(nb, tt, d)
    h = (_rms(x, g1_ref[...]) * (1.0 + sc_ref[...]) + sh_ref[...]).reshape(nb * tt, d).astype(BF16)

    def conv(u, slot, c):
        ubuf_ref[slot, :, 8:8 + tt, :] = u
        ubuf_ref[slot, :, 6:8, :] = carry_ref[c]
        cw = cw_ref[:, c * FF_TILE:(c + 1) * FF_TILE]
        y = (cw[0:1, :] * ubuf_ref[slot, :, 6:6 + tt, :] + cw[1:2, :] * ubuf_ref[slot, :, 7:7 + tt, :]
             + cw[2:3, :] * u)
        tail = ubuf_ref[slot, :, 6 + tt:8 + tt, :]
        carry_ref[c] = tail
        st_ref[:, :, c * FF_TILE:(c + 1) * FF_TILE] = tail
        return y

    def up_proj(c):
        wg = wu_ref[:, c * FF_TILE:(c + 1) * FF_TILE]
        wv = wu_ref[:, D_FF + c * FF_TILE:D_FF + (c + 1) * FF_TILE]
        return _dot(h, wg).reshape(nb, tt, FF_TILE), _dot(h, wv).reshape(nb, tt, FF_TILE)

    acc = None
    ahead = [up_proj(c) for c in range(min(FF_AHEAD, N_FF_TILES))]
    for g0 in range(0, N_FF_TILES, FF_GROUP):
        acts = []
        for c in range(g0, min(g0 + FF_GROUP, N_FF_TILES)):
            ug, uv = ahead.pop(0)
            if c + FF_AHEAD < N_FF_TILES:
                ahead.append(up_proj(c + FF_AHEAD))
            slot = 2 * (c % FF_SLOTS)
            yg = conv(ug, slot, c)
            yv = conv(uv, slot + 1, N_FF_TILES + c)
            acts.append((_silu(yg) * yv).reshape(nb * tt, FF_TILE).astype(BF16))
        a = jnp.concatenate(acts, axis=-1) if len(acts) > 1 else acts[0]
        part = _dot(a, wd_ref[g0 * FF_TILE:g0 * FF_TILE + a.shape[-1], :])
        acc = part if acc is None else acc + part
    y = _rms(acc, g2_ref[...]).reshape(nb, tt, d)
    o_ref[...] = x + gate_ref[...] * y


def _ffn_call(x, oa, ob, wo, go, gateo, g1, scale, shift, gate, g2, buf0, wu, wd, cw):
    B, T, D = x.shape
    nb, tt = _tiles(B, T)
    assert T >= FFN_CONV - 1
    return pl.pallas_call(
        _ffn_kernel,
        out_shape=(jax.ShapeDtypeStruct((B, T, D), F32),
                   jax.ShapeDtypeStruct((B, FFN_CONV - 1, 2 * D_FF), F32)),
        grid=(B // nb, T // tt),
        in_specs=[
            pl.BlockSpec((nb, tt, D), lambda b, i: (b, i, 0)),
            pl.BlockSpec((nb, tt, HALF), lambda b, i: (b, i, 0)),
            pl.BlockSpec((nb, tt, HALF), lambda b, i: (b, i, 0)),
            pl.BlockSpec((D, D), lambda b, i: (0, 0), pipeline_mode=pl.Buffered(1)),
            pl.BlockSpec((1, D), lambda b, i: (0, 0)),
            pl.BlockSpec((nb, 1, D), lambda b, i: (b, 0, 0)),
            pl.BlockSpec((1, D), lambda b, i: (0, 0)),
            pl.BlockSpec((nb, 1, D), lambda b, i: (b, 0, 0)),
            pl.BlockSpec((nb, 1, D), lambda b, i: (b, 0, 0)),
            pl.BlockSpec((nb, 1, D), lambda b, i: (b, 0, 0)),
            pl.BlockSpec((1, D), lambda b, i: (0, 0)),
            pl.BlockSpec((nb, FFN_CONV - 1, 2 * D_FF), lambda b, i: (b, 0, 0)),
            pl.BlockSpec((D, 2 * D_FF), lambda b, i: (0, 0), pipeline_mode=pl.Buffered(1)),
            pl.BlockSpec((D_FF, D), lambda b, i: (0, 0), pipeline_mode=pl.Buffered(1)),
            pl.BlockSpec((FFN_CONV, 2 * D_FF), lambda b, i: (0, 0)),
        ],
        out_specs=(pl.BlockSpec((nb, tt, D), lambda b, i: (b, i, 0)),
                   pl.BlockSpec((nb, FFN_CONV - 1, 2 * D_FF), lambda b, i: (b, 0, 0))),
        scratch_shapes=[
            pltpu.VMEM((2 * N_FF_TILES, nb, FFN_CONV - 1, FF_TILE), F32),
            pltpu.VMEM((2 * FF_SLOTS, nb, 8 + tt, FF_TILE), F32),
        ],
        compiler_params=_cparams("arbitrary", "arbitrary"),
        name="conv_ffn",
    )(x, oa, ob, wo, go.reshape(1, D), gateo[:, None, :], g1.reshape(1, D), scale[:, None, :], shift[:, None, :],
      gate[:, None, :], g2.reshape(1, D), buf0, wu, wd, cw)


def _hgrn_kernel(hq_ref, hf_ref, hi_ref, hz_ref, lb_ref, ng_ref, s0_ref, o_ref, st_ref):
    L = hq_ref.shape[1]
    i = pl.program_id(1)

    @pl.when(i == 0)
    def _():
        st_ref[...] = s0_ref[...]

    row = lax.broadcasted_iota(jnp.int32, (L, HEAD_DIM), 0)
    r2 = lax.broadcasted_iota(jnp.int32, (L, L), 0)
    c2 = lax.broadcasted_iota(jnp.int32, (L, L), 1)
    heads = range(HEADS)
    sls = [slice(hd * HEAD_DIM, (hd + 1) * HEAD_DIM) for hd in heads]
    q, k, v, b = ([None] * HEADS for _ in range(4))
    for hd in heads:
        z = hf_ref[0, :, sls[hd]]
        lb = lb_ref[:, sls[hd]]
        en = jnp.exp(-jnp.abs(z))
        inv = 1.0 / (1.0 + en)
        sig_pos = jnp.where(z >= 0.0, inv, en * inv)
        sig_neg = jnp.where(z >= 0.0, en * inv, inv)
        g = jnp.log(lb + (1.0 - lb) * sig_pos)
        k[hd] = (1.0 - lb) * sig_neg
        q[hd] = _silu(hq_ref[0, :, sls[hd]])
        v[hd] = hi_ref[0, :, sls[hd]].astype(BF16)
        b[hd] = _cumsum_rows(g)
    att = [jnp.where(r2 == c2, _dot_nt(q[hd].astype(BF16), k[hd].astype(BF16)), 0.0) for hd in heads]
    h = L // 2
    while h >= 1:
        sh = _log2(2 * h)
        keep = jnp.logical_and((r2 >> sh) == (c2 >> sh),
                               jnp.logical_and((r2 & (2 * h - 1)) >= h, (c2 & (2 * h - 1)) < h))
        for hd in heads:
            r = _block_row_bcast(b[hd], row, h)
            e = jnp.exp2(jnp.abs(b[hd] - r) * (-math.log2(math.e)))
            qt = (q[hd] * e).astype(BF16)
            kt = (k[hd] * e).astype(BF16)
            att[hd] = jnp.where(keep, _dot_nt(qt, kt), att[hd])
        h //= 2
    for hd in heads:
        st = st_ref[0, hd]
        o = _dot(att[hd].astype(BF16), v[hd]) + _dot_nt((q[hd] * jnp.exp(b[hd])).astype(BF16), st.astype(BF16))
        b_last = b[hd][L - 1:L, :]
        kd = (k[hd] * jnp.exp(b_last - b[hd])).astype(BF16)
        st_ref[0, hd] = jnp.exp(b_last) * st + _dot_tn(v[hd], kd)
        o = _rms(o, ng_ref[...]) * _silu(hz_ref[0, :, sls[hd]])
        o_ref[0, :, sls[hd]] = o.astype(BF16)


def _hgrn_call(proj, lb, ng, s0t):
    B, T, _ = proj.shape
    L = 128 if T % 128 == 0 else T
    assert T % L == 0 and L & (L - 1) == 0 and L >= 8
    col = lambda c: pl.BlockSpec((1, L, HALF), lambda b, i, c=c: (b, i, c))
    return pl.pallas_call(
        _hgrn_kernel,
        out_shape=(jax.ShapeDtypeStruct((B, T, HALF), BF16),
                   jax.ShapeDtypeStruct((B, HEADS, HEAD_DIM, HEAD_DIM), F32)),
        grid=(B, T // L),
        in_specs=[col(0), col(1), col(2), col(3),
                  pl.BlockSpec((1, HALF), lambda b, i: (0, 0)),
                  pl.BlockSpec((1, HEAD_DIM), lambda b, i: (0, 0)),
                  pl.BlockSpec((1, HEADS, HEAD_DIM, HEAD_DIM), lambda b, i: (b, 0, 0, 0))],
        out_specs=(pl.BlockSpec((1, L, HALF), lambda b, i: (b, i, 0)),
                   pl.BlockSpec((1, HEADS, HEAD_DIM, HEAD_DIM), lambda b, i: (b, 0, 0, 0))),
        compiler_params=_cparams("arbitrary", "arbitrary"),
        name="hgrn2",
    )(proj, proj, proj, proj, lb.reshape(1, HALF), ng.reshape(1, HEAD_DIM), s0t)


def _lru_kernel(lx_ref, ly_ref, buf0_ref, h0_ref, cw_ref, cb_ref, wa_ref, wx_ref, ba_ref, bx_ref,
                lam_ref, o_ref, hl_ref, bufo_ref, xp_ref):
    tt = lx_ref.shape[1]
    i = pl.program_id(1)
    npad = LRU_CONV - 1

    @pl.when(i == 0)
    def _():
        xp_ref[8 - npad:8, :] = buf0_ref[0]
        hl_ref[0] = h0_ref[0]

    x = lx_ref[0]
    xp_ref[8:8 + tt, :] = x
    xc = cb_ref[...] + cw_ref[npad:npad + 1, :] * x
    for tap in range(npad):
        xc = xc + cw_ref[tap:tap + 1, :] * xp_ref[8 - npad + tap:8 - npad + tap + tt, :]
    tail = xp_ref[8 + tt - npad:8 + tt, :]
    xp_ref[8 - npad:8, :] = tail
    bufo_ref[0] = tail

    xb = xc.astype(BF16)
    half = HALF // 2
    rpre = jnp.concatenate([_dot(xb[:, 0:half], wa_ref[0]), _dot(xb[:, half:HALF], wa_ref[1])], axis=-1)
    ipre = jnp.concatenate([_dot(xb[:, 0:half], wx_ref[0]), _dot(xb[:, half:HALF], wx_ref[1])], axis=-1)
    r = jax.nn.sigmoid(rpre + ba_ref[...])
    ig = jax.nn.sigmoid(ipre + bx_ref[...])
    log_a = -LRU_C * r * _softplus(-lam_ref[...])
    a = jnp.exp(log_a)
    u = jnp.sqrt(jnp.maximum(-jnp.tanh(log_a) * (1.0 + a * a), SQRT_FLOOR)) * ig * xc

    a = a.reshape(tt // LRU_GROUP, LRU_GROUP, HALF)
    u = u.reshape(tt // LRU_GROUP, LRU_GROUP, HALF)
    pos = lax.broadcasted_iota(jnp.int32, a.shape, 1)
    s = 1
    while s < LRU_GROUP:
        keep = pos >= s
        a_sh = jnp.where(keep, pltpu.roll(a, s, 1), 1.0)
        u_sh = jnp.where(keep, pltpu.roll(u, s, 1), 0.0)
        u = a * u_sh + u
        a = a * a_sh
        s *= 2
    carry = hl_ref[0]
    groups = []
    for gi in range(tt // LRU_GROUP):
        hg = u[gi] + a[gi] * carry
        carry = hg[LRU_GROUP - 1:LRU_GROUP, :]
        groups.append(hg)
    hseq = jnp.concatenate(groups, axis=0) if len(groups) > 1 else groups[0]
    hl_ref[0] = carry
    o_ref[0] = (hseq * _gelu_tanh(ly_ref[0])).astype(BF16)


def _lru_call(proj, buf0, h0, cw, cb, wa_bd, wx_bd, ba, bx, lam):
    B, T, _ = proj.shape
    tt = 256 if T % 256 == 0 else T
    assert T % tt == 0 and T >= LRU_CONV - 1 and tt % 8 == 0
    vec = pl.BlockSpec((1, HALF), lambda b, i: (0, 0))
    wspec = pl.BlockSpec((2, HALF // 2, HALF // 2), lambda b, i: (0, 0, 0))
    return pl.pallas_call(
        _lru_kernel,
        out_shape=(jax.ShapeDtypeStruct((B, T, HALF), BF16),
                   jax.ShapeDtypeStruct((B, 1, HALF), F32),
                   jax.ShapeDtypeStruct((B, LRU_CONV - 1, HALF), F32)),
        grid=(B, T // tt),
        in_specs=[pl.BlockSpec((1, tt, HALF), lambda b, i: (b, i, 4)),
                  pl.BlockSpec((1, tt, HALF), lambda b, i: (b, i, 5)),
                  pl.BlockSpec((1, LRU_CONV - 1, HALF), lambda b, i: (b, 0, 0)),
                  pl.BlockSpec((1, 1, HALF), lambda b, i: (b, 0, 0)),
                  pl.BlockSpec((LRU_CONV, HALF), lambda b, i: (0, 0)),
                  vec, wspec, wspec, vec, vec, vec],
        out_specs=(pl.BlockSpec((1, tt, HALF), lambda b, i: (b, i, 0)),
                   pl.BlockSpec((1, 1, HALF), lambda b, i: (b, 0, 0)),
                   pl.BlockSpec((1, LRU_CONV - 1, HALF), lambda b, i: (b, 0, 0))),
        scratch_shapes=[pltpu.VMEM((8 + tt, HALF), F32)],
        compiler_params=_cparams("arbitrary", "arbitrary"),
        name="rglru",
    )(proj, proj, buf0, h0[:, None, :], cw, cb.reshape(1, HALF), wa_bd, wx_bd,
      ba.reshape(1, HALF), bx.reshape(1, HALF), lam.reshape(1, HALF))


def _gdn_kernel(qkv_ref, gz_ref, sm_ref, smt_ref, buf0_ref, cw_ref, pcol_ref, prow_ref, ng_ref, s0_ref,
                o_ref, st_ref, bufo_ref, xp_ref, *, L):
    tt = qkv_ref.shape[1]
    nc = tt // L
    sh = _log2(L)
    i = pl.program_id(1)
    npad = GD_CONV - 1

    @pl.when(i == 0)
    def _():
        xp_ref[8 - npad:8, :] = buf0_ref[0]
        st_ref[...] = s0_ref[...]

    x = qkv_ref[0]
    xp_ref[8:8 + tt, :] = x
    xc = cw_ref[npad:npad + 1, :] * x
    for tap in range(npad):
        xc = xc + cw_ref[tap:tap + 1, :] * xp_ref[8 - npad + tap:8 - npad + tap + tt, :]
    tail = xp_ref[8 + tt - npad:8 + tt, :]
    xp_ref[8 - npad:8, :] = tail
    bufo_ref[0] = tail
    xc = _silu(xc)

    r2 = lax.broadcasted_iota(jnp.int32, (tt, tt), 0)
    c2 = lax.broadcasted_iota(jnp.int32, (tt, tt), 1)
    same = (r2 >> sh) == (c2 >> sh)
    incl = jnp.logical_and(same, c2 <= r2)
    strict = jnp.logical_and(same, c2 < r2)
    tri_lo = jnp.where(incl, 1.0, 0.0).astype(BF16)
    tri_up = jnp.where(jnp.logical_and(same, r2 <= c2), 1.0, 0.0).astype(BF16)
    eye = jnp.where(r2 == c2, 1.0, 0.0)

    sm = sm_ref[0]
    beta_cols = jax.nn.sigmoid(sm)
    la_cols = -jnp.exp(pcol_ref[0:1, :]) * _softplus(sm + pcol_ref[1:2, :])
    c1, c2_, c3 = _split3(la_cols)
    g_cols = _dot(tri_lo, c1) + _dot(tri_lo, c2_) + _dot(tri_lo, c3)
    la_rows = -jnp.exp(prow_ref[:, 0:1]) * _softplus(smt_ref[0] + prow_ref[:, 1:2])
    w1, w2, w3 = _split3(la_rows)
    g_rows = _dot(w1, tri_up) + _dot(w2, tri_up) + _dot(w3, tri_up)

    heads = range(HEADS)
    q, k, v, beta, gcol, kb, dec, m = ([None] * HEADS for _ in range(8))
    for hd in heads:
        qh = xc[:, hd * HEAD_DIM:(hd + 1) * HEAD_DIM]
        kh = xc[:, HALF + hd * HEAD_DIM:HALF + (hd + 1) * HEAD_DIM]
        v[hd] = xc[:, 2 * HALF + hd * HEAD_DIM:2 * HALF + (hd + 1) * HEAD_DIM]
        q[hd] = qh * lax.rsqrt(jnp.sum(qh * qh, axis=-1, keepdims=True) + EPS) * (HEAD_DIM ** -0.5)
        k[hd] = kh * lax.rsqrt(jnp.sum(kh * kh, axis=-1, keepdims=True) + EPS)
        beta[hd] = beta_cols[:, SMALL_GB + hd:SMALL_GB + hd + 1]
        gcol[hd] = g_cols[:, SMALL_GA + hd:SMALL_GA + hd + 1]
        grow = g_rows[HEADS + hd:HEADS + hd + 1, :]
        kb[hd] = k[hd].astype(BF16)
        dec[hd] = jnp.exp(jnp.where(incl, gcol[hd] - grow, NEG_BIG))
        m[hd] = beta[hd] * _dot_nt(kb[hd], kb[hd]) * jnp.where(strict, dec[hd], 0.0)

    pair = (r2 >> 1) == (c2 >> 1)
    tinv = [eye - jnp.where(pair, m[hd], 0.0) for hd in heads]
    s = 2
    while s < L:
        ssh = _log2(s)
        lower_left = jnp.logical_and((r2 >> (ssh + 1)) == (c2 >> (ssh + 1)), (r2 >> ssh) != (c2 >> ssh))
        tb = [tinv[hd].astype(BF16) for hd in heads]
        tc = [_dot(tb[hd], jnp.where(lower_left, m[hd], 0.0).astype(BF16)).astype(BF16) for hd in heads]
        tinv = [tinv[hd] - _dot(tc[hd], tb[hd]) for hd in heads]
        s *= 2

    eg = [jnp.exp(gcol[hd]) for hd in heads]
    sol = [_dot(tinv[hd].astype(BF16),
                jnp.concatenate([beta[hd] * v[hd], (beta[hd] * eg[hd]) * k[hd]], axis=-1).astype(BF16))
           for hd in heads]
    u_v = [sol[hd][:, 0:HEAD_DIM] for hd in heads]
    w_k = [sol[hd][:, HEAD_DIM:2 * HEAD_DIM].astype(BF16) for hd in heads]
    qb = [q[hd].astype(BF16) for hd in heads]
    qk = [(_dot_nt(qb[hd], kb[hd]) * dec[hd]).astype(BF16) for hd in heads]

    a_c, p_c, n_c = ([[None] * nc for _ in heads] for _ in range(3))
    for c in range(nc):
        rs = slice(c * L, (c + 1) * L)
        for hd in heads:
            g_last = gcol[hd][(c + 1) * L - 1:(c + 1) * L, :]
            kd = (k[hd][rs] * jnp.exp(g_last - gcol[hd][rs])).astype(BF16)
            a_c[hd][c] = jnp.exp(g_last)
            p_c[hd][c] = (-_dot_tn(kd, w_k[hd][rs])).astype(BF16)
            n_c[hd][c] = _dot_tn(kd, u_v[hd][rs].astype(BF16))
    S = [[st_ref[0, hd]] for hd in heads]
    for c in range(nc):
        for hd in heads:
            s_cur = S[hd][c]
            S[hd].append(a_c[hd][c] * s_cur + _dot(p_c[hd][c], s_cur.astype(BF16)) + n_c[hd][c])
    us = [[] for _ in heads]
    inters = [[] for _ in heads]
    for c in range(nc):
        rs = slice(c * L, (c + 1) * L)
        for hd in heads:
            Sb = S[hd][c].astype(BF16)
            us[hd].append(u_v[hd][rs] - _dot(w_k[hd][rs], Sb))
            inters[hd].append(eg[hd][rs] * _dot(qb[hd][rs], Sb))
    for hd in heads:
        sl = slice(hd * HEAD_DIM, (hd + 1) * HEAD_DIM)
        st_ref[0, hd] = S[hd][nc]
        u_all = jnp.concatenate(us[hd], axis=0) if nc > 1 else us[hd][0]
        inter = jnp.concatenate(inters[hd], axis=0) if nc > 1 else inters[hd][0]
        o = _dot(qk[hd], u_all.astype(BF16)) + inter
        o = _rms(o, ng_ref[...]) * _silu(gz_ref[0, :, sl])
        o_ref[0, :, sl] = o.astype(BF16)


def _gdn_call(proj, small_t, buf0, cw, pcol, prow, ng, s0):
    B, T, _ = proj.shape
    L = CHUNK if T % CHUNK == 0 else T
    tt = 256 if T % 256 == 0 else T
    assert T % tt == 0 and tt % L == 0 and T >= GD_CONV - 1 and L >= 2
    W = 3 * HALF
    return pl.pallas_call(
        functools.partial(_gdn_kernel, L=L),
        out_shape=(jax.ShapeDtypeStruct((B, T, HALF), BF16),
                   jax.ShapeDtypeStruct((B, HEADS, HEAD_DIM, HEAD_DIM), F32),
                   jax.ShapeDtypeStruct((B, GD_CONV - 1, W), F32)),
        grid=(B, T // tt),
        in_specs=[pl.BlockSpec((1, tt, W), lambda b, i: (b, i, 0)),
                  pl.BlockSpec((1, tt, HALF), lambda b, i: (b, i, 3)),
                  pl.BlockSpec((1, tt, 128), lambda b, i: (b, i, 21)),
                  pl.BlockSpec((1, 2 * HEADS, tt), lambda b, i: (b, 0, i)),
                  pl.BlockSpec((1, GD_CONV - 1, W), lambda b, i: (b, 0, 0)),
                  pl.BlockSpec((GD_CONV, W), lambda b, i: (0, 0)),
                  pl.BlockSpec((2, 128), lambda b, i: (0, 0)),
                  pl.BlockSpec((2 * HEADS, 2), lambda b, i: (0, 0)),
                  pl.BlockSpec((1, HEAD_DIM), lambda b, i: (0, 0)),
                  pl.BlockSpec((1, HEADS, HEAD_DIM, HEAD_DIM), lambda b, i: (b, 0, 0, 0))],
        out_specs=(pl.BlockSpec((1, tt, HALF), lambda b, i: (b, i, 0)),
                   pl.BlockSpec((1, HEADS, HEAD_DIM, HEAD_DIM), lambda b, i: (b, 0, 0, 0)),
                   pl.BlockSpec((1, GD_CONV - 1, W), lambda b, i: (b, 0, 0))),
        scratch_shapes=[pltpu.VMEM((8 + tt, W), F32)],
        compiler_params=_cparams("arbitrary", "arbitrary"),
        name="gated_deltanet",
    )(proj, proj, proj, small_t, buf0, cw, pcol, prow, ng.reshape(1, HEAD_DIM), s0)


def _rope64(x, cs, sn):
    half = MLA_ROPE // 2
    swapped = jnp.concatenate([x[:, half:], x[:, :half]], axis=-1)
    return x * cs + swapped * sn


def _expand_kv(c_kv, k_r, wkvb_ref, kf_ref, v_ref):
    kv = _dot(c_kv.astype(BF16), wkvb_ref[...])
    for hd in range(HEADS):
        base = hd * 2 * HEAD_DIM
        kf_ref[0, hd] = jnp.concatenate([kv[:, base:base + MLA_NOPE], k_r], axis=-1).astype(BF16)
        v = kv[:, base + MLA_NOPE:base + 2 * HEAD_DIM]
        v_ref[0, hd] = jnp.concatenate([v, jnp.ones_like(v)], axis=-1).astype(BF16)


def _mla_prep_kernel(qa_ref, kva_ref, sm_ref, cs_ref, sn_ref, qng_ref, wqb_ref, kvng_ref, wkvb_ref, *rest):
    q_ref, ckv_ref, kr_ref, kf_ref, v_ref = rest[-5:]
    ckv_ref = ckv_ref.at[0]
    kr_ref = kr_ref.at[0]
    cs = cs_ref[...]
    sn = sn_ref[...]
    qn = _rms(qa_ref[0], qng_ref[...]).astype(BF16)
    qh = _dot(qn, wqb_ref[...]) * (MLA_SCALE * math.log2(math.e))
    for hd in range(HEADS):
        nope = qh[:, hd * MLA_NOPE:(hd + 1) * MLA_NOPE]
        off = HEADS * MLA_NOPE + hd * MLA_ROPE
        rot = _rope64(qh[:, off:off + MLA_ROPE], cs, sn)
        q_ref[0, hd] = jnp.concatenate([nope, rot], axis=-1).astype(BF16)
    c_kv = _rms(kva_ref[0], kvng_ref[...])
    ckv_ref[0] = c_kv
    k_r = _rope64(sm_ref[0, :, SMALL_KR:SMALL_KR + MLA_ROPE], cs, sn)
    kr_ref[0] = k_r
    _expand_kv(c_kv, k_r, wkvb_ref, kf_ref, v_ref)


def _mla_prep_call(proj, cs, sn, qng, wqb, kvng, wkvb, j, state_bufs):
    B, T, _ = proj.shape
    tt = 512 if T % 512 == 0 else T
    c2 = lambda b, i: (0, 0)
    any_spec = pl.BlockSpec(memory_space=pl.ANY)
    n_fixed = 9
    extra = list(state_bufs)
    aliases = {n_fixed: 1, n_fixed + 1: 2}
    return pl.pallas_call(
        _mla_prep_kernel,
        out_shape=(jax.ShapeDtypeStruct((B, HEADS, T, MLA_QK), BF16),
                   jax.ShapeDtypeStruct((N_CD, B, T, MLA_KV_RANK), F32),
                   jax.ShapeDtypeStruct((N_CD, B, T, MLA_ROPE), F32),
                   jax.ShapeDtypeStruct((B, HEADS, T, MLA_QK), BF16),
                   jax.ShapeDtypeStruct((B, HEADS, T, 2 * HEAD_DIM), BF16)),
        grid=(B, T // tt),
        in_specs=[pl.BlockSpec((1, tt, MLA_Q_RANK), lambda b, i: (b, i, 6)),
                  pl.BlockSpec((1, tt, MLA_KV_RANK), lambda b, i: (b, i, 8)),
                  pl.BlockSpec((1, tt, 128), lambda b, i: (b, i, 21)),
                  pl.BlockSpec((tt, MLA_ROPE), lambda b, i: (i, 0)),
                  pl.BlockSpec((tt, MLA_ROPE), lambda b, i: (i, 0)),
                  pl.BlockSpec((1, MLA_Q_RANK), c2),
                  pl.BlockSpec((MLA_Q_RANK, HEADS * MLA_QK), c2),
                  pl.BlockSpec((1, MLA_KV_RANK), c2),
                  pl.BlockSpec((MLA_KV_RANK, HEADS * 2 * HEAD_DIM), c2)] + [any_spec] * len(extra),
        out_specs=(pl.BlockSpec((1, HEADS, tt, MLA_QK), lambda b, i: (b, 0, i, 0)),
                   pl.BlockSpec((1, 1, tt, MLA_KV_RANK), lambda b, i: (j, b, i, 0)),
                   pl.BlockSpec((1, 1, tt, MLA_ROPE), lambda b, i: (j, b, i, 0)),
                   pl.BlockSpec((1, HEADS, tt, MLA_QK), lambda b, i: (b, 0, i, 0)),
                   pl.BlockSpec((1, HEADS, tt, 2 * HEAD_DIM), lambda b, i: (b, 0, i, 0))),
        input_output_aliases=aliases,
        compiler_params=_cparams("arbitrary", "arbitrary"),
        name="mla_prep",
    )(proj, proj, proj, cs, sn, qng.reshape(1, MLA_Q_RANK), wqb, kvng.reshape(1, MLA_KV_RANK), wkvb, *extra)


def _attn_cached_kernel(q_ref, latp_ref, krp_ref, latn_ref, krn_ref, w_ref, o_ref):
    T = q_ref.shape[2]
    P = latp_ref.shape[1]
    sh = _log2(CHUNK)
    latp = latp_ref[0].astype(BF16)
    krp = krp_ref[0].astype(BF16)
    latn = latn_ref[0, 0].astype(BF16)
    krn = krn_ref[0, 0].astype(BF16)
    q_chunk = (P + lax.broadcasted_iota(jnp.int32, (T, 1), 0)) >> sh
    vis_p = (lax.broadcasted_iota(jnp.int32, (T, P), 1) >> sh) <= q_chunk
    vis_n = ((P + lax.broadcasted_iota(jnp.int32, (T, T), 1)) >> sh) <= q_chunk
    for hd in range(HEADS):
        base = hd * 2 * HEAD_DIM
        q = q_ref[0, hd]
        q_lat = _dot_nt(q[:, 0:MLA_NOPE], w_ref[:, base:base + MLA_NOPE]).astype(BF16)
        q_rope = q[:, MLA_NOPE:MLA_QK]
        s_p = jnp.where(vis_p, _dot_nt(q_lat, latp) + _dot_nt(q_rope, krp), NEG_BIG)
        s_n = jnp.where(vis_n, _dot_nt(q_lat, latn) + _dot_nt(q_rope, krn), NEG_BIG)
        m = jnp.maximum(jnp.max(s_p, axis=-1, keepdims=True), jnp.max(s_n, axis=-1, keepdims=True))
        p_p = jnp.exp2(s_p - m)
        p_n = jnp.exp2(s_n - m)
        denom = jnp.sum(p_p, axis=-1, keepdims=True) + jnp.sum(p_n, axis=-1, keepdims=True)
        o_lat = (_dot(p_p.astype(BF16), latp) + _dot(p_n.astype(BF16), latn)) / denom
        o = _dot(o_lat.astype(BF16), w_ref[:, base + MLA_NOPE:base + 2 * HEAD_DIM])
        o_ref[0, :, hd * HEAD_DIM:(hd + 1) * HEAD_DIM] = o.astype(BF16)


def _attn_cached_call(q, lat_past, kr_past, lat_all, kr_all, j, wkvb):
    B, H, T, _ = q.shape
    P = lat_past.shape[1]
    return pl.pallas_call(
        _attn_cached_kernel,
        out_shape=jax.ShapeDtypeStruct((B, T, H * HEAD_DIM), BF16),
        grid=(B,),
        in_specs=[pl.BlockSpec((1, H, T, MLA_QK), lambda b: (b, 0, 0, 0)),
                  pl.BlockSpec((1, P, MLA_KV_RANK), lambda b: (b, 0, 0)),
                  pl.BlockSpec((1, P, MLA_ROPE), lambda b: (b, 0, 0)),
                  pl.BlockSpec((1, 1, T, MLA_KV_RANK), lambda b: (j, b, 0, 0)),
                  pl.BlockSpec((1, 1, T, MLA_ROPE), lambda b: (j, b, 0, 0)),
                  pl.BlockSpec((MLA_KV_RANK, H * 2 * HEAD_DIM), lambda b: (0, 0))],
        out_specs=pl.BlockSpec((1, T, H * HEAD_DIM), lambda b: (b, 0, 0)),
        compiler_params=_cparams("arbitrary"),
        name="mla_attention_cached",
    )(q, lat_past, kr_past, lat_all, kr_all, wkvb)


def _attn_kernel(q_ref, k_ref, v_ref, o_ref, m_ref, acc_ref, sa_ref, sb_ref, **static):
    for hh in range(q_ref.shape[1]):
        _attn_head(q_ref.at[0, hh], k_ref.at[0, hh], v_ref.at[0, hh], o_ref.at[0, :, hh * HEAD_DIM:(hh + 1) * HEAD_DIM],
                   m_ref, acc_ref, sa_ref, sb_ref, **static)


def _attn_head(q_ref, k_ref, v_ref, o_ref, m_ref, acc_ref, sa_ref, sb_ref, *, past_len, tq, tk, nk):
    i = pl.program_id(2)
    sh = _log2(CHUNK)
    lanes = HEAD_DIM
    m_ref[...] = jnp.full(m_ref.shape, NEG_BIG, F32)
    acc_ref[...] = jnp.zeros(acc_ref.shape, F32)
    q = q_ref[...]
    q_lo = past_len + i * tq
    n_full = jnp.minimum(nk, (((q_lo >> sh) + 1) * CHUNK) // tk)
    n_need = jnp.minimum(nk, ((((q_lo + tq - 1) >> sh) + 1) * CHUNK + tk - 1) // tk)

    def scores(j):
        return _dot_nt(q, k_ref[pl.ds(pl.multiple_of(j * tk, tk), tk), :])

    def masked_scores(j):
        k_lo = j * tk
        qc = (q_lo + lax.broadcasted_iota(jnp.int32, (tq, tk), 0)) >> sh
        kc = (k_lo + lax.broadcasted_iota(jnp.int32, (tq, tk), 1)) >> sh
        return jnp.where(kc <= qc, scores(j), NEG_BIG)

    def update(j, s_ref):
        k_lo = pl.multiple_of(j * tk, tk)
        m_prev = m_ref[...]
        m_new = jnp.maximum(m_prev, jnp.max(s_ref[...], axis=-1, keepdims=True))
        alpha = jnp.exp2(m_prev - m_new)
        if tk % lanes == 0:
            p = jnp.exp2(s_ref[...] - jnp.tile(m_new, (1, tk // lanes)))
        else:
            p = jnp.exp2(s_ref[...] - m_new[:, 0:1])
        pv = _dot(p.astype(BF16), v_ref[pl.ds(k_lo, tk), :])
        acc_ref[...] = jnp.tile(alpha, (1, 2)) * acc_ref[...] + pv
        m_ref[...] = m_new

    def body_pair(g, carry):
        sb_ref[...] = scores(2 * g + 1)
        update(2 * g, sa_ref)
        sa_ref[...] = scores(2 * g + 2)
        update(2 * g + 1, sb_ref)
        return carry

    sa_ref[...] = scores(0)
    n_pairs = jnp.maximum(n_full - 1, 0) // 2
    lax.fori_loop(0, n_pairs, body_pair, 0)

    left = n_full - 2 * n_pairs
    has_masked = n_need > n_full
    j_masked = jnp.minimum(n_full, nk - 1)

    @pl.when(left == 1)
    def _():
        sb_ref[...] = masked_scores(j_masked)
        update(n_full - 1, sa_ref)

    @pl.when(jnp.logical_and(left == 1, has_masked))
    def _():
        update(n_full, sb_ref)

    @pl.when(left == 2)
    def _():
        sb_ref[...] = scores(n_full - 1)
        update(n_full - 2, sa_ref)
        sa_ref[...] = masked_scores(j_masked)
        update(n_full - 1, sb_ref)

    @pl.when(jnp.logical_and(left == 2, has_masked))
    def _():
        update(n_full, sa_ref)

    def body_masked(j, carry):
        sa_ref[...] = masked_scores(j)
        update(j, sa_ref)
        return carry

    lax.fori_loop(jnp.where(n_full > 0, n_full + 1, 0), n_need, body_masked, 0)
    o_ref[...] = (acc_ref[:, 0:HEAD_DIM] / acc_ref[:, HEAD_DIM:2 * HEAD_DIM]).astype(BF16)


def _attn_call(q, kf, v, past_len):
    B, H, Tq, _ = q.shape
    Tk = kf.shape[2]
    tq = 1024 if Tq % 1024 == 0 else Tq
    tk = 1024 if Tk % 1024 == 0 else Tk
    nk = Tk // tk
    kv_bytes = Tk * (MLA_QK + 2 * HEAD_DIM) * 2
    hb = H if H * kv_bytes <= ATTN_KV_BLOCK_BYTES else 1
    return pl.pallas_call(
        functools.partial(_attn_kernel, past_len=past_len, tq=tq, tk=tk, nk=nk),
        out_shape=jax.ShapeDtypeStruct((B, Tq, H * HEAD_DIM), BF16),
        grid=(B, H // hb, Tq // tq),
        in_specs=[pl.BlockSpec((1, hb, tq, MLA_QK), lambda b, h, i: (b, h, i, 0)),
                  pl.BlockSpec((1, hb, Tk, MLA_QK), lambda b, h, i: (b, h, 0, 0)),
                  pl.BlockSpec((1, hb, Tk, 2 * HEAD_DIM), lambda b, h, i: (b, h, 0, 0))],
        out_specs=pl.BlockSpec((1, tq, hb * HEAD_DIM), lambda b, h, i: (b, i, h)),
        scratch_shapes=[pltpu.VMEM((tq, HEAD_DIM), F32), pltpu.VMEM((tq, 2 * HEAD_DIM), F32),
                        pltpu.VMEM((tq, tk), F32), pltpu.VMEM((tq, tk), F32)],
        compiler_params=_cparams("arbitrary", "arbitrary", "arbitrary"),
        name="mla_attention",
    )(q, kf, v)


def _block_diag_pairs(w):
    per = (HALF // 2) // LRU_BLOCK
    w4 = w.reshape(2, per, LRU_BLOCK, LRU_BLOCK)
    eye = jnp.eye(per, dtype=w.dtype)
    out = w4[:, :, :, None, :] * eye[None, :, None, :, None]
    return out.reshape(2, HALF // 2, HALF // 2).astype(BF16)


def _prep_cd_w_in(w):
    o = 3 * HALF
    qkv, gz = w[:, :o], w[:, o:o + HALF]
    o += HALF
    gb, ga = w[:, o:o + HEADS], w[:, o + HEADS:o + 2 * HEADS]
    o += 2 * HEADS
    qa, kva = w[:, o:o + MLA_Q_RANK], w[:, o + MLA_Q_RANK:o + MLA_Q_RANK + MLA_KV_RANK]
    o += MLA_Q_RANK + MLA_KV_RANK
    kr = w[:, o:o + MLA_ROPE]
    assert (SMALL_KR, SMALL_GB, SMALL_GA) == (0, MLA_ROPE, MLA_ROPE + HEADS)
    fill = jnp.zeros((w.shape[0], 128 - MLA_ROPE - 2 * HEADS), w.dtype)
    out = jnp.concatenate([qkv, gz, kva, qa, kr, gb, ga, fill], axis=-1)
    assert out.shape[1] == CD_COLS
    return out.astype(BF16)


def _prep_wqb(w):
    w4 = w.reshape(MLA_Q_RANK, HEADS, MLA_QK)
    nope = w4[:, :, :MLA_NOPE].reshape(MLA_Q_RANK, HEADS * MLA_NOPE)
    rope = w4[:, :, MLA_NOPE:].reshape(MLA_Q_RANK, HEADS * MLA_ROPE)
    return jnp.concatenate([nope, rope], axis=-1).astype(BF16)


def _rope_tables(T, past_len):
    half = MLA_ROPE // 2
    freqs = jnp.exp(-math.log(ROPE_THETA) * jnp.arange(half, dtype=F32) / half)
    pos = past_len + jnp.arange(T, dtype=jnp.int32)
    ang = pos.astype(F32)[:, None] * freqs
    cos, sin = jnp.cos(ang), jnp.sin(ang)
    return jnp.concatenate([cos, cos], axis=-1), jnp.concatenate([-sin, sin], axis=-1)


def _run_group(x, mods, hg_s, lru_h, lru_buf, gd_s, gd_buf, lat_past, kr_past, ffn_buf, W):
    B, T, _ = x.shape
    n_hg, n_lru, n_lrub, n_gd, n_gdb, n_ffn = ([] for _ in range(6))
    mla_state = (jnp.zeros((N_CD, B, T, MLA_KV_RANK), F32), jnp.zeros((N_CD, B, T, MLA_ROPE), F32))
    for l in range(DEPTH):
        j = l // 2
        shift1, scale1, gate1, shift2, scale2, gate2 = jnp.split(mods[l], 6, axis=-1)
        g = W['norm_g'][l]
        if l % 2 == 0:
            proj = _nmm_call(x, g[0], scale1, shift1, W['ab_w_in'][j])
            o_a, s_hg_t = _hgrn_call(proj, W['lower_bounds'][j], W['hgrn_norm_g'][j],
                                     jnp.swapaxes(hg_s[j], -1, -2))
            o_b, s_lru, s_lrub = _lru_call(proj, lru_buf[j], lru_h[j], W['lru_conv_w'][j], W['lru_conv_b'][j],
                                           W['lru_wa_bd'][j], W['lru_wx_bd'][j], W['lru_b_a'][j],
                                           W['lru_b_x'][j], W['lru_lambda'][j])
            n_hg.append(jnp.swapaxes(s_hg_t, -1, -2))
            n_lru.append(s_lru[:, 0, :])
            n_lrub.append(s_lrub)
            mix_a, mix_b, w_out = o_a, o_b, W['ab_w_out'][j]
        else:
            past_len = lat_past.shape[2]
            if _tiles(B, T)[0] == 1:
                proj, small_t = _nmm_call(x, g[0], scale1, shift1, W['cd_w_in'][j], gates_t=True)
            else:
                proj = _nmm_call(x, g[0], scale1, shift1, W['cd_w_in'][j])
                small_t = jnp.swapaxes(proj[:, :, CD_COLS - 128 + SMALL_GB:CD_COLS - 128 + SMALL_GB + 2 * HEADS], 1, 2)
            o_c, s_gd, s_gdb = _gdn_call(proj, small_t, gd_buf[j], W['gdn_conv_w'][j], W['gdn_pcol'][j],
                                         W['gdn_prow'][j], W['gdn_norm_g'][j], gd_s[j])
            cs, sn = _rope_tables(T, past_len)
            q, lat_all, kr_all, kf, v = _mla_prep_call(proj, cs, sn, W['mla_q_norm_g'][j], W['mla_w_qb'][j],
                                                       W['mla_kv_norm_g'][j], W['mla_w_kvb'][j], j, mla_state)
            mla_state = (lat_all, kr_all)
            if past_len > 0:
                o_d = _attn_cached_call(q, lat_past[j], kr_past[j], lat_all, kr_all, j, W['mla_w_kvb'][j])
            else:
                o_d = _attn_call(q, kf, v, 0)
            n_gd.append(s_gd)
            n_gdb.append(s_gdb)
            mix_a, mix_b, w_out = o_c, o_d, W['cd_w_out'][j]
        x, s_ffn = _ffn_call(x, mix_a, mix_b, w_out, g[1], gate1, g[2], scale2, shift2, gate2, g[3], ffn_buf[l],
                             W['ffn_wu'][l], W['ffn_wd'][l], W['ffn_cw'][l])
        n_ffn.append(s_ffn)
    return x, (jnp.stack(n_hg), jnp.stack(n_lru), jnp.stack(n_lrub), jnp.stack(n_gd), jnp.stack(n_gdb),
               mla_state[0], mla_state[1], jnp.stack(n_ffn))


def _prep_weights(norm_g, ab_w_in, ab_w_out, hgrn_lb_logits, hgrn_norm_g, lru_conv_w, lru_conv_b, lru_w_a, lru_b_a,
                  lru_w_x, lru_b_x, lru_lambda, cd_w_in, cd_w_out, gdn_conv_w, gdn_a_log, gdn_dt_bias, gdn_norm_g,
                  mla_q_norm_g, mla_w_qb, mla_kv_norm_g, mla_w_kvb, ffn_w_up, ffn_conv_w, ffn_w_down):
    lb_p = jax.nn.softmax(hgrn_lb_logits.astype(F32), axis=0)
    pcol = jnp.pad(jnp.stack([gdn_a_log, gdn_dt_bias], axis=1),
                   ((0, 0), (0, 0), (SMALL_GA, 128 - SMALL_GA - HEADS)))
    prow = jnp.pad(jnp.stack([gdn_a_log, gdn_dt_bias], axis=-1), ((0, 0), (HEADS, 0), (0, 0)))
    W = dict(
        norm_g=norm_g,
        ab_w_in=ab_w_in.astype(BF16), ab_w_out=ab_w_out.astype(BF16),
        lower_bounds=jnp.cumsum(lb_p, axis=0) - lb_p[0:1],
        hgrn_norm_g=hgrn_norm_g, lru_conv_w=lru_conv_w, lru_conv_b=lru_conv_b,
        lru_wa_bd=jax.vmap(_block_diag_pairs)(lru_w_a), lru_wx_bd=jax.vmap(_block_diag_pairs)(lru_w_x),
        lru_b_a=lru_b_a, lru_b_x=lru_b_x, lru_lambda=lru_lambda,
        cd_w_in=jax.vmap(_prep_cd_w_in)(cd_w_in), cd_w_out=cd_w_out.astype(BF16),
        gdn_conv_w=gdn_conv_w, gdn_pcol=pcol, gdn_prow=prow, gdn_norm_g=gdn_norm_g,
        mla_q_norm_g=mla_q_norm_g, mla_w_qb=jax.vmap(_prep_wqb)(mla_w_qb),
        mla_kv_norm_g=mla_kv_norm_g, mla_w_kvb=mla_w_kvb.astype(BF16),
        ffn_wu=ffn_w_up.astype(BF16), ffn_wd=ffn_w_down.astype(BF16), ffn_cw=ffn_conv_w,
    )
    return W


def kernel(x_prompt, x_sample, c_prompt, c_sample, state_hgrn, state_rglru, state_rglru_conv, state_gdn, state_gdn_conv, cache_mla_latent, cache_mla_krope, state_ffn_conv, ada_w, ada_b, norm_g, ab_w_in, ab_w_out, hgrn_lb_logits, hgrn_norm_g, lru_conv_w, lru_conv_b, lru_w_a, lru_b_a, lru_w_x, lru_b_x, lru_lambda, cd_w_in, cd_w_out, gdn_conv_w, gdn_a_log, gdn_dt_bias, gdn_norm_g, mla_q_norm_g, mla_w_qb, mla_kv_norm_g, mla_w_kvb, ffn_w_up, ffn_conv_w, ffn_w_down):
    bp, bs = x_prompt.shape[0], x_sample.shape[0]
    W = _prep_weights(norm_g, ab_w_in, ab_w_out, hgrn_lb_logits, hgrn_norm_g, lru_conv_w, lru_conv_b, lru_w_a, lru_b_a,
                      lru_w_x, lru_b_x, lru_lambda, cd_w_in, cd_w_out, gdn_conv_w, gdn_a_log, gdn_dt_bias,
                      gdn_norm_g, mla_q_norm_g, mla_w_qb, mla_kv_norm_g, mla_w_kvb, ffn_w_up, ffn_conv_w,
                      ffn_w_down)
    rows = bp + bs
    rows_pad = -(-rows // 8) * 8
    c_all = jnp.concatenate([c_prompt, c_sample, jnp.zeros((rows_pad - rows, D_MODEL), F32)], axis=0)
    mods = _ada_call(c_all, ada_w, ada_b)
    dt_ = x_prompt.dtype
    y_prompt, p_states = _run_group(
        x_prompt, mods[:, :bp],
        jnp.zeros((N_AB, bp, HEADS, HEAD_DIM, HEAD_DIM), F32),
        jnp.zeros((N_AB, bp, HALF), F32),
        jnp.zeros((N_AB, bp, LRU_CONV - 1, HALF), dt_),
        jnp.zeros((N_CD, bp, HEADS, HEAD_DIM, HEAD_DIM), F32),
        jnp.zeros((N_CD, bp, GD_CONV - 1, 3 * HALF), dt_),
        jnp.zeros((N_CD, bp, 0, MLA_KV_RANK), dt_),
        jnp.zeros((N_CD, bp, 0, MLA_ROPE), dt_),
        jnp.zeros((DEPTH, bp, FFN_CONV - 1, 2 * D_FF), dt_),
        W)
    y_sample, s_states = _run_group(
        x_sample, mods[:, bp:rows], state_hgrn, state_rglru, state_rglru_conv, state_gdn, state_gdn_conv,
        cache_mla_latent, cache_mla_krope, state_ffn_conv, W)
    return (y_prompt, y_sample) + tuple(p_states) + tuple(s_states)
```

```python
import functools
import math

import jax
import jax.numpy as jnp
from jax import lax
from jax.experimental import pallas as pl
from jax.experimental.pallas import tpu as pltpu

F32 = jnp.float32
BF16 = jnp.bfloat16

D_MODEL = 1024
DEPTH = 4
CHUNK = 64
HALF = D_MODEL // 2
N_AB = (DEPTH + 1) // 2
N_CD = DEPTH // 2
HEADS = 4
HEAD_DIM = HALF // HEADS
LRU_BLOCKS = 8
LRU_BLOCK = HALF // LRU_BLOCKS
LRU_CONV = 4
LRU_C = 8.0
LRU_GROUP = 8
GD_CONV = 4
MLA_NOPE = 128
MLA_ROPE = 64
MLA_QK = MLA_NOPE + MLA_ROPE
MLA_Q_RANK = 384
MLA_KV_RANK = 256
MLA_SCALE = (MLA_NOPE + MLA_ROPE) ** -0.5
ROPE_THETA = 10000.0
D_FF = 2816
FFN_CONV = 3
FF_TILE = 256
N_FF_TILES = D_FF // FF_TILE
FF_GROUP = 6
FF_SLOTS = 4
FF_AHEAD = 2
EPS = 1e-6
NEG_BIG = -1e30
SQRT_FLOOR = 1e-12
CD_COLS = 2816
SMALL_KR = 0
SMALL_GB = 64
SMALL_GA = 68

VMEM_LIMIT_BYTES = 56 * 1024 * 1024
ATTN_KV_BLOCK_BYTES = 8 * 1024 * 1024


def _cparams(*sem):
    return pltpu.CompilerParams(dimension_semantics=sem, vmem_limit_bytes=VMEM_LIMIT_BYTES)


def _dot(a, b):
    return jnp.dot(a, b, preferred_element_type=F32)


def _dot_nt(a, b):
    return lax.dot_general(a, b, (((1,), (1,)), ((), ())), preferred_element_type=F32)


def _dot_tn(a, b):
    return lax.dot_general(a, b, (((0,), (0,)), ((), ())), preferred_element_type=F32)


def _rms(x, g):
    return x * lax.rsqrt(jnp.mean(x * x, axis=-1, keepdims=True) + EPS) * g


def _silu(x):
    return x * jax.nn.sigmoid(x)


def _softplus(x):
    return jnp.maximum(x, 0.0) + jnp.log1p(jnp.exp(-jnp.abs(x)))


def _gelu_tanh(x):
    return 0.5 * x * (1.0 + jnp.tanh(math.sqrt(2.0 / math.pi) * (x + 0.044715 * (x * x * x))))


def _split3(x):
    x1 = x.astype(BF16)
    r1 = x - x1.astype(F32)
    x2 = r1.astype(BF16)
    x3 = (r1 - x2.astype(F32)).astype(BF16)
    return x1, x2, x3


def _log2(n):
    assert n & (n - 1) == 0
    return n.bit_length() - 1


def _cumsum_rows(x):
    n, c = x.shape
    group = min(n, 8)
    x3 = x.reshape(n // group, group, c)
    pos = lax.broadcasted_iota(jnp.int32, x3.shape, 1)
    s = 1
    while s < group:
        x3 = x3 + jnp.where(pos >= s, pltpu.roll(x3, s, 1), 0.0)
        s *= 2
    if n == group:
        return x3.reshape(n, c)
    groups = [x3[0]]
    for gi in range(1, n // group):
        groups.append(x3[gi] + groups[-1][group - 1:group, :])
    return jnp.concatenate(groups, axis=0)


def _block_row_bcast(b, row, h):
    L, n = b.shape
    blk = 2 * h
    if blk >= 8:
        b3 = b.reshape(L // blk, blk, n)
        return jnp.broadcast_to(b3[:, h - 1:h, :], (L // blk, blk, n)).reshape(L, n)
    group = min(L, 8)
    x0 = jnp.where((row & (blk - 1)) == h - 1, b, 0.0).reshape(L // group, group, n)
    out = x0
    for j in range(1, h + 1):
        out = out + pltpu.roll(x0, j, 1)
    for j in range(1, h):
        out = out + pltpu.roll(x0, group - j, 1)
    return out.reshape(L, n)


def _ada_kernel(c_ref, w_ref, b_ref, o_ref):
    c = _silu(c_ref[...]).astype(BF16)
    o_ref[0] = _dot(c, w_ref[0].astype(BF16)) + b_ref[0]


def _ada_call(c_all, ada_w, ada_b):
    rows = c_all.shape[0]
    tn = 2048
    return pl.pallas_call(
        _ada_kernel,
        out_shape=jax.ShapeDtypeStruct((DEPTH, rows, 6 * D_MODEL), F32),
        grid=(DEPTH, 6 * D_MODEL // tn),
        in_specs=[
            pl.BlockSpec((rows, D_MODEL), lambda l, j: (0, 0)),
            pl.BlockSpec((1, D_MODEL, tn), lambda l, j: (l, 0, j)),
            pl.BlockSpec((1, 1, tn), lambda l, j: (l, 0, j)),
        ],
        out_specs=pl.BlockSpec((1, rows, tn), lambda l, j: (l, 0, j)),
        compiler_params=_cparams("arbitrary", "arbitrary"),
        name="ada_mod",
    )(c_all, ada_w, ada_b.reshape(DEPTH, 1, 6 * D_MODEL))


def _nmm_kernel(x_ref, g_ref, sc_ref, sh_ref, w_ref, o_ref, *maybe_t_ref):
    nb, tt, d = x_ref.shape
    h = _rms(x_ref[...], g_ref[...]) * (1.0 + sc_ref[...]) + sh_ref[...]
    y = _dot(h.reshape(nb * tt, d).astype(BF16), w_ref[...])
    o_ref[...] = y.reshape(nb, tt, y.shape[-1])
    if maybe_t_ref:
        (t_ref,) = maybe_t_ref
        yt = y[:, y.shape[-1] - 128:].T
        t_ref[0] = yt[SMALL_GB:SMALL_GB + 2 * HEADS, :]


def _tiles(B, T):
    if T >= 512:
        return 1, 512
    assert B * T <= 512
    return B, T


def _nmm_call(x, g, scale, shift, w, gates_t=False):
    B, T, D = x.shape
    N = w.shape[1]
    nb, tt = _tiles(B, T)
    out_shape = jax.ShapeDtypeStruct((B, T, N), F32)
    out_specs = pl.BlockSpec((nb, tt, N), lambda b, i: (b, i, 0))
    if gates_t:
        assert nb == 1 and tt % 128 == 0
        out_shape = (out_shape, jax.ShapeDtypeStruct((B, 2 * HEADS, T), F32))
        out_specs = (out_specs, pl.BlockSpec((1, 2 * HEADS, tt), lambda b, i: (b, 0, i)))
    return pl.pallas_call(
        _nmm_kernel,
        out_shape=out_shape,
        grid=(B // nb, T // tt),
        in_specs=[
            pl.BlockSpec((nb, tt, D), lambda b, i: (b, i, 0)),
            pl.BlockSpec((1, D), lambda b, i: (0, 0)),
            pl.BlockSpec((nb, 1, D), lambda b, i: (b, 0, 0)),
            pl.BlockSpec((nb, 1, D), lambda b, i: (b, 0, 0)),
            pl.BlockSpec((D, N), lambda b, i: (0, 0)),
        ],
        out_specs=out_specs,
        compiler_params=_cparams("arbitrary", "arbitrary"),
        name="norm_mod_proj",
    )(x, g.reshape(1, D), scale[:, None, :], shift[:, None, :], w)


def _ffn_kernel(x_ref, oa_ref, ob_ref, wo_ref, go_ref, gateo_ref, g1_ref, sc_ref, sh_ref, gate_ref, g2_ref,
                buf0_ref, wu_ref, wd_ref, cw_ref, o_ref, st_ref, carry_ref, ubuf_ref):
    nb, tt, d = x_ref.shape
    i = pl.program_id(1)

    @pl.when(i == 0)
    def _():
        for c in range(2 * N_FF_TILES):
            carry_ref[c] = buf0_ref[:, :, c * FF_TILE:(c + 1) * FF_TILE]

    oa = oa_ref[...].reshape(nb * tt, HALF)
    ob = ob_ref[...].reshape(nb * tt, HALF)
    mix = _dot(oa, wo_ref[0:HALF, :]) + _dot(ob, wo_ref[HALF:2 * HALF, :])
    x = x_ref[...] + gateo_ref[...] * _rms(mix, go_ref[...]).reshape(nb, tt, d)
    h = (_rms(x, g1_ref[...]) * (1.0 + sc_ref[...]) + sh_ref[...]).reshape(nb * tt, d).astype(BF16)

    def conv(u, slot, c):
        ubuf_ref[slot, :, 8:8 + tt, :] = u
        ubuf_ref[slot, :, 6:8, :] = carry_ref[c]
        cw = cw_ref[:, c * FF_TILE:(c + 1) * FF_TILE]
        y = (cw[0:1, :] * ubuf_ref[slot, :, 6:6 + tt, :] + cw[1:2, :] * ubuf_ref[slot, :, 7:7 + tt, :]
             + cw[2:3, :] * u)
        tail = ubuf_ref[slot, :, 6 + tt:8 + tt, :]
        carry_ref[c] = tail
        st_ref[:, :, c * FF_TILE:(c + 1) * FF_TILE] = tail
        return y

    def up_proj(c):
        wg = wu_ref[:, c * FF_TILE:(c + 1) * FF_TILE]
        wv = wu_ref[:, D_FF + c * FF_TILE:D_FF + (c + 1) * FF_TILE]
        return _dot(h, wg).reshape(nb, tt, FF_TILE), _dot(h, wv).reshape(nb, tt, FF_TILE)

    acc = None
    ahead = [up_proj(c) for c in range(min(FF_AHEAD, N_FF_TILES))]
    for g0 in range(0, N_FF_TILES, FF_GROUP):
        acts = []
        for c in range(g0, min(g0 + FF_GROUP, N_FF_TILES)):
            ug, uv = ahead.pop(0)
            if c + FF_AHEAD < N_FF_TILES:
                ahead.append(up_proj(c + FF_AHEAD))
            slot = 2 * (c % FF_SLOTS)
            yg = conv(ug, slot, c)
            yv = conv(uv, slot + 1, N_FF_TILES + c)
            acts.append((_silu(yg) * yv).reshape(nb * tt, FF_TILE).astype(BF16))
        a = jnp.concatenate(acts, axis=-1) if len(acts) > 1 else acts[0]
        part = _dot(a, wd_ref[g0 * FF_TILE:g0 * FF_TILE + a.shape[-1], :])
        acc = part if acc is None else acc + part
    y = _rms(acc, g2_ref[...]).reshape(nb, tt, d)
    o_ref[...] = x + gate_ref[...] * y


def _ffn_call(x, oa, ob, wo, go, gateo, g1, scale, shift, gate, g2, buf0, wu, wd, cw):
    B, T, D = x.shape
    nb, tt = _tiles(B, T)
    assert T >= FFN_CONV - 1
    return pl.pallas_call(
        _ffn_kernel,
        out_shape=(jax.ShapeDtypeStruct((B, T, D), F32),
                   jax.ShapeDtypeStruct((B, FFN_CONV - 1, 2 * D_FF), F32)),
        grid=(B // nb, T // tt),
        in_specs=[
            pl.BlockSpec((nb, tt, D), lambda b, i: (b, i, 0)),
            pl.BlockSpec((nb, tt, HALF), lambda b, i: (b, i, 0)),
            pl.BlockSpec((nb, tt, HALF), lambda b, i: (b, i, 0)),
            pl.BlockSpec((D, D), lambda b, i: (0, 0), pipeline_mode=pl.Buffered(1)),
            pl.BlockSpec((1, D), lambda b, i: (0, 0)),
            pl.BlockSpec((nb, 1, D), lambda b, i: (b, 0, 0)),
            pl.BlockSpec((1, D), lambda b, i: (0, 0)),
            pl.BlockSpec((nb, 1, D), lambda b, i: (b, 0, 0)),
            pl.BlockSpec((nb, 1, D), lambda b, i: (b, 0, 0)),
            pl.BlockSpec((nb, 1, D), lambda b, i: (b, 0, 0)),
            pl.BlockSpec((1, D), lambda b, i: (0, 0)),
            pl.BlockSpec((nb, FFN_CONV - 1, 2 * D_FF), lambda b, i: (b, 0, 0)),
            pl.BlockSpec((D, 2 * D_FF), lambda b, i: (0, 0), pipeline_mode=pl.Buffered(1)),
            pl.BlockSpec((D_FF, D), lambda b, i: (0, 0), pipeline_mode=pl.Buffered(1)),
            pl.BlockSpec((FFN_CONV, 2 * D_FF), lambda b, i: (0, 0)),
        ],
        out_specs=(pl.BlockSpec((nb, tt, D), lambda b, i: (b, i, 0)),
                   pl.BlockSpec((nb, FFN_CONV - 1, 2 * D_FF), lambda b, i: (b, 0, 0))),
        scratch_shapes=[
            pltpu.VMEM((2 * N_FF_TILES, nb, FFN_CONV - 1, FF_TILE), F32),
            pltpu.VMEM((2 * FF_SLOTS, nb, 8 + tt, FF_TILE), F32),
        ],
        compiler_params=_cparams("arbitrary", "arbitrary"),
        name="conv_ffn",
    )(x, oa, ob, wo, go.reshape(1, D), gateo[:, None, :], g1.reshape(1, D), scale[:, None, :], shift[:, None, :],
      gate[:, None, :], g2.reshape(1, D), buf0, wu, wd, cw)


def _hgrn_kernel(hq_ref, hf_ref, hi_ref, hz_ref, lb_ref, ng_ref, s0_ref, o_ref, st_ref):
    L = hq_ref.shape[1]
    i = pl.program_id(1)

    @pl.when(i == 0)
    def _():
        st_ref[...] = s0_ref[...]

    row = lax.broadcasted_iota(jnp.int32, (L, HEAD_DIM), 0)
    r2 = lax.broadcasted_iota(jnp.int32, (L, L), 0)
    c2 = lax.broadcasted_iota(jnp.int32, (L, L), 1)
    heads = range(HEADS)
    sls = [slice(hd * HEAD_DIM, (hd + 1) * HEAD_DIM) for hd in heads]
    q, k, v, b = ([None] * HEADS for _ in range(4))
    for hd in heads:
        z = hf_ref[0, :, sls[hd]]
        lb = lb_ref[:, sls[hd]]
        en = jnp.exp(-jnp.abs(z))
        inv = 1.0 / (1.0 + en)
        sig_pos = jnp.where(z >= 0.0, inv, en * inv)
        sig_neg = jnp.where(z >= 0.0, en * inv, inv)
        g = jnp.log(lb + (1.0 - lb) * sig_pos)
        k[hd] = (1.0 - lb) * sig_neg
        q[hd] = _silu(hq_ref[0, :, sls[hd]])
        v[hd] = hi_ref[0, :, sls[hd]].astype(BF16)
        b[hd] = _cumsum_rows(g)
    att = [jnp.where(r2 == c2, _dot_nt(q[hd].astype(BF16), k[hd].astype(BF16)), 0.0) for hd in heads]
    h = L // 2
    while h >= 1:
        sh = _log2(2 * h)
        keep = jnp.logical_and((r2 >> sh) == (c2 >> sh),
                               jnp.logical_and((r2 & (2 * h - 1)) >= h, (c2 & (2 * h - 1)) < h))
        for hd in heads:
            r = _block_row_bcast(b[hd], row, h)
            e = jnp.exp2(jnp.abs(b[hd] - r) * (-math.log2(math.e)))
            qt = (q[hd] * e).astype(BF16)
            kt = (k[hd] * e).astype(BF16)
            att[hd] = jnp.where(keep, _dot_nt(qt, kt), att[hd])
        h //= 2
    for hd in heads:
        st = st_ref[0, hd]
        o = _dot(att[hd].astype(BF16), v[hd]) + _dot_nt((q[hd] * jnp.exp(b[hd])).astype(BF16), st.astype(BF16))
        b_last = b[hd][L - 1:L, :]
        kd = (k[hd] * jnp.exp(b_last - b[hd])).astype(BF16)
        st_ref[0, hd] = jnp.exp(b_last) * st + _dot_tn(v[hd], kd)
        o = _rms(o, ng_ref[...]) * _silu(hz_ref[0, :, sls[hd]])
        o_ref[0, :, sls[hd]] = o.astype(BF16)


def _hgrn_call(proj, lb, ng, s0t):
    B, T, _ = proj.shape
    L = 128 if T % 128 == 0 else T
    assert T % L == 0 and L & (L - 1) == 0 and L >= 8
    col = lambda c: pl.BlockSpec((1, L, HALF), lambda b, i, c=c: (b, i, c))
    return pl.pallas_call(
        _hgrn_kernel,
        out_shape=(jax.ShapeDtypeStruct((B, T, HALF), BF16),
                   jax.ShapeDtypeStruct((B, HEADS, HEAD_DIM, HEAD_DIM), F32)),
        grid=(B, T // L),
        in_specs=[col(0), col(1), col(2), col(3),
                  pl.BlockSpec((1, HALF), lambda b, i: (0, 0)),
                  pl.BlockSpec((1, HEAD_DIM), lambda b, i: (0, 0)),
                  pl.BlockSpec((1, HEADS, HEAD_DIM, HEAD_DIM), lambda b, i: (b, 0, 0, 0))],
        out_specs=(pl.BlockSpec((1, L, HALF), lambda b, i: (b, i, 0)),
                   pl.BlockSpec((1, HEADS, HEAD_DIM, HEAD_DIM), lambda b, i: (b, 0, 0, 0))),
        compiler_params=_cparams("arbitrary", "arbitrary"),
        name="hgrn2",
    )(proj, proj, proj, proj, lb.reshape(1, HALF), ng.reshape(1, HEAD_DIM), s0t)


def _lru_kernel(lx_ref, ly_ref, buf0_ref, h0_ref, cw_ref, cb_ref, wa_ref, wx_ref, ba_ref, bx_ref,
                lam_ref, o_ref, hl_ref, bufo_ref, xp_ref):
    tt = lx_ref.shape[1]
    i = pl.program_id(1)
    npad = LRU_CONV - 1

    @pl.when(i == 0)
    def _():
        xp_ref[8 - npad:8, :] = buf0_ref[0]
        hl_ref[0] = h0_ref[0]

    x = lx_ref[0]
    xp_ref[8:8 + tt, :] = x
    xc = cb_ref[...] + cw_ref[npad:npad + 1, :] * x
    for tap in range(npad):
        xc = xc + cw_ref[tap:tap + 1, :] * xp_ref[8 - npad + tap:8 - npad + tap + tt, :]
    tail = xp_ref[8 + tt - npad:8 + tt, :]
    xp_ref[8 - npad:8, :] = tail
    bufo_ref[0] = tail

    xb = xc.astype(BF16)
    half = HALF // 2
    rpre = jnp.concatenate([_dot(xb[:, 0:half], wa_ref[0]), _dot(xb[:, half:HALF], wa_ref[1])], axis=-1)
    ipre = jnp.concatenate([_dot(xb[:, 0:half], wx_ref[0]), _dot(xb[:, half:HALF], wx_ref[1])], axis=-1)
    r = jax.nn.sigmoid(rpre + ba_ref[...])
    ig = jax.nn.sigmoid(ipre + bx_ref[...])
    log_a = -LRU_C * r * _softplus(-lam_ref[...])
    a = jnp.exp(log_a)
    u = jnp.sqrt(jnp.maximum(-jnp.tanh(log_a) * (1.0 + a * a), SQRT_FLOOR)) * ig * xc

    a = a.reshape(tt // LRU_GROUP, LRU_GROUP, HALF)
    u = u.reshape(tt // LRU_GROUP, LRU_GROUP, HALF)
    pos = lax.broadcasted_iota(jnp.int32, a.shape, 1)
    s = 1
    while s < LRU_GROUP:
        keep = pos >= s
        a_sh = jnp.where(keep, pltpu.roll(a, s, 1), 1.0)
        u_sh = jnp.where(keep, pltpu.roll(u, s, 1), 0.0)
        u = a * u_sh + u
        a = a * a_sh
        s *= 2
    carry = hl_ref[0]
    groups = []
    for gi in range(tt // LRU_GROUP):
        hg = u[gi] + a[gi] * carry
        carry = hg[LRU_GROUP - 1:LRU_GROUP, :]
        groups.append(hg)
    hseq = jnp.concatenate(groups, axis=0) if len(groups) > 1 else groups[0]
    hl_ref[0] = carry
    o_ref[0] = (hseq * _gelu_tanh(ly_ref[0])).astype(BF16)


def _lru_call(proj, buf0, h0, cw, cb, wa_bd, wx_bd, ba, bx, lam):
    B, T, _ = proj.shape
    tt = 256 if T % 256 == 0 else T
    assert T % tt == 0 and T >= LRU_CONV - 1 and tt % 8 == 0
    vec = pl.BlockSpec((1, HALF), lambda b, i: (0, 0))
    wspec = pl.BlockSpec((2, HALF // 2, HALF // 2), lambda b, i: (0, 0, 0))
    return pl.pallas_call(
        _lru_kernel,
        out_shape=(jax.ShapeDtypeStruct((B, T, HALF), BF16),
                   jax.ShapeDtypeStruct((B, 1, HALF), F32),
                   jax.ShapeDtypeStruct((B, LRU_CONV - 1, HALF), F32)),
        grid=(B, T // tt),
        in_specs=[pl.BlockSpec((1, tt, HALF), lambda b, i: (b, i, 4)),
                  pl.BlockSpec((1, tt, HALF), lambda b, i: (b, i, 5)),
                  pl.BlockSpec((1, LRU_CONV - 1, HALF), lambda b, i: (b, 0, 0)),
                  pl.BlockSpec((1, 1, HALF), lambda b, i: (b, 0, 0)),
                  pl.BlockSpec((LRU_CONV, HALF), lambda b, i: (0, 0)),
                  vec, wspec, wspec, vec, vec, vec],
        out_specs=(pl.BlockSpec((1, tt, HALF), lambda b, i: (b, i, 0)),
                   pl.BlockSpec((1, 1, HALF), lambda b, i: (b, 0, 0)),
                   pl.BlockSpec((1, LRU_CONV - 1, HALF), lambda b, i: (b, 0, 0))),
        scratch_shapes=[pltpu.VMEM((8 + tt, HALF), F32)],
        compiler_params=_cparams("arbitrary", "arbitrary"),
        name="rglru",
    )(proj, proj, buf0, h0[:, None, :], cw, cb.reshape(1, HALF), wa_bd, wx_bd,
      ba.reshape(1, HALF), bx.reshape(1, HALF), lam.reshape(1, HALF))


def _gdn_kernel(qkv_ref, gz_ref, sm_ref, smt_ref, buf0_ref, cw_ref, pcol_ref, prow_ref, ng_ref, s0_ref,
                o_ref, st_ref, bufo_ref, xp_ref, *, L):
    tt = qkv_ref.shape[1]
    nc = tt // L
    sh = _log2(L)
    i = pl.program_id(1)
    npad = GD_CONV - 1

    @pl.when(i == 0)
    def _():
        xp_ref[8 - npad:8, :] = buf0_ref[0]
        st_ref[...] = s0_ref[...]

    x = qkv_ref[0]
    xp_ref[8:8 + tt, :] = x
    xc = cw_ref[npad:npad + 1, :] * x
    for tap in range(npad):
        xc = xc + cw_ref[tap:tap + 1, :] * xp_ref[8 - npad + tap:8 - npad + tap + tt, :]
    tail = xp_ref[8 + tt - npad:8 + tt, :]
    xp_ref[8 - npad:8, :] = tail
    bufo_ref[0] = tail
    xc = _silu(xc)

    r2 = lax.broadcasted_iota(jnp.int32, (tt, tt), 0)
    c2 = lax.broadcasted_iota(jnp.int32, (tt, tt), 1)
    same = (r2 >> sh) == (c2 >> sh)
    incl = jnp.logical_and(same, c2 <= r2)
    strict = jnp.logical_and(same, c2 < r2)
    tri_lo = jnp.where(incl, 1.0, 0.0).astype(BF16)
    tri_up = jnp.where(jnp.logical_and(same, r2 <= c2), 1.0, 0.0).astype(BF16)
    eye = jnp.where(r2 == c2, 1.0, 0.0)

    sm = sm_ref[0]
    beta_cols = jax.nn.sigmoid(sm)
    la_cols = -jnp.exp(pcol_ref[0:1, :]) * _softplus(sm + pcol_ref[1:2, :])
    c1, c2_, c3 = _split3(la_cols)
    g_cols = _dot(tri_lo, c1) + _dot(tri_lo, c2_) + _dot(tri_lo, c3)
    la_rows = -jnp.exp(prow_ref[:, 0:1]) * _softplus(smt_ref[0] + prow_ref[:, 1:2])
    w1, w2, w3 = _split3(la_rows)
    g_rows = _dot(w1, tri_up) + _dot(w2, tri_up) + _dot(w3, tri_up)

    heads = range(HEADS)
    q, k, v, beta, gcol, kb, dec, m = ([None] * HEADS for _ in range(8))
    for hd in heads:
        qh = xc[:, hd * HEAD_DIM:(hd + 1) * HEAD_DIM]
        kh = xc[:, HALF + hd * HEAD_DIM:HALF + (hd + 1) * HEAD_DIM]
        v[hd] = xc[:, 2 * HALF + hd * HEAD_DIM:2 * HALF + (hd + 1) * HEAD_DIM]
        q[hd] = qh * lax.rsqrt(jnp.sum(qh * qh, axis=-1, keepdims=True) + EPS) * (HEAD_DIM ** -0.5)
        k[hd] = kh * lax.rsqrt(jnp.sum(kh * kh, axis=-1, keepdims=True) + EPS)
        beta[hd] = beta_cols[:, SMALL_GB + hd:SMALL_GB + hd + 1]
        gcol[hd] = g_cols[:, SMALL_GA + hd:SMALL_GA + hd + 1]
        grow = g_rows[HEADS + hd:HEADS + hd + 1, :]
        kb[hd] = k[hd].astype(BF16)
        dec[hd] = jnp.exp(jnp.where(incl, gcol[hd] - grow, NEG_BIG))
        m[hd] = beta[hd] * _dot_nt(kb[hd], kb[hd]) * jnp.where(strict, dec[hd], 0.0)

    pair = (r2 >> 1) == (c2 >> 1)
    tinv = [eye - jnp.where(pair, m[hd], 0.0) for hd in heads]
    s = 2
    while s < L:
        ssh = _log2(s)
        lower_left = jnp.logical_and((r2 >> (ssh + 1)) == (c2 >> (ssh + 1)), (r2 >> ssh) != (c2 >> ssh))
        tb = [tinv[hd].astype(BF16) for hd in heads]
        tc = [_dot(tb[hd], jnp.where(lower_left, m[hd], 0.0).astype(BF16)).astype(BF16) for hd in heads]
        tinv = [tinv[hd] - _dot(tc[hd], tb[hd]) for hd in heads]
        s *= 2

    eg = [jnp.exp(gcol[hd]) for hd in heads]
    sol = [_dot(tinv[hd].astype(BF16),
                jnp.concatenate([beta[hd] * v[hd], (beta[hd] * eg[hd]) * k[hd]], axis=-1).astype(BF16))
           for hd in heads]
    u_v = [sol[hd][:, 0:HEAD_DIM] for hd in heads]
    w_k = [sol[hd][:, HEAD_DIM:2 * HEAD_DIM].astype(BF16) for hd in heads]
    qb = [q[hd].astype(BF16) for hd in heads]
    qk = [(_dot_nt(qb[hd], kb[hd]) * dec[hd]).astype(BF16) for hd in heads]

    a_c, p_c, n_c = ([[None] * nc for _ in heads] for _ in range(3))
    for c in range(nc):
        rs = slice(c * L, (c + 1) * L)
        for hd in heads:
            g_last = gcol[hd][(c + 1) * L - 1:(c + 1) * L, :]
            kd = (k[hd][rs] * jnp.exp(g_last - gcol[hd][rs])).astype(BF16)
            a_c[hd][c] = jnp.exp(g_last)
            p_c[hd][c] = (-_dot_tn(kd, w_k[hd][rs])).astype(BF16)
            n_c[hd][c] = _dot_tn(kd, u_v[hd][rs].astype(BF16))
    S = [[st_ref[0, hd]] for hd in heads]
    for c in range(nc):
        for hd in heads:
            s_cur = S[hd][c]
            S[hd].append(a_c[hd][c] * s_cur + _dot(p_c[hd][c], s_cur.astype(BF16)) + n_c[hd][c])
    us = [[] for _ in heads]
    inters = [[] for _ in heads]
    for c in range(nc):
        rs = slice(c * L, (c + 1) * L)
        for hd in heads:
            Sb = S[hd][c].astype(BF16)
            us[hd].append(u_v[hd][rs] - _dot(w_k[hd][rs], Sb))
            inters[hd].append(eg[hd][rs] * _dot(qb[hd][rs], Sb))
    for hd in heads:
        sl = slice(hd * HEAD_DIM, (hd + 1) * HEAD_DIM)
        st_ref[0, hd] = S[hd][nc]
        u_all = jnp.concatenate(us[hd], axis=0) if nc > 1 else us[hd][0]
        inter = jnp.concatenate(inters[hd], axis=0) if nc > 1 else inters[hd][0]
        o = _dot(qk[hd], u_all.astype(BF16)) + inter
        o = _rms(o, ng_ref[...]) * _silu(gz_ref[0, :, sl])
        o_ref[0, :, sl] = o.astype(BF16)


def _gdn_call(proj, small_t, buf0, cw, pcol, prow, ng, s0):
    B, T, _ = proj.shape
    L = CHUNK if T % CHUNK == 0 else T
    tt = 256 if T % 256 == 0 else T
    assert T % tt == 0 and tt % L == 0 and T >= GD_CONV - 1 and L >= 2
    W = 3 * HALF
    return pl.pallas_call(
        functools.partial(_gdn_kernel, L=L),
        out_shape=(jax.ShapeDtypeStruct((B, T, HALF), BF16),
                   jax.ShapeDtypeStruct((B, HEADS, HEAD_DIM, HEAD_DIM), F32),
                   jax.ShapeDtypeStruct((B, GD_CONV - 1, W), F32)),
        grid=(B, T // tt),
        in_specs=[pl.BlockSpec((1, tt, W), lambda b, i: (b, i, 0)),
                  pl.BlockSpec((1, tt, HALF), lambda b, i: (b, i, 3)),
                  pl.BlockSpec((1, tt, 128), lambda b, i: (b, i, 21)),
                  pl.BlockSpec((1, 2 * HEADS, tt), lambda b, i: (b, 0, i)),
                  pl.BlockSpec((1, GD_CONV - 1, W), lambda b, i: (b, 0, 0)),
                  pl.BlockSpec((GD_CONV, W), lambda b, i: (0, 0)),
                  pl.BlockSpec((2, 128), lambda b, i: (0, 0)),
                  pl.BlockSpec((2 * HEADS, 2), lambda b, i: (0, 0)),
                  pl.BlockSpec((1, HEAD_DIM), lambda b, i: (0, 0)),
                  pl.BlockSpec((1, HEADS, HEAD_DIM, HEAD_DIM), lambda b, i: (b, 0, 0, 0))],
        out_specs=(pl.BlockSpec((1, tt, HALF), lambda b, i: (b, i, 0)),
                   pl.BlockSpec((1, HEADS, HEAD_DIM, HEAD_DIM), lambda b, i: (b, 0, 0, 0)),
                   pl.BlockSpec((1, GD_CONV - 1, W), lambda b, i: (b, 0, 0))),
        scratch_shapes=[pltpu.VMEM((8 + tt, W), F32)],
        compiler_params=_cparams("arbitrary", "arbitrary"),
        name="gated_deltanet",
    )(proj, proj, proj, small_t, buf0, cw, pcol, prow, ng.reshape(1, HEAD_DIM), s0)


def _rope64(x, cs, sn):
    half = MLA_ROPE // 2
    swapped = jnp.concatenate([x[:, half:], x[:, :half]], axis=-1)
    return x * cs + swapped * sn


def _expand_kv(c_kv, k_r, wkvb_ref, kf_ref, v_ref):
    kv = _dot(c_kv.astype(BF16), wkvb_ref[...])
    for hd in range(HEADS):
        base = hd * 2 * HEAD_DIM
        kf_ref[0, hd] = jnp.concatenate([kv[:, base:base + MLA_NOPE], k_r], axis=-1).astype(BF16)
        v = kv[:, base + MLA_NOPE:base + 2 * HEAD_DIM]
        v_ref[0, hd] = jnp.concatenate([v, jnp.ones_like(v)], axis=-1).astype(BF16)


def _mla_prep_kernel(qa_ref, kva_ref, sm_ref, cs_ref, sn_ref, qng_ref, wqb_ref, kvng_ref, wkvb_ref, *rest):
    q_ref, ckv_ref, kr_ref, kf_ref, v_ref = rest[-5:]
    ckv_ref = ckv_ref.at[0]
    kr_ref = kr_ref.at[0]
    cs = cs_ref[...]
    sn = sn_ref[...]
    qn = _rms(qa_ref[0], qng_ref[...]).astype(BF16)
    qh = _dot(qn, wqb_ref[...]) * (MLA_SCALE * math.log2(math.e))
    for hd in range(HEADS):
        nope = qh[:, hd * MLA_NOPE:(hd + 1) * MLA_NOPE]
        off = HEADS * MLA_NOPE + hd * MLA_ROPE
        rot = _rope64(qh[:, off:off + MLA_ROPE], cs, sn)
        q_ref[0, hd] = jnp.concatenate([nope, rot], axis=-1).astype(BF16)
    c_kv = _rms(kva_ref[0], kvng_ref[...])
    ckv_ref[0] = c_kv
    k_r = _rope64(sm_ref[0, :, SMALL_KR:SMALL_KR + MLA_ROPE], cs, sn)
    kr_ref[0] = k_r
    _expand_kv(c_kv, k_r, wkvb_ref, kf_ref, v_ref)


def _mla_prep_call(proj, cs, sn, qng, wqb, kvng, wkvb, j, state_bufs):
    B, T, _ = proj.shape
    tt = 512 if T % 512 == 0 else T
    c2 = lambda b, i: (0, 0)
    any_spec = pl.BlockSpec(memory_space=pl.ANY)
    n_fixed = 9
    extra = list(state_bufs)
    aliases = {n_fixed: 1, n_fixed + 1: 2}
    return pl.pallas_call(
        _mla_prep_kernel,
        out_shape=(jax.ShapeDtypeStruct((B, HEADS, T, MLA_QK), BF16),
                   jax.ShapeDtypeStruct((N_CD, B, T, MLA_KV_RANK), F32),
                   jax.ShapeDtypeStruct((N_CD, B, T, MLA_ROPE), F32),
                   jax.ShapeDtypeStruct((B, HEADS, T, MLA_QK), BF16),
                   jax.ShapeDtypeStruct((B, HEADS, T, 2 * HEAD_DIM), BF16)),
        grid=(B, T // tt),
        in_specs=[pl.BlockSpec((1, tt, MLA_Q_RANK), lambda b, i: (b, i, 6)),
                  pl.BlockSpec((1, tt, MLA_KV_RANK), lambda b, i: (b, i, 8)),
                  pl.BlockSpec((1, tt, 128), lambda b, i: (b, i, 21)),
                  pl.BlockSpec((tt, MLA_ROPE), lambda b, i: (i, 0)),
                  pl.BlockSpec((tt, MLA_ROPE), lambda b, i: (i, 0)),
                  pl.BlockSpec((1, MLA_Q_RANK), c2),
                  pl.BlockSpec((MLA_Q_RANK, HEADS * MLA_QK), c2),
                  pl.BlockSpec((1, MLA_KV_RANK), c2),
                  pl.BlockSpec((MLA_KV_RANK, HEADS * 2 * HEAD_DIM), c2)] + [any_spec] * len(extra),
        out_specs=(pl.BlockSpec((1, HEADS, tt, MLA_QK), lambda b, i: (b, 0, i, 0)),
                   pl.BlockSpec((1, 1, tt, MLA_KV_RANK), lambda b, i: (j, b, i, 0)),
                   pl.BlockSpec((1, 1, tt, MLA_ROPE), lambda b, i: (j, b, i, 0)),
                   pl.BlockSpec((1, HEADS, tt, MLA_QK), lambda b, i: (b, 0, i, 0)),
                   pl.BlockSpec((1, HEADS, tt, 2 * HEAD_DIM), lambda b, i: (b, 0, i, 0))),
        input_output_aliases=aliases,
        compiler_params=_cparams("arbitrary", "arbitrary"),
        name="mla_prep",
    )(proj, proj, proj, cs, sn, qng.reshape(1, MLA_Q_RANK), wqb, kvng.reshape(1, MLA_KV_RANK), wkvb, *extra)


def _attn_cached_kernel(q_ref, latp_ref, krp_ref, latn_ref, krn_ref, w_ref, o_ref):
    T = q_ref.shape[2]
    P = latp_ref.shape[1]
    sh = _log2(CHUNK)
    latp = latp_ref[0].astype(BF16)
    krp = krp_ref[0].astype(BF16)
    latn = latn_ref[0, 0].astype(BF16)
    krn = krn_ref[0, 0].astype(BF16)
    q_lat, q_rope = [], []
    for hd in range(HEADS):
        base = hd * 2 * HEAD_DIM
        q = q_ref[0, hd]
        q_lat.append(_dot_nt(q[:, 0:MLA_NOPE], w_ref[:, base:base + MLA_NOPE]).astype(BF16))
        q_rope.append(q[:, MLA_NOPE:MLA_QK])
    q_lat = jnp.concatenate(q_lat, axis=0)
    q_rope = jnp.concatenate(q_rope, axis=0)
    rows = HEADS * T
    t_of_row = jnp.concatenate([lax.broadcasted_iota(jnp.int32, (T, 1), 0)] * HEADS, axis=0)
    q_chunk = (P + t_of_row) >> sh
    vis_p = (lax.broadcasted_iota(jnp.int32, (rows, P), 1) >> sh) <= q_chunk
    vis_n = ((P + lax.broadcasted_iota(jnp.int32, (rows, T), 1)) >> sh) <= q_chunk
    s_p = jnp.where(vis_p, _dot_nt(q_lat, latp) + _dot_nt(q_rope, krp), NEG_BIG)
    s_n = jnp.where(vis_n, _dot_nt(q_lat, latn) + _dot_nt(q_rope, krn), NEG_BIG)
    m = jnp.maximum(jnp.max(s_p, axis=-1, keepdims=True), jnp.max(s_n, axis=-1, keepdims=True))
    p_p = jnp.exp2(s_p - m)
    p_n = jnp.exp2(s_n - m)
    denom = jnp.sum(p_p, axis=-1, keepdims=True) + jnp.sum(p_n, axis=-1, keepdims=True)
    o_lat = ((_dot(p_p.astype(BF16), latp) + _dot(p_n.astype(BF16), latn)) / denom).astype(BF16)
    for hd in range(HEADS):
        base = hd * 2 * HEAD_DIM
        o = _dot(o_lat[hd * T:(hd + 1) * T], w_ref[:, base + MLA_NOPE:base + 2 * HEAD_DIM])
        o_ref[0, :, hd * HEAD_DIM:(hd + 1) * HEAD_DIM] = o.astype(BF16)


def _attn_cached_call(q, lat_past, kr_past, lat_all, kr_all, j, wkvb):
    B, H, T, _ = q.shape
    P = lat_past.shape[1]
    return pl.pallas_call(
        _attn_cached_kernel,
        out_shape=jax.ShapeDtypeStruct((B, T, H * HEAD_DIM), BF16),
        grid=(B,),
        in_specs=[pl.BlockSpec((1, H, T, MLA_QK), lambda b: (b, 0, 0, 0)),
                  pl.BlockSpec((1, P, MLA_KV_RANK), lambda b: (b, 0, 0)),
                  pl.BlockSpec((1, P, MLA_ROPE), lambda b: (b, 0, 0)),
                  pl.BlockSpec((1, 1, T, MLA_KV_RANK), lambda b: (j, b, 0, 0)),
                  pl.BlockSpec((1, 1, T, MLA_ROPE), lambda b: (j, b, 0, 0)),
                  pl.BlockSpec((MLA_KV_RANK, H * 2 * HEAD_DIM), lambda b: (0, 0))],
        out_specs=pl.BlockSpec((1, T, H * HEAD_DIM), lambda b: (b, 0, 0)),
        compiler_params=_cparams("arbitrary"),
        name="mla_attention_cached",
    )(q, lat_past, kr_past, lat_all, kr_all, wkvb)


def _attn_kernel(q_ref, k_ref, v_ref, o_ref, m_ref, acc_ref, sa_ref, sb_ref, **static):
    for hh in range(q_ref.shape[1]):
        _attn_head(q_ref.at[0, hh], k_ref.at[0, hh], v_ref.at[0, hh], o_ref.at[0, :, hh * HEAD_DIM:(hh + 1) * HEAD_DIM],
                   m_ref, acc_ref, sa_ref, sb_ref, **static)


def _attn_head(q_ref, k_ref, v_ref, o_ref, m_ref, acc_ref, sa_ref, sb_ref, *, past_len, tq, tk, nk):
    i = pl.program_id(2)
    sh = _log2(CHUNK)
    lanes = HEAD_DIM
    m_ref[...] = jnp.full(m_ref.shape, NEG_BIG, F32)
    acc_ref[...] = jnp.zeros(acc_ref.shape, F32)
    q = q_ref[...]
    q_lo = past_len + i * tq
    n_full = jnp.minimum(nk, (((q_lo >> sh) + 1) * CHUNK) // tk)
    n_need = jnp.minimum(nk, ((((q_lo + tq - 1) >> sh) + 1) * CHUNK + tk - 1) // tk)

    def scores(j):
        return _dot_nt(q, k_ref[pl.ds(pl.multiple_of(j * tk, tk), tk), :])

    def masked_scores(j):
        k_lo = j * tk
        qc = (q_lo + lax.broadcasted_iota(jnp.int32, (tq, tk), 0)) >> sh
        kc = (k_lo + lax.broadcasted_iota(jnp.int32, (tq, tk), 1)) >> sh
        return jnp.where(kc <= qc, scores(j), NEG_BIG)

    def update(j, s_ref):
        k_lo = pl.multiple_of(j * tk, tk)
        m_prev = m_ref[...]
        m_new = jnp.maximum(m_prev, jnp.max(s_ref[...], axis=-1, keepdims=True))
        alpha = jnp.exp2(m_prev - m_new)
        if tk % lanes == 0:
            p = jnp.exp2(s_ref[...] - jnp.tile(m_new, (1, tk // lanes)))
        else:
            p = jnp.exp2(s_ref[...] - m_new[:, 0:1])
        pv = _dot(p.astype(BF16), v_ref[pl.ds(k_lo, tk), :])
        acc_ref[...] = jnp.tile(alpha, (1, 2)) * acc_ref[...] + pv
        m_ref[...] = m_new

    def body_pair(g, carry):
        sb_ref[...] = scores(2 * g + 1)
        update(2 * g, sa_ref)
        sa_ref[...] = scores(2 * g + 2)
        update(2 * g + 1, sb_ref)
        return carry

    sa_ref[...] = scores(0)
    n_pairs = jnp.maximum(n_full - 1, 0) // 2
    lax.fori_loop(0, n_pairs, body_pair, 0)

    left = n_full - 2 * n_pairs
    has_masked = n_need > n_full
    j_masked = jnp.minimum(n_full, nk - 1)

    @pl.when(left == 1)
    def _():
        sb_ref[...] = masked_scores(j_masked)
        update(n_full - 1, sa_ref)

    @pl.when(jnp.logical_and(left == 1, has_masked))
    def _():
        update(n_full, sb_ref)

    @pl.when(left == 2)
    def _():
        sb_ref[...] = scores(n_full - 1)
        update(n_full - 2, sa_ref)
        sa_ref[...] = masked_scores(j_masked)
        update(n_full - 1, sb_ref)

    @pl.when(jnp.logical_and(left == 2, has_masked))
    def _():
        update(n_full, sa_ref)

    def body_masked(j, carry):
        sa_ref[...] = masked_scores(j)
        update(j, sa_ref)
        return carry

    lax.fori_loop(jnp.where(n_full > 0, n_full + 1, 0), n_need, body_masked, 0)
    o_ref[...] = (acc_ref[:, 0:HEAD_DIM] / acc_ref[:, HEAD_DIM:2 * HEAD_DIM]).astype(BF16)


def _attn_call(q, kf, v, past_len):
    B, H, Tq, _ = q.shape
    Tk = kf.shape[2]
    tq = 1024 if Tq % 1024 == 0 else Tq
    tk = 1024 if Tk % 1024 == 0 else Tk
    nk = Tk // tk
    kv_bytes = Tk * (MLA_QK + 2 * HEAD_DIM) * 2
    hb = H if H * kv_bytes <= ATTN_KV_BLOCK_BYTES else 1
    return pl.pallas_call(
        functools.partial(_attn_kernel, past_len=past_len, tq=tq, tk=tk, nk=nk),
        out_shape=jax.ShapeDtypeStruct((B, Tq, H * HEAD_DIM), BF16),
        grid=(B, H // hb, Tq // tq),
        in_specs=[pl.BlockSpec((1, hb, tq, MLA_QK), lambda b, h, i: (b, h, i, 0)),
                  pl.BlockSpec((1, hb, Tk, MLA_QK), lambda b, h, i: (b, h, 0, 0)),
                  pl.BlockSpec((1, hb, Tk, 2 * HEAD_DIM), lambda b, h, i: (b, h, 0, 0))],
        out_specs=pl.BlockSpec((1, tq, hb * HEAD_DIM), lambda b, h, i: (b, i, h)),
        scratch_shapes=[pltpu.VMEM((tq, HEAD_DIM), F32), pltpu.VMEM((tq, 2 * HEAD_DIM), F32),
                        pltpu.VMEM((tq, tk), F32), pltpu.VMEM((tq, tk), F32)],
        compiler_params=_cparams("arbitrary", "arbitrary", "arbitrary"),
        name="mla_attention",
    )(q, kf, v)


def _block_diag_pairs(w):
    per = (HALF // 2) // LRU_BLOCK
    w4 = w.reshape(2, per, LRU_BLOCK, LRU_BLOCK)
    eye = jnp.eye(per, dtype=w.dtype)
    out = w4[:, :, :, None, :] * eye[None, :, None, :, None]
    return out.reshape(2, HALF // 2, HALF // 2).astype(BF16)


def _prep_cd_w_in(w):
    o = 3 * HALF
    qkv, gz = w[:, :o], w[:, o:o + HALF]
    o += HALF
    gb, ga = w[:, o:o + HEADS], w[:, o + HEADS:o + 2 * HEADS]
    o += 2 * HEADS
    qa, kva = w[:, o:o + MLA_Q_RANK], w[:, o + MLA_Q_RANK:o + MLA_Q_RANK + MLA_KV_RANK]
    o += MLA_Q_RANK + MLA_KV_RANK
    kr = w[:, o:o + MLA_ROPE]
    assert (SMALL_KR, SMALL_GB, SMALL_GA) == (0, MLA_ROPE, MLA_ROPE + HEADS)
    fill = jnp.zeros((w.shape[0], 128 - MLA_ROPE - 2 * HEADS), w.dtype)
    out = jnp.concatenate([qkv, gz, kva, qa, kr, gb, ga, fill], axis=-1)
    assert out.shape[1] == CD_COLS
    return out.astype(BF16)


def _prep_wqb(w):
    w4 = w.reshape(MLA_Q_RANK, HEADS, MLA_QK)
    nope = w4[:, :, :MLA_NOPE].reshape(MLA_Q_RANK, HEADS * MLA_NOPE)
    rope = w4[:, :, MLA_NOPE:].reshape(MLA_Q_RANK, HEADS * MLA_ROPE)
    return jnp.concatenate([nope, rope], axis=-1).astype(BF16)


def _rope_tables(T, past_len):
    half = MLA_ROPE // 2
    freqs = jnp.exp(-math.log(ROPE_THETA) * jnp.arange(half, dtype=F32) / half)
    pos = past_len + jnp.arange(T, dtype=jnp.int32)
    ang = pos.astype(F32)[:, None] * freqs
    cos, sin = jnp.cos(ang), jnp.sin(ang)
    return jnp.concatenate([cos, cos], axis=-1), jnp.concatenate([-sin, sin], axis=-1)


def _run_group(x, mods, hg_s, lru_h, lru_buf, gd_s, gd_buf, lat_past, kr_past, ffn_buf, W):
    B, T, _ = x.shape
    n_hg, n_lru, n_lrub, n_gd, n_gdb, n_ffn = ([] for _ in range(6))
    mla_state = (jnp.zeros((N_CD, B, T, MLA_KV_RANK), F32), jnp.zeros((N_CD, B, T, MLA_ROPE), F32))
    for l in range(DEPTH):
        j = l // 2
        shift1, scale1, gate1, shift2, scale2, gate2 = jnp.split(mods[l], 6, axis=-1)
        g = W['norm_g'][l]
        if l % 2 == 0:
            proj = _nmm_call(x, g[0], scale1, shift1, W['ab_w_in'][j])
            o_a, s_hg_t = _hgrn_call(proj, W['lower_bounds'][j], W['hgrn_norm_g'][j],
                                     jnp.swapaxes(hg_s[j], -1, -2))
            o_b, s_lru, s_lrub = _lru_call(proj, lru_buf[j], lru_h[j], W['lru_conv_w'][j], W['lru_conv_b'][j],
                                           W['lru_wa_bd'][j], W['lru_wx_bd'][j], W['lru_b_a'][j],
                                           W['lru_b_x'][j], W['lru_lambda'][j])
            n_hg.append(jnp.swapaxes(s_hg_t, -1, -2))
            n_lru.append(s_lru[:, 0, :])
            n_lrub.append(s_lrub)
            mix_a, mix_b, w_out = o_a, o_b, W['ab_w_out'][j]
        else:
            past_len = lat_past.shape[2]
            if _tiles(B, T)[0] == 1:
                proj, small_t = _nmm_call(x, g[0], scale1, shift1, W['cd_w_in'][j], gates_t=True)
            else:
                proj = _nmm_call(x, g[0], scale1, shift1, W['cd_w_in'][j])
                small_t = jnp.swapaxes(proj[:, :, CD_COLS - 128 + SMALL_GB:CD_COLS - 128 + SMALL_GB + 2 * HEADS], 1, 2)
            o_c, s_gd, s_gdb = _gdn_call(proj, small_t, gd_buf[j], W['gdn_conv_w'][j], W['gdn_pcol'][j],
                                         W['gdn_prow'][j], W['gdn_norm_g'][j], gd_s[j])
            cs, sn = _rope_tables(T, past_len)
            q, lat_all, kr_all, kf, v = _mla_prep_call(proj, cs, sn, W['mla_q_norm_g'][j], W['mla_w_qb'][j],
                                                       W['mla_kv_norm_g'][j], W['mla_w_kvb'][j], j, mla_state)
            mla_state = (lat_all, kr_all)
            if past_len > 0:
                o_d = _attn_cached_call(q, lat_past[j], kr_past[j], lat_all, kr_all, j, W['mla_w_kvb'][j])
            else:
                o_d = _attn_call(q, kf, v, 0)
            n_gd.append(s_gd)
            n_gdb.append(s_gdb)
            mix_a, mix_b, w_out = o_c, o_d, W['cd_w_out'][j]
        x, s_ffn = _ffn_call(x, mix_a, mix_b, w_out, g[1], gate1, g[2], scale2, shift2, gate2, g[3], ffn_buf[l],
                             W['ffn_wu'][l], W['ffn_wd'][l], W['ffn_cw'][l])
        n_ffn.append(s_ffn)
    return x, (jnp.stack(n_hg), jnp.stack(n_lru), jnp.stack(n_lrub), jnp.stack(n_gd), jnp.stack(n_gdb),
               mla_state[0], mla_state[1], jnp.stack(n_ffn))


def _prep_weights(norm_g, ab_w_in, ab_w_out, hgrn_lb_logits, hgrn_norm_g, lru_conv_w, lru_conv_b, lru_w_a, lru_b_a,
                  lru_w_x, lru_b_x, lru_lambda, cd_w_in, cd_w_out, gdn_conv_w, gdn_a_log, gdn_dt_bias, gdn_norm_g,
                  mla_q_norm_g, mla_w_qb, mla_kv_norm_g, mla_w_kvb, ffn_w_up, ffn_conv_w, ffn_w_down):
    lb_p = jax.nn.softmax(hgrn_lb_logits.astype(F32), axis=0)
    pcol = jnp.pad(jnp.stack([gdn_a_log, gdn_dt_bias], axis=1),
                   ((0, 0), (0, 0), (SMALL_GA, 128 - SMALL_GA - HEADS)))
    prow = jnp.pad(jnp.stack([gdn_a_log, gdn_dt_bias], axis=-1), ((0, 0), (HEADS, 0), (0, 0)))
    W = dict(
        norm_g=norm_g,
        ab_w_in=ab_w_in.astype(BF16), ab_w_out=ab_w_out.astype(BF16),
        lower_bounds=jnp.cumsum(lb_p, axis=0) - lb_p[0:1],
        hgrn_norm_g=hgrn_norm_g, lru_conv_w=lru_conv_w, lru_conv_b=lru_conv_b,
        lru_wa_bd=jax.vmap(_block_diag_pairs)(lru_w_a), lru_wx_bd=jax.vmap(_block_diag_pairs)(lru_w_x),
        lru_b_a=lru_b_a, lru_b_x=lru_b_x, lru_lambda=lru_lambda,
        cd_w_in=jax.vmap(_prep_cd_w_in)(cd_w_in), cd_w_out=cd_w_out.astype(BF16),
        gdn_conv_w=gdn_conv_w, gdn_pcol=pcol, gdn_prow=prow, gdn_norm_g=gdn_norm_g,
        mla_q_norm_g=mla_q_norm_g, mla_w_qb=jax.vmap(_prep_wqb)(mla_w_qb),
        mla_kv_norm_g=mla_kv_norm_g, mla_w_kvb=mla_w_kvb.astype(BF16),
        ffn_wu=ffn_w_up.astype(BF16), ffn_wd=ffn_w_down.astype(BF16), ffn_cw=ffn_conv_w,
    )
    return W


def kernel(x_prompt, x_sample, c_prompt, c_sample, state_hgrn, state_rglru, state_rglru_conv, state_gdn, state_gdn_conv, cache_mla_latent, cache_mla_krope, state_ffn_conv, ada_w, ada_b, norm_g, ab_w_in, ab_w_out, hgrn_lb_logits, hgrn_norm_g, lru_conv_w, lru_conv_b, lru_w_a, lru_b_a, lru_w_x, lru_b_x, lru_lambda, cd_w_in, cd_w_out, gdn_conv_w, gdn_a_log, gdn_dt_bias, gdn_norm_g, mla_q_norm_g, mla_w_qb, mla_kv_norm_g, mla_w_kvb, ffn_w_up, ffn_conv_w, ffn_w_down):
    bp, bs = x_prompt.shape[0], x_sample.shape[0]
    W = _prep_weights(norm_g, ab_w_in, ab_w_out, hgrn_lb_logits, hgrn_norm_g, lru_conv_w, lru_conv_b, lru_w_a, lru_b_a,
                      lru_w_x, lru_b_x, lru_lambda, cd_w_in, cd_w_out, gdn_conv_w, gdn_a_log, gdn_dt_bias,
                      gdn_norm_g, mla_q_norm_g, mla_w_qb, mla_kv_norm_g, mla_w_kvb, ffn_w_up, ffn_conv_w,
                      ffn_w_down)
    rows = bp + bs
    rows_pad = -(-rows // 8) * 8
    c_all = jnp.concatenate([c_prompt, c_sample, jnp.zeros((rows_pad - rows, D_MODEL), F32)], axis=0)
    mods = _ada_call(c_all, ada_w, ada_b)
    dt_ = x_prompt.dtype
    y_prompt, p_states = _run_group(
        x_prompt, mods[:, :bp],
        jnp.zeros((N_AB, bp, HEADS, HEAD_DIM, HEAD_DIM), F32),
        jnp.zeros((N_AB, bp, HALF), F32),
        jnp.zeros((N_AB, bp, LRU_CONV - 1, HALF), dt_),
        jnp.zeros((N_CD, bp, HEADS, HEAD_DIM, HEAD_DIM), F32),
        jnp.zeros((N_CD, bp, GD_CONV - 1, 3 * HALF), dt_),
        jnp.zeros((N_CD, bp, 0, MLA_KV_RANK), dt_),
        jnp.zeros((N_CD, bp, 0, MLA_ROPE), dt_),
        jnp.zeros((DEPTH, bp, FFN_CONV - 1, 2 * D_FF), dt_),
        W)
    y_sample, s_states = _run_group(
        x_sample, mods[:, bp:rows], state_hgrn, state_rglru, state_rglru_conv, state_gdn, state_gdn_conv,
        cache_mla_latent, cache_mla_krope, state_ffn_conv, W)
    return (y_prompt, y_sample) + tuple(p_states) + tuple(s_states)
```

```python
import functools
import math

import jax
import jax.numpy as jnp
from jax import lax
from jax.experimental import pallas as pl
from jax.experimental.pallas import tpu as pltpu

F32 = jnp.float32
BF16 = jnp.bfloat16

D_MODEL = 1024
DEPTH = 4
CHUNK = 64
HALF = D_MODEL // 2
N_AB = (DEPTH + 1) // 2
N_CD = DEPTH // 2
HEADS = 4
HEAD_DIM = HALF // HEADS
LRU_BLOCKS = 8
LRU_BLOCK = HALF // LRU_BLOCKS
LRU_CONV = 4
LRU_C = 8.0
LRU_GROUP = 8
GD_CONV = 4
MLA_NOPE = 128
MLA_ROPE = 64
MLA_QK = MLA_NOPE + MLA_ROPE
MLA_Q_RANK = 384
MLA_KV_RANK = 256
MLA_SCALE = (MLA_NOPE + MLA_ROPE) ** -0.5
ROPE_THETA = 10000.0
D_FF = 2816
FFN_CONV = 3
FF_TILE = 256
N_FF_TILES = D_FF // FF_TILE
FF_GROUP = 6
FF_SLOTS = 4
FF_AHEAD = 2
EPS = 1e-6
NEG_BIG = -1e30
SQRT_FLOOR = 1e-12
CD_COLS = 2816
SMALL_KR = 0
SMALL_GB = 64
SMALL_GA = 68

VMEM_LIMIT_BYTES = 56 * 1024 * 1024
ATTN_KV_BLOCK_BYTES = 8 * 1024 * 1024


def _cparams(*sem):
    return pltpu.CompilerParams(dimension_semantics=sem, vmem_limit_bytes=VMEM_LIMIT_BYTES)


def _dot(a, b):
    return jnp.dot(a, b, preferred_element_type=F32)


def _dot_nt(a, b):
    return lax.dot_general(a, b, (((1,), (1,)), ((), ())), preferred_element_type=F32)


def _dot_tn(a, b):
    return lax.dot_general(a, b, (((0,), (0,)), ((), ())), preferred_element_type=F32)


def _rms(x, g):
    return x * lax.rsqrt(jnp.mean(x * x, axis=-1, keepdims=True) + EPS) * g


def _silu(x):
    return x * jax.nn.sigmoid(x)


def _softplus(x):
    return jnp.maximum(x, 0.0) + jnp.log1p(jnp.exp(-jnp.abs(x)))


def _gelu_tanh(x):
    return 0.5 * x * (1.0 + jnp.tanh(math.sqrt(2.0 / math.pi) * (x + 0.044715 * (x * x * x))))


def _split3(x):
    x1 = x.astype(BF16)
    r1 = x - x1.astype(F32)
    x2 = r1.astype(BF16)
    x3 = (r1 - x2.astype(F32)).astype(BF16)
    return x1, x2, x3


def _log2(n):
    assert n & (n - 1) == 0
    return n.bit_length() - 1


def _cumsum_rows(x):
    n, c = x.shape
    group = min(n, 8)
    x3 = x.reshape(n // group, group, c)
    pos = lax.broadcasted_iota(jnp.int32, x3.shape, 1)
    s = 1
    while s < group:
        x3 = x3 + jnp.where(pos >= s, pltpu.roll(x3, s, 1), 0.0)
        s *= 2
    if n == group:
        return x3.reshape(n, c)
    groups = [x3[0]]
    for gi in range(1, n // group):
        groups.append(x3[gi] + groups[-1][group - 1:group, :])
    return jnp.concatenate(groups, axis=0)


def _block_row_bcast(b, row, h):
    L, n = b.shape
    blk = 2 * h
    if blk >= 8:
        b3 = b.reshape(L // blk, blk, n)
        return jnp.broadcast_to(b3[:, h - 1:h, :], (L // blk, blk, n)).reshape(L, n)
    group = min(L, 8)
    x0 = jnp.where((row & (blk - 1)) == h - 1, b, 0.0).reshape(L // group, group, n)
    out = x0
    for j in range(1, h + 1):
        out = out + pltpu.roll(x0, j, 1)
    for j in range(1, h):
        out = out + pltpu.roll(x0, group - j, 1)
    return out.reshape(L, n)


def _ada_kernel(c_ref, w_ref, b_ref, o_ref):
    c = _silu(c_ref[...]).astype(BF16)
    o_ref[0] = _dot(c, w_ref[0].astype(BF16)) + b_ref[0]


def _ada_call(c_all, ada_w, ada_b):
    rows = c_all.shape[0]
    tn = 2048
    return pl.pallas_call(
        _ada_kernel,
        out_shape=jax.ShapeDtypeStruct((DEPTH, rows, 6 * D_MODEL), F32),
        grid=(DEPTH, 6 * D_MODEL // tn),
        in_specs=[
            pl.BlockSpec((rows, D_MODEL), lambda l, j: (0, 0)),
            pl.BlockSpec((1, D_MODEL, tn), lambda l, j: (l, 0, j)),
            pl.BlockSpec((1, 1, tn), lambda l, j: (l, 0, j)),
        ],
        out_specs=pl.BlockSpec((1, rows, tn), lambda l, j: (l, 0, j)),
        compiler_params=_cparams("arbitrary", "arbitrary"),
        name="ada_mod",
    )(c_all, ada_w, ada_b.reshape(DEPTH, 1, 6 * D_MODEL))


def _nmm_kernel(x_ref, g_ref, sc_ref, sh_ref, w_ref, o_ref, *maybe_t_ref):
    nb, tt, d = x_ref.shape
    h = _rms(x_ref[...], g_ref[...]) * (1.0 + sc_ref[...]) + sh_ref[...]
    y = _dot(h.reshape(nb * tt, d).astype(BF16), w_ref[...])
    o_ref[...] = y.reshape(nb, tt, y.shape[-1])
    if maybe_t_ref:
        (t_ref,) = maybe_t_ref
        yt = y[:, y.shape[-1] - 128:].T
        t_ref[0] = yt[SMALL_GB:SMALL_GB + 2 * HEADS, :]


def _tiles(B, T):
    if T >= 512:
        return 1, 512
    assert B * T <= 512
    return B, T


def _nmm_call(x, g, scale, shift, w, gates_t=False):
    B, T, D = x.shape
    N = w.shape[1]
    nb, tt = _tiles(B, T)
    out_shape = jax.ShapeDtypeStruct((B, T, N), F32)
    out_specs = pl.BlockSpec((nb, tt, N), lambda b, i: (b, i, 0))
    if gates_t:
        assert nb == 1 and tt % 128 == 0
        out_shape = (out_shape, jax.ShapeDtypeStruct((B, 2 * HEADS, T), F32))
        out_specs = (out_specs, pl.BlockSpec((1, 2 * HEADS, tt), lambda b, i: (b, 0, i)))
    return pl.pallas_call(
        _nmm_kernel,
        out_shape=out_shape,
        grid=(B // nb, T // tt),
        in_specs=[
            pl.BlockSpec((nb, tt, D), lambda b, i: (b, i, 0)),
            pl.BlockSpec((1, D), lambda b, i: (0, 0)),
            pl.BlockSpec((nb, 1, D), lambda b, i: (b, 0, 0)),
            pl.BlockSpec((nb, 1, D), lambda b, i: (b, 0, 0)),
            pl.BlockSpec((D, N), lambda b, i: (0, 0)),
        ],
        out_specs=out_specs,
        compiler_params=_cparams("arbitrary", "arbitrary"),
        name="norm_mod_proj",
    )(x, g.reshape(1, D), scale[:, None, :], shift[:, None, :], w)


def _ffn_kernel(x_ref, oa_ref, ob_ref, wo_ref, go_ref, gateo_ref, g1_ref, sc_ref, sh_ref, gate_ref, g2_ref,
                buf0_ref, wu_ref, wd_ref, cw_ref, o_ref, st_ref, carry_ref, ubuf_ref):
    nb, tt, d = x_ref.shape
    i = pl.program_id(1)

    @pl.when(i == 0)
    def _():
        for c in range(2 * N_FF_TILES):
            carry_ref[c] = buf0_ref[:, :, c * FF_TILE:(c + 1) * FF_TILE]

    oa = oa_ref[...].reshape(nb * tt, HALF)
    ob = ob_ref[...].reshape(nb * tt, HALF)
    mix = _dot(oa, wo_ref[0:HALF, :]) + _dot(ob, wo_ref[HALF:2 * HALF, :])
    x = x_ref[...] + gateo_ref[...] * _rms(mix, go_ref[...]).reshape(nb, tt, d)
    h = (_rms(x, g1_ref[...]) * (1.0 + sc_ref[...]) + sh_ref[...]).reshape(nb * tt, d).astype(BF16)

    def conv(u, slot, c):
        ubuf_ref[slot, :, 8:8 + tt, :] = u
        ubuf_ref[slot, :, 6:8, :] = carry_ref[c]
        cw = cw_ref[:, c * FF_TILE:(c + 1) * FF_TILE]
        y = (cw[0:1, :] * ubuf_ref[slot, :, 6:6 + tt, :] + cw[1:2, :] * ubuf_ref[slot, :, 7:7 + tt, :]
             + cw[2:3, :] * u)
        tail = ubuf_ref[slot, :, 6 + tt:8 + tt, :]
        carry_ref[c] = tail
        st_ref[:, :, c * FF_TILE:(c + 1) * FF_TILE] = tail
        return y

    def up_proj(c):
        wg = wu_ref[:, c * FF_TILE:(c + 1) * FF_TILE]
        wv = wu_ref[:, D_FF + c * FF_TILE:D_FF + (c + 1) * FF_TILE]
        return _dot(h, wg).reshape(nb, tt, FF_TILE), _dot(h, wv).reshape(nb, tt, FF_TILE)

    acc = None
    ahead = [up_proj(c) for c in range(min(FF_AHEAD, N_FF_TILES))]
    for g0 in range(0, N_FF_TILES, FF_GROUP):
        acts = []
        for c in range(g0, min(g0 + FF_GROUP, N_FF_TILES)):
            ug, uv = ahead.pop(0)
            if c + FF_AHEAD < N_FF_TILES:
                ahead.append(up_proj(c + FF_AHEAD))
            slot = 2 * (c % FF_SLOTS)
            yg = conv(ug, slot, c)
            yv = conv(uv, slot + 1, N_FF_TILES + c)
            acts.append((_silu(yg) * yv).reshape(nb * tt, FF_TILE).astype(BF16))
        a = jnp.concatenate(acts, axis=-1) if len(acts) > 1 else acts[0]
        part = _dot(a, wd_ref[g0 * FF_TILE:g0 * FF_TILE + a.shape[-1], :])
        acc = part if acc is None else acc + part
    y = _rms(acc, g2_ref[...]).reshape(nb, tt, d)
    o_ref[...] = x + gate_ref[...] * y


def _ffn_call(x, oa, ob, wo, go, gateo, g1, scale, shift, gate, g2, buf0, wu, wd, cw):
    B, T, D = x.shape
    nb, tt = _tiles(B, T)
    assert T >= FFN_CONV - 1
    return pl.pallas_call(
        _ffn_kernel,
        out_shape=(jax.ShapeDtypeStruct((B, T, D), F32),
                   jax.ShapeDtypeStruct((B, FFN_CONV - 1, 2 * D_FF), F32)),
        grid=(B // nb, T // tt),
        in_specs=[
            pl.BlockSpec((nb, tt, D), lambda b, i: (b, i, 0)),
            pl.BlockSpec((nb, tt, HALF), lambda b, i: (b, i, 0)),
            pl.BlockSpec((nb, tt, HALF), lambda b, i: (b, i, 0)),
            pl.BlockSpec((D, D), lambda b, i: (0, 0), pipeline_mode=pl.Buffered(1)),
            pl.BlockSpec((1, D), lambda b, i: (0, 0)),
            pl.BlockSpec((nb, 1, D), lambda b, i: (b, 0, 0)),
            pl.BlockSpec((1, D), lambda b, i: (0, 0)),
            pl.BlockSpec((nb, 1, D), lambda b, i: (b, 0, 0)),
            pl.BlockSpec((nb, 1, D), lambda b, i: (b, 0, 0)),
            pl.BlockSpec((nb, 1, D), lambda b, i: (b, 0, 0)),
            pl.BlockSpec((1, D), lambda b, i: (0, 0)),
            pl.BlockSpec((nb, FFN_CONV - 1, 2 * D_FF), lambda b, i: (b, 0, 0)),
            pl.BlockSpec((D, 2 * D_FF), lambda b, i: (0, 0), pipeline_mode=pl.Buffered(1)),
            pl.BlockSpec((D_FF, D), lambda b, i: (0, 0), pipeline_mode=pl.Buffered(1)),
            pl.BlockSpec((FFN_CONV, 2 * D_FF), lambda b, i: (0, 0)),
        ],
        out_specs=(pl.BlockSpec((nb, tt, D), lambda b, i: (b, i, 0)),
                   pl.BlockSpec((nb, FFN_CONV - 1, 2 * D_FF), lambda b, i: (b, 0, 0))),
        scratch_shapes=[
            pltpu.VMEM((2 * N_FF_TILES, nb, FFN_CONV - 1, FF_TILE), F32),
            pltpu.VMEM((2 * FF_SLOTS, nb, 8 + tt, FF_TILE), F32),
        ],
        compiler_params=_cparams("arbitrary", "arbitrary"),
        name="conv_ffn",
    )(x, oa, ob, wo, go.reshape(1, D), gateo[:, None, :], g1.reshape(1, D), scale[:, None, :], shift[:, None, :],
      gate[:, None, :], g2.reshape(1, D), buf0, wu, wd, cw)


def _hgrn_kernel(hq_ref, hf_ref, hi_ref, hz_ref, lb_ref, ng_ref, s0_ref, o_ref, st_ref):
    L = hq_ref.shape[1]
    i = pl.program_id(1)

    @pl.when(i == 0)
    def _():
        st_ref[...] = s0_ref[...]

    row = lax.broadcasted_iota(jnp.int32, (L, HEAD_DIM), 0)
    r2 = lax.broadcasted_iota(jnp.int32, (L, L), 0)
    c2 = lax.broadcasted_iota(jnp.int32, (L, L), 1)
    heads = range(HEADS)
    sls = [slice(hd * HEAD_DIM, (hd + 1) * HEAD_DIM) for hd in heads]
    q, k, v, b = ([None] * HEADS for _ in range(4))
    for hd in heads:
        z = hf_ref[0, :, sls[hd]]
        lb = lb_ref[:, sls[hd]]
        en = jnp.exp(-jnp.abs(z))
        inv = 1.0 / (1.0 + en)
        sig_pos = jnp.where(z >= 0.0, inv, en * inv)
        sig_neg = jnp.where(z >= 0.0, en * inv, inv)
        g = jnp.log(lb + (1.0 - lb) * sig_pos)
        k[hd] = (1.0 - lb) * sig_neg
        q[hd] = _silu(hq_ref[0, :, sls[hd]])
        v[hd] = hi_ref[0, :, sls[hd]].astype(BF16)
        b[hd] = _cumsum_rows(g)
    att = [jnp.where(r2 == c2, _dot_nt(q[hd].astype(BF16), k[hd].astype(BF16)), 0.0) for hd in heads]
    h = L // 2
    while h >= 1:
        sh = _log2(2 * h)
        keep = jnp.logical_and((r2 >> sh) == (c2 >> sh),
                               jnp.logical_and((r2 & (2 * h - 1)) >= h, (c2 & (2 * h - 1)) < h))
        for hd in heads:
            r = _block_row_bcast(b[hd], row, h)
            e = jnp.exp2(jnp.abs(b[hd] - r) * (-math.log2(math.e)))
            qt = (q[hd] * e).astype(BF16)
            kt = (k[hd] * e).astype(BF16)
            att[hd] = jnp.where(keep, _dot_nt(qt, kt), att[hd])
        h //= 2
    for hd in heads:
        st = st_ref[0, hd]
        o = _dot(att[hd].astype(BF16), v[hd]) + _dot_nt((q[hd] * jnp.exp(b[hd])).astype(BF16), st.astype(BF16))
        b_last = b[hd][L - 1:L, :]
        kd = (k[hd] * jnp.exp(b_last - b[hd])).astype(BF16)
        st_ref[0, hd] = jnp.exp(b_last) * st + _dot_tn(v[hd], kd)
        o = _rms(o, ng_ref[...]) * _silu(hz_ref[0, :, sls[hd]])
        o_ref[0, :, sls[hd]] = o.astype(BF16)


def _hgrn_call(proj, lb, ng, s0t):
    B, T, _ = proj.shape
    L = 128 if T % 128 == 0 else T
    assert T % L == 0 and L & (L - 1) == 0 and L >= 8
    col = lambda c: pl.BlockSpec((1, L, HALF), lambda b, i, c=c: (b, i, c))
    return pl.pallas_call(
        _hgrn_kernel,
        out_shape=(jax.ShapeDtypeStruct((B, T, HALF), BF16),
                   jax.ShapeDtypeStruct((B, HEADS, HEAD_DIM, HEAD_DIM), F32)),
        grid=(B, T // L),
        in_specs=[col(0), col(1), col(2), col(3),
                  pl.BlockSpec((1, HALF), lambda b, i: (0, 0)),
                  pl.BlockSpec((1, HEAD_DIM), lambda b, i: (0, 0)),
                  pl.BlockSpec((1, HEADS, HEAD_DIM, HEAD_DIM), lambda b, i: (b, 0, 0, 0))],
        out_specs=(pl.BlockSpec((1, L, HALF), lambda b, i: (b, i, 0)),
                   pl.BlockSpec((1, HEADS, HEAD_DIM, HEAD_DIM), lambda b, i: (b, 0, 0, 0))),
        compiler_params=_cparams("arbitrary", "arbitrary"),
        name="hgrn2",
    )(proj, proj, proj, proj, lb.reshape(1, HALF), ng.reshape(1, HEAD_DIM), s0t)


def _lru_kernel(lx_ref, ly_ref, buf0_ref, h0_ref, cw_ref, cb_ref, wa_ref, wx_ref, ba_ref, bx_ref,
                lam_ref, o_ref, hl_ref, bufo_ref, xp_ref):
    tt = lx_ref.shape[1]
    i = pl.program_id(1)
    npad = LRU_CONV - 1

    @pl.when(i == 0)
    def _():
        xp_ref[8 - npad:8, :] = buf0_ref[0]
        hl_ref[0] = h0_ref[0]

    x = lx_ref[0]
    xp_ref[8:8 + tt, :] = x
    xc = cb_ref[...] + cw_ref[npad:npad + 1, :] * x
    for tap in range(npad):
        xc = xc + cw_ref[tap:tap + 1, :] * xp_ref[8 - npad + tap:8 - npad + tap + tt, :]
    tail = xp_ref[8 + tt - npad:8 + tt, :]
    xp_ref[8 - npad:8, :] = tail
    bufo_ref[0] = tail

    xb = xc.astype(BF16)
    half = HALF // 2
    rpre = jnp.concatenate([_dot(xb[:, 0:half], wa_ref[0]), _dot(xb[:, half:HALF], wa_ref[1])], axis=-1)
    ipre = jnp.concatenate([_dot(xb[:, 0:half], wx_ref[0]), _dot(xb[:, half:HALF], wx_ref[1])], axis=-1)
    r = jax.nn.sigmoid(rpre + ba_ref[...])
    ig = jax.nn.sigmoid(ipre + bx_ref[...])
    log_a = -LRU_C * r * _softplus(-lam_ref[...])
    a = jnp.exp(log_a)
    u = jnp.sqrt(jnp.maximum(-jnp.tanh(log_a) * (1.0 + a * a), SQRT_FLOOR)) * ig * xc

    a = a.reshape(tt // LRU_GROUP, LRU_GROUP, HALF)
    u = u.reshape(tt // LRU_GROUP, LRU_GROUP, HALF)
    pos = lax.broadcasted_iota(jnp.int32, a.shape, 1)
    s = 1
    while s < LRU_GROUP:
        keep = pos >= s
        a_sh = jnp.where(keep, pltpu.roll(a, s, 1), 1.0)
        u_sh = jnp.where(keep, pltpu.roll(u, s, 1), 0.0)
        u = a * u_sh + u
        a = a * a_sh
        s *= 2
    carry = hl_ref[0]
    groups = []
    for gi in range(tt // LRU_GROUP):
        hg = u[gi] + a[gi] * carry
        carry = hg[LRU_GROUP - 1:LRU_GROUP, :]
        groups.append(hg)
    hseq = jnp.concatenate(groups, axis=0) if len(groups) > 1 else groups[0]
    hl_ref[0] = carry
    o_ref[0] = (hseq * _gelu_tanh(ly_ref[0])).astype(BF16)


def _lru_call(proj, buf0, h0, cw, cb, wa_bd, wx_bd, ba, bx, lam):
    B, T, _ = proj.shape
    tt = 256 if T % 256 == 0 else T
    assert T % tt == 0 and T >= LRU_CONV - 1 and tt % 8 == 0
    vec = pl.BlockSpec((1, HALF), lambda b, i: (0, 0))
    wspec = pl.BlockSpec((2, HALF // 2, HALF // 2), lambda b, i: (0, 0, 0))
    return pl.pallas_call(
        _lru_kernel,
        out_shape=(jax.ShapeDtypeStruct((B, T, HALF), BF16),
                   jax.ShapeDtypeStruct((B, 1, HALF), F32),
                   jax.ShapeDtypeStruct((B, LRU_CONV - 1, HALF), F32)),
        grid=(B, T // tt),
        in_specs=[pl.BlockSpec((1, tt, HALF), lambda b, i: (b, i, 4)),
                  pl.BlockSpec((1, tt, HALF), lambda b, i: (b, i, 5)),
                  pl.BlockSpec((1, LRU_CONV - 1, HALF), lambda b, i: (b, 0, 0)),
                  pl.BlockSpec((1, 1, HALF), lambda b, i: (b, 0, 0)),
                  pl.BlockSpec((LRU_CONV, HALF), lambda b, i: (0, 0)),
                  vec, wspec, wspec, vec, vec, vec],
        out_specs=(pl.BlockSpec((1, tt, HALF), lambda b, i: (b, i, 0)),
                   pl.BlockSpec((1, 1, HALF), lambda b, i: (b, 0, 0)),
                   pl.BlockSpec((1, LRU_CONV - 1, HALF), lambda b, i: (b, 0, 0))),
        scratch_shapes=[pltpu.VMEM((8 + tt, HALF), F32)],
        compiler_params=_cparams("arbitrary", "arbitrary"),
        name="rglru",
    )(proj, proj, buf0, h0[:, None, :], cw, cb.reshape(1, HALF), wa_bd, wx_bd,
      ba.reshape(1, HALF), bx.reshape(1, HALF), lam.reshape(1, HALF))


def _gdn_kernel(qkv_ref, gz_ref, sm_ref, smt_ref, buf0_ref, cw_ref, pcol_ref, prow_ref, ng_ref, s0_ref,
                o_ref, st_ref, bufo_ref, xp_ref, *, L):
    tt = qkv_ref.shape[1]
    nc = tt // L
    sh = _log2(L)
    i = pl.program_id(1)
    npad = GD_CONV - 1

    @pl.when(i == 0)
    def _():
        xp_ref[8 - npad:8, :] = buf0_ref[0]
        st_ref[...] = s0_ref[...]

    x = qkv_ref[0]
    xp_ref[8:8 + tt, :] = x
    xc = cw_ref[npad:npad + 1, :] * x
    for tap in range(npad):
        xc = xc + cw_ref[tap:tap + 1, :] * xp_ref[8 - npad + tap:8 - npad + tap + tt, :]
    tail = xp_ref[8 + tt - npad:8 + tt, :]
    xp_ref[8 - npad:8, :] = tail
    bufo_ref[0] = tail
    xc = _silu(xc)

    r2 = lax.broadcasted_iota(jnp.int32, (tt, tt), 0)
    c2 = lax.broadcasted_iota(jnp.int32, (tt, tt), 1)
    same = (r2 >> sh) == (c2 >> sh)
    incl = jnp.logical_and(same, c2 <= r2)
    strict = jnp.logical_and(same, c2 < r2)
    tri_lo = jnp.where(incl, 1.0, 0.0).astype(BF16)
    tri_up = jnp.where(jnp.logical_and(same, r2 <= c2), 1.0, 0.0).astype(BF16)
    eye = jnp.where(r2 == c2, 1.0, 0.0)

    sm = sm_ref[0]
    beta_cols = jax.nn.sigmoid(sm)
    la_cols = -jnp.exp(pcol_ref[0:1, :]) * _softplus(sm + pcol_ref[1:2, :])
    c1, c2_, c3 = _split3(la_cols)
    g_cols = _dot(tri_lo, c1) + _dot(tri_lo, c2_) + _dot(tri_lo, c3)
    la_rows = -jnp.exp(prow_ref[:, 0:1]) * _softplus(smt_ref[0] + prow_ref[:, 1:2])
    w1, w2, w3 = _split3(la_rows)
    g_rows = _dot(w1, tri_up) + _dot(w2, tri_up) + _dot(w3, tri_up)

    heads = range(HEADS)
    q, k, v, beta, gcol, kb, dec, m = ([None] * HEADS for _ in range(8))
    for hd in heads:
        qh = xc[:, hd * HEAD_DIM:(hd + 1) * HEAD_DIM]
        kh = xc[:, HALF + hd * HEAD_DIM:HALF + (hd + 1) * HEAD_DIM]
        v[hd] = xc[:, 2 * HALF + hd * HEAD_DIM:2 * HALF + (hd + 1) * HEAD_DIM]
        q[hd] = qh * lax.rsqrt(jnp.sum(qh * qh, axis=-1, keepdims=True) + EPS) * (HEAD_DIM ** -0.5)
        k[hd] = kh * lax.rsqrt(jnp.sum(kh * kh, axis=-1, keepdims=True) + EPS)
        beta[hd] = beta_cols[:, SMALL_GB + hd:SMALL_GB + hd + 1]
        gcol[hd] = g_cols[:, SMALL_GA + hd:SMALL_GA + hd + 1]
        grow = g_rows[HEADS + hd:HEADS + hd + 1, :]
        kb[hd] = k[hd].astype(BF16)
        dec[hd] = jnp.exp(jnp.where(incl, gcol[hd] - grow, NEG_BIG))
        m[hd] = beta[hd] * _dot_nt(kb[hd], kb[hd]) * jnp.where(strict, dec[hd], 0.0)

    pair = (r2 >> 1) == (c2 >> 1)
    tinv = [eye - jnp.where(pair, m[hd], 0.0) for hd in heads]
    s = 2
    while s < L:
        ssh = _log2(s)
        lower_left = jnp.logical_and((r2 >> (ssh + 1)) == (c2 >> (ssh + 1)), (r2 >> ssh) != (c2 >> ssh))
        tb = [tinv[hd].astype(BF16) for hd in heads]
        tc = [_dot(tb[hd], jnp.where(lower_left, m[hd], 0.0).astype(BF16)).astype(BF16) for hd in heads]
        tinv = [tinv[hd] - _dot(tc[hd], tb[hd]) for hd in heads]
        s *= 2

    eg = [jnp.exp(gcol[hd]) for hd in heads]
    sol = [_dot(tinv[hd].astype(BF16),
                jnp.concatenate([beta[hd] * v[hd], (beta[hd] * eg[hd]) * k[hd]], axis=-1).astype(BF16))
           for hd in heads]
    u_v = [sol[hd][:, 0:HEAD_DIM] for hd in heads]
    w_k = [sol[hd][:, HEAD_DIM:2 * HEAD_DIM].astype(BF16) for hd in heads]
    qb = [q[hd].astype(BF16) for hd in heads]
    qk = [(_dot_nt(qb[hd], kb[hd]) * dec[hd]).astype(BF16) for hd in heads]

    a_c, p_c, n_c = ([[None] * nc for _ in heads] for _ in range(3))
    for c in range(nc):
        rs = slice(c * L, (c + 1) * L)
        for hd in heads:
            g_last = gcol[hd][(c + 1) * L - 1:(c + 1) * L, :]
            kd = (k[hd][rs] * jnp.exp(g_last - gcol[hd][rs])).astype(BF16)
            a_c[hd][c] = jnp.exp(g_last)
            p_c[hd][c] = (-_dot_tn(kd, w_k[hd][rs])).astype(BF16)
            n_c[hd][c] = _dot_tn(kd, u_v[hd][rs].astype(BF16))
    S = [[st_ref[0, hd]] for hd in heads]
    for c in range(nc):
        for hd in heads:
            s_cur = S[hd][c]
            S[hd].append(a_c[hd][c] * s_cur + _dot(p_c[hd][c], s_cur.astype(BF16)) + n_c[hd][c])
    us = [[] for _ in heads]
    inters = [[] for _ in heads]
    for c in range(nc):
        rs = slice(c * L, (c + 1) * L)
        for hd in heads:
            Sb = S[hd][c].astype(BF16)
            us[hd].append(u_v[hd][rs] - _dot(w_k[hd][rs], Sb))
            inters[hd].append(eg[hd][rs] * _dot(qb[hd][rs], Sb))
    for hd in heads:
        sl = slice(hd * HEAD_DIM, (hd + 1) * HEAD_DIM)
        st_ref[0, hd] = S[hd][nc]
        u_all = jnp.concatenate(us[hd], axis=0) if nc > 1 else us[hd][0]
        inter = jnp.concatenate(inters[hd], axis=0) if nc > 1 else inters[hd][0]
        o = _dot(qk[hd], u_all.astype(BF16)) + inter
        o = _rms(o, ng_ref[...]) * _silu(gz_ref[0, :, sl])
        o_ref[0, :, sl] = o.astype(BF16)


def _gdn_call(proj, small_t, buf0, cw, pcol, prow, ng, s0):
    B, T, _ = proj.shape
    L = CHUNK if T % CHUNK == 0 else T
    tt = 256 if T % 256 == 0 else T
    assert T % tt == 0 and tt % L == 0 and T >= GD_CONV - 1 and L >= 2
    W = 3 * HALF
    return pl.pallas_call(
        functools.partial(_gdn_kernel, L=L),
        out_shape=(jax.ShapeDtypeStruct((B, T, HALF), BF16),
                   jax.ShapeDtypeStruct((B, HEADS, HEAD_DIM, HEAD_DIM), F32),
                   jax.ShapeDtypeStruct((B, GD_CONV - 1, W), F32)),
        grid=(B, T // tt),
        in_specs=[pl.BlockSpec((1, tt, W), lambda b, i: (b, i, 0)),
                  pl.BlockSpec((1, tt, HALF), lambda b, i: (b, i, 3)),
                  pl.BlockSpec((1, tt, 128), lambda b, i: (b, i, 21)),
                  pl.BlockSpec((1, 2 * HEADS, tt), lambda b, i: (b, 0, i)),
                  pl.BlockSpec((1, GD_CONV - 1, W), lambda b, i: (b, 0, 0)),
                  pl.BlockSpec((GD_CONV, W), lambda b, i: (0, 0)),
                  pl.BlockSpec((2, 128), lambda b, i: (0, 0)),
                  pl.BlockSpec((2 * HEADS, 2), lambda b, i: (0, 0)),
                  pl.BlockSpec((1, HEAD_DIM), lambda b, i: (0, 0)),
                  pl.BlockSpec((1, HEADS, HEAD_DIM, HEAD_DIM), lambda b, i: (b, 0, 0, 0))],
        out_specs=(pl.BlockSpec((1, tt, HALF), lambda b, i: (b, i, 0)),
                   pl.BlockSpec((1, HEADS, HEAD_DIM, HEAD_DIM), lambda b, i: (b, 0, 0, 0)),
                   pl.BlockSpec((1, GD_CONV - 1, W), lambda b, i: (b, 0, 0))),
        scratch_shapes=[pltpu.VMEM((8 + tt, W), F32)],
        compiler_params=_cparams("arbitrary", "arbitrary"),
        name="gated_deltanet",
    )(proj, proj, proj, small_t, buf0, cw, pcol, prow, ng.reshape(1, HEAD_DIM), s0)


def _rope64(x, cs, sn):
    half = MLA_ROPE // 2
    swapped = jnp.concatenate([x[:, half:], x[:, :half]], axis=-1)
    return x * cs + swapped * sn


def _expand_kv(c_kv, k_r, wkvb_ref, kf_ref, v_ref):
    kv = _dot(c_kv.astype(BF16), wkvb_ref[...])
    for hd in range(HEADS):
        base = hd * 2 * HEAD_DIM
        kf_ref[0, hd] = jnp.concatenate([kv[:, base:base + MLA_NOPE], k_r], axis=-1).astype(BF16)
        v = kv[:, base + MLA_NOPE:base + 2 * HEAD_DIM]
        v_ref[0, hd] = jnp.concatenate([v, jnp.ones_like(v)], axis=-1).astype(BF16)


def _mla_prep_kernel(qa_ref, kva_ref, sm_ref, cs_ref, sn_ref, qng_ref, wqb_ref, kvng_ref, wkvb_ref, *rest):
    q_ref, ckv_ref, kr_ref, kf_ref, v_ref = rest[-5:]
    ckv_ref = ckv_ref.at[0]
    kr_ref = kr_ref.at[0]
    cs = cs_ref[...]
    sn = sn_ref[...]
    qn = _rms(qa_ref[0], qng_ref[...]).astype(BF16)
    qh = _dot(qn, wqb_ref[...]) * (MLA_SCALE * math.log2(math.e))
    for hd in range(HEADS):
        nope = qh[:, hd * MLA_NOPE:(hd + 1) * MLA_NOPE]
        off = HEADS * MLA_NOPE + hd * MLA_ROPE
        rot = _rope64(qh[:, off:off + MLA_ROPE], cs, sn)
        q_ref[0, hd] = jnp.concatenate([nope, rot], axis=-1).astype(BF16)
    c_kv = _rms(kva_ref[0], kvng_ref[...])
    ckv_ref[0] = c_kv
    k_r = _rope64(sm_ref[0, :, SMALL_KR:SMALL_KR + MLA_ROPE], cs, sn)
    kr_ref[0] = k_r
    _expand_kv(c_kv, k_r, wkvb_ref, kf_ref, v_ref)


def _mla_prep_call(proj, cs, sn, qng, wqb, kvng, wkvb, j, state_bufs):
    B, T, _ = proj.shape
    tt = 512 if T % 512 == 0 else T
    c2 = lambda b, i: (0, 0)
    any_spec = pl.BlockSpec(memory_space=pl.ANY)
    n_fixed = 9
    extra = list(state_bufs)
    aliases = {n_fixed: 1, n_fixed + 1: 2}
    return pl.pallas_call(
        _mla_prep_kernel,
        out_shape=(jax.ShapeDtypeStruct((B, HEADS, T, MLA_QK), BF16),
                   jax.ShapeDtypeStruct((N_CD, B, T, MLA_KV_RANK), F32),
                   jax.ShapeDtypeStruct((N_CD, B, T, MLA_ROPE), F32),
                   jax.ShapeDtypeStruct((B, HEADS, T, MLA_QK), BF16),
                   jax.ShapeDtypeStruct((B, HEADS, T, 2 * HEAD_DIM), BF16)),
        grid=(B, T // tt),
        in_specs=[pl.BlockSpec((1, tt, MLA_Q_RANK), lambda b, i: (b, i, 6)),
                  pl.BlockSpec((1, tt, MLA_KV_RANK), lambda b, i: (b, i, 8)),
                  pl.BlockSpec((1, tt, 128), lambda b, i: (b, i, 21)),
                  pl.BlockSpec((tt, MLA_ROPE), lambda b, i: (i, 0)),
                  pl.BlockSpec((tt, MLA_ROPE), lambda b, i: (i, 0)),
                  pl.BlockSpec((1, MLA_Q_RANK), c2),
                  pl.BlockSpec((MLA_Q_RANK, HEADS * MLA_QK), c2),
                  pl.BlockSpec((1, MLA_KV_RANK), c2),
                  pl.BlockSpec((MLA_KV_RANK, HEADS * 2 * HEAD_DIM), c2)] + [any_spec] * len(extra),
        out_specs=(pl.BlockSpec((1, HEADS, tt, MLA_QK), lambda b, i: (b, 0, i, 0)),
                   pl.BlockSpec((1, 1, tt, MLA_KV_RANK), lambda b, i: (j, b, i, 0)),
                   pl.BlockSpec((1, 1, tt, MLA_ROPE), lambda b, i: (j, b, i, 0)),
                   pl.BlockSpec((1, HEADS, tt, MLA_QK), lambda b, i: (b, 0, i, 0)),
                   pl.BlockSpec((1, HEADS, tt, 2 * HEAD_DIM), lambda b, i: (b, 0, i, 0))),
        input_output_aliases=aliases,
        compiler_params=_cparams("arbitrary", "arbitrary"),
        name="mla_prep",
    )(proj, proj, proj, cs, sn, qng.reshape(1, MLA_Q_RANK), wqb, kvng.reshape(1, MLA_KV_RANK), wkvb, *extra)


def _attn_cached_kernel(q_ref, latp_ref, krp_ref, latn_ref, krn_ref, w_ref, o_ref):
    T = q_ref.shape[2]
    P = latp_ref.shape[1]
    sh = _log2(CHUNK)
    latp = latp_ref[0].astype(BF16)
    krp = krp_ref[0].astype(BF16)
    latn = latn_ref[0, 0].astype(BF16)
    krn = krn_ref[0, 0].astype(BF16)
    q_lat, q_rope = [], []
    for hd in range(HEADS):
        base = hd * 2 * HEAD_DIM
        q = q_ref[0, hd]
        q_lat.append(_dot_nt(q[:, 0:MLA_NOPE], w_ref[:, base:base + MLA_NOPE]).astype(BF16))
        q_rope.append(q[:, MLA_NOPE:MLA_QK])
    q_lat = jnp.concatenate(q_lat, axis=0)
    q_rope = jnp.concatenate(q_rope, axis=0)
    rows = HEADS * T
    t_of_row = jnp.concatenate([lax.broadcasted_iota(jnp.int32, (T, 1), 0)] * HEADS, axis=0)
    q_chunk = (P + t_of_row) >> sh
    vis_p = (lax.broadcasted_iota(jnp.int32, (rows, P), 1) >> sh) <= q_chunk
    vis_n = ((P + lax.broadcasted_iota(jnp.int32, (rows, T), 1)) >> sh) <= q_chunk
    s_p = jnp.where(vis_p, _dot_nt(q_lat, latp) + _dot_nt(q_rope, krp), NEG_BIG)
    s_n = jnp.where(vis_n, _dot_nt(q_lat, latn) + _dot_nt(q_rope, krn), NEG_BIG)
    m = jnp.maximum(jnp.max(s_p, axis=-1, keepdims=True), jnp.max(s_n, axis=-1, keepdims=True))
    p_p = jnp.exp2(s_p - m)
    p_n = jnp.exp2(s_n - m)
    denom = jnp.sum(p_p, axis=-1, keepdims=True) + jnp.sum(p_n, axis=-1, keepdims=True)
    o_lat = ((_dot(p_p.astype(BF16), latp) + _dot(p_n.astype(BF16), latn)) / denom).astype(BF16)
    for hd in range(HEADS):
        base = hd * 2 * HEAD_DIM
        o = _dot(o_lat[hd * T:(hd + 1) * T], w_ref[:, base + MLA_NOPE:base + 2 * HEAD_DIM])
        o_ref[0, :, hd * HEAD_DIM:(hd + 1) * HEAD_DIM] = o.astype(BF16)


def _attn_cached_call(q, lat_past, kr_past, lat_all, kr_all, j, wkvb):
    B, H, T, _ = q.shape
    P = lat_past.shape[1]
    return pl.pallas_call(
        _attn_cached_kernel,
        out_shape=jax.ShapeDtypeStruct((B, T, H * HEAD_DIM), BF16),
        grid=(B,),
        in_specs=[pl.BlockSpec((1, H, T, MLA_QK), lambda b: (b, 0, 0, 0)),
                  pl.BlockSpec((1, P, MLA_KV_RANK), lambda b: (b, 0, 0)),
                  pl.BlockSpec((1, P, MLA_ROPE), lambda b: (b, 0, 0)),
                  pl.BlockSpec((1, 1, T, MLA_KV_RANK), lambda b: (j, b, 0, 0)),
                  pl.BlockSpec((1, 1, T, MLA_ROPE), lambda b: (j, b, 0, 0)),
                  pl.BlockSpec((MLA_KV_RANK, H * 2 * HEAD_DIM), lambda b: (0, 0))],
        out_specs=pl.BlockSpec((1, T, H * HEAD_DIM), lambda b: (b, 0, 0)),
        compiler_params=_cparams("arbitrary"),
        name="mla_attention_cached",
    )(q, lat_past, kr_past, lat_all, kr_all, wkvb)


def _attn_kernel(q_ref, k_ref, v_ref, o_ref, m_ref, acc_ref, sa_ref, sb_ref, **static):
    for hh in range(q_ref.shape[1]):
        _attn_head(q_ref.at[0, hh], k_ref.at[0, hh], v_ref.at[0, hh], o_ref.at[0, :, hh * HEAD_DIM:(hh + 1) * HEAD_DIM],
                   m_ref, acc_ref, sa_ref, sb_ref, **static)


def _attn_head(q_ref, k_ref, v_ref, o_ref, m_ref, acc_ref, sa_ref, sb_ref, *, past_len, tq, tk, nk):
    i = pl.program_id(2)
    sh = _log2(CHUNK)
    lanes = HEAD_DIM
    m_ref[...] = jnp.full(m_ref.shape, NEG_BIG, F32)
    acc_ref[...] = jnp.zeros(acc_ref.shape, F32)
    q = q_ref[...]
    q_lo = past_len + i * tq
    n_full = jnp.minimum(nk, (((q_lo >> sh) + 1) * CHUNK) // tk)
    n_need = jnp.minimum(nk, ((((q_lo + tq - 1) >> sh) + 1) * CHUNK + tk - 1) // tk)

    def scores(j):
        return _dot_nt(q, k_ref[pl.ds(pl.multiple_of(j * tk, tk), tk), :])

    def masked_scores(j):
        k_lo = j * tk
        qc = (q_lo + lax.broadcasted_iota(jnp.int32, (tq, tk), 0)) >> sh
        kc = (k_lo + lax.broadcasted_iota(jnp.int32, (tq, tk), 1)) >> sh
        return jnp.where(kc <= qc, scores(j), NEG_BIG)

    def update(j, s_ref):
        k_lo = pl.multiple_of(j * tk, tk)
        m_prev = m_ref[...]
        m_new = jnp.maximum(m_prev, jnp.max(s_ref[...], axis=-1, keepdims=True))
        alpha = jnp.exp2(m_prev - m_new)
        if tk % lanes == 0:
            p = jnp.exp2(s_ref[...] - jnp.tile(m_new, (1, tk // lanes)))
        else:
            p = jnp.exp2(s_ref[...] - m_new[:, 0:1])
        pv = _dot(p.astype(BF16), v_ref[pl.ds(k_lo, tk), :])
        acc_ref[...] = jnp.tile(alpha, (1, 2)) * acc_ref[...] + pv
        m_ref[...] = m_new

    def body_pair(g, carry):
        sb_ref[...] = scores(2 * g + 1)
        update(2 * g, sa_ref)
        sa_ref[...] = scores(2 * g + 2)
        update(2 * g + 1, sb_ref)
        return carry

    sa_ref[...] = scores(0)
    n_pairs = jnp.maximum(n_full - 1, 0) // 2
    lax.fori_loop(0, n_pairs, body_pair, 0)

    left = n_full - 2 * n_pairs
    has_masked = n_need > n_full
    j_masked = jnp.minimum(n_full, nk - 1)

    @pl.when(left == 1)
    def _():
        sb_ref[...] = masked_scores(j_masked)
        update(n_full - 1, sa_ref)

    @pl.when(jnp.logical_and(left == 1, has_masked))
    def _():
        update(n_full, sb_ref)

    @pl.when(left == 2)
    def _():
        sb_ref[...] = scores(n_full - 1)
        update(n_full - 2, sa_ref)
        sa_ref[...] = masked_scores(j_masked)
        update(n_full - 1, sb_ref)

    @pl.when(jnp.logical_and(left == 2, has_masked))
    def _():
        update(n_full, sa_ref)

    def body_masked(j, carry):
        sa_ref[...] = masked_scores(j)
        update(j, sa_ref)
        return carry

    lax.fori_loop(jnp.where(n_full > 0, n_full + 1, 0), n_need, body_masked, 0)
    o_ref[...] = (acc_ref[:, 0:HEAD_DIM] / acc_ref[:, HEAD_DIM:2 * HEAD_DIM]).astype(BF16)


def _attn_call(q, kf, v):
    past_len = 0
    B, H, Tq, _ = q.shape
    Tk = kf.shape[2]
    tq = 1024 if Tq % 1024 == 0 else Tq
    tk = 1024 if Tk % 1024 == 0 else Tk
    nk = Tk // tk
    kv_bytes = Tk * (MLA_QK + 2 * HEAD_DIM) * 2
    hb = H if H * kv_bytes <= ATTN_KV_BLOCK_BYTES else 1
    return pl.pallas_call(
        functools.partial(_attn_kernel, past_len=past_len, tq=tq, tk=tk, nk=nk),
        out_shape=jax.ShapeDtypeStruct((B, Tq, H * HEAD_DIM), BF16),
        grid=(B, H // hb, Tq // tq),
        in_specs=[pl.BlockSpec((1, hb, tq, MLA_QK), lambda b, h, i: (b, h, i, 0)),
                  pl.BlockSpec((1, hb, Tk, MLA_QK), lambda b, h, i: (b, h, 0, 0)),
                  pl.BlockSpec((1, hb, Tk, 2 * HEAD_DIM), lambda b, h, i: (b, h, 0, 0))],
        out_specs=pl.BlockSpec((1, tq, hb * HEAD_DIM), lambda b, h, i: (b, i, h)),
        scratch_shapes=[pltpu.VMEM((tq, HEAD_DIM), F32), pltpu.VMEM((tq, 2 * HEAD_DIM), F32),
                        pltpu.VMEM((tq, tk), F32), pltpu.VMEM((tq, tk), F32)],
        compiler_params=_cparams("arbitrary", "arbitrary", "arbitrary"),
        name="mla_attention",
    )(q, kf, v)


def _block_diag_pairs(w):
    per = (HALF // 2) // LRU_BLOCK
    w4 = w.reshape(2, per, LRU_BLOCK, LRU_BLOCK)
    eye = jnp.eye(per, dtype=w.dtype)
    out = w4[:, :, :, None, :] * eye[None, :, None, :, None]
    return out.reshape(2, HALF // 2, HALF // 2).astype(BF16)


def _prep_cd_w_in(w):
    o = 3 * HALF
    qkv, gz = w[:, :o], w[:, o:o + HALF]
    o += HALF
    gb, ga = w[:, o:o + HEADS], w[:, o + HEADS:o + 2 * HEADS]
    o += 2 * HEADS
    qa, kva = w[:, o:o + MLA_Q_RANK], w[:, o + MLA_Q_RANK:o + MLA_Q_RANK + MLA_KV_RANK]
    o += MLA_Q_RANK + MLA_KV_RANK
    kr = w[:, o:o + MLA_ROPE]
    assert (SMALL_KR, SMALL_GB, SMALL_GA) == (0, MLA_ROPE, MLA_ROPE + HEADS)
    fill = jnp.zeros((w.shape[0], 128 - MLA_ROPE - 2 * HEADS), w.dtype)
    out = jnp.concatenate([qkv, gz, kva, qa, kr, gb, ga, fill], axis=-1)
    assert out.shape[1] == CD_COLS
    return out.astype(BF16)


def _prep_wqb(w):
    w4 = w.reshape(MLA_Q_RANK, HEADS, MLA_QK)
    nope = w4[:, :, :MLA_NOPE].reshape(MLA_Q_RANK, HEADS * MLA_NOPE)
    rope = w4[:, :, MLA_NOPE:].reshape(MLA_Q_RANK, HEADS * MLA_ROPE)
    return jnp.concatenate([nope, rope], axis=-1).astype(BF16)


def _rope_tables(T, past_len):
    half = MLA_ROPE // 2
    freqs = jnp.exp(-math.log(ROPE_THETA) * jnp.arange(half, dtype=F32) / half)
    pos = past_len + jnp.arange(T, dtype=jnp.int32)
    ang = pos.astype(F32)[:, None] * freqs
    cos, sin = jnp.cos(ang), jnp.sin(ang)
    return jnp.concatenate([cos, cos], axis=-1), jnp.concatenate([-sin, sin], axis=-1)


def _run_group(x, mods, hg_s, lru_h, lru_buf, gd_s, gd_buf, lat_past, kr_past, ffn_buf, W):
    B, T, _ = x.shape
    n_hg, n_lru, n_lrub, n_gd, n_gdb, n_ffn = ([] for _ in range(6))
    mla_state = (jnp.zeros((N_CD, B, T, MLA_KV_RANK), F32), jnp.zeros((N_CD, B, T, MLA_ROPE), F32))
    for l in range(DEPTH):
        j = l // 2
        shift1, scale1, gate1, shift2, scale2, gate2 = jnp.split(mods[l], 6, axis=-1)
        g = W['norm_g'][l]
        if l % 2 == 0:
            proj = _nmm_call(x, g[0], scale1, shift1, W['ab_w_in'][j])
            o_a, s_hg_t = _hgrn_call(proj, W['lower_bounds'][j], W['hgrn_norm_g'][j],
                                     jnp.swapaxes(hg_s[j], -1, -2))
            o_b, s_lru, s_lrub = _lru_call(proj, lru_buf[j], lru_h[j], W['lru_conv_w'][j], W['lru_conv_b'][j],
                                           W['lru_wa_bd'][j], W['lru_wx_bd'][j], W['lru_b_a'][j],
                                           W['lru_b_x'][j], W['lru_lambda'][j])
            n_hg.append(jnp.swapaxes(s_hg_t, -1, -2))
            n_lru.append(s_lru[:, 0, :])
            n_lrub.append(s_lrub)
            mix_a, mix_b, w_out = o_a, o_b, W['ab_w_out'][j]
        else:
            past_len = lat_past.shape[2]
            if _tiles(B, T)[0] == 1:
                proj, small_t = _nmm_call(x, g[0], scale1, shift1, W['cd_w_in'][j], gates_t=True)
            else:
                proj = _nmm_call(x, g[0], scale1, shift1, W['cd_w_in'][j])
                small_t = jnp.swapaxes(proj[:, :, CD_COLS - 128 + SMALL_GB:CD_COLS - 128 + SMALL_GB + 2 * HEADS], 1, 2)
            o_c, s_gd, s_gdb = _gdn_call(proj, small_t, gd_buf[j], W['gdn_conv_w'][j], W['gdn_pcol'][j],
                                         W['gdn_prow'][j], W['gdn_norm_g'][j], gd_s[j])
            cs, sn = _rope_tables(T, past_len)
            q, lat_all, kr_all, kf, v = _mla_prep_call(proj, cs, sn, W['mla_q_norm_g'][j], W['mla_w_qb'][j],
                                                       W['mla_kv_norm_g'][j], W['mla_w_kvb'][j], j, mla_state)
            mla_state = (lat_all, kr_all)
            if past_len > 0:
                o_d = _attn_cached_call(q, lat_past[j], kr_past[j], lat_all, kr_all, j, W['mla_w_kvb'][j])
            else:
                o_d = _attn_call(q, kf, v)
            n_gd.append(s_gd)
            n_gdb.append(s_gdb)
            mix_a, mix_b, w_out = o_c, o_d, W['cd_w_out'][j]
        x, s_ffn = _ffn_call(x, mix_a, mix_b, w_out, g[1], gate1, g[2], scale2, shift2, gate2, g[3], ffn_buf[l],
                             W['ffn_wu'][l], W['ffn_wd'][l], W['ffn_cw'][l])
        n_ffn.append(s_ffn)
    return x, (jnp.stack(n_hg), jnp.stack(n_lru), jnp.stack(n_lrub), jnp.stack(n_gd), jnp.stack(n_gdb),
               mla_state[0], mla_state[1], jnp.stack(n_ffn))


def _prep_weights(norm_g, ab_w_in, ab_w_out, hgrn_lb_logits, hgrn_norm_g, lru_conv_w, lru_conv_b, lru_w_a, lru_b_a,
                  lru_w_x, lru_b_x, lru_lambda, cd_w_in, cd_w_out, gdn_conv_w, gdn_a_log, gdn_dt_bias, gdn_norm_g,
                  mla_q_norm_g, mla_w_qb, mla_kv_norm_g, mla_w_kvb, ffn_w_up, ffn_conv_w, ffn_w_down):
    lb_p = jax.nn.softmax(hgrn_lb_logits.astype(F32), axis=0)
    pcol = jnp.pad(jnp.stack([gdn_a_log, gdn_dt_bias], axis=1),
                   ((0, 0), (0, 0), (SMALL_GA, 128 - SMALL_GA - HEADS)))
    prow = jnp.pad(jnp.stack([gdn_a_log, gdn_dt_bias], axis=-1), ((0, 0), (HEADS, 0), (0, 0)))
    W = dict(
        norm_g=norm_g,
        ab_w_in=ab_w_in.astype(BF16), ab_w_out=ab_w_out.astype(BF16),
        lower_bounds=jnp.cumsum(lb_p, axis=0) - lb_p[0:1],
        hgrn_norm_g=hgrn_norm_g, lru_conv_w=lru_conv_w, lru_conv_b=lru_conv_b,
        lru_wa_bd=jax.vmap(_block_diag_pairs)(lru_w_a), lru_wx_bd=jax.vmap(_block_diag_pairs)(lru_w_x),
        lru_b_a=lru_b_a, lru_b_x=lru_b_x, lru_lambda=lru_lambda,
        cd_w_in=jax.vmap(_prep_cd_w_in)(cd_w_in), cd_w_out=cd_w_out.astype(BF16),
        gdn_conv_w=gdn_conv_w, gdn_pcol=pcol, gdn_prow=prow, gdn_norm_g=gdn_norm_g,
        mla_q_norm_g=mla_q_norm_g, mla_w_qb=jax.vmap(_prep_wqb)(mla_w_qb),
        mla_kv_norm_g=mla_kv_norm_g, mla_w_kvb=mla_w_kvb.astype(BF16),
        ffn_wu=ffn_w_up.astype(BF16), ffn_wd=ffn_w_down.astype(BF16), ffn_cw=ffn_conv_w,
    )
    return W


def kernel(x_prompt, x_sample, c_prompt, c_sample, state_hgrn, state_rglru, state_rglru_conv, state_gdn, state_gdn_conv, cache_mla_latent, cache_mla_krope, state_ffn_conv, ada_w, ada_b, norm_g, ab_w_in, ab_w_out, hgrn_lb_logits, hgrn_norm_g, lru_conv_w, lru_conv_b, lru_w_a, lru_b_a, lru_w_x, lru_b_x, lru_lambda, cd_w_in, cd_w_out, gdn_conv_w, gdn_a_log, gdn_dt_bias, gdn_norm_g, mla_q_norm_g, mla_w_qb, mla_kv_norm_g, mla_w_kvb, ffn_w_up, ffn_conv_w, ffn_w_down):
    bp, bs = x_prompt.shape[0], x_sample.shape[0]
    W = _prep_weights(norm_g, ab_w_in, ab_w_out, hgrn_lb_logits, hgrn_norm_g, lru_conv_w, lru_conv_b, lru_w_a, lru_b_a,
                      lru_w_x, lru_b_x, lru_lambda, cd_w_in, cd_w_out, gdn_conv_w, gdn_a_log, gdn_dt_bias,
                      gdn_norm_g, mla_q_norm_g, mla_w_qb, mla_kv_norm_g, mla_w_kvb, ffn_w_up, ffn_conv_w,
                      ffn_w_down)
    rows = bp + bs
    rows_pad = -(-rows // 8) * 8
    c_all = jnp.concatenate([c_prompt, c_sample, jnp.zeros((rows_pad - rows, D_MODEL), F32)], axis=0)
    mods = _ada_call(c_all, ada_w, ada_b)
    dt_ = x_prompt.dtype
    y_prompt, p_states = _run_group(
        x_prompt, mods[:, :bp],
        jnp.zeros((N_AB, bp, HEADS, HEAD_DIM, HEAD_DIM), F32),
        jnp.zeros((N_AB, bp, HALF), F32),
        jnp.zeros((N_AB, bp, LRU_CONV - 1, HALF), dt_),
        jnp.zeros((N_CD, bp, HEADS, HEAD_DIM, HEAD_DIM), F32),
        jnp.zeros((N_CD, bp, GD_CONV - 1, 3 * HALF), dt_),
        jnp.zeros((N_CD, bp, 0, MLA_KV_RANK), dt_),
        jnp.zeros((N_CD, bp, 0, MLA_ROPE), dt_),
        jnp.zeros((DEPTH, bp, FFN_CONV - 1, 2 * D_FF), dt_),
        W)
    y_sample, s_states = _run_group(
        x_sample, mods[:, bp:rows], state_hgrn, state_rglru, state_rglru_conv, state_gdn, state_gdn_conv,
        cache_mla_latent, cache_mla_krope, state_ffn_conv, W)
    return (y_prompt, y_sample) + tuple(p_states) + tuple(s_states)
```
